```python
import jax
import jax.numpy as jnp
from jax import lax
import numpy as np

D_MODEL = 4096
BATCH = 2
SEQ = 8192
DEPTH = 2

CTX_LEN = 256
GRID_W = 64
EPS = 1e-6

N_MIXERS = 4
GROUP_W = D_MODEL // N_MIXERS

RWKV_HEAD = 64
RWKV_HEADS = GROUP_W // RWKV_HEAD
RWKV_DECAY_RANK = 64
RWKV_ICLR_RANK = 64
RWKV_GATE_RANK = 160
RWKV_GN_EPS = 64e-5
RWKV_COLS = 3 * GROUP_W + 2 * RWKV_DECAY_RANK + 2 * RWKV_ICLR_RANK + RWKV_GATE_RANK

POOL_WINDOWS = (2, 4, 8, 16)
POOL_GROUP = GROUP_W // len(POOL_WINDOWS)
POOL_COLS = GROUP_W

MLSTM_HEADS = 4
MLSTM_DV = GROUP_W // MLSTM_HEADS
MLSTM_DK = MLSTM_DV // 2
MLSTM_CHUNK = 64
MLSTM_COLS = 2 * MLSTM_HEADS * MLSTM_DK + 2 * GROUP_W + 4 * MLSTM_HEADS
ROPE_BASE = 10000.0

NA_HEADS = 8
NA_HEAD = GROUP_W // NA_HEADS
WIN_H = 8
WIN_W = 16
NA_COLS = 3 * GROUP_W

IN_COLS = RWKV_COLS + POOL_COLS + MLSTM_COLS + NA_COLS

N_GROUPS = 4
EXPERTS_PER_GROUP = 4
N_EXPERTS = N_GROUPS * EXPERTS_PER_GROUP
TOP_K = 2
EXPERT_FF = D_MODEL // 4
MOE_BLOCK = 256

kernel_name = 'hybrid_parallel_heads_dit_block'


def split_cols(z, sizes):
    cuts = [int(s) for s in np.cumsum(sizes)[:-1]]
    return jnp.split(z, cuts, axis=-1)


def rms_norm(x, g):
    xf = x.astype(jnp.float32)
    y = xf * lax.rsqrt(jnp.mean(xf * xf, axis=-1, keepdims=True) + EPS)
    return (y * g.astype(jnp.float32)).astype(x.dtype)


def shift3(z):
    zp = jnp.pad(z, ((0, 0), (1, 1), (0, 0)))
    return zp[:, :-2], zp[:, 2:]


def rwkv_stream(p, lp):
    f32 = jnp.float32
    p = p.astype(f32)
    prev, nxt = shift3(p)
    mu = lp['rwkv_mu'].astype(f32)
    p = p + mu[0] * (prev - p) + mu[1] * (nxt - p)
    r, k, v, wd_f, wd_b, ad_f, ad_b, gd = split_cols(
        p, [GROUP_W] * 3 + [RWKV_DECAY_RANK] * 2 + [RWKV_ICLR_RANK] * 2 + [RWKV_GATE_RANK])
    B, T, _ = p.shape
    hd = lambda z: z.reshape(B, T, RWKV_HEADS, RWKV_HEAD)
    kk = hd(k * lp['rwkv_k_k'])
    kk = kk * lax.rsqrt(jnp.sum(kk * kk, axis=-1, keepdims=True) + EPS)
    dirs = []
    for d, (wd, ad) in enumerate(((wd_f, ad_f), (wd_b, ad_b))):
        wlog = -jax.nn.softplus(-(lp['rwkv_w0'][d] + jnp.tanh(wd) @ lp['rwkv_w_up'][d])) - 0.5
        decay = jnp.exp(-jnp.exp(wlog))
        a = jax.nn.sigmoid(lp['rwkv_a0'][d] + ad @ lp['rwkv_a_up'][d])
        kd = k * (1.0 + (a - 1.0) * lp['rwkv_k_a'])
        dirs.append((hd(decay), hd(kd), hd(a)))
    g = jax.nn.sigmoid(gd) @ lp['rwkv_g_up']
    return dict(r=hd(r), k=hd(k), v=hd(v), kk=kk, g=g, dirs=dirs)


def rwkv_scan(state, r, w, k, v, kk, a, reverse):
    xs = tuple(jnp.moveaxis(z, 1, 0) for z in (r, w, k, v, kk, a))

    def step(S, inp):
        rt, wt, kt, vt, kkt, at = inp
        sa = jnp.einsum('bhvk,bhk->bhv', S, -kkt)
        S = S * wt[:, :, None, :] + sa[..., None] * (kkt * at)[:, :, None, :] + vt[..., None] * kt[:, :, None, :]
        return S, jnp.einsum('bhvk,bhk->bhv', S, rt)

    S, ys = lax.scan(step, state, xs, reverse=reverse)
    return S, jnp.moveaxis(ys, 0, 1)


def rwkv_readout(y, f, lp):
    B, T, H, N = y.shape
    mean = jnp.mean(y, axis=-1, keepdims=True)
    var = jnp.mean(jnp.square(y - mean), axis=-1, keepdims=True)
    yn = ((y - mean) * lax.rsqrt(var + RWKV_GN_EPS)).reshape(B, T, GROUP_W)
    yn = yn * lp['rwkv_ln'][0] + lp['rwkv_ln'][1]
    bonus = jnp.sum(f['r'] * f['k'] * lp['rwkv_r_k'], axis=-1, keepdims=True) * f['v']
    return (yn + bonus.reshape(B, T, GROUP_W)) * f['g']


def rwkv_mixer(p_ctx, p_lat, lp):
    fc, fl = rwkv_stream(p_ctx, lp), rwkv_stream(p_lat, lp)
    B = p_lat.shape[0]
    y_c = 0.0
    y_l = 0.0
    for d, rev in enumerate((False, True)):
        s0 = jnp.zeros((B, RWKV_HEADS, RWKV_HEAD, RWKV_HEAD), jnp.float32)
        wc, kc, ac = fc['dirs'][d]
        s_ctx, yc = rwkv_scan(s0, fc['r'], wc, kc, fc['v'], fc['kk'], ac, rev)
        wl, kl, al = fl['dirs'][d]
        _, yl = rwkv_scan(s_ctx, fl['r'], wl, kl, fl['v'], fl['kk'], al, rev)
        y_c = y_c + yc
        y_l = y_l + yl
    return (rwkv_readout(y_c, fc, lp).astype(p_ctx.dtype), rwkv_readout(y_l, fl, lp).astype(p_lat.dtype))


def pool_mixer(p, lp):
    T = p.shape[1]
    pf = p.astype(jnp.float32)
    cs = jnp.pad(jnp.cumsum(pf, axis=1), ((0, 0), (1, 0), (0, 0)))
    t = jnp.arange(T)
    outs = []
    for gi, win in enumerate(POOL_WINDOWS):
        sl = slice(gi * POOL_GROUP, (gi + 1) * POOL_GROUP)
        lo = jnp.clip(t - win // 2, 0, T)
        hi = jnp.clip(t + win // 2, 0, T)
        csg = cs[..., sl]
        mean = (csg[:, hi] - csg[:, lo]) / (hi - lo).astype(jnp.float32)[None, :, None]
        z = (mean - pf[..., sl]).astype(p.dtype)
        outs.append(z @ lp['pool_w'][gi])
    return jnp.concatenate(outs, axis=-1) * lp['pool_scale']


def rope_2d(T):
    t = jnp.arange(T)
    row = (t // GRID_W).astype(jnp.float32)
    col = (t % GRID_W).astype(jnp.float32)
    n_pairs = MLSTM_DK // 4
    inv = ROPE_BASE ** (-jnp.arange(n_pairs, dtype=jnp.float32) / n_pairs)
    ang = jnp.concatenate([row[:, None] * inv, col[:, None] * inv], axis=-1)
    return jnp.cos(ang)[None, :, None, :], jnp.sin(ang)[None, :, None, :]


def apply_rope(x, cos, sin):
    xf = x.astype(jnp.float32).reshape(*x.shape[:-1], -1, 2)
    x1, x2 = xf[..., 0], xf[..., 1]
    out = jnp.stack([x1 * cos - x2 * sin, x1 * sin + x2 * cos], axis=-1)
    return out.reshape(x.shape).astype(x.dtype)


def mlstm_stream(p, lp, rope):
    f32 = jnp.float32
    B, T, _ = p.shape
    q, k, v, o, gates = split_cols(p, [MLSTM_HEADS * MLSTM_DK] * 2 + [GROUP_W] * 2 + [4 * MLSTM_HEADS])
    qk = jnp.concatenate([q, k], axis=-1)
    prev, nxt = shift3(qk)
    cw = lp['mlstm_conv']
    qk = jax.nn.silu(cw[0] * prev + cw[1] * qk + cw[2] * nxt)
    q, k = jnp.split(qk, 2, axis=-1)
    q = q.reshape(B, T, MLSTM_HEADS, MLSTM_DK)
    k = k.reshape(B, T, MLSTM_HEADS, MLSTM_DK)
    if rope is not None:
        q = apply_rope(q, *rope)
        k = apply_rope(k, *rope)
    k = k * (MLSTM_DK ** -0.5)
    v = v.reshape(B, T, MLSTM_HEADS, MLSTM_DV)
    gates = gates.astype(f32).reshape(B, T, 2, 2, MLSTM_HEADS) + lp['mlstm_gate_b']
    heads = tuple(jnp.moveaxis(z.astype(f32), 1, 2) for z in (q, k, v))
    i_pre = [jnp.moveaxis(gates[:, :, d, 0], 1, 2) for d in range(2)]
    log_f = [jax.nn.log_sigmoid(jnp.moveaxis(gates[:, :, d, 1], 1, 2)) for d in range(2)]
    return heads, i_pre, log_f, o


def mlstm_chunkwise(q, k, v, log_f, i_pre, state):
    B, H, T, _ = q.shape
    L = MLSTM_CHUNK
    nc = T // L
    chunks = lambda z: jnp.moveaxis(z.reshape(B, H, nc, L, *z.shape[3:]), 2, 0)
    xs = tuple(chunks(z) for z in (q, k, v, log_f, i_pre))
    causal = jnp.tril(jnp.ones((L, L), bool))

    def step(carry, inp):
        C, n, m = carry
        qc, kc, vc, lf, ig = inp
        b = jnp.cumsum(lf, axis=-1)
        dlog = jnp.where(causal, b[..., :, None] - b[..., None, :] + ig[..., None, :], -jnp.inf)
        inter = b + m[..., None]
        m_t = jnp.maximum(inter, jnp.max(dlog, axis=-1))
        dmat = jnp.exp(dlog - m_t[..., None])
        w_inter = jnp.exp(inter - m_t)
        s = jnp.einsum('bhtd,bhsd->bhts', qc, kc) * dmat
        num = jnp.einsum('bhts,bhsv->bhtv', s, vc) + w_inter[..., None] * jnp.einsum('bhtd,bhdv->bhtv', qc, C)
        den = jnp.sum(s, axis=-1) + w_inter * jnp.einsum('bhtd,bhd->bht', qc, n)
        h = num / jnp.maximum(jnp.abs(den), jnp.exp(-m_t))[..., None]
        b_end = b[..., -1]
        g_s = b_end[..., None] - b + ig
        m_new = jnp.maximum(b_end + m, jnp.max(g_s, axis=-1))
        w_s = jnp.exp(g_s - m_new[..., None])
        carry_decay = jnp.exp(b_end + m - m_new)
        C = carry_decay[..., None, None] * C + jnp.einsum('bhs,bhsd,bhsv->bhdv', w_s, kc, vc)
        n = carry_decay[..., None] * n + jnp.einsum('bhs,bhsd->bhd', w_s, kc)
        return (C, n, m_new), h

    state, hs = lax.scan(step, state, xs)
    return state, jnp.moveaxis(hs, 0, 2).reshape(B, H, T, -1)


def mlstm_readout(h, o, lp):
    h = jnp.moveaxis(h, 1, 2)
    B, T = h.shape[:2]
    hn = h * lax.rsqrt(jnp.mean(h * h, axis=-1, keepdims=True) + EPS)
    return hn.reshape(B, T, GROUP_W) * lp['mlstm_norm'] * jax.nn.sigmoid(o.astype(jnp.float32))


def mlstm_mixer(p_ctx, p_lat, lp, rope):
    (qc, kc, vc), ic, lfc, oc = mlstm_stream(p_ctx, lp, None)
    (ql, kl, vl), il, lfl, ol = mlstm_stream(p_lat, lp, rope)
    B = p_lat.shape[0]
    hc_sum = 0.0
    hl_sum = 0.0
    for d in range(2):
        rev = (lambda z: jnp.flip(z, axis=2)) if d == 1 else (lambda z: z)
        st = (jnp.zeros((B, MLSTM_HEADS, MLSTM_DK, MLSTM_DV), jnp.float32),
              jnp.zeros((B, MLSTM_HEADS, MLSTM_DK), jnp.float32),
              jnp.zeros((B, MLSTM_HEADS), jnp.float32))
        st, hc = mlstm_chunkwise(rev(qc), rev(kc), rev(vc), rev(lfc[d]), rev(ic[d]), st)
        _, hl = mlstm_chunkwise(rev(ql), rev(kl), rev(vl), rev(lfl[d]), rev(il[d]), st)
        hc_sum = hc_sum + rev(hc)
        hl_sum = hl_sum + rev(hl)
    return (mlstm_readout(hc_sum, oc, lp).astype(p_ctx.dtype), mlstm_readout(hl_sum, ol, lp).astype(p_lat.dtype))


def neighbourhood_attention(q, k, v, kc, vc, rpb):
    B, T, H, Dh = q.shape
    rows = T // GRID_W
    kh = min(WIN_H, rows)
    q = q.reshape(B, rows, GRID_W, H, Dh)
    k = k.reshape(B, rows, GRID_W, H, Dh)
    v = v.reshape(B, rows, GRID_W, H, Dh)
    col = jnp.arange(GRID_W)
    c0 = jnp.clip(col - WIN_W // 2, 0, GRID_W - WIN_W)
    cidx = c0[:, None] + jnp.arange(WIN_W)[None, :]
    dc = cidx - col[:, None] + (WIN_W - 1)
    n_loc = kh * WIN_W

    def row_block(r):
        r0 = jnp.clip(r - kh // 2, 0, rows - kh)
        kb = lax.dynamic_slice_in_dim(k, r0, kh, axis=1)
        vb = lax.dynamic_slice_in_dim(v, r0, kh, axis=1)
        kw = kb[:, :, cidx]
        vw = vb[:, :, cidx]
        qr = lax.dynamic_index_in_dim(q, r, axis=1, keepdims=False)
        dr = r0 + jnp.arange(kh) - r + (WIN_H - 1)
        bias = jnp.transpose(rpb[:, dr[:, None, None], dc[None]], (0, 2, 1, 3))
        s_loc = jnp.einsum('bqhd,brqwhd->bhqrw', qr, kw).astype(jnp.float32) + bias[None].astype(jnp.float32)
        s_ctx = jnp.einsum('bqhd,bchd->bhqc', qr, kc).astype(jnp.float32)
        s = jnp.concatenate([s_loc.reshape(B, H, GRID_W, n_loc), s_ctx], axis=-1)
        pr = jax.nn.softmax(s, axis=-1).astype(v.dtype)
        p_loc = pr[..., :n_loc].reshape(B, H, GRID_W, kh, WIN_W)
        return (jnp.einsum('bhqrw,brqwhd->bqhd', p_loc, vw)
                + jnp.einsum('bhqc,bchd->bqhd', pr[..., n_loc:], vc))

    out = lax.map(row_block, jnp.arange(rows))
    return jnp.moveaxis(out, 0, 1).reshape(B, T, H * Dh)


def na_mixer(p_ctx, p_lat, lp):
    def qkv(p):
        B, T, _ = p.shape
        q, k, v = [z.reshape(B, T, NA_HEADS, NA_HEAD) for z in jnp.split(p, 3, axis=-1)]
        return rms_norm(q, lp['na_qk_norm'][0]) * (NA_HEAD ** -0.5), rms_norm(k, lp['na_qk_norm'][1]), v
    qc, kc, vc = qkv(p_ctx)
    ql, kl, vl = qkv(p_lat)
    B, L = p_ctx.shape[:2]
    s = jnp.einsum('bqhd,bkhd->bhqk', qc, kc).astype(jnp.float32)
    y_ctx = jnp.einsum('bhqk,bkhd->bqhd', jax.nn.softmax(s, axis=-1).astype(vc.dtype), vc).reshape(B, L, GROUP_W)
    y_lat = neighbourhood_attention(ql, kl, vl, kc, vc, lp['na_rpb'])
    return y_ctx, y_lat


def token_mixers(p_ctx, p_lat, lp, rope):
    sizes = [RWKV_COLS, POOL_COLS, MLSTM_COLS, NA_COLS]
    a_c, b_c, c_c, d_c = split_cols(p_ctx, sizes)
    a_l, b_l, c_l, d_l = split_cols(p_lat, sizes)
    ya = rwkv_mixer(a_c, a_l, lp)
    yb = (pool_mixer(b_c, lp), pool_mixer(b_l, lp))
    yc = mlstm_mixer(c_c, c_l, lp, rope)
    yd = na_mixer(d_c, d_l, lp)
    y_ctx = jnp.concatenate([ya[0], yb[0], yc[0], yd[0]], axis=-1)
    y_lat = jnp.concatenate([ya[1], yb[1], yc[1], yd[1]], axis=-1)
    return y_ctx, y_lat


def hier_moe(h, lp):
    N, D = h.shape
    g_prob = jax.nn.softmax((h @ lp['router_g_w']).astype(jnp.float32) + lp['router_g_b'], axis=-1)
    g_p, g_idx = lax.top_k(g_prob, 1)
    e_logits = ((h @ lp['router_e_w']).astype(jnp.float32) + lp['router_e_b']).reshape(N, N_GROUPS, EXPERTS_PER_GROUP)
    sel = jnp.broadcast_to(g_idx[:, :, None], (N, 1, EXPERTS_PER_GROUP))
    e_logits = jnp.take_along_axis(e_logits, sel, axis=1)[:, 0]
    e_top, e_idx = lax.top_k(e_logits, TOP_K)
    wts = jax.nn.softmax(e_top, axis=-1) * g_p
    expert = g_idx * EXPERTS_PER_GROUP + e_idx

    A = N * TOP_K
    e_flat = expert.reshape(A)
    w_flat = wts.reshape(A)
    tok = (jnp.arange(A) // TOP_K).astype(jnp.int32)
    order = jnp.argsort(e_flat)
    e_sorted = e_flat[order]
    counts = jnp.bincount(e_flat, length=N_EXPERTS)
    padded = (counts + MOE_BLOCK - 1) // MOE_BLOCK * MOE_BLOCK
    ends = jnp.cumsum(padded)
    slot = (ends - padded)[e_sorted] + jnp.arange(A) - (jnp.cumsum(counts) - counts)[e_sorted]
    n_blocks = -(-(A + N_EXPERTS * (MOE_BLOCK - 1)) // MOE_BLOCK)
    P = n_blocks * MOE_BLOCK
    slot_tok = jnp.zeros((P,), jnp.int32).at[slot].set(tok[order])
    slot_w = jnp.zeros((P,), jnp.float32).at[slot].set(w_flat[order])
    block_expert = jnp.minimum(jnp.searchsorted(ends, jnp.arange(n_blocks) * MOE_BLOCK, side='right'), N_EXPERTS - 1)

    def run_block(args):
        bi, e = args
        xb = h[lax.dynamic_slice_in_dim(slot_tok, bi * MOE_BLOCK, MOE_BLOCK)]
        return (jax.nn.silu(xb @ lp['moe_w_gate'][e]) * (xb @ lp['moe_w_up'][e])) @ lp['moe_w_down'][e]

    yb = lax.map(run_block, (jnp.arange(n_blocks), block_expert))
    return jnp.zeros_like(h).at[slot_tok].add(yb.reshape(P, D) * slot_w[:, None].astype(h.dtype))


def setup_inputs(seed: int = 0) -> dict:
    key = jax.random.key(seed)
    keys = jax.random.split(key, 48)
    cnt = [0]

    def nk():
        k_ = keys[cnt[0]]
        cnt[0] += 1
        return k_

    def nrm(shape, scale):
        return jax.random.normal(nk(), shape, jnp.float32) * scale

    def unif(shape, lo, hi):
        return jax.random.uniform(nk(), shape, jnp.float32, lo, hi)

    def gain(shape):
        return 1.0 + nrm(shape, 0.1)

    L, D = DEPTH, D_MODEL
    inp = {}
    inp['x'] = nrm((BATCH, SEQ, D), 1.0)
    inp['c'] = nrm((BATCH, D), 1.0)
    inp['ctx'] = nrm((BATCH, CTX_LEN, D), 1.0)
    inp['c_ctx'] = nrm((D,), 1.0)
    inp['ada_w'] = nrm((L, D, 6 * D), 0.5 * D ** -0.5)
    inp['ada_b'] = nrm((L, 6 * D), 0.02)
    inp['norm_mix'] = gain((L, D))
    inp['norm_ffn'] = gain((L, D))
    inp['w_in'] = nrm((L, D, IN_COLS), D ** -0.5)
    inp['w_out'] = nrm((L, D, D), D ** -0.5)
    inp['rwkv_mu'] = unif((L, 2, RWKV_COLS), 0.0, 0.5)
    inp['rwkv_w0'] = unif((L, 2, GROUP_W), -6.0, -1.0)
    inp['rwkv_w_up'] = nrm((L, 2, RWKV_DECAY_RANK, GROUP_W), 0.5 * RWKV_DECAY_RANK ** -0.5)
    inp['rwkv_a0'] = nrm((L, 2, GROUP_W), 0.5)
    inp['rwkv_a_up'] = nrm((L, 2, RWKV_ICLR_RANK, GROUP_W), 0.5 * RWKV_ICLR_RANK ** -0.5)
    inp['rwkv_g_up'] = nrm((L, RWKV_GATE_RANK, GROUP_W), RWKV_GATE_RANK ** -0.5)
    inp['rwkv_k_k'] = gain((L, GROUP_W))
    inp['rwkv_k_a'] = gain((L, GROUP_W))
    inp['rwkv_r_k'] = nrm((L, RWKV_HEADS, RWKV_HEAD), 0.1)
    inp['rwkv_ln'] = jnp.stack([gain((L, GROUP_W)), nrm((L, GROUP_W), 0.02)], axis=1)
    inp['pool_w'] = nrm((L, len(POOL_WINDOWS), POOL_GROUP, POOL_GROUP), POOL_GROUP ** -0.5)
    inp['pool_scale'] = gain((L, GROUP_W))
    inp['mlstm_conv'] = jnp.array([0.0, 1.0, 0.0], jnp.float32)[None, :, None] + nrm((L, 3, 2 * MLSTM_HEADS * MLSTM_DK), 0.3)
    inp['mlstm_gate_b'] = jnp.array([0.0, 3.0], jnp.float32)[None, None, :, None] + nrm((L, 2, 2, MLSTM_HEADS), 0.5)
    inp['mlstm_norm'] = gain((L, GROUP_W))
    inp['na_qk_norm'] = gain((L, 2, NA_HEAD))
    inp['na_rpb'] = nrm((L, NA_HEADS, 2 * WIN_H - 1, 2 * WIN_W - 1), 0.2)
    inp['router_g_w'] = nrm((L, D, N_GROUPS), D ** -0.5)
    inp['router_g_b'] = nrm((L, N_GROUPS), 0.01)
    inp['router_e_w'] = nrm((L, D, N_EXPERTS), D ** -0.5)
    inp['router_e_b'] = nrm((L, N_EXPERTS), 0.01)
    inp['moe_w_gate'] = nrm((L, N_EXPERTS, D, EXPERT_FF), D ** -0.5)
    inp['moe_w_up'] = nrm((L, N_EXPERTS, D, EXPERT_FF), D ** -0.5)
    inp['moe_w_down'] = nrm((L, N_EXPERTS, EXPERT_FF, D), EXPERT_FF ** -0.5)
    return inp


def reference(x, c, ctx, c_ctx, ada_w, ada_b, norm_mix, norm_ffn, w_in, w_out, rwkv_mu, rwkv_w0, rwkv_w_up, rwkv_a0, rwkv_a_up, rwkv_g_up, rwkv_k_k, rwkv_k_a, rwkv_r_k, rwkv_ln, pool_w, pool_scale, mlstm_conv, mlstm_gate_b, mlstm_norm, na_qk_norm, na_rpb, router_g_w, router_g_b, router_e_w, router_e_b, moe_w_gate, moe_w_up, moe_w_down):
    B, T, D = x.shape
    L = ctx.shape[1]
    rope = rope_2d(T)
    s_lat = jax.nn.silu(c)
    s_ctx = jax.nn.silu(c_ctx)
    h_lat, h_ctx = x, ctx
    for l in range(DEPTH):
        last = l == DEPTH - 1
        lp = dict(rwkv_mu=rwkv_mu[l], rwkv_w0=rwkv_w0[l], rwkv_w_up=rwkv_w_up[l], rwkv_a0=rwkv_a0[l],
                  rwkv_a_up=rwkv_a_up[l], rwkv_g_up=rwkv_g_up[l], rwkv_k_k=rwkv_k_k[l], rwkv_k_a=rwkv_k_a[l],
                  rwkv_r_k=rwkv_r_k[l], rwkv_ln=rwkv_ln[l], pool_w=pool_w[l], pool_scale=pool_scale[l],
                  mlstm_conv=mlstm_conv[l], mlstm_gate_b=mlstm_gate_b[l], mlstm_norm=mlstm_norm[l],
                  na_qk_norm=na_qk_norm[l], na_rpb=na_rpb[l], router_g_w=router_g_w[l], router_g_b=router_g_b[l],
                  router_e_w=router_e_w[l], router_e_b=router_e_b[l], moe_w_gate=moe_w_gate[l],
                  moe_w_up=moe_w_up[l], moe_w_down=moe_w_down[l])
        m_lat = jnp.split((s_lat @ ada_w[l] + ada_b[l])[:, None, :], 6, axis=-1)
        m_ctx = jnp.split(s_ctx @ ada_w[l] + ada_b[l], 6, axis=-1)
        u_lat = rms_norm(h_lat, norm_mix[l]) * (1.0 + m_lat[1]) + m_lat[0]
        u_ctx = rms_norm(h_ctx, norm_mix[l]) * (1.0 + m_ctx[1]) + m_ctx[0]
        y_ctx, y_lat = token_mixers(u_ctx @ w_in[l], u_lat @ w_in[l], lp, rope)
        h_lat = h_lat + m_lat[2] * (y_lat @ w_out[l])
        v_lat = rms_norm(h_lat, norm_ffn[l]) * (1.0 + m_lat[4]) + m_lat[3]
        if last:
            f_lat = hier_moe(v_lat.reshape(B * T, D), lp).reshape(B, T, D)
        else:
            h_ctx = h_ctx + m_ctx[2] * (y_ctx @ w_out[l])
            v_ctx = rms_norm(h_ctx, norm_ffn[l]) * (1.0 + m_ctx[4]) + m_ctx[3]
            f = hier_moe(jnp.concatenate([v_ctx.reshape(B * L, D), v_lat.reshape(B * T, D)], axis=0), lp)
            h_ctx = h_ctx + m_ctx[5] * f[:B * L].reshape(B, L, D)
            f_lat = f[B * L:].reshape(B, T, D)
        h_lat = h_lat + m_lat[5] * f_lat
    return h_lat
```

```python
import functools
import math

import numpy as np
import jax
import jax.numpy as jnp
from jax import lax
from jax.experimental import pallas as pl
from jax.experimental.pallas import tpu as pltpu

F32 = jnp.float32
BF16 = jnp.bfloat16

D_MODEL = 4096
DEPTH = 2
GRID_W = 64
EPS = 1e-6
GROUP_W = 1024

RWKV_HEAD = 64
RWKV_DECAY_RANK = 64
RWKV_ICLR_RANK = 64
RWKV_GATE_RANK = 160
RWKV_GN_EPS = 64e-5
RWKV_COLS = 3 * GROUP_W + 2 * RWKV_DECAY_RANK + 2 * RWKV_ICLR_RANK + RWKV_GATE_RANK
RWKV_PAD_COLS = 3 * GROUP_W + 256 + 256

POOL_WINDOWS = (2, 4, 8, 16)
POOL_GROUP = 256
POOL_HALO = 8

MLSTM_HEADS = 4
MLSTM_DV = 256
MLSTM_DK = 128
MLSTM_COLS = 2 * MLSTM_HEADS * MLSTM_DK + 2 * GROUP_W + 4 * MLSTM_HEADS
MLSTM_PAD_COLS = 2 * MLSTM_HEADS * MLSTM_DK + 2 * GROUP_W + 128
ROPE_BASE = 10000.0

NA_HEADS = 8
NA_HEAD = 128
WIN_H = 8
WIN_W = 16
NA_COLS = 3 * GROUP_W
NA_QROWS = 4
NA_KROWS = 3 * NA_QROWS

N_GROUPS = 4
EXPERTS_PER_GROUP = 4
N_EXPERTS = 16
TOP_K = 2
EXPERT_FF = 1024
MOE_BLOCK = 256
MOE_FF_TILE = 512

LANES = 128
SUBLANES = 8
VMEM_LIMIT = 56 * 1024 * 1024
NEG = -1e30

RWKV_CHUNK = 64
RWKV_PAIRS = 8
MLSTM_CHUNK = 256


def _cp(*sem):
    return pltpu.CompilerParams(dimension_semantics=sem, vmem_limit_bytes=VMEM_LIMIT)


def _dot(a, b):
    return jnp.dot(a.astype(BF16), b.astype(BF16), preferred_element_type=F32)


def _dot_nt(a, b):
    return lax.dot_general(a.astype(BF16), b.astype(BF16), (((1,), (1,)), ((), ())),
                           preferred_element_type=F32)


def _dot_tn(a, b):
    return lax.dot_general(a.astype(BF16), b.astype(BF16), (((0,), (0,)), ((), ())),
                           preferred_element_type=F32)


def _split2(x):
    hi = x.astype(BF16)
    lo = (x - hi.astype(F32)).astype(BF16)
    return hi, lo


def _dot3(a, b, dot=_dot):
    ah, al = _split2(a)
    bh, bl = _split2(b)
    return dot(ah, bh) + (dot(ah, bl) + dot(al, bh))


def _dot_exact_lhs(a_bf16, b):
    b0 = b.astype(BF16)
    r1 = b - b0.astype(F32)
    b1 = r1.astype(BF16)
    b2 = (r1 - b1.astype(F32)).astype(BF16)
    return _dot(a_bf16, b0) + (_dot(a_bf16, b1) + _dot(a_bf16, b2))


def _sigmoid(x):
    return 1.0 / (1.0 + jnp.exp(-x))


def _silu(x):
    return x * _sigmoid(x)


def _log_sigmoid(x):
    return jnp.minimum(x, 0.0) - jnp.log(1.0 + jnp.exp(-jnp.abs(x)))


def _head_sum(x, width):
    r = lax.broadcasted_iota(jnp.int32, (LANES, LANES), 0) // width
    c = lax.broadcasted_iota(jnp.int32, (LANES, LANES), 1) // width
    e = jnp.where(r == c, 1.0, 0.0).astype(BF16)
    outs = []
    for j in range(x.shape[1] // LANES):
        hi, lo = _split2(x[:, j * LANES:(j + 1) * LANES])
        outs.append(_dot(hi, e) + _dot(lo, e))
    return jnp.concatenate(outs, axis=1)


def _ada_kernel(c_ref, w_ref, b_ref, o_ref):
    s = _silu(c_ref[...])
    o_ref[0] = _dot(s, w_ref[0]) + b_ref[0]


def ada_modulation(cvec, ada_w, ada_b):
    L, D, N = ada_w.shape
    tn = 512
    return pl.pallas_call(
        _ada_kernel,
        grid=(L, N // tn),
        in_specs=[pl.BlockSpec((SUBLANES, D), lambda l, j: (0, 0)),
                  pl.BlockSpec((1, D, tn), lambda l, j: (l, 0, j)),
                  pl.BlockSpec((1, 1, tn), lambda l, j: (l, 0, j))],
        out_specs=pl.BlockSpec((1, SUBLANES, tn), lambda l, j: (l, 0, j)),
        out_shape=jax.ShapeDtypeStruct((L, SUBLANES, N), F32),
        compiler_params=_cp("parallel", "parallel"),
        name="ada_modulation",
    )(cvec, ada_w, ada_b.reshape(L, 1, N))


def _normmod_kernel(h_ref, g_ref, sh_ref, sc_ref, o_ref):
    x = h_ref[0]
    y = x * lax.rsqrt(jnp.mean(x * x, axis=-1, keepdims=True) + EPS) * g_ref[...]
    o_ref[0] = (y * (1.0 + sc_ref[0]) + sh_ref[0]).astype(o_ref.dtype)


def norm_modulate(h, g, shift, scale):
    B, T, D = h.shape
    tm = 256
    vec = pl.BlockSpec((1, 1, D), lambda b, i: (b, 0, 0))
    return pl.pallas_call(
        _normmod_kernel,
        grid=(B, T // tm),
        in_specs=[pl.BlockSpec((1, tm, D), lambda b, i: (b, i, 0)),
                  pl.BlockSpec((1, D), lambda b, i: (0, 0)), vec, vec],
        out_specs=pl.BlockSpec((1, tm, D), lambda b, i: (b, i, 0)),
        out_shape=jax.ShapeDtypeStruct((B, T, D), BF16),
        compiler_params=_cp("parallel", "parallel"),
        name="norm_modulate",
    )(h, g.reshape(1, D), shift, scale)


def _mm_kernel(x_ref, w_ref, o_ref):
    o_ref[0] = jnp.dot(x_ref[0], w_ref[...], preferred_element_type=F32).astype(o_ref.dtype)


def _pick_tile(n, cands):
    for t in cands:
        if n % t == 0:
            return t
    raise ValueError(n)


def matmul(x, w, out_dtype=F32):
    B, T, K = x.shape
    N = w.shape[1]
    tm = _pick_tile(T, (512, 256))
    tn = _pick_tile(N, (1024, 896, 768, 640, 512))
    return pl.pallas_call(
        _mm_kernel,
        grid=(B, T // tm, N // tn),
        in_specs=[pl.BlockSpec((1, tm, K), lambda b, i, j: (b, i, 0)),
                  pl.BlockSpec((K, tn), lambda b, i, j: (0, j))],
        out_specs=pl.BlockSpec((1, tm, tn), lambda b, i, j: (b, i, j)),
        out_shape=jax.ShapeDtypeStruct((B, T, N), out_dtype),
        compiler_params=_cp("parallel", "parallel", "arbitrary"),
        name="in_proj",
    )(x, w)


def _outproj_kernel(ya_ref, yb_ref, yc_ref, yd_ref, w_ref, h_ref, gate_ref, o_ref):
    acc = jnp.dot(ya_ref[0], w_ref[0:GROUP_W, :], preferred_element_type=F32)
    acc += jnp.dot(yb_ref[0], w_ref[GROUP_W:2 * GROUP_W, :], preferred_element_type=F32)
    acc += jnp.dot(yc_ref[0], w_ref[2 * GROUP_W:3 * GROUP_W, :], preferred_element_type=F32)
    acc += jnp.dot(yd_ref[0], w_ref[3 * GROUP_W:4 * GROUP_W, :], preferred_element_type=F32)
    o_ref[0] = h_ref[0] + gate_ref[0] * acc


def out_proj_residual(ys, w, h, gate):
    B, T, D = h.shape
    tm = _pick_tile(T, (512, 256))
    tn = 512
    ysp = pl.BlockSpec((1, tm, GROUP_W), lambda b, i, j: (b, i, 0))
    return pl.pallas_call(
        _outproj_kernel,
        grid=(B, T // tm, D // tn),
        in_specs=[ysp, ysp, ysp, ysp,
                  pl.BlockSpec((D, tn), lambda b, i, j: (0, j)),
                  pl.BlockSpec((1, tm, tn), lambda b, i, j: (b, i, j)),
                  pl.BlockSpec((1, 1, tn), lambda b, i, j: (b, 0, j))],
        out_specs=pl.BlockSpec((1, tm, tn), lambda b, i, j: (b, i, j)),
        out_shape=jax.ShapeDtypeStruct((B, T, D), F32),
        compiler_params=_cp("parallel", "parallel", "arbitrary"),
        name="out_proj",
    )(*ys, w, h, gate)


def _halo_specs(tm, width, halo=SUBLANES):
    per = tm // halo

    def prev_map(b, i):
        return (b, jnp.maximum(i * per - 1, 0), 0)

    def next_map(nb):
        return lambda b, i: (b, jnp.minimum((i + 1) * per, nb - 1), 0)

    return prev_map, next_map


def _shift_rows(x, prev_row, next_row):
    tm = x.shape[0]
    row = lax.broadcasted_iota(jnp.int32, x.shape, 0)
    prev = jnp.where(row == 0, prev_row, pltpu.roll(x, 1, axis=0))
    nxt = jnp.where(row == tm - 1, next_row, pltpu.roll(x, tm - 1, axis=0))
    return prev, nxt


def _rwkv_prep_kernel(p_ref, pp_ref, pn_ref, mu_ref, wup_ref, w0_ref, aup_ref, a0_ref, gup_ref,
                      kk_w_ref, ka_w_ref, rk_w_ref,
                      r_ref, kk_ref, v_ref, bonus_ref, g_ref, lw_ref, kd_ref, bb_ref):
    i = pl.program_id(1)
    last = pl.num_programs(1) - 1
    x = p_ref[0]
    prev_row = jnp.where(i == 0, 0.0, pp_ref[0, SUBLANES - 1:SUBLANES, :])
    next_row = jnp.where(i == last, 0.0, pn_ref[0, 0:1, :])
    prev, nxt = _shift_rows(x, prev_row, next_row)
    x = x + mu_ref[0:1, :] * (prev - x) + mu_ref[1:2, :] * (nxt - x)

    r = x[:, 0:GROUP_W]
    k = x[:, GROUP_W:2 * GROUP_W]
    v = x[:, 2 * GROUP_W:3 * GROUP_W]
    low = x[:, 3 * GROUP_W:3 * GROUP_W + 256]
    gd = x[:, 3 * GROUP_W + 256:3 * GROUP_W + 512]

    zw = _dot(jnp.tanh(low[:, 0:LANES]), wup_ref[...]) + w0_ref[...]
    za = _dot(low[:, LANES:2 * LANES], aup_ref[...]) + a0_ref[...]
    kkr = k * kk_w_ref[...]
    kk = kkr * lax.rsqrt(_head_sum(kkr * kkr, RWKV_HEAD) + EPS)
    r_ref[0] = r
    kk_ref[0] = kk
    v_ref[0] = v
    bonus_ref[0] = _head_sum(r * k * rk_w_ref[...], RWKV_HEAD) * v
    g_ref[0] = _dot(_sigmoid(gd), gup_ref[...])
    for d in range(2):
        sl = slice(d * GROUP_W, (d + 1) * GROUP_W)
        lw_ref[d, 0] = (-math.exp(-0.5)) * _sigmoid(zw[:, sl])
        a = _sigmoid(za[:, sl])
        kd_ref[d, 0] = k * (1.0 + (a - 1.0) * ka_w_ref[...])
        bb_ref[d, 0] = kk * a


def rwkv_prep(p, w):
    B, T, W = p.shape
    tm = 128
    prev_map, next_map = _halo_specs(tm, W)
    nb8 = T // SUBLANES
    full = lambda shape: pl.BlockSpec(shape, lambda b, i: tuple(0 for _ in shape))
    tok = pl.BlockSpec((1, tm, GROUP_W), lambda b, i: (b, i, 0))
    tokd = pl.BlockSpec((2, 1, tm, GROUP_W), lambda b, i: (0, b, i, 0))
    sh = jax.ShapeDtypeStruct((B, T, GROUP_W), F32)
    shd = jax.ShapeDtypeStruct((2, B, T, GROUP_W), F32)
    return pl.pallas_call(
        _rwkv_prep_kernel,
        grid=(B, T // tm),
        in_specs=[pl.BlockSpec((1, tm, W), lambda b, i: (b, i, 0)),
                  pl.BlockSpec((1, SUBLANES, W), prev_map),
                  pl.BlockSpec((1, SUBLANES, W), next_map(nb8)),
                  full((2, W)), full((LANES, 2 * GROUP_W)), full((1, 2 * GROUP_W)),
                  full((LANES, 2 * GROUP_W)), full((1, 2 * GROUP_W)), full((256, GROUP_W)),
                  full((1, GROUP_W)), full((1, GROUP_W)), full((1, GROUP_W))],
        out_specs=[tok, tok, tok, tok, tok, tokd, tokd, tokd],
        out_shape=[sh, sh, sh, sh, sh, shd, shd, shd],
        compiler_params=_cp("parallel", "parallel"),
        name="rwkv_prep",
    )(p, p, p, w["mu"], w["w_up"], w["w0"], w["a_up"], w["a0"], w["g_up"],
      w["k_k"], w["k_a"], w["r_k"])


def _dform(x, m0):
    return jnp.concatenate([jnp.where(m0, x, 0.0), jnp.where(m0, 0.0, x)], axis=0)


def _rwkv_pair(r, kk, v, lw, kd, bb, st, rev):
    C = r.shape[0]
    C2 = 2 * C
    sgn = jnp.where(rev, -1, 1)
    rr = lax.broadcasted_iota(jnp.int32, (C, C), 0)
    cc = lax.broadcasted_iota(jnp.int32, (C, C), 1)
    tri = jnp.where((rr - cc) * sgn >= 0, 1.0, 0.0).astype(BF16)
    lin = _dot_exact_lhs(tri, lw)
    lex = lin - lw
    ltot = jnp.sum(lw, axis=0, keepdims=True)
    lane = lax.broadcasted_iota(jnp.int32, (C, LANES), 1)
    m0 = lane < RWKV_HEAD
    einv = jnp.exp(-lin)
    ew = jnp.exp(ltot - lin)
    a_d = _dform(-kk * jnp.exp(lex), m0)
    r_d = _dform(r * jnp.exp(lin), m0)
    k_d = _dform(kd * einv, m0)
    b_d = _dform(bb * einv, m0)
    v_d = _dform(v, m0)
    kw_d = _dform(kd * ew, m0)
    bw_d = _dform(bb * ew, m0)

    rd = lax.broadcasted_iota(jnp.int32, (C2, C2), 0)
    cd = lax.broadcasted_iota(jnp.int32, (C2, C2), 1)
    same = (rd // C) == (cd // C)
    ahead = (rd % C - cd % C) * sgn
    strict = same & (ahead > 0)
    incl = same & (ahead >= 0)

    gram = _dot3(jnp.concatenate([a_d, r_d], axis=0), jnp.concatenate([k_d, b_d], axis=0), _dot_nt)
    a_ak = jnp.where(strict, gram[:C2, :C2], 0.0)
    a_ab = jnp.where(strict, gram[:C2, C2:], 0.0)
    a_rk = jnp.where(incl, gram[C2:, :C2], 0.0)
    a_rb = jnp.where(incl, gram[C2:, C2:], 0.0)

    eye = jnp.where(rd == cd, 1.0, 0.0)
    tinv = eye + a_ab
    apow = a_ab
    for _ in range(int(math.log2(C)) - 1):
        apow = _dot3(apow, apow)
        tinv = tinv + _dot3(tinv, apow)

    x = jnp.concatenate([a_d, _dot3(a_ak, v_d)], axis=1)
    p = _dot3(tinv, x)
    p1, p2 = p[:, :LANES], p[:, LANES:]
    ur = _dot3(jnp.concatenate([p1, r_d], axis=0), st)
    u = ur[:C2] + p2
    y2 = ur[C2:] + _dot3(a_rk, v_d) + _dot3(a_rb, u)
    y = y2[:C] + y2[C:]
    wc = jnp.broadcast_to(jnp.exp(ltot), (LANES, LANES))
    wcol = jnp.sum(jnp.where(rd == cd, wc, 0.0), axis=1, keepdims=True)
    st_new = st * wcol + _dot3(jnp.concatenate([bw_d, kw_d], axis=0),
                               jnp.concatenate([u, v_d], axis=0), _dot_tn)
    return y, st_new


def _rwkv_scan_kernel(r_ref, kk_ref, v_ref, lw_ref, kd_ref, bb_ref, s0_ref, y_ref, sT_ref, st_scr,
                      *, npairs):
    d = pl.program_id(0)
    c = pl.program_id(3)
    nc = pl.num_programs(3)

    @pl.when(c == 0)
    def _():
        st_scr[...] = s0_ref[0, 0]

    rev = d == 1
    for j in range(npairs):
        sl = slice(j * LANES, (j + 1) * LANES)
        y, st_new = _rwkv_pair(r_ref[0, :, sl], kk_ref[0, :, sl], v_ref[0, :, sl],
                               lw_ref[0, 0, :, sl], kd_ref[0, 0, :, sl], bb_ref[0, 0, :, sl],
                               st_scr[j], rev)
        y_ref[0, 0, :, sl] = y
        st_scr[j] = st_new

    @pl.when(c == nc - 1)
    def _():
        sT_ref[0, 0] = st_scr[...]


def rwkv_scan(f, s0, npairs=2):
    r, kk, v, lw, kd, bb = f["r"], f["kk"], f["v"], f["lw"], f["kd"], f["bb"]
    B, T, _ = r.shape
    C = RWKV_CHUNK
    nc = T // C
    ng = RWKV_PAIRS // npairs
    W = npairs * LANES
    chunk = lambda d, c: c + d * (nc - 1 - 2 * c)
    tok = pl.BlockSpec((1, C, W), lambda d, b, g, c: (b, chunk(d, c), g))
    tokd = pl.BlockSpec((1, 1, C, W), lambda d, b, g, c: (d, b, chunk(d, c), g))
    stsp = pl.BlockSpec((1, 1, npairs, LANES, LANES), lambda d, b, g, c: (d, b, g, 0, 0))
    return pl.pallas_call(
        functools.partial(_rwkv_scan_kernel, npairs=npairs),
        grid=(2, B, ng, nc),
        in_specs=[tok, tok, tok, tokd, tokd, tokd, stsp],
        out_specs=[tokd, stsp],
        out_shape=[jax.ShapeDtypeStruct((2, B, T, GROUP_W), F32),
                   jax.ShapeDtypeStruct(s0.shape, F32)],
        scratch_shapes=[pltpu.VMEM((npairs, LANES, LANES), F32)],
        compiler_params=_cp("parallel", "parallel", "parallel", "arbitrary"),
        name="rwkv_scan",
    )(r, kk, v, lw, kd, bb, s0)


def _rwkv_readout_kernel(yf_ref, yb_ref, bonus_ref, g_ref, ln_ref, o_ref):
    y = yf_ref[0, 0] + yb_ref[0, 0]
    mean = _head_sum(y, RWKV_HEAD) * (1.0 / RWKV_HEAD)
    cen = y - mean
    var = _head_sum(cen * cen, RWKV_HEAD) * (1.0 / RWKV_HEAD)
    yn = cen * lax.rsqrt(var + RWKV_GN_EPS) * ln_ref[0:1, :] + ln_ref[1:2, :]
    o_ref[0] = ((yn + bonus_ref[0]) * g_ref[0]).astype(o_ref.dtype)


def rwkv_readout(y, bonus, g, ln):
    _, B, T, W = y.shape
    tm = 256
    tok = pl.BlockSpec((1, tm, W), lambda b, i: (b, i, 0))
    return pl.pallas_call(
        _rwkv_readout_kernel,
        grid=(B, T // tm),
        in_specs=[pl.BlockSpec((1, 1, tm, W), lambda b, i: (0, b, i, 0)),
                  pl.BlockSpec((1, 1, tm, W), lambda b, i: (1, b, i, 0)),
                  tok, tok, pl.BlockSpec((2, W), lambda b, i: (0, 0))],
        out_specs=tok,
        out_shape=jax.ShapeDtypeStruct((B, T, W), BF16),
        compiler_params=_cp("parallel", "parallel"),
        name="rwkv_readout",
    )(y, y, bonus, g, ln)


def _pool_kernel(p_ref, pp_ref, pn_ref, w_ref, sc_ref, o_ref, *, total):
    i = pl.program_id(1)
    last = pl.num_programs(1) - 1
    x = p_ref[0]
    tm = x.shape[0]
    n = tm + 2 * POOL_HALO
    ext = jnp.concatenate([jnp.where(i == 0, 0.0, pp_ref[0]), x,
                           jnp.where(i == last, 0.0, pn_ref[0])], axis=0)
    t = i * tm + lax.broadcasted_iota(jnp.int32, (tm, 1), 0)
    outs = []
    for gi, win in enumerate(POOL_WINDOWS):
        sl = slice(gi * POOL_GROUP, (gi + 1) * POOL_GROUP)
        e = ext[:, sl]
        s = e + pltpu.roll(e, 1, axis=0)
        step = 1
        while 2 * step < win:
            s = pltpu.roll(s, step, axis=0) + pltpu.roll(s, n - step, axis=0)
            step *= 2
        h = win // 2
        cnt = (jnp.minimum(t + h, total) - jnp.maximum(t - h, 0)).astype(F32)
        z = s[POOL_HALO:POOL_HALO + tm] / cnt - x[:, sl]
        outs.append(_dot(z, w_ref[gi]))
    o_ref[0] = (jnp.concatenate(outs, axis=1) * sc_ref[...]).astype(o_ref.dtype)


def pool_mixer(p, pool_w, pool_scale):
    B, T, W = p.shape
    tm = 256
    prev_map, next_map = _halo_specs(tm, W)
    return pl.pallas_call(
        functools.partial(_pool_kernel, total=T),
        grid=(B, T // tm),
        in_specs=[pl.BlockSpec((1, tm, W), lambda b, i: (b, i, 0)),
                  pl.BlockSpec((1, POOL_HALO, W), prev_map),
                  pl.BlockSpec((1, POOL_HALO, W), next_map(T // POOL_HALO)),
                  pl.BlockSpec(pool_w.shape, lambda b, i: (0, 0, 0)),
                  pl.BlockSpec((1, W), lambda b, i: (0, 0))],
        out_specs=pl.BlockSpec((1, tm, W), lambda b, i: (b, i, 0)),
        out_shape=jax.ShapeDtypeStruct((B, T, W), BF16),
        compiler_params=_cp("parallel", "parallel"),
        name="pool_mixer",
    )(p, p, p, pool_w, pool_scale.reshape(1, W))


def _mlstm_prep_kernel(p_ref, pp_ref, pn_ref, g_ref, cw_ref, gb_ref, *rest, rope):
    if rope:
        cos_ref, sin_ref, q_ref, k_ref, go_ref = rest
    else:
        q_ref, k_ref, go_ref = rest
    i = pl.program_id(1)
    last = pl.num_programs(1) - 1
    x = p_ref[0]
    prev_row = jnp.where(i == 0, 0.0, pp_ref[0, SUBLANES - 1:SUBLANES, :])
    next_row = jnp.where(i == last, 0.0, pn_ref[0, 0:1, :])
    prev, nxt = _shift_rows(x, prev_row, next_row)
    qk = _silu(cw_ref[0:1, :] * prev + cw_ref[1:2, :] * x + cw_ref[2:3, :] * nxt)
    if rope:
        w = qk.shape[1]
        lane = lax.broadcasted_iota(jnp.int32, qk.shape, 1)
        partner = jnp.where(lane % 2 == 0, pltpu.roll(qk, w - 1, axis=1), pltpu.roll(qk, 1, axis=1))
        reps = w // LANES
        cos = jnp.concatenate([cos_ref[...]] * reps, axis=1)
        sin = jnp.concatenate([sin_ref[...]] * reps, axis=1)
        qk = qk * cos + partner * sin
    half = MLSTM_HEADS * MLSTM_DK
    q_ref[0] = qk[:, :half].astype(BF16)
    k_ref[0] = (qk[:, half:] * (MLSTM_DK ** -0.5)).astype(BF16)
    g = g_ref[0] + gb_ref[...]
    lane = lax.broadcasted_iota(jnp.int32, g.shape, 1)
    go_ref[0] = jnp.where((lane // MLSTM_HEADS) % 2 == 1, _log_sigmoid(g), g)


def mlstm_prep(p, conv_w, gate_b, rope_tabs):
    B, T, W = p.shape
    tm = 256
    QK = 2 * MLSTM_HEADS * MLSTM_DK
    prev_map, next_map = _halo_specs(tm, QK)
    rope = rope_tabs is not None
    in_specs = [pl.BlockSpec((1, tm, QK), lambda b, i: (b, i, 0)),
                pl.BlockSpec((1, SUBLANES, QK), prev_map),
                pl.BlockSpec((1, SUBLANES, QK), next_map(T // SUBLANES)),
                pl.BlockSpec((1, tm, LANES), lambda b, i: (b, i, (W - LANES) // LANES)),
                pl.BlockSpec((3, QK), lambda b, i: (0, 0)),
                pl.BlockSpec((1, LANES), lambda b, i: (0, 0))]
    args = [p, p, p, p, conv_w, gate_b]
    if rope:
        in_specs += [pl.BlockSpec((tm, LANES), lambda b, i: (i, 0))] * 2
        args += list(rope_tabs)
    half = MLSTM_HEADS * MLSTM_DK
    return pl.pallas_call(
        functools.partial(_mlstm_prep_kernel, rope=rope),
        grid=(B, T // tm),
        in_specs=in_specs,
        out_specs=[pl.BlockSpec((1, tm, half), lambda b, i: (b, i, 0)),
                   pl.BlockSpec((1, tm, half), lambda b, i: (b, i, 0)),
                   pl.BlockSpec((1, tm, LANES), lambda b, i: (b, i, 0))],
        out_shape=[jax.ShapeDtypeStruct((B, T, half), BF16),
                   jax.ShapeDtypeStruct((B, T, half), BF16),
                   jax.ShapeDtypeStruct((B, T, LANES), F32)],
        compiler_params=_cp("parallel", "parallel"),
        name="mlstm_prep",
    )(*args)


def _mlstm_chunk_kernel(q_ref, k_ref, v_ref, gc_ref, gr_ref, c0_ref, n0_ref, m0_ref,
                        h_ref, cT_ref, nT_ref, mT_ref, c_scr, n_scr, m_scr):
    d = pl.program_id(0)
    hh = pl.program_id(2)
    c = pl.program_id(3)
    nc = pl.num_programs(3)

    @pl.when(c == 0)
    def _():
        c_scr[...] = c0_ref[0, 0, 0]
        n_scr[...] = n0_ref[0, 0, 0]
        m_scr[...] = m0_ref[0, 0, 0]

    rev = d == 1
    q = q_ref[0]
    k = k_ref[0]
    v = v_ref[0]
    L = q.shape[0]
    rr = lax.broadcasted_iota(jnp.int32, (L, L), 0)
    cc = lax.broadcasted_iota(jnp.int32, (L, L), 1)
    sgn = jnp.where(rev, -1, 1)
    seen = (rr - cc) * sgn >= 0
    tri = jnp.where(seen, 1.0, 0.0).astype(BF16)
    tri_t = jnp.where((cc - rr) * sgn >= 0, 1.0, 0.0).astype(BF16)

    gc = gc_ref[0]
    gr = gr_ref[0]
    lane = lax.broadcasted_iota(jnp.int32, gc.shape, 1)
    subl = lax.broadcasted_iota(jnp.int32, gr.shape, 0)
    i_lane = d * (2 * MLSTM_HEADS) + hh
    f_lane = i_lane + MLSTM_HEADS
    pick_c = lambda a, idx: jnp.sum(jnp.where(lane == idx, a, 0.0), axis=1, keepdims=True)
    pick_r = lambda a, idx: jnp.sum(jnp.where(subl == idx, a, 0.0), axis=0, keepdims=True)
    ig_c = pick_c(gc, i_lane)
    lf_c = pick_c(gc, f_lane)
    b_c = pick_c(_dot_exact_lhs(tri, gc), f_lane)
    ig_r = pick_r(gr, i_lane)
    gr_hi = gr.astype(BF16)
    gr_r1 = gr - gr_hi.astype(F32)
    gr_mid = gr_r1.astype(BF16)
    gr_lo = (gr_r1 - gr_mid.astype(F32)).astype(BF16)
    b_r = pick_r(_dot(gr_hi, tri_t) + (_dot(gr_mid, tri_t) + _dot(gr_lo, tri_t)), f_lane)

    m_prev = m_scr[...]
    dlog = jnp.where(seen, b_c - b_r + ig_r, NEG)
    inter = b_c + m_prev
    m_t = jnp.maximum(inter, jnp.max(dlog, axis=1, keepdims=True))
    dmat = jnp.exp(dlog - m_t)
    w_inter = jnp.exp(inter - m_t)
    s = _dot_nt(q, k) * dmat
    cst = c_scr[...]
    nst = n_scr[...]
    num = _dot(s, v) + w_inter * _dot(q, cst)
    den = jnp.sum(s, axis=1, keepdims=True) + w_inter * jnp.sum(q.astype(F32) * nst, axis=1, keepdims=True)
    h_ref[0, 0] = num / jnp.maximum(jnp.abs(den), jnp.exp(-m_t))

    b_end = jnp.sum(lf_c, axis=0, keepdims=True)
    g_s = b_end - b_c + ig_c
    m_new = jnp.maximum(b_end + m_prev, jnp.max(g_s, axis=0, keepdims=True))
    kw = k.astype(F32) * jnp.exp(g_s - m_new)
    decay = jnp.exp(b_end + m_prev - m_new)
    c_scr[...] = decay * cst + _dot_tn(kw, v)
    n_scr[...] = decay * nst + jnp.sum(kw, axis=0, keepdims=True)
    m_scr[...] = m_new

    @pl.when(c == nc - 1)
    def _():
        cT_ref[0, 0, 0] = c_scr[...]
        nT_ref[0, 0, 0] = n_scr[...]
        mT_ref[0, 0, 0] = m_scr[...]


def mlstm_scan(q, k, p, gates, gates_t, state):
    B, T, _ = q.shape
    L = min(MLSTM_CHUNK, T)
    nc = T // L
    H = MLSTM_HEADS
    c0, n0, m0 = state
    chunk = lambda d, c: c + d * (nc - 1 - 2 * c)
    v_off = (2 * H * MLSTM_DK) // MLSTM_DV
    csp = pl.BlockSpec((1, 1, 1, MLSTM_DK, MLSTM_DV), lambda d, b, h, c: (d, b, h, 0, 0))
    nsp = pl.BlockSpec((1, 1, 1, 1, MLSTM_DK), lambda d, b, h, c: (d, b, h, 0, 0))
    msp = pl.BlockSpec((1, 1, 1, 1, 1), lambda d, b, h, c: (d, b, h, 0, 0))
    return pl.pallas_call(
        _mlstm_chunk_kernel,
        grid=(2, B, H, nc),
        in_specs=[pl.BlockSpec((1, L, MLSTM_DK), lambda d, b, h, c: (b, chunk(d, c), h)),
                  pl.BlockSpec((1, L, MLSTM_DK), lambda d, b, h, c: (b, chunk(d, c), h)),
                  pl.BlockSpec((1, L, MLSTM_DV), lambda d, b, h, c: (b, chunk(d, c), v_off + h)),
                  pl.BlockSpec((1, L, LANES), lambda d, b, h, c: (b, chunk(d, c), 0)),
                  pl.BlockSpec((1, 4 * H, L), lambda d, b, h, c: (b, 0, chunk(d, c))),
                  csp, nsp, msp],
        out_specs=[pl.BlockSpec((1, 1, L, MLSTM_DV), lambda d, b, h, c: (d, b, chunk(d, c), h)),
                   csp, nsp, msp],
        out_shape=[jax.ShapeDtypeStruct((2, B, T, GROUP_W), F32),
                   jax.ShapeDtypeStruct(c0.shape, F32),
                   jax.ShapeDtypeStruct(n0.shape, F32),
                   jax.ShapeDtypeStruct(m0.shape, F32)],
        scratch_shapes=[pltpu.VMEM((MLSTM_DK, MLSTM_DV), F32),
                        pltpu.VMEM((1, MLSTM_DK), F32),
                        pltpu.VMEM((1, 1), F32)],
        compiler_params=_cp("parallel", "parallel", "parallel", "arbitrary"),
        name="mlstm_scan",
    )(q, k, p, gates, gates_t, c0, n0, m0)


def _mlstm_readout_kernel(hf_ref, hb_ref, o_ref, nw_ref, out_ref):
    h = hf_ref[0, 0] + hb_ref[0, 0]
    outs = []
    for j in range(MLSTM_HEADS):
        hj = h[:, j * MLSTM_DV:(j + 1) * MLSTM_DV]
        outs.append(hj * lax.rsqrt(jnp.mean(hj * hj, axis=-1, keepdims=True) + EPS))
    hn = jnp.concatenate(outs, axis=1)
    out_ref[0] = (hn * nw_ref[...] * _sigmoid(o_ref[0])).astype(out_ref.dtype)


def mlstm_readout(h, p, norm_w):
    _, B, T, W = h.shape
    tm = 256
    o_blk = (2 * MLSTM_HEADS * MLSTM_DK + GROUP_W) // GROUP_W
    return pl.pallas_call(
        _mlstm_readout_kernel,
        grid=(B, T // tm),
        in_specs=[pl.BlockSpec((1, 1, tm, W), lambda b, i: (0, b, i, 0)),
                  pl.BlockSpec((1, 1, tm, W), lambda b, i: (1, b, i, 0)),
                  pl.BlockSpec((1, tm, W), lambda b, i: (b, i, o_blk)),
                  pl.BlockSpec((1, W), lambda b, i: (0, 0))],
        out_specs=pl.BlockSpec((1, tm, W), lambda b, i: (b, i, 0)),
        out_shape=jax.ShapeDtypeStruct((B, T, W), BF16),
        compiler_params=_cp("parallel", "parallel"),
        name="mlstm_readout",
    )(h, h, p, norm_w.reshape(1, W))


def _na_prep_kernel(p_ref, nw_ref, q_ref, k_ref, v_ref):
    x = p_ref[0]
    for j in range(NA_HEADS):
        sl = slice(j * NA_HEAD, (j + 1) * NA_HEAD)
        for src, dst, row, scale in ((0, q_ref, 0, NA_HEAD ** -0.5), (GROUP_W, k_ref, 1, 1.0)):
            z = x[:, src + j * NA_HEAD:src + (j + 1) * NA_HEAD]
            zn = z * lax.rsqrt(jnp.mean(z * z, axis=-1, keepdims=True) + EPS) * nw_ref[row:row + 1, :]
            dst[0, :, sl] = (zn * scale).astype(BF16)
    v_ref[0] = x[:, 2 * GROUP_W:].astype(BF16)


def na_prep(p, qk_norm):
    B, T, W = p.shape
    tm = 256
    tok = pl.BlockSpec((1, tm, GROUP_W), lambda b, i: (b, i, 0))
    sh = jax.ShapeDtypeStruct((B, T, GROUP_W), BF16)
    return pl.pallas_call(
        _na_prep_kernel,
        grid=(B, T // tm),
        in_specs=[pl.BlockSpec((1, tm, W), lambda b, i: (b, i, 0)),
                  pl.BlockSpec((2, NA_HEAD), lambda b, i: (0, 0))],
        out_specs=[tok, tok, tok],
        out_shape=[sh, sh, sh],
        compiler_params=_cp("parallel", "parallel"),
        name="na_prep",
    )(p, qk_norm)


def _softmax_rows(s):
    m = jnp.max(s, axis=1, keepdims=True)
    e = jnp.exp(s - m)
    return e / jnp.sum(e, axis=1, keepdims=True)


def _na_ctx_kernel(q_ref, k_ref, v_ref, o_ref):
    for j in range(NA_HEADS):
        sl = slice(j * NA_HEAD, (j + 1) * NA_HEAD)
        pr = _softmax_rows(_dot_nt(q_ref[0, :, sl], k_ref[0, :, sl]))
        o_ref[0, :, sl] = _dot(pr, v_ref[0, :, sl]).astype(o_ref.dtype)


def na_ctx_attention(q, k, v):
    B, T, W = q.shape
    tok = pl.BlockSpec((1, T, W), lambda b: (b, 0, 0))
    return pl.pallas_call(
        _na_ctx_kernel,
        grid=(B,),
        in_specs=[tok, tok, tok],
        out_specs=tok,
        out_shape=jax.ShapeDtypeStruct((B, T, W), BF16),
        compiler_params=_cp("parallel"),
        name="na_ctx_attention",
    )(q, k, v)


def _na_lat_kernel(q_ref, k0_ref, k1_ref, k2_ref, v0_ref, v1_ref, v2_ref, kc_ref, vc_ref, bias_ref, o_ref):
    for j in range(NA_HEADS):
        sl = slice(j * NA_HEAD, (j + 1) * NA_HEAD)
        q = q_ref[0, :, sl]
        kcat = jnp.concatenate([k0_ref[0, :, sl], k1_ref[0, :, sl], k2_ref[0, :, sl]], axis=0)
        vcat = jnp.concatenate([v0_ref[0, :, sl], v1_ref[0, :, sl], v2_ref[0, :, sl],
                                vc_ref[0, :, sl]], axis=0)
        s = jnp.concatenate([_dot_nt(q, kcat) + bias_ref[0, j], _dot_nt(q, kc_ref[0, :, sl])], axis=1)
        o_ref[0, :, sl] = _dot(_softmax_rows(s), vcat).astype(o_ref.dtype)


def na_bias_table(rpb, rows):
    nq, nk = NA_QROWS, NA_KROWS
    starts = np.array([0, nq, rows - nq])
    r = starts[:, None] + np.arange(nq)[None, :]
    kr = starts[:, None] - nq + np.arange(nk)[None, :]
    r0 = np.clip(r - WIN_H // 2, 0, rows - WIN_H)
    rv = (kr[:, None, :] >= r0[:, :, None]) & (kr[:, None, :] < r0[:, :, None] + WIN_H)
    dr = np.clip(kr[:, None, :] - r[:, :, None] + WIN_H - 1, 0, 2 * WIN_H - 2)
    col = np.arange(GRID_W)
    c0 = np.clip(col - WIN_W // 2, 0, GRID_W - WIN_W)
    cv = (col[None, :] >= c0[:, None]) & (col[None, :] < c0[:, None] + WIN_W)
    dc = np.clip(col[None, :] - col[:, None] + WIN_W - 1, 0, 2 * WIN_W - 2)
    dr_i = np.broadcast_to(dr[:, :, None, :, None], (3, nq, GRID_W, nk, GRID_W))
    dc_i = np.broadcast_to(dc[None, None, :, None, :], (3, nq, GRID_W, nk, GRID_W))
    valid = rv[:, :, None, :, None] & cv[None, None, :, None, :]
    tab = rpb[:, dr_i, dc_i]
    tab = jnp.where(valid[None], tab, NEG)
    return jnp.transpose(tab, (1, 0, 2, 3, 4, 5)).reshape(3, NA_HEADS, nq * GRID_W, nk * GRID_W)


def na_lat_attention(q, k, v, kc, vc, bias):
    B, T, W = q.shape
    tq = NA_QROWS * GRID_W
    nb = T // tq
    ctx_len = kc.shape[1]
    qsp = pl.BlockSpec((1, tq, W), lambda b, i: (b, i, 0))
    prv = pl.BlockSpec((1, tq, W), lambda b, i: (b, jnp.maximum(i - 1, 0), 0))
    nxt = pl.BlockSpec((1, tq, W), lambda b, i: (b, jnp.minimum(i + 1, nb - 1), 0))
    csp = pl.BlockSpec((1, ctx_len, W), lambda b, i: (b, 0, 0))
    pattern = lambda i: jnp.where(i == 0, 0, jnp.where(i == nb - 1, 2, 1))
    return pl.pallas_call(
        _na_lat_kernel,
        grid=(B, nb),
        in_specs=[qsp, prv, qsp, nxt, prv, qsp, nxt, csp, csp,
                  pl.BlockSpec((1, NA_HEADS, tq, NA_KROWS * GRID_W), lambda b, i: (pattern(i), 0, 0, 0))],
        out_specs=qsp,
        out_shape=jax.ShapeDtypeStruct((B, T, W), BF16),
        compiler_params=_cp("parallel", "arbitrary"),
        name="na_lat_attention",
    )(q, k, k, k, v, v, v, kc, vc, bias)


def _ffn_norm_router_kernel(h_ref, g_ref, sh_ref, sc_ref, rw_ref, rb_ref, v_ref, ids_ref, wts_ref):
    x = h_ref[0]
    y = x * lax.rsqrt(jnp.mean(x * x, axis=-1, keepdims=True) + EPS) * g_ref[...]
    v = y * (1.0 + sc_ref[0]) + sh_ref[0]
    v_ref[0] = v.astype(BF16)
    vh, vl = _split2(v)
    logits = (_dot(vh, rw_ref[0]) + (_dot(vh, rw_ref[1]) + _dot(vl, rw_ref[0]))) + rb_ref[...]
    lane = lax.broadcasted_iota(jnp.int32, logits.shape, 1).astype(F32)
    first = lambda mask: jnp.min(jnp.where(mask, lane, float(LANES)), axis=1, keepdims=True)

    g_mask = lane < N_GROUPS
    gl = jnp.where(g_mask, logits, NEG)
    gmax = jnp.max(gl, axis=1, keepdims=True)
    g_p = 1.0 / jnp.sum(jnp.exp(gl - gmax), axis=1, keepdims=True)
    g_idx = first(g_mask & (gl == gmax))
    e_lane = lane - N_GROUPS
    e_mask = (e_lane >= 0) & (e_lane < N_EXPERTS) & (jnp.floor(e_lane * (1.0 / EXPERTS_PER_GROUP)) == g_idx)
    el = jnp.where(e_mask, logits, NEG)
    e1 = jnp.max(el, axis=1, keepdims=True)
    i1 = first(e_mask & (el == e1))
    el2 = jnp.where(lane == i1, NEG, el)
    e2 = jnp.max(el2, axis=1, keepdims=True)
    i2 = first(e_mask & (lane != i1) & (el2 == e2))
    x2 = jnp.exp(e2 - e1)
    w1 = g_p / (1.0 + x2)
    w2 = g_p * x2 / (1.0 + x2)
    ids_ref[0] = jnp.where(lane == 0, i1 - N_GROUPS, jnp.where(lane == 1, i2 - N_GROUPS, 0.0)).astype(jnp.int32)
    wts_ref[0] = jnp.where(lane == 0, w1, jnp.where(lane == 1, w2, 0.0))


def ffn_norm_router(h, g, shift, scale, rw, rb):
    B, T, D = h.shape
    tm = 256
    vec = pl.BlockSpec((1, 1, D), lambda b, i: (b, 0, 0))
    lan = pl.BlockSpec((1, tm, LANES), lambda b, i: (b, i, 0))
    return pl.pallas_call(
        _ffn_norm_router_kernel,
        grid=(B, T // tm),
        in_specs=[pl.BlockSpec((1, tm, D), lambda b, i: (b, i, 0)),
                  pl.BlockSpec((1, D), lambda b, i: (0, 0)), vec, vec,
                  pl.BlockSpec((2, D, LANES), lambda b, i: (0, 0, 0)),
                  pl.BlockSpec((1, LANES), lambda b, i: (0, 0))],
        out_specs=[pl.BlockSpec((1, tm, D), lambda b, i: (b, i, 0)), lan, lan],
        out_shape=[jax.ShapeDtypeStruct((B, T, D), BF16),
                   jax.ShapeDtypeStruct((B, T, LANES), jnp.int32),
                   jax.ShapeDtypeStruct((B, T, LANES), F32)],
        compiler_params=_cp("parallel", "parallel"),
        name="ffn_norm_router",
    )(h, g.reshape(1, D), shift, scale, rw, rb)


def _moe_kernel(be_ref, x_ref, wg_ref, wu_ref, wd_ref, sw_ref, o_ref):
    f = pl.program_id(1)
    x = x_ref[...]
    hg = jnp.dot(x, wg_ref[0], preferred_element_type=F32)
    hu = jnp.dot(x, wu_ref[0], preferred_element_type=F32)
    y = _dot(_silu(hg) * hu, wd_ref[0]) * sw_ref[...]

    @pl.when(f == 0)
    def _():
        o_ref[...] = y

    @pl.when(f > 0)
    def _():
        o_ref[...] += y


def moe_experts(xb, block_expert, slot_w, wg, wu, wd):
    P, D = xb.shape
    nb = P // MOE_BLOCK
    nf = EXPERT_FF // MOE_FF_TILE
    grid_spec = pltpu.PrefetchScalarGridSpec(
        num_scalar_prefetch=1,
        grid=(nb, nf),
        in_specs=[pl.BlockSpec((MOE_BLOCK, D), lambda i, f, be: (i, 0)),
                  pl.BlockSpec((1, D, MOE_FF_TILE), lambda i, f, be: (be[i], 0, f)),
                  pl.BlockSpec((1, D, MOE_FF_TILE), lambda i, f, be: (be[i], 0, f)),
                  pl.BlockSpec((1, MOE_FF_TILE, D), lambda i, f, be: (be[i], f, 0)),
                  pl.BlockSpec((MOE_BLOCK, 1), lambda i, f, be: (i, 0))],
        out_specs=pl.BlockSpec((MOE_BLOCK, D), lambda i, f, be: (i, 0)),
    )
    return pl.pallas_call(
        _moe_kernel,
        grid_spec=grid_spec,
        out_shape=jax.ShapeDtypeStruct((P, D), F32),
        compiler_params=_cp("parallel", "arbitrary"),
        name="moe_experts",
    )(block_expert, xb, wg, wu, wd, slot_w.reshape(P, 1))


def _residual_kernel(h_ref, f_ref, gate_ref, o_ref):
    o_ref[0] = h_ref[0] + gate_ref[0] * f_ref[0]


def gated_residual(h, f, gate):
    B, T, D = h.shape
    tm = 256
    tok = pl.BlockSpec((1, tm, D), lambda b, i: (b, i, 0))
    return pl.pallas_call(
        _residual_kernel,
        grid=(B, T // tm),
        in_specs=[tok, tok, pl.BlockSpec((1, 1, D), lambda b, i: (b, 0, 0))],
        out_specs=tok,
        out_shape=jax.ShapeDtypeStruct((B, T, D), F32),
        compiler_params=_cp("parallel", "parallel"),
        name="gated_residual",
    )(h, f, gate)


def moe_dispatch(ids, wts):
    N = ids.shape[0]
    A = N * TOP_K
    e_flat = ids.reshape(A)
    order = jnp.argsort(e_flat)
    e_sorted = e_flat[order]
    counts = jnp.bincount(e_flat, length=N_EXPERTS)
    padded = (counts + MOE_BLOCK - 1) // MOE_BLOCK * MOE_BLOCK
    ends = jnp.cumsum(padded)
    slot = (ends - padded)[e_sorted] + jnp.arange(A) - (jnp.cumsum(counts) - counts)[e_sorted]
    n_blocks = -(-(A + N_EXPERTS * (MOE_BLOCK - 1)) // MOE_BLOCK)
    P = n_blocks * MOE_BLOCK
    slot_tok = jnp.zeros((P,), jnp.int32).at[slot].set((order // TOP_K).astype(jnp.int32))
    slot_w = jnp.zeros((P,), F32).at[slot].set(wts.reshape(A)[order])
    slot_of = jnp.zeros((A,), jnp.int32).at[order].set(slot.astype(jnp.int32)).reshape(N, TOP_K)
    block_expert = jnp.minimum(jnp.searchsorted(ends, jnp.arange(n_blocks) * MOE_BLOCK, side="right"),
                               N_EXPERTS - 1).astype(jnp.int32)
    return slot_tok, slot_w, slot_of, block_expert


def hier_moe(v_rows, ids, wts, wg, wu, wd):
    slot_tok, slot_w, slot_of, block_expert = moe_dispatch(ids, wts)
    xb = jnp.take(v_rows, slot_tok, axis=0)
    yb = moe_experts(xb, block_expert, slot_w, wg, wu, wd)
    return jnp.take(yb, slot_of[:, 0], axis=0) + jnp.take(yb, slot_of[:, 1], axis=0)


def _layer_weights(l, P):
    w = {}
    w_in = P["w_in"][l]
    o = 0
    a = w_in[:, o:o + RWKV_COLS]; o += RWKV_COLS
    b = w_in[:, o:o + GROUP_W]; o += GROUP_W
    c = w_in[:, o:o + MLSTM_COLS]; o += MLSTM_COLS
    d = w_in[:, o:o + NA_COLS]
    D = w_in.shape[0]
    pad_a = RWKV_PAD_COLS - RWKV_COLS
    w["in_a"] = jnp.concatenate([a, jnp.zeros((D, pad_a), F32)], axis=1).astype(BF16)
    w["in_b"] = b.astype(BF16)
    w["in_c"] = jnp.concatenate([c, jnp.zeros((D, MLSTM_PAD_COLS - MLSTM_COLS), F32)], axis=1).astype(BF16)
    w["in_d"] = d.astype(BF16)
    w["out"] = P["w_out"][l].astype(BF16)

    z = jnp.zeros((RWKV_DECAY_RANK, GROUP_W), F32)
    wup, aup = P["rwkv_w_up"][l], P["rwkv_a_up"][l]
    rw = {
        "mu": jnp.concatenate([P["rwkv_mu"][l], jnp.zeros((2, pad_a), F32)], axis=1),
        "w_up": jnp.concatenate([jnp.concatenate([wup[0], z], axis=1),
                                 jnp.concatenate([z, wup[1]], axis=1)], axis=0).astype(BF16),
        "a_up": jnp.concatenate([jnp.concatenate([aup[0], z], axis=1),
                                 jnp.concatenate([z, aup[1]], axis=1)], axis=0).astype(BF16),
        "w0": P["rwkv_w0"][l].reshape(1, 2 * GROUP_W),
        "a0": P["rwkv_a0"][l].reshape(1, 2 * GROUP_W),
        "g_up": jnp.concatenate([P["rwkv_g_up"][l],
                                 jnp.zeros((256 - RWKV_GATE_RANK, GROUP_W), F32)], axis=0).astype(BF16),
        "k_k": P["rwkv_k_k"][l].reshape(1, GROUP_W),
        "k_a": P["rwkv_k_a"][l].reshape(1, GROUP_W),
        "r_k": P["rwkv_r_k"][l].reshape(1, GROUP_W),
    }
    w["rwkv"] = rw
    w["rwkv_ln"] = P["rwkv_ln"][l]
    w["pool_w"] = P["pool_w"][l].astype(BF16)
    w["pool_scale"] = P["pool_scale"][l]
    w["mlstm_conv"] = P["mlstm_conv"][l]
    gb = P["mlstm_gate_b"][l].reshape(1, 4 * MLSTM_HEADS)
    w["mlstm_gate_b"] = jnp.concatenate([gb, jnp.zeros((1, LANES - 4 * MLSTM_HEADS), F32)], axis=1)
    w["mlstm_norm"] = P["mlstm_norm"][l]
    w["na_qk_norm"] = P["na_qk_norm"][l]
    w["na_rpb"] = P["na_rpb"][l]
    rcat = jnp.concatenate([P["router_g_w"][l], P["router_e_w"][l],
                            jnp.zeros((D, LANES - N_GROUPS - N_EXPERTS), F32)], axis=1)
    rhi = rcat.astype(BF16)
    w["router_w"] = jnp.stack([rhi, (rcat - rhi.astype(F32)).astype(BF16)])
    w["router_b"] = jnp.concatenate([P["router_g_b"][l], P["router_e_b"][l],
                                     jnp.zeros((LANES - N_GROUPS - N_EXPERTS,), F32)]).reshape(1, LANES)
    w["moe_gate"] = P["moe_w_gate"][l].astype(BF16)
    w["moe_up"] = P["moe_w_up"][l].astype(BF16)
    w["moe_down"] = P["moe_w_down"][l].astype(BF16)
    return w


def rope_tables(T):
    t = jnp.arange(T)
    row = (t // GRID_W).astype(F32)
    col = (t % GRID_W).astype(F32)
    n_pairs = MLSTM_DK // 4
    inv = ROPE_BASE ** (-jnp.arange(n_pairs, dtype=F32) / n_pairs)
    ang = jnp.concatenate([row[:, None] * inv, col[:, None] * inv], axis=-1)
    cos = jnp.repeat(jnp.cos(ang), 2, axis=1)
    sin = jnp.repeat(jnp.sin(ang), 2, axis=1) * jnp.tile(jnp.array([-1.0, 1.0], F32), MLSTM_DK // 2)
    return cos, sin


def rwkv_mixer(p_ctx, p_lat, w, need_ctx):
    B = p_lat.shape[0]
    names = ("r", "kk", "v", "bonus", "g", "lw", "kd", "bb")
    fc = dict(zip(names, rwkv_prep(p_ctx, w["rwkv"])))
    fl = dict(zip(names, rwkv_prep(p_lat, w["rwkv"])))
    s0 = jnp.zeros((2, B, RWKV_PAIRS, LANES, LANES), F32)
    y_c, s_ctx = rwkv_scan(fc, s0)
    y_l, _ = rwkv_scan(fl, s_ctx)
    out_l = rwkv_readout(y_l, fl["bonus"], fl["g"], w["rwkv_ln"])
    out_c = rwkv_readout(y_c, fc["bonus"], fc["g"], w["rwkv_ln"]) if need_ctx else None
    return out_c, out_l


def mlstm_mixer(p_ctx, p_lat, w, rope, need_ctx):
    B = p_lat.shape[0]
    H = MLSTM_HEADS
    state = (jnp.zeros((2, B, H, MLSTM_DK, MLSTM_DV), F32),
             jnp.zeros((2, B, H, 1, MLSTM_DK), F32),
             jnp.zeros((2, B, H, 1, 1), F32))
    outs = []
    for p, tabs in ((p_ctx, None), (p_lat, rope)):
        q, k, gates = mlstm_prep(p, w["mlstm_conv"], w["mlstm_gate_b"], tabs)
        gates_t = jnp.swapaxes(gates[:, :, :4 * H], 1, 2)
        h, *state = mlstm_scan(q, k, p, gates, gates_t, tuple(state))
        outs.append(h)
    out_l = mlstm_readout(outs[1], p_lat, w["mlstm_norm"])
    out_c = mlstm_readout(outs[0], p_ctx, w["mlstm_norm"]) if need_ctx else None
    return out_c, out_l


def na_mixer(p_ctx, p_lat, w, need_ctx):
    qc, kc, vc = na_prep(p_ctx, w["na_qk_norm"])
    ql, kl, vl = na_prep(p_lat, w["na_qk_norm"])
    rows = p_lat.shape[1] // GRID_W
    bias = na_bias_table(w["na_rpb"], rows)
    out_l = na_lat_attention(ql, kl, vl, kc, vc, bias)
    out_c = na_ctx_attention(qc, kc, vc) if need_ctx else None
    return out_c, out_l


def token_mixers(u_ctx, u_lat, w, rope, need_ctx):
    outs_c, outs_l = [], []
    pa = (matmul(u_ctx, w["in_a"]), matmul(u_lat, w["in_a"]))
    oc, ol = rwkv_mixer(pa[0], pa[1], w, need_ctx)
    outs_c.append(oc); outs_l.append(ol)
    pb = (matmul(u_ctx, w["in_b"]), matmul(u_lat, w["in_b"]))
    outs_l.append(pool_mixer(pb[1], w["pool_w"], w["pool_scale"]))
    outs_c.append(pool_mixer(pb[0], w["pool_w"], w["pool_scale"]) if need_ctx else None)
    pc = (matmul(u_ctx, w["in_c"]), matmul(u_lat, w["in_c"]))
    oc, ol = mlstm_mixer(pc[0], pc[1], w, rope, need_ctx)
    outs_c.append(oc); outs_l.append(ol)
    pd = (matmul(u_ctx, w["in_d"]), matmul(u_lat, w["in_d"]))
    oc, ol = na_mixer(pd[0], pd[1], w, need_ctx)
    outs_c.append(oc); outs_l.append(ol)
    return outs_c, outs_l


def kernel(x, c, ctx, c_ctx, ada_w, ada_b, norm_mix, norm_ffn, w_in, w_out, rwkv_mu, rwkv_w0, rwkv_w_up, rwkv_a0, rwkv_a_up, rwkv_g_up, rwkv_k_k, rwkv_k_a, rwkv_r_k, rwkv_ln, pool_w, pool_scale, mlstm_conv, mlstm_gate_b, mlstm_norm, na_qk_norm, na_rpb, router_g_w, router_g_b, router_e_w, router_e_b, moe_w_gate, moe_w_up, moe_w_down):
    P = dict(w_in=w_in, w_out=w_out, rwkv_mu=rwkv_mu, rwkv_w0=rwkv_w0, rwkv_w_up=rwkv_w_up,
             rwkv_a0=rwkv_a0, rwkv_a_up=rwkv_a_up, rwkv_g_up=rwkv_g_up, rwkv_k_k=rwkv_k_k,
             rwkv_k_a=rwkv_k_a, rwkv_r_k=rwkv_r_k, rwkv_ln=rwkv_ln, pool_w=pool_w, pool_scale=pool_scale,
             mlstm_conv=mlstm_conv, mlstm_gate_b=mlstm_gate_b, mlstm_norm=mlstm_norm,
             na_qk_norm=na_qk_norm, na_rpb=na_rpb, router_g_w=router_g_w, router_g_b=router_g_b,
             router_e_w=router_e_w, router_e_b=router_e_b, moe_w_gate=moe_w_gate, moe_w_up=moe_w_up,
             moe_w_down=moe_w_down)
    B, T, D = x.shape
    Lc = ctx.shape[1]
    depth = ada_w.shape[0]
    rope = rope_tables(T)

    cvec = jnp.concatenate([c, c_ctx[None, :], jnp.zeros((SUBLANES - B - 1, D), F32)], axis=0)
    mods = ada_modulation(cvec, ada_w, ada_b)

    h_lat, h_ctx = x, ctx
    for l in range(depth):
        last = l == depth - 1
        w = _layer_weights(l, P)
        m = mods[l].reshape(SUBLANES, 6, D)
        m_lat = [m[:B, i][:, None, :] for i in range(6)]
        m_ctx = [jnp.broadcast_to(m[B, i][None, None, :], (B, 1, D)) for i in range(6)]

        u_lat = norm_modulate(h_lat, norm_mix[l], m_lat[0], m_lat[1])
        u_ctx = norm_modulate(h_ctx, norm_mix[l], m_ctx[0], m_ctx[1])
        ys_ctx, ys_lat = token_mixers(u_ctx, u_lat, w, rope, not last)
        h_lat = out_proj_residual(ys_lat, w["out"], h_lat, m_lat[2])
        v_lat, ids_l, wts_l = ffn_norm_router(h_lat, norm_ffn[l], m_lat[3], m_lat[4],
                                               w["router_w"], w["router_b"])
        if last:
            f = hier_moe(v_lat.reshape(B * T, D), ids_l.reshape(B * T, LANES)[:, :TOP_K],
                         wts_l.reshape(B * T, LANES)[:, :TOP_K], w["moe_gate"], w["moe_up"], w["moe_down"])
            f_lat = f.reshape(B, T, D)
        else:
            h_ctx = out_proj_residual(ys_ctx, w["out"], h_ctx, m_ctx[2])
            v_ctx, ids_c, wts_c = ffn_norm_router(h_ctx, norm_ffn[l], m_ctx[3], m_ctx[4],
                                                   w["router_w"], w["router_b"])
            rows = jnp.concatenate([v_ctx.reshape(B * Lc, D), v_lat.reshape(B * T, D)], axis=0)
            ids = jnp.concatenate([ids_c.reshape(B * Lc, LANES), ids_l.reshape(B * T, LANES)], axis=0)
            wts = jnp.concatenate([wts_c.reshape(B * Lc, LANES), wts_l.reshape(B * T, LANES)], axis=0)
            f = hier_moe(rows, ids[:, :TOP_K], wts[:, :TOP_K], w["moe_gate"], w["moe_up"], w["moe_down"])
            h_ctx = gated_residual(h_ctx, f[:B * Lc].reshape(B, Lc, D), m_ctx[5])
            f_lat = f[B * Lc:].reshape(B, T, D)
        h_lat = gated_residual(h_lat, f_lat, m_lat[5])
    return h_lat
```

```python
import functools
import math

import numpy as np
import jax
import jax.numpy as jnp
from jax import lax
from jax.experimental import pallas as pl
from jax.experimental.pallas import tpu as pltpu

F32 = jnp.float32
BF16 = jnp.bfloat16

D_MODEL = 4096
DEPTH = 2
GRID_W = 64
EPS = 1e-6
GROUP_W = 1024

RWKV_HEAD = 64
RWKV_DECAY_RANK = 64
RWKV_ICLR_RANK = 64
RWKV_GATE_RANK = 160
RWKV_GN_EPS = 64e-5
RWKV_COLS = 3 * GROUP_W + 2 * RWKV_DECAY_RANK + 2 * RWKV_ICLR_RANK + RWKV_GATE_RANK
RWKV_PAD_COLS = 3 * GROUP_W + 256 + 256

POOL_WINDOWS = (2, 4, 8, 16)
POOL_GROUP = 256
POOL_HALO = 8

MLSTM_HEADS = 4
MLSTM_DV = 256
MLSTM_DK = 128
MLSTM_COLS = 2 * MLSTM_HEADS * MLSTM_DK + 2 * GROUP_W + 4 * MLSTM_HEADS
MLSTM_PAD_COLS = 2 * MLSTM_HEADS * MLSTM_DK + 2 * GROUP_W + 128
ROPE_BASE = 10000.0

NA_HEADS = 8
NA_HEAD = 128
WIN_H = 8
WIN_W = 16
NA_COLS = 3 * GROUP_W
NA_QROWS = 4
NA_KROWS = 3 * NA_QROWS

N_GROUPS = 4
EXPERTS_PER_GROUP = 4
N_EXPERTS = 16
TOP_K = 2
EXPERT_FF = 1024
MOE_BLOCK = 256
MOE_FF_TILE = 512

LANES = 128
SUBLANES = 8
VMEM_LIMIT = 56 * 1024 * 1024
NEG = -1e30

RWKV_CHUNK = 64
RWKV_PAIRS = 8
MLSTM_CHUNK = 256


def _cp(*sem):
    return pltpu.CompilerParams(dimension_semantics=sem, vmem_limit_bytes=VMEM_LIMIT)


def _dot(a, b):
    return jnp.dot(a.astype(BF16), b.astype(BF16), preferred_element_type=F32)


def _dot_nt(a, b):
    return lax.dot_general(a.astype(BF16), b.astype(BF16), (((1,), (1,)), ((), ())),
                           preferred_element_type=F32)


def _dot_tn(a, b):
    return lax.dot_general(a.astype(BF16), b.astype(BF16), (((0,), (0,)), ((), ())),
                           preferred_element_type=F32)


def _split2(x):
    hi = x.astype(BF16)
    lo = (x - hi.astype(F32)).astype(BF16)
    return hi, lo


def _dot3(a, b, dot=_dot):
    ah, al = _split2(a)
    bh, bl = _split2(b)
    return dot(ah, bh) + (dot(ah, bl) + dot(al, bh))


def _dot_exact_lhs(a_bf16, b):
    b0 = b.astype(BF16)
    r1 = b - b0.astype(F32)
    b1 = r1.astype(BF16)
    b2 = (r1 - b1.astype(F32)).astype(BF16)
    return _dot(a_bf16, b0) + (_dot(a_bf16, b1) + _dot(a_bf16, b2))


def _sigmoid(x):
    return 1.0 / (1.0 + jnp.exp(-x))


def _silu(x):
    return x * _sigmoid(x)


def _log_sigmoid(x):
    return jnp.minimum(x, 0.0) - jnp.log(1.0 + jnp.exp(-jnp.abs(x)))


def _head_sum(x, width):
    r = lax.broadcasted_iota(jnp.int32, (LANES, LANES), 0) // width
    c = lax.broadcasted_iota(jnp.int32, (LANES, LANES), 1) // width
    e = jnp.where(r == c, 1.0, 0.0).astype(BF16)
    outs = []
    for j in range(x.shape[1] // LANES):
        hi, lo = _split2(x[:, j * LANES:(j + 1) * LANES])
        outs.append(_dot(hi, e) + _dot(lo, e))
    return jnp.concatenate(outs, axis=1)


def _ada_kernel(c_ref, w_ref, b_ref, o_ref):
    s = _silu(c_ref[...])
    o_ref[0] = _dot(s, w_ref[0]) + b_ref[0]


def ada_modulation(cvec, ada_w, ada_b):
    L, D, N = ada_w.shape
    tn = 512
    return pl.pallas_call(
        _ada_kernel,
        grid=(L, N // tn),
        in_specs=[pl.BlockSpec((SUBLANES, D), lambda l, j: (0, 0)),
                  pl.BlockSpec((1, D, tn), lambda l, j: (l, 0, j)),
                  pl.BlockSpec((1, 1, tn), lambda l, j: (l, 0, j))],
        out_specs=pl.BlockSpec((1, SUBLANES, tn), lambda l, j: (l, 0, j)),
        out_shape=jax.ShapeDtypeStruct((L, SUBLANES, N), F32),
        compiler_params=_cp("parallel", "parallel"),
        name="ada_modulation",
    )(cvec, ada_w, ada_b.reshape(L, 1, N))


def _normmod_kernel(h_ref, g_ref, sh_ref, sc_ref, o_ref):
    x = h_ref[0]
    y = x * lax.rsqrt(jnp.mean(x * x, axis=-1, keepdims=True) + EPS) * g_ref[...]
    o_ref[0] = (y * (1.0 + sc_ref[0]) + sh_ref[0]).astype(o_ref.dtype)


def norm_modulate(h, g, shift, scale):
    B, T, D = h.shape
    tm = 256
    vec = pl.BlockSpec((1, 1, D), lambda b, i: (b, 0, 0))
    return pl.pallas_call(
        _normmod_kernel,
        grid=(B, T // tm),
        in_specs=[pl.BlockSpec((1, tm, D), lambda b, i: (b, i, 0)),
                  pl.BlockSpec((1, D), lambda b, i: (0, 0)), vec, vec],
        out_specs=pl.BlockSpec((1, tm, D), lambda b, i: (b, i, 0)),
        out_shape=jax.ShapeDtypeStruct((B, T, D), BF16),
        compiler_params=_cp("parallel", "parallel"),
        name="norm_modulate",
    )(h, g.reshape(1, D), shift, scale)


def _mm_kernel(x_ref, w_ref, o_ref):
    o_ref[0] = jnp.dot(x_ref[0], w_ref[...], preferred_element_type=F32).astype(o_ref.dtype)


def _pick_tile(n, cands):
    for t in cands:
        if n % t == 0:
            return t
    raise ValueError(n)


def matmul(x, w, out_dtype=F32):
    B, T, K = x.shape
    N = w.shape[1]
    tm = _pick_tile(T, (512, 256))
    tn = _pick_tile(N, (1024, 896, 768, 640, 512))
    return pl.pallas_call(
        _mm_kernel,
        grid=(B, T // tm, N // tn),
        in_specs=[pl.BlockSpec((1, tm, K), lambda b, i, j: (b, i, 0)),
                  pl.BlockSpec((K, tn), lambda b, i, j: (0, j))],
        out_specs=pl.BlockSpec((1, tm, tn), lambda b, i, j: (b, i, j)),
        out_shape=jax.ShapeDtypeStruct((B, T, N), out_dtype),
        compiler_params=_cp("parallel", "parallel", "arbitrary"),
        name="in_proj",
    )(x, w)


def _outproj_kernel(ya_ref, yb_ref, yc_ref, yd_ref, w_ref, h_ref, gate_ref, o_ref):
    acc = jnp.dot(ya_ref[0], w_ref[0:GROUP_W, :], preferred_element_type=F32)
    acc += jnp.dot(yb_ref[0], w_ref[GROUP_W:2 * GROUP_W, :], preferred_element_type=F32)
    acc += jnp.dot(yc_ref[0], w_ref[2 * GROUP_W:3 * GROUP_W, :], preferred_element_type=F32)
    acc += jnp.dot(yd_ref[0], w_ref[3 * GROUP_W:4 * GROUP_W, :], preferred_element_type=F32)
    o_ref[0] = h_ref[0] + gate_ref[0] * acc


def out_proj_residual(ys, w, h, gate):
    B, T, D = h.shape
    tm = _pick_tile(T, (512, 256))
    tn = 512
    ysp = pl.BlockSpec((1, tm, GROUP_W), lambda b, i, j: (b, i, 0))
    return pl.pallas_call(
        _outproj_kernel,
        grid=(B, T // tm, D // tn),
        in_specs=[ysp, ysp, ysp, ysp,
                  pl.BlockSpec((D, tn), lambda b, i, j: (0, j)),
                  pl.BlockSpec((1, tm, tn), lambda b, i, j: (b, i, j)),
                  pl.BlockSpec((1, 1, tn), lambda b, i, j: (b, 0, j))],
        out_specs=pl.BlockSpec((1, tm, tn), lambda b, i, j: (b, i, j)),
        out_shape=jax.ShapeDtypeStruct((B, T, D), F32),
        compiler_params=_cp("parallel", "parallel", "arbitrary"),
        name="out_proj",
    )(*ys, w, h, gate)


def _halo_specs(tm, width, halo=SUBLANES):
    per = tm // halo

    def prev_map(b, i):
        return (b, jnp.maximum(i * per - 1, 0), 0)

    def next_map(nb):
        return lambda b, i: (b, jnp.minimum((i + 1) * per, nb - 1), 0)

    return prev_map, next_map


def _shift_rows(x, prev_row, next_row):
    tm = x.shape[0]
    row = lax.broadcasted_iota(jnp.int32, x.shape, 0)
    prev = jnp.where(row == 0, prev_row, pltpu.roll(x, 1, axis=0))
    nxt = jnp.where(row == tm - 1, next_row, pltpu.roll(x, tm - 1, axis=0))
    return prev, nxt


def _rwkv_prep_kernel(p_ref, pp_ref, pn_ref, mu_ref, wup_ref, w0_ref, aup_ref, a0_ref, gup_ref,
                      kk_w_ref, ka_w_ref, rk_w_ref,
                      r_ref, kk_ref, v_ref, bonus_ref, g_ref, lw_ref, kd_ref, bb_ref):
    i = pl.program_id(1)
    last = pl.num_programs(1) - 1
    x = p_ref[0]
    prev_row = jnp.where(i == 0, 0.0, pp_ref[0, SUBLANES - 1:SUBLANES, :])
    next_row = jnp.where(i == last, 0.0, pn_ref[0, 0:1, :])
    prev, nxt = _shift_rows(x, prev_row, next_row)
    x = x + mu_ref[0:1, :] * (prev - x) + mu_ref[1:2, :] * (nxt - x)

    r = x[:, 0:GROUP_W]
    k = x[:, GROUP_W:2 * GROUP_W]
    v = x[:, 2 * GROUP_W:3 * GROUP_W]
    low = x[:, 3 * GROUP_W:3 * GROUP_W + 256]
    gd = x[:, 3 * GROUP_W + 256:3 * GROUP_W + 512]

    zw = _dot(jnp.tanh(low[:, 0:LANES]), wup_ref[...]) + w0_ref[...]
    za = _dot(low[:, LANES:2 * LANES], aup_ref[...]) + a0_ref[...]
    kkr = k * kk_w_ref[...]
    kk = kkr * lax.rsqrt(_head_sum(kkr * kkr, RWKV_HEAD) + EPS)
    r_ref[0] = r
    kk_ref[0] = kk
    v_ref[0] = v
    bonus_ref[0] = _head_sum(r * k * rk_w_ref[...], RWKV_HEAD) * v
    g_ref[0] = _dot(_sigmoid(gd), gup_ref[...])
    for d in range(2):
        sl = slice(d * GROUP_W, (d + 1) * GROUP_W)
        lw_ref[d, 0] = (-math.exp(-0.5)) * _sigmoid(zw[:, sl])
        a = _sigmoid(za[:, sl])
        kd_ref[d, 0] = k * (1.0 + (a - 1.0) * ka_w_ref[...])
        bb_ref[d, 0] = kk * a


def rwkv_prep(p, w):
    B, T, W = p.shape
    tm = 128
    prev_map, next_map = _halo_specs(tm, W)
    nb8 = T // SUBLANES
    full = lambda shape: pl.BlockSpec(shape, lambda b, i: tuple(0 for _ in shape))
    tok = pl.BlockSpec((1, tm, GROUP_W), lambda b, i: (b, i, 0))
    tokd = pl.BlockSpec((2, 1, tm, GROUP_W), lambda b, i: (0, b, i, 0))
    sh = jax.ShapeDtypeStruct((B, T, GROUP_W), F32)
    shd = jax.ShapeDtypeStruct((2, B, T, GROUP_W), F32)
    return pl.pallas_call(
        _rwkv_prep_kernel,
        grid=(B, T // tm),
        in_specs=[pl.BlockSpec((1, tm, W), lambda b, i: (b, i, 0)),
                  pl.BlockSpec((1, SUBLANES, W), prev_map),
                  pl.BlockSpec((1, SUBLANES, W), next_map(nb8)),
                  full((2, W)), full((LANES, 2 * GROUP_W)), full((1, 2 * GROUP_W)),
                  full((LANES, 2 * GROUP_W)), full((1, 2 * GROUP_W)), full((256, GROUP_W)),
                  full((1, GROUP_W)), full((1, GROUP_W)), full((1, GROUP_W))],
        out_specs=[tok, tok, tok, tok, tok, tokd, tokd, tokd],
        out_shape=[sh, sh, sh, sh, sh, shd, shd, shd],
        compiler_params=_cp("parallel", "parallel"),
        name="rwkv_prep",
    )(p, p, p, w["mu"], w["w_up"], w["w0"], w["a_up"], w["a0"], w["g_up"],
      w["k_k"], w["k_a"], w["r_k"])


def _dform(x, m0):
    return jnp.concatenate([jnp.where(m0, x, 0.0), jnp.where(m0, 0.0, x)], axis=0)


def _rwkv_chunk(tiles, sts, rev):
    n = len(tiles)
    C = tiles[0][0].shape[0]
    C2 = 2 * C
    sgn = jnp.where(rev, -1, 1)
    rr = lax.broadcasted_iota(jnp.int32, (C, C), 0)
    cc = lax.broadcasted_iota(jnp.int32, (C, C), 1)
    tri = jnp.where((rr - cc) * sgn >= 0, 1.0, 0.0).astype(BF16)
    lane = lax.broadcasted_iota(jnp.int32, (C, LANES), 1)
    m0 = lane < RWKV_HEAD
    rd = lax.broadcasted_iota(jnp.int32, (C2, C2), 0)
    cd = lax.broadcasted_iota(jnp.int32, (C2, C2), 1)
    same = (rd // C) == (cd // C)
    ahead = (rd % C - cd % C) * sgn
    strict = same & (ahead > 0)
    incl = same & (ahead >= 0)
    diag = rd == cd
    each = range(n)

    lins = [_dot_exact_lhs(tri, t[3]) for t in tiles]
    ltots = [jnp.sum(t[3], axis=0, keepdims=True) for t in tiles]
    a_d, r_d, k_d, b_d, v_d, kw_d, bw_d = [], [], [], [], [], [], []
    for (r, kk, v, lw, kd, bb), lin, ltot in zip(tiles, lins, ltots):
        einv = jnp.exp(-lin)
        ew = jnp.exp(ltot - lin)
        a_d.append(_dform(-kk * jnp.exp(lin - lw), m0).astype(BF16))
        r_d.append(_dform(r * jnp.exp(lin), m0).astype(BF16))
        k_d.append(_dform(kd * einv, m0).astype(BF16))
        b_d.append(_dform(bb * einv, m0).astype(BF16))
        v_d.append(_dform(v, m0).astype(BF16))
        kw_d.append(_dform(kd * ew, m0).astype(BF16))
        bw_d.append(_dform(bb * ew, m0).astype(BF16))

    grams = [_dot_nt(jnp.concatenate([a_d[i], r_d[i]], axis=0), jnp.concatenate([k_d[i], b_d[i]], axis=0))
             for i in each]
    a_ak = [jnp.where(strict, g[:C2, :C2], 0.0).astype(BF16) for g in grams]
    a_ab = [jnp.where(strict, g[:C2, C2:], 0.0) for g in grams]
    a_rk = [jnp.where(incl, g[C2:, :C2], 0.0).astype(BF16) for g in grams]
    a_rb = [jnp.where(incl, g[C2:, C2:], 0.0).astype(BF16) for g in grams]

    tinv = [jnp.where(diag, 1.0, 0.0) + a for a in a_ab]
    apow = a_ab
    for _ in range(int(math.log2(C)) - 1):
        apow = [_dot(a, a) for a in apow]
        tinv = [t + _dot(t, a) for t, a in zip(tinv, apow)]

    akv = [_dot(a_ak[i], v_d[i]) for i in each]
    p = [_dot(tinv[i], jnp.concatenate([a_d[i], akv[i].astype(BF16)], axis=1)) for i in each]
    ur = [_dot(jnp.concatenate([p[i][:, :LANES].astype(BF16), r_d[i]], axis=0), sts[i]) for i in each]
    u = [ur[i][:C2] + p[i][:, LANES:] for i in each]
    y2 = [ur[i][C2:] + _dot(a_rk[i], v_d[i]) + _dot(a_rb[i], u[i]) for i in each]
    ys = [y[:C] + y[C:] for y in y2]
    new = []
    for i in each:
        wc = jnp.broadcast_to(jnp.exp(ltots[i]), (LANES, LANES))
        wcol = jnp.sum(jnp.where(diag, wc, 0.0), axis=1, keepdims=True)
        new.append(sts[i] * wcol + _dot_tn(jnp.concatenate([bw_d[i], kw_d[i]], axis=0),
                                           jnp.concatenate([u[i].astype(BF16), v_d[i]], axis=0)))
    return ys, new


def _rwkv_scan_kernel(r_ref, kk_ref, v_ref, lw_ref, kd_ref, bb_ref, s0_ref, y_ref, sT_ref, st_scr,
                      *, npairs):
    d = pl.program_id(0)
    c = pl.program_id(3)
    nc = pl.num_programs(3)

    @pl.when(c == 0)
    def _():
        st_scr[...] = s0_ref[0, 0]

    lanes = [slice(j * LANES, (j + 1) * LANES) for j in range(npairs)]
    tiles = [(r_ref[0, :, sl], kk_ref[0, :, sl], v_ref[0, :, sl],
              lw_ref[0, 0, :, sl], kd_ref[0, 0, :, sl], bb_ref[0, 0, :, sl]) for sl in lanes]
    ys, new = _rwkv_chunk(tiles, [st_scr[j] for j in range(npairs)], d == 1)
    for j, sl in enumerate(lanes):
        y_ref[0, 0, :, sl] = ys[j]
        st_scr[j] = new[j]

    @pl.when(c == nc - 1)
    def _():
        sT_ref[0, 0] = st_scr[...]


def rwkv_scan(f, s0, npairs=RWKV_PAIRS):
    r, kk, v, lw, kd, bb = f["r"], f["kk"], f["v"], f["lw"], f["kd"], f["bb"]
    B, T, _ = r.shape
    C = RWKV_CHUNK
    nc = T // C
    ng = RWKV_PAIRS // npairs
    W = npairs * LANES
    chunk = lambda d, c: c + d * (nc - 1 - 2 * c)
    tok = pl.BlockSpec((1, C, W), lambda d, b, g, c: (b, chunk(d, c), g))
    tokd = pl.BlockSpec((1, 1, C, W), lambda d, b, g, c: (d, b, chunk(d, c), g))
    stsp = pl.BlockSpec((1, 1, npairs, LANES, LANES), lambda d, b, g, c: (d, b, g, 0, 0))
    return pl.pallas_call(
        functools.partial(_rwkv_scan_kernel, npairs=npairs),
        grid=(2, B, ng, nc),
        in_specs=[tok, tok, tok, tokd, tokd, tokd, stsp],
        out_specs=[tokd, stsp],
        out_shape=[jax.ShapeDtypeStruct((2, B, T, GROUP_W), F32),
                   jax.ShapeDtypeStruct(s0.shape, F32)],
        scratch_shapes=[pltpu.VMEM((npairs, LANES, LANES), F32)],
        compiler_params=_cp("parallel", "parallel", "parallel", "arbitrary"),
        name="rwkv_scan",
    )(r, kk, v, lw, kd, bb, s0)


def _rwkv_readout_kernel(yf_ref, yb_ref, bonus_ref, g_ref, ln_ref, o_ref):
    y = yf_ref[0, 0] + yb_ref[0, 0]
    mean = _head_sum(y, RWKV_HEAD) * (1.0 / RWKV_HEAD)
    cen = y - mean
    var = _head_sum(cen * cen, RWKV_HEAD) * (1.0 / RWKV_HEAD)
    yn = cen * lax.rsqrt(var + RWKV_GN_EPS) * ln_ref[0:1, :] + ln_ref[1:2, :]
    o_ref[0] = ((yn + bonus_ref[0]) * g_ref[0]).astype(o_ref.dtype)


def rwkv_readout(y, bonus, g, ln):
    _, B, T, W = y.shape
    tm = 256
    tok = pl.BlockSpec((1, tm, W), lambda b, i: (b, i, 0))
    return pl.pallas_call(
        _rwkv_readout_kernel,
        grid=(B, T // tm),
        in_specs=[pl.BlockSpec((1, 1, tm, W), lambda b, i: (0, b, i, 0)),
                  pl.BlockSpec((1, 1, tm, W), lambda b, i: (1, b, i, 0)),
                  tok, tok, pl.BlockSpec((2, W), lambda b, i: (0, 0))],
        out_specs=tok,
        out_shape=jax.ShapeDtypeStruct((B, T, W), BF16),
        compiler_params=_cp("parallel", "parallel"),
        name="rwkv_readout",
    )(y, y, bonus, g, ln)


def _pool_kernel(p_ref, pp_ref, pn_ref, w_ref, sc_ref, o_ref, *, total):
    i = pl.program_id(1)
    last = pl.num_programs(1) - 1
    x = p_ref[0]
    tm = x.shape[0]
    n = tm + 2 * POOL_HALO
    ext = jnp.concatenate([jnp.where(i == 0, 0.0, pp_ref[0]), x,
                           jnp.where(i == last, 0.0, pn_ref[0])], axis=0)
    t = i * tm + lax.broadcasted_iota(jnp.int32, (tm, 1), 0)
    outs = []
    for gi, win in enumerate(POOL_WINDOWS):
        sl = slice(gi * POOL_GROUP, (gi + 1) * POOL_GROUP)
        e = ext[:, sl]
        s = e + pltpu.roll(e, 1, axis=0)
        step = 1
        while 2 * step < win:
            s = pltpu.roll(s, step, axis=0) + pltpu.roll(s, n - step, axis=0)
            step *= 2
        h = win // 2
        cnt = (jnp.minimum(t + h, total) - jnp.maximum(t - h, 0)).astype(F32)
        z = s[POOL_HALO:POOL_HALO + tm] / cnt - x[:, sl]
        outs.append(_dot(z, w_ref[gi]))
    o_ref[0] = (jnp.concatenate(outs, axis=1) * sc_ref[...]).astype(o_ref.dtype)


def pool_mixer(p, pool_w, pool_scale):
    B, T, W = p.shape
    tm = 256
    prev_map, next_map = _halo_specs(tm, W)
    return pl.pallas_call(
        functools.partial(_pool_kernel, total=T),
        grid=(B, T // tm),
        in_specs=[pl.BlockSpec((1, tm, W), lambda b, i: (b, i, 0)),
                  pl.BlockSpec((1, POOL_HALO, W), prev_map),
                  pl.BlockSpec((1, POOL_HALO, W), next_map(T // POOL_HALO)),
                  pl.BlockSpec(pool_w.shape, lambda b, i: (0, 0, 0)),
                  pl.BlockSpec((1, W), lambda b, i: (0, 0))],
        out_specs=pl.BlockSpec((1, tm, W), lambda b, i: (b, i, 0)),
        out_shape=jax.ShapeDtypeStruct((B, T, W), BF16),
        compiler_params=_cp("parallel", "parallel"),
        name="pool_mixer",
    )(p, p, p, pool_w, pool_scale.reshape(1, W))


def _mlstm_prep_kernel(p_ref, pp_ref, pn_ref, g_ref, cw_ref, gb_ref, *rest, rope):
    if rope:
        cos_ref, sin_ref, q_ref, k_ref, go_ref = rest
    else:
        q_ref, k_ref, go_ref = rest
    i = pl.program_id(1)
    last = pl.num_programs(1) - 1
    x = p_ref[0]
    prev_row = jnp.where(i == 0, 0.0, pp_ref[0, SUBLANES - 1:SUBLANES, :])
    next_row = jnp.where(i == last, 0.0, pn_ref[0, 0:1, :])
    prev, nxt = _shift_rows(x, prev_row, next_row)
    qk = _silu(cw_ref[0:1, :] * prev + cw_ref[1:2, :] * x + cw_ref[2:3, :] * nxt)
    if rope:
        w = qk.shape[1]
        lane = lax.broadcasted_iota(jnp.int32, qk.shape, 1)
        partner = jnp.where(lane % 2 == 0, pltpu.roll(qk, w - 1, axis=1), pltpu.roll(qk, 1, axis=1))
        reps = w // LANES
        cos = jnp.concatenate([cos_ref[...]] * reps, axis=1)
        sin = jnp.concatenate([sin_ref[...]] * reps, axis=1)
        qk = qk * cos + partner * sin
    half = MLSTM_HEADS * MLSTM_DK
    q_ref[0] = qk[:, :half].astype(BF16)
    k_ref[0] = (qk[:, half:] * (MLSTM_DK ** -0.5)).astype(BF16)
    g = g_ref[0] + gb_ref[...]
    lane = lax.broadcasted_iota(jnp.int32, g.shape, 1)
    go_ref[0] = jnp.where((lane // MLSTM_HEADS) % 2 == 1, _log_sigmoid(g), g)


def mlstm_prep(p, conv_w, gate_b, rope_tabs):
    B, T, W = p.shape
    tm = 256
    QK = 2 * MLSTM_HEADS * MLSTM_DK
    prev_map, next_map = _halo_specs(tm, QK)
    rope = rope_tabs is not None
    in_specs = [pl.BlockSpec((1, tm, QK), lambda b, i: (b, i, 0)),
                pl.BlockSpec((1, SUBLANES, QK), prev_map),
                pl.BlockSpec((1, SUBLANES, QK), next_map(T // SUBLANES)),
                pl.BlockSpec((1, tm, LANES), lambda b, i: (b, i, (W - LANES) // LANES)),
                pl.BlockSpec((3, QK), lambda b, i: (0, 0)),
                pl.BlockSpec((1, LANES), lambda b, i: (0, 0))]
    args = [p, p, p, p, conv_w, gate_b]
    if rope:
        in_specs += [pl.BlockSpec((tm, LANES), lambda b, i: (i, 0))] * 2
        args += list(rope_tabs)
    half = MLSTM_HEADS * MLSTM_DK
    return pl.pallas_call(
        functools.partial(_mlstm_prep_kernel, rope=rope),
        grid=(B, T // tm),
        in_specs=in_specs,
        out_specs=[pl.BlockSpec((1, tm, half), lambda b, i: (b, i, 0)),
                   pl.BlockSpec((1, tm, half), lambda b, i: (b, i, 0)),
                   pl.BlockSpec((1, tm, LANES), lambda b, i: (b, i, 0))],
        out_shape=[jax.ShapeDtypeStruct((B, T, half), BF16),
                   jax.ShapeDtypeStruct((B, T, half), BF16),
                   jax.ShapeDtypeStruct((B, T, LANES), F32)],
        compiler_params=_cp("parallel", "parallel"),
        name="mlstm_prep",
    )(*args)


def _mlstm_chunk_kernel(q_ref, k_ref, v_ref, gc_ref, gr_ref, c0_ref, n0_ref, m0_ref,
                        h_ref, cT_ref, nT_ref, mT_ref, c_scr, n_scr, m_scr):
    d = pl.program_id(0)
    hh = pl.program_id(2)
    c = pl.program_id(3)
    nc = pl.num_programs(3)

    @pl.when(c == 0)
    def _():
        c_scr[...] = c0_ref[0, 0, 0]
        n_scr[...] = n0_ref[0, 0, 0]
        m_scr[...] = m0_ref[0, 0, 0]

    rev = d == 1
    q = q_ref[0]
    k = k_ref[0]
    v = v_ref[0]
    L = q.shape[0]
    rr = lax.broadcasted_iota(jnp.int32, (L, L), 0)
    cc = lax.broadcasted_iota(jnp.int32, (L, L), 1)
    sgn = jnp.where(rev, -1, 1)
    seen = (rr - cc) * sgn >= 0
    tri = jnp.where(seen, 1.0, 0.0).astype(BF16)
    tri_t = jnp.where((cc - rr) * sgn >= 0, 1.0, 0.0).astype(BF16)

    gc = gc_ref[0]
    gr = gr_ref[0]
    lane = lax.broadcasted_iota(jnp.int32, gc.shape, 1)
    subl = lax.broadcasted_iota(jnp.int32, gr.shape, 0)
    i_lane = d * (2 * MLSTM_HEADS) + hh
    f_lane = i_lane + MLSTM_HEADS
    pick_c = lambda a, idx: jnp.sum(jnp.where(lane == idx, a, 0.0), axis=1, keepdims=True)
    pick_r = lambda a, idx: jnp.sum(jnp.where(subl == idx, a, 0.0), axis=0, keepdims=True)
    ig_c = pick_c(gc, i_lane)
    lf_c = pick_c(gc, f_lane)
    b_c = pick_c(_dot_exact_lhs(tri, gc), f_lane)
    ig_r = pick_r(gr, i_lane)
    gr_hi = gr.astype(BF16)
    gr_r1 = gr - gr_hi.astype(F32)
    gr_mid = gr_r1.astype(BF16)
    gr_lo = (gr_r1 - gr_mid.astype(F32)).astype(BF16)
    b_r = pick_r(_dot(gr_hi, tri_t) + (_dot(gr_mid, tri_t) + _dot(gr_lo, tri_t)), f_lane)

    m_prev = m_scr[...]
    dlog = jnp.where(seen, b_c - b_r + ig_r, NEG)
    inter = b_c + m_prev
    m_t = jnp.maximum(inter, jnp.max(dlog, axis=1, keepdims=True))
    dmat = jnp.exp(dlog - m_t)
    w_inter = jnp.exp(inter - m_t)
    s = _dot_nt(q, k) * dmat
    cst = c_scr[...]
    nst = n_scr[...]
    num = _dot(s, v) + w_inter * _dot(q, cst)
    den = jnp.sum(s, axis=1, keepdims=True) + w_inter * jnp.sum(q.astype(F32) * nst, axis=1, keepdims=True)
    h_ref[0, 0] = num / jnp.maximum(jnp.abs(den), jnp.exp(-m_t))

    b_end = jnp.sum(lf_c, axis=0, keepdims=True)
    g_s = b_end - b_c + ig_c
    m_new = jnp.maximum(b_end + m_prev, jnp.max(g_s, axis=0, keepdims=True))
    kw = k.astype(F32) * jnp.exp(g_s - m_new)
    decay = jnp.exp(b_end + m_prev - m_new)
    c_scr[...] = decay * cst + _dot_tn(kw, v)
    n_scr[...] = decay * nst + jnp.sum(kw, axis=0, keepdims=True)
    m_scr[...] = m_new

    @pl.when(c == nc - 1)
    def _():
        cT_ref[0, 0, 0] = c_scr[...]
        nT_ref[0, 0, 0] = n_scr[...]
        mT_ref[0, 0, 0] = m_scr[...]


def mlstm_scan(q, k, p, gates, gates_t, state):
    B, T, _ = q.shape
    L = min(MLSTM_CHUNK, T)
    nc = T // L
    H = MLSTM_HEADS
    c0, n0, m0 = state
    chunk = lambda d, c: c + d * (nc - 1 - 2 * c)
    v_off = (2 * H * MLSTM_DK) // MLSTM_DV
    csp = pl.BlockSpec((1, 1, 1, MLSTM_DK, MLSTM_DV), lambda d, b, h, c: (d, b, h, 0, 0))
    nsp = pl.BlockSpec((1, 1, 1, 1, MLSTM_DK), lambda d, b, h, c: (d, b, h, 0, 0))
    msp = pl.BlockSpec((1, 1, 1, 1, 1), lambda d, b, h, c: (d, b, h, 0, 0))
    return pl.pallas_call(
        _mlstm_chunk_kernel,
        grid=(2, B, H, nc),
        in_specs=[pl.BlockSpec((1, L, MLSTM_DK), lambda d, b, h, c: (b, chunk(d, c), h)),
                  pl.BlockSpec((1, L, MLSTM_DK), lambda d, b, h, c: (b, chunk(d, c), h)),
                  pl.BlockSpec((1, L, MLSTM_DV), lambda d, b, h, c: (b, chunk(d, c), v_off + h)),
                  pl.BlockSpec((1, L, LANES), lambda d, b, h, c: (b, chunk(d, c), 0)),
                  pl.BlockSpec((1, 4 * H, L), lambda d, b, h, c: (b, 0, chunk(d, c))),
                  csp, nsp, msp],
        out_specs=[pl.BlockSpec((1, 1, L, MLSTM_DV), lambda d, b, h, c: (d, b, chunk(d, c), h)),
                   csp, nsp, msp],
        out_shape=[jax.ShapeDtypeStruct((2, B, T, GROUP_W), F32),
                   jax.ShapeDtypeStruct(c0.shape, F32),
                   jax.ShapeDtypeStruct(n0.shape, F32),
                   jax.ShapeDtypeStruct(m0.shape, F32)],
        scratch_shapes=[pltpu.VMEM((MLSTM_DK, MLSTM_DV), F32),
                        pltpu.VMEM((1, MLSTM_DK), F32),
                        pltpu.VMEM((1, 1), F32)],
        compiler_params=_cp("parallel", "parallel", "parallel", "arbitrary"),
        name="mlstm_scan",
    )(q, k, p, gates, gates_t, c0, n0, m0)


def _mlstm_readout_kernel(hf_ref, hb_ref, o_ref, nw_ref, out_ref):
    h = hf_ref[0, 0] + hb_ref[0, 0]
    outs = []
    for j in range(MLSTM_HEADS):
        hj = h[:, j * MLSTM_DV:(j + 1) * MLSTM_DV]
        outs.append(hj * lax.rsqrt(jnp.mean(hj * hj, axis=-1, keepdims=True) + EPS))
    hn = jnp.concatenate(outs, axis=1)
    out_ref[0] = (hn * nw_ref[...] * _sigmoid(o_ref[0])).astype(out_ref.dtype)


def mlstm_readout(h, p, norm_w):
    _, B, T, W = h.shape
    tm = 256
    o_blk = (2 * MLSTM_HEADS * MLSTM_DK + GROUP_W) // GROUP_W
    return pl.pallas_call(
        _mlstm_readout_kernel,
        grid=(B, T // tm),
        in_specs=[pl.BlockSpec((1, 1, tm, W), lambda b, i: (0, b, i, 0)),
                  pl.BlockSpec((1, 1, tm, W), lambda b, i: (1, b, i, 0)),
                  pl.BlockSpec((1, tm, W), lambda b, i: (b, i, o_blk)),
                  pl.BlockSpec((1, W), lambda b, i: (0, 0))],
        out_specs=pl.BlockSpec((1, tm, W), lambda b, i: (b, i, 0)),
        out_shape=jax.ShapeDtypeStruct((B, T, W), BF16),
        compiler_params=_cp("parallel", "parallel"),
        name="mlstm_readout",
    )(h, h, p, norm_w.reshape(1, W))


def _na_prep_kernel(p_ref, nw_ref, q_ref, k_ref, v_ref):
    x = p_ref[0]
    for j in range(NA_HEADS):
        sl = slice(j * NA_HEAD, (j + 1) * NA_HEAD)
        for src, dst, row, scale in ((0, q_ref, 0, NA_HEAD ** -0.5), (GROUP_W, k_ref, 1, 1.0)):
            z = x[:, src + j * NA_HEAD:src + (j + 1) * NA_HEAD]
            zn = z * lax.rsqrt(jnp.mean(z * z, axis=-1, keepdims=True) + EPS) * nw_ref[row:row + 1, :]
            dst[0, :, sl] = (zn * scale).astype(BF16)
    v_ref[0] = x[:, 2 * GROUP_W:].astype(BF16)


def na_prep(p, qk_norm):
    B, T, W = p.shape
    tm = 256
    tok = pl.BlockSpec((1, tm, GROUP_W), lambda b, i: (b, i, 0))
    sh = jax.ShapeDtypeStruct((B, T, GROUP_W), BF16)
    return pl.pallas_call(
        _na_prep_kernel,
        grid=(B, T // tm),
        in_specs=[pl.BlockSpec((1, tm, W), lambda b, i: (b, i, 0)),
                  pl.BlockSpec((2, NA_HEAD), lambda b, i: (0, 0))],
        out_specs=[tok, tok, tok],
        out_shape=[sh, sh, sh],
        compiler_params=_cp("parallel", "parallel"),
        name="na_prep",
    )(p, qk_norm)


def _softmax_rows(s):
    m = jnp.max(s, axis=1, keepdims=True)
    e = jnp.exp(s - m)
    return e / jnp.sum(e, axis=1, keepdims=True)


def _na_ctx_kernel(q_ref, k_ref, v_ref, o_ref):
    for j in range(NA_HEADS):
        sl = slice(j * NA_HEAD, (j + 1) * NA_HEAD)
        pr = _softmax_rows(_dot_nt(q_ref[0, :, sl], k_ref[0, :, sl]))
        o_ref[0, :, sl] = _dot(pr, v_ref[0, :, sl]).astype(o_ref.dtype)


def na_ctx_attention(q, k, v):
    B, T, W = q.shape
    tok = pl.BlockSpec((1, T, W), lambda b: (b, 0, 0))
    return pl.pallas_call(
        _na_ctx_kernel,
        grid=(B,),
        in_specs=[tok, tok, tok],
        out_specs=tok,
        out_shape=jax.ShapeDtypeStruct((B, T, W), BF16),
        compiler_params=_cp("parallel"),
        name="na_ctx_attention",
    )(q, k, v)


def _na_lat_kernel(q_ref, k0_ref, k1_ref, k2_ref, v0_ref, v1_ref, v2_ref, kc_ref, vc_ref, bias_ref, o_ref):
    for j in range(NA_HEADS):
        sl = slice(j * NA_HEAD, (j + 1) * NA_HEAD)
        q = q_ref[0, :, sl]
        kcat = jnp.concatenate([k0_ref[0, :, sl], k1_ref[0, :, sl], k2_ref[0, :, sl]], axis=0)
        vcat = jnp.concatenate([v0_ref[0, :, sl], v1_ref[0, :, sl], v2_ref[0, :, sl],
                                vc_ref[0, :, sl]], axis=0)
        s = jnp.concatenate([_dot_nt(q, kcat) + bias_ref[0, j], _dot_nt(q, kc_ref[0, :, sl])], axis=1)
        o_ref[0, :, sl] = _dot(_softmax_rows(s), vcat).astype(o_ref.dtype)


def na_bias_table(rpb, rows):
    nq, nk = NA_QROWS, NA_KROWS
    starts = np.array([0, nq, rows - nq])
    r = starts[:, None] + np.arange(nq)[None, :]
    kr = starts[:, None] - nq + np.arange(nk)[None, :]
    r0 = np.clip(r - WIN_H // 2, 0, rows - WIN_H)
    rv = (kr[:, None, :] >= r0[:, :, None]) & (kr[:, None, :] < r0[:, :, None] + WIN_H)
    dr = np.clip(kr[:, None, :] - r[:, :, None] + WIN_H - 1, 0, 2 * WIN_H - 2)
    col = np.arange(GRID_W)
    c0 = np.clip(col - WIN_W // 2, 0, GRID_W - WIN_W)
    cv = (col[None, :] >= c0[:, None]) & (col[None, :] < c0[:, None] + WIN_W)
    dc = np.clip(col[None, :] - col[:, None] + WIN_W - 1, 0, 2 * WIN_W - 2)
    dr_i = np.broadcast_to(dr[:, :, None, :, None], (3, nq, GRID_W, nk, GRID_W))
    dc_i = np.broadcast_to(dc[None, None, :, None, :], (3, nq, GRID_W, nk, GRID_W))
    valid = rv[:, :, None, :, None] & cv[None, None, :, None, :]
    tab = rpb[:, dr_i, dc_i]
    tab = jnp.where(valid[None], tab, NEG)
    return jnp.transpose(tab, (1, 0, 2, 3, 4, 5)).reshape(3, NA_HEADS, nq * GRID_W, nk * GRID_W)


def na_lat_attention(q, k, v, kc, vc, bias):
    B, T, W = q.shape
    tq = NA_QROWS * GRID_W
    nb = T // tq
    ctx_len = kc.shape[1]
    qsp = pl.BlockSpec((1, tq, W), lambda b, i: (b, i, 0))
    prv = pl.BlockSpec((1, tq, W), lambda b, i: (b, jnp.maximum(i - 1, 0), 0))
    nxt = pl.BlockSpec((1, tq, W), lambda b, i: (b, jnp.minimum(i + 1, nb - 1), 0))
    csp = pl.BlockSpec((1, ctx_len, W), lambda b, i: (b, 0, 0))
    pattern = lambda i: jnp.where(i == 0, 0, jnp.where(i == nb - 1, 2, 1))
    return pl.pallas_call(
        _na_lat_kernel,
        grid=(B, nb),
        in_specs=[qsp, prv, qsp, nxt, prv, qsp, nxt, csp, csp,
                  pl.BlockSpec((1, NA_HEADS, tq, NA_KROWS * GRID_W), lambda b, i: (pattern(i), 0, 0, 0))],
        out_specs=qsp,
        out_shape=jax.ShapeDtypeStruct((B, T, W), BF16),
        compiler_params=_cp("parallel", "arbitrary"),
        name="na_lat_attention",
    )(q, k, k, k, v, v, v, kc, vc, bias)


def _ffn_norm_router_body(h_ref, g_ref, sh_ref, sc_ref, rw_ref, rb_ref, v_ref, ids_ref, wts_ref):
    x = h_ref[0]
    y = x * lax.rsqrt(jnp.mean(x * x, axis=-1, keepdims=True) + EPS) * g_ref[...]
    v = y * (1.0 + sc_ref[0]) + sh_ref[0]
    v_ref[...] = v
    vh, vl = _split2(v)
    logits = (_dot(vh, rw_ref[0]) + (_dot(vh, rw_ref[1]) + _dot(vl, rw_ref[0]))) + rb_ref[...]
    lane = lax.broadcasted_iota(jnp.int32, logits.shape, 1).astype(F32)
    first = lambda mask: jnp.min(jnp.where(mask, lane, float(LANES)), axis=1, keepdims=True)

    g_mask = lane < N_GROUPS
    gl = jnp.where(g_mask, logits, NEG)
    gmax = jnp.max(gl, axis=1, keepdims=True)
    g_p = 1.0 / jnp.sum(jnp.exp(gl - gmax), axis=1, keepdims=True)
    g_idx = first(g_mask & (gl == gmax))
    e_lane = lane - N_GROUPS
    e_mask = (e_lane >= 0) & (e_lane < N_EXPERTS) & (jnp.floor(e_lane * (1.0 / EXPERTS_PER_GROUP)) == g_idx)
    el = jnp.where(e_mask, logits, NEG)
    e1 = jnp.max(el, axis=1, keepdims=True)
    i1 = first(e_mask & (el == e1))
    el2 = jnp.where(lane == i1, NEG, el)
    e2 = jnp.max(el2, axis=1, keepdims=True)
    i2 = first(e_mask & (lane != i1) & (el2 == e2))
    x2 = jnp.exp(e2 - e1)
    w1 = g_p / (1.0 + x2)
    w2 = g_p * x2 / (1.0 + x2)
    ids_ref[...] = jnp.where(lane == 0, i1 - N_GROUPS, jnp.where(lane == 1, i2 - N_GROUPS, 0.0)).astype(jnp.int32)
    wts_ref[...] = jnp.where(lane == 0, w1, jnp.where(lane == 1, w2, 0.0))


def _ffn_norm_router_kernel(*refs, bounds):
    ns = len(bounds) - 1
    g_ref, rw_ref, rb_ref, v_ref, ids_ref, wts_ref = refs[3 * ns:]
    r = pl.program_id(0)
    for s in range(ns):
        h_ref, sh_ref, sc_ref = refs[3 * s:3 * s + 3]

        @pl.when((r >= bounds[s]) & (r < bounds[s + 1]))
        def _():
            _ffn_norm_router_body(h_ref, g_ref, sh_ref, sc_ref, rw_ref, rb_ref, v_ref, ids_ref, wts_ref)


def ffn_norm_router(streams, g, rw, rb):
    D = g.shape[0]
    tm = 256
    bounds = [0]
    in_specs, args = [], []
    for h, shift, scale in streams:
        B, T, _ = h.shape
        nt = T // tm
        lo, nblk = bounds[-1], B * nt
        bounds.append(lo + nblk)
        local = lambda r, lo=lo, nblk=nblk: jnp.clip(r - lo, 0, nblk - 1)
        in_specs += [pl.BlockSpec((1, tm, D), lambda r, f=local, nt=nt: (f(r) // nt, f(r) % nt, 0)),
                     pl.BlockSpec((1, 1, D), lambda r, f=local, nt=nt: (f(r) // nt, 0, 0)),
                     pl.BlockSpec((1, 1, D), lambda r, f=local, nt=nt: (f(r) // nt, 0, 0))]
        args += [h, shift, scale]
    n_rows = bounds[-1] * tm
    in_specs += [pl.BlockSpec((1, D), lambda r: (0, 0)),
                 pl.BlockSpec((2, D, LANES), lambda r: (0, 0, 0)),
                 pl.BlockSpec((1, LANES), lambda r: (0, 0))]
    args += [g.reshape(1, D), rw, rb]
    lan = pl.BlockSpec((tm, LANES), lambda r: (r, 0))
    return pl.pallas_call(
        functools.partial(_ffn_norm_router_kernel, bounds=tuple(bounds)),
        grid=(bounds[-1],),
        in_specs=in_specs,
        out_specs=[pl.BlockSpec((tm, D), lambda r: (r, 0)), lan, lan],
        out_shape=[jax.ShapeDtypeStruct((n_rows, D), F32),
                   jax.ShapeDtypeStruct((n_rows, LANES), jnp.int32),
                   jax.ShapeDtypeStruct((n_rows, LANES), F32)],
        compiler_params=_cp("parallel"),
        name="ffn_norm_router",
    )(*args)


def _moe_kernel(be_ref, nv_ref, tok_ref, tokn_ref, dst_ref, sw_ref, v_hbm, wg_hbm, wu_hbm, wd_hbm, out_hbm,
                xbuf, ybuf, wg_buf, wu_buf, wd_buf, gsem, ssem, wsem):
    i = pl.program_id(0)
    nb = pl.num_programs(0)
    slot = i % 2

    def start_gather(idx_ref, s):
        def body(j, carry):
            pltpu.make_async_copy(v_hbm.at[pl.ds(idx_ref[0, 0, j], 1)], xbuf.at[s, pl.ds(j, 1)],
                                  gsem.at[s]).start()
            return carry
        lax.fori_loop(0, MOE_BLOCK, body, 0)

    def wait_gather(s):
        pltpu.make_async_copy(v_hbm.at[pl.ds(0, MOE_BLOCK)], xbuf.at[s], gsem.at[s]).wait()

    def scatter_row(s, j):
        return pltpu.make_async_copy(ybuf.at[s, pl.ds(j, 1)], out_hbm.at[pl.ds(dst_ref[0, 0, j], 1)],
                                     ssem.at[s])

    def start_scatter(s, n):
        def body(j, carry):
            scatter_row(s, j).start()
            return carry
        lax.fori_loop(0, n, body, 0)

    def wait_scatter(s, n):
        def body(j, carry):
            pltpu.make_async_copy(ybuf.at[s, pl.ds(0, 1)], out_hbm.at[pl.ds(0, 1)], ssem.at[s]).wait()
            return carry
        lax.fori_loop(0, n, body, 0)

    @pl.when(i == 0)
    def _():
        start_gather(tok_ref, 0)

    @pl.when(i + 1 < nb)
    def _():
        start_gather(tokn_ref, 1 - slot)

    e = be_ref[i]
    nv = nv_ref[i]

    @pl.when((nv > 0) & ((i == 0) | (e != be_ref[jnp.maximum(i - 1, 0)])))
    def _():
        copies = [pltpu.make_async_copy(wg_hbm.at[e], wg_buf, wsem.at[0]),
                  pltpu.make_async_copy(wu_hbm.at[e], wu_buf, wsem.at[1]),
                  pltpu.make_async_copy(wd_hbm.at[e], wd_buf, wsem.at[2])]
        for cp in copies:
            cp.start()
        for cp in copies:
            cp.wait()

    wait_gather(slot)

    @pl.when(i >= 2)
    def _():
        wait_scatter(slot, nv_ref[jnp.maximum(i - 2, 0)])

    @pl.when(nv > 0)
    def _():
        x = xbuf[slot].astype(BF16)
        hg = jnp.dot(x, wg_buf[...], preferred_element_type=F32)
        hu = jnp.dot(x, wu_buf[...], preferred_element_type=F32)
        ybuf[slot] = _dot(_silu(hg) * hu, wd_buf[...]) * sw_ref[...]
        start_scatter(slot, nv)

    @pl.when(i == nb - 1)
    def _():
        wait_scatter(slot, nv)

        @pl.when(nb > 1)
        def _():
            wait_scatter(1 - slot, nv_ref[jnp.maximum(i - 1, 0)])


def moe_experts(v_rows, block_expert, block_valid, slot_tok, slot_dst, slot_w, wg, wu, wd, n_out):
    P = slot_tok.shape[0]
    D = v_rows.shape[1]
    nb = P // MOE_BLOCK
    idx = lambda a: a.reshape(nb, 1, MOE_BLOCK)
    smem = lambda imap: pl.BlockSpec((1, 1, MOE_BLOCK), imap, memory_space=pltpu.SMEM)
    hbm = pl.BlockSpec(memory_space=pl.ANY)
    grid_spec = pltpu.PrefetchScalarGridSpec(
        num_scalar_prefetch=2,
        grid=(nb,),
        in_specs=[smem(lambda i, be, nv: (i, 0, 0)),
                  smem(lambda i, be, nv: (jnp.minimum(i + 1, nb - 1), 0, 0)),
                  smem(lambda i, be, nv: (i, 0, 0)),
                  pl.BlockSpec((MOE_BLOCK, 1), lambda i, be, nv: (i, 0)),
                  hbm, hbm, hbm, hbm],
        out_specs=hbm,
        scratch_shapes=[pltpu.VMEM((2, MOE_BLOCK, D), F32),
                        pltpu.VMEM((2, MOE_BLOCK, D), F32),
                        pltpu.VMEM((D, EXPERT_FF), BF16),
                        pltpu.VMEM((D, EXPERT_FF), BF16),
                        pltpu.VMEM((EXPERT_FF, D), BF16),
                        pltpu.SemaphoreType.DMA((2,)),
                        pltpu.SemaphoreType.DMA((2,)),
                        pltpu.SemaphoreType.DMA((3,))],
    )
    return pl.pallas_call(
        _moe_kernel,
        grid_spec=grid_spec,
        out_shape=jax.ShapeDtypeStruct((n_out, D), F32),
        compiler_params=_cp("arbitrary"),
        name="moe_experts",
    )(block_expert, block_valid, idx(slot_tok), idx(slot_tok), idx(slot_dst), slot_w.reshape(P, 1),
      v_rows, wg, wu, wd)


def _moe_combine_kernel(h_ref, f0_ref, f1_ref, gate_ref, o_ref):
    o_ref[0] = h_ref[0] + gate_ref[0] * (f0_ref[...] + f1_ref[...])


def moe_combine(h, out2, gate, row_offset, n_rows):
    B, T, D = h.shape
    tm = 256
    nt = T // tm
    blk0 = row_offset // tm
    tok = pl.BlockSpec((1, tm, D), lambda b, i: (b, i, 0))
    return pl.pallas_call(
        _moe_combine_kernel,
        grid=(B, nt),
        in_specs=[tok,
                  pl.BlockSpec((tm, D), lambda b, i: (blk0 + b * nt + i, 0)),
                  pl.BlockSpec((tm, D), lambda b, i: (n_rows // tm + blk0 + b * nt + i, 0)),
                  pl.BlockSpec((1, 1, D), lambda b, i: (b, 0, 0))],
        out_specs=tok,
        out_shape=jax.ShapeDtypeStruct((B, T, D), F32),
        compiler_params=_cp("parallel", "parallel"),
        name="moe_combine",
    )(h, out2, out2, gate)


def moe_dispatch(ids, wts):
    N = ids.shape[0]
    A = N * TOP_K
    e_flat = ids.reshape(A)
    order = jnp.argsort(e_flat)
    e_sorted = e_flat[order]
    counts = jnp.bincount(e_flat, length=N_EXPERTS)
    padded = (counts + MOE_BLOCK - 1) // MOE_BLOCK * MOE_BLOCK
    ends = jnp.cumsum(padded)
    slot = (ends - padded)[e_sorted] + jnp.arange(A) - (jnp.cumsum(counts) - counts)[e_sorted]
    n_blocks = -(-(A + N_EXPERTS * (MOE_BLOCK - 1)) // MOE_BLOCK)
    P = n_blocks * MOE_BLOCK
    tok = (order // TOP_K).astype(jnp.int32)
    dst = ((order % TOP_K) * N).astype(jnp.int32) + tok
    slot_tok = jnp.zeros((P,), jnp.int32).at[slot].set(tok)
    slot_dst = jnp.zeros((P,), jnp.int32).at[slot].set(dst)
    slot_w = jnp.zeros((P,), F32).at[slot].set(wts.reshape(A)[order])
    block_expert = jnp.minimum(jnp.searchsorted(ends, jnp.arange(n_blocks) * MOE_BLOCK, side="right"),
                               N_EXPERTS - 1).astype(jnp.int32)
    real = jnp.zeros((P,), jnp.int32).at[slot].set(1)
    block_valid = jnp.sum(real.reshape(n_blocks, MOE_BLOCK), axis=1).astype(jnp.int32)
    return slot_tok, slot_dst, slot_w, block_expert, block_valid


def hier_moe(v_rows, ids, wts, wg, wu, wd):
    N = v_rows.shape[0]
    slot_tok, slot_dst, slot_w, block_expert, block_valid = moe_dispatch(ids, wts)
    return moe_experts(v_rows, block_expert, block_valid, slot_tok, slot_dst, slot_w, wg, wu, wd, TOP_K * N)


def _layer_weights(l, P):
    w = {}
    w_in = P["w_in"][l]
    o = 0
    a = w_in[:, o:o + RWKV_COLS]; o += RWKV_COLS
    b = w_in[:, o:o + GROUP_W]; o += GROUP_W
    c = w_in[:, o:o + MLSTM_COLS]; o += MLSTM_COLS
    d = w_in[:, o:o + NA_COLS]
    D = w_in.shape[0]
    pad_a = RWKV_PAD_COLS - RWKV_COLS
    w["in_a"] = jnp.concatenate([a, jnp.zeros((D, pad_a), F32)], axis=1).astype(BF16)
    w["in_b"] = b.astype(BF16)
    w["in_c"] = jnp.concatenate([c, jnp.zeros((D, MLSTM_PAD_COLS - MLSTM_COLS), F32)], axis=1).astype(BF16)
    w["in_d"] = d.astype(BF16)
    w["out"] = P["w_out"][l].astype(BF16)

    z = jnp.zeros((RWKV_DECAY_RANK, GROUP_W), F32)
    wup, aup = P["rwkv_w_up"][l], P["rwkv_a_up"][l]
    rw = {
        "mu": jnp.concatenate([P["rwkv_mu"][l], jnp.zeros((2, pad_a), F32)], axis=1),
        "w_up": jnp.concatenate([jnp.concatenate([wup[0], z], axis=1),
                                 jnp.concatenate([z, wup[1]], axis=1)], axis=0).astype(BF16),
        "a_up": jnp.concatenate([jnp.concatenate([aup[0], z], axis=1),
                                 jnp.concatenate([z, aup[1]], axis=1)], axis=0).astype(BF16),
        "w0": P["rwkv_w0"][l].reshape(1, 2 * GROUP_W),
        "a0": P["rwkv_a0"][l].reshape(1, 2 * GROUP_W),
        "g_up": jnp.concatenate([P["rwkv_g_up"][l],
                                 jnp.zeros((256 - RWKV_GATE_RANK, GROUP_W), F32)], axis=0).astype(BF16),
        "k_k": P["rwkv_k_k"][l].reshape(1, GROUP_W),
        "k_a": P["rwkv_k_a"][l].reshape(1, GROUP_W),
        "r_k": P["rwkv_r_k"][l].reshape(1, GROUP_W),
    }
    w["rwkv"] = rw
    w["rwkv_ln"] = P["rwkv_ln"][l]
    w["pool_w"] = P["pool_w"][l].astype(BF16)
    w["pool_scale"] = P["pool_scale"][l]
    w["mlstm_conv"] = P["mlstm_conv"][l]
    gb = P["mlstm_gate_b"][l].reshape(1, 4 * MLSTM_HEADS)
    w["mlstm_gate_b"] = jnp.concatenate([gb, jnp.zeros((1, LANES - 4 * MLSTM_HEADS), F32)], axis=1)
    w["mlstm_norm"] = P["mlstm_norm"][l]
    w["na_qk_norm"] = P["na_qk_norm"][l]
    w["na_rpb"] = P["na_rpb"][l]
    rcat = jnp.concatenate([P["router_g_w"][l], P["router_e_w"][l],
                            jnp.zeros((D, LANES - N_GROUPS - N_EXPERTS), F32)], axis=1)
    rhi = rcat.astype(BF16)
    w["router_w"] = jnp.stack([rhi, (rcat - rhi.astype(F32)).astype(BF16)])
    w["router_b"] = jnp.concatenate([P["router_g_b"][l], P["router_e_b"][l],
                                     jnp.zeros((LANES - N_GROUPS - N_EXPERTS,), F32)]).reshape(1, LANES)
    w["moe_gate"] = P["moe_w_gate"][l].astype(BF16)
    w["moe_up"] = P["moe_w_up"][l].astype(BF16)
    w["moe_down"] = P["moe_w_down"][l].astype(BF16)
    return w


def rope_tables(T):
    t = jnp.arange(T)
    row = (t // GRID_W).astype(F32)
    col = (t % GRID_W).astype(F32)
    n_pairs = MLSTM_DK // 4
    inv = ROPE_BASE ** (-jnp.arange(n_pairs, dtype=F32) / n_pairs)
    ang = jnp.concatenate([row[:, None] * inv, col[:, None] * inv], axis=-1)
    cos = jnp.repeat(jnp.cos(ang), 2, axis=1)
    sin = jnp.repeat(jnp.sin(ang), 2, axis=1) * jnp.tile(jnp.array([-1.0, 1.0], F32), MLSTM_DK // 2)
    return cos, sin


def rwkv_mixer(p_ctx, p_lat, w, need_ctx):
    B = p_lat.shape[0]
    names = ("r", "kk", "v", "bonus", "g", "lw", "kd", "bb")
    fc = dict(zip(names, rwkv_prep(p_ctx, w["rwkv"])))
    fl = dict(zip(names, rwkv_prep(p_lat, w["rwkv"])))
    s0 = jnp.zeros((2, B, RWKV_PAIRS, LANES, LANES), F32)
    y_c, s_ctx = rwkv_scan(fc, s0)
    y_l, _ = rwkv_scan(fl, s_ctx)
    out_l = rwkv_readout(y_l, fl["bonus"], fl["g"], w["rwkv_ln"])
    out_c = rwkv_readout(y_c, fc["bonus"], fc["g"], w["rwkv_ln"]) if need_ctx else None
    return out_c, out_l


def mlstm_mixer(p_ctx, p_lat, w, rope, need_ctx):
    B = p_lat.shape[0]
    H = MLSTM_HEADS
    state = (jnp.zeros((2, B, H, MLSTM_DK, MLSTM_DV), F32),
             jnp.zeros((2, B, H, 1, MLSTM_DK), F32),
             jnp.zeros((2, B, H, 1, 1), F32))
    outs = []
    for p, tabs in ((p_ctx, None), (p_lat, rope)):
        q, k, gates = mlstm_prep(p, w["mlstm_conv"], w["mlstm_gate_b"], tabs)
        gates_t = jnp.swapaxes(gates[:, :, :4 * H], 1, 2)
        h, *state = mlstm_scan(q, k, p, gates, gates_t, tuple(state))
        outs.append(h)
    out_l = mlstm_readout(outs[1], p_lat, w["mlstm_norm"])
    out_c = mlstm_readout(outs[0], p_ctx, w["mlstm_norm"]) if need_ctx else None
    return out_c, out_l


def na_mixer(p_ctx, p_lat, w, need_ctx):
    qc, kc, vc = na_prep(p_ctx, w["na_qk_norm"])
    ql, kl, vl = na_prep(p_lat, w["na_qk_norm"])
    rows = p_lat.shape[1] // GRID_W
    bias = na_bias_table(w["na_rpb"], rows)
    out_l = na_lat_attention(ql, kl, vl, kc, vc, bias)
    out_c = na_ctx_attention(qc, kc, vc) if need_ctx else None
    return out_c, out_l


def token_mixers(u_ctx, u_lat, w, rope, need_ctx):
    outs_c, outs_l = [], []
    pa = (matmul(u_ctx, w["in_a"]), matmul(u_lat, w["in_a"]))
    oc, ol = rwkv_mixer(pa[0], pa[1], w, need_ctx)
    outs_c.append(oc); outs_l.append(ol)
    pb = (matmul(u_ctx, w["in_b"]), matmul(u_lat, w["in_b"]))
    outs_l.append(pool_mixer(pb[1], w["pool_w"], w["pool_scale"]))
    outs_c.append(pool_mixer(pb[0], w["pool_w"], w["pool_scale"]) if need_ctx else None)
    pc = (matmul(u_ctx, w["in_c"]), matmul(u_lat, w["in_c"]))
    oc, ol = mlstm_mixer(pc[0], pc[1], w, rope, need_ctx)
    outs_c.append(oc); outs_l.append(ol)
    pd = (matmul(u_ctx, w["in_d"]), matmul(u_lat, w["in_d"]))
    oc, ol = na_mixer(pd[0], pd[1], w, need_ctx)
    outs_c.append(oc); outs_l.append(ol)
    return outs_c, outs_l


def kernel(x, c, ctx, c_ctx, ada_w, ada_b, norm_mix, norm_ffn, w_in, w_out, rwkv_mu, rwkv_w0, rwkv_w_up, rwkv_a0, rwkv_a_up, rwkv_g_up, rwkv_k_k, rwkv_k_a, rwkv_r_k, rwkv_ln, pool_w, pool_scale, mlstm_conv, mlstm_gate_b, mlstm_norm, na_qk_norm, na_rpb, router_g_w, router_g_b, router_e_w, router_e_b, moe_w_gate, moe_w_up, moe_w_down):
    P = dict(w_in=w_in, w_out=w_out, rwkv_mu=rwkv_mu, rwkv_w0=rwkv_w0, rwkv_w_up=rwkv_w_up,
             rwkv_a0=rwkv_a0, rwkv_a_up=rwkv_a_up, rwkv_g_up=rwkv_g_up, rwkv_k_k=rwkv_k_k,
             rwkv_k_a=rwkv_k_a, rwkv_r_k=rwkv_r_k, rwkv_ln=rwkv_ln, pool_w=pool_w, pool_scale=pool_scale,
             mlstm_conv=mlstm_conv, mlstm_gate_b=mlstm_gate_b, mlstm_norm=mlstm_norm,
             na_qk_norm=na_qk_norm, na_rpb=na_rpb, router_g_w=router_g_w, router_g_b=router_g_b,
             router_e_w=router_e_w, router_e_b=router_e_b, moe_w_gate=moe_w_gate, moe_w_up=moe_w_up,
             moe_w_down=moe_w_down)
    B, T, D = x.shape
    Lc = ctx.shape[1]
    depth = ada_w.shape[0]
    rope = rope_tables(T)

    cvec = jnp.concatenate([c, c_ctx[None, :], jnp.zeros((SUBLANES - B - 1, D), F32)], axis=0)
    mods = ada_modulation(cvec, ada_w, ada_b)

    h_lat, h_ctx = x, ctx
    for l in range(depth):
        last = l == depth - 1
        w = _layer_weights(l, P)
        m = mods[l].reshape(SUBLANES, 6, D)
        m_lat = [m[:B, i][:, None, :] for i in range(6)]
        m_ctx = [jnp.broadcast_to(m[B, i][None, None, :], (B, 1, D)) for i in range(6)]

        u_lat = norm_modulate(h_lat, norm_mix[l], m_lat[0], m_lat[1])
        u_ctx = norm_modulate(h_ctx, norm_mix[l], m_ctx[0], m_ctx[1])
        ys_ctx, ys_lat = token_mixers(u_ctx, u_lat, w, rope, not last)
        h_lat = out_proj_residual(ys_lat, w["out"], h_lat, m_lat[2])
        n_rows = B * T if last else B * (T + Lc)
        streams = [(h_lat, m_lat[3], m_lat[4])]
        if not last:
            h_ctx = out_proj_residual(ys_ctx, w["out"], h_ctx, m_ctx[2])
            streams.append((h_ctx, m_ctx[3], m_ctx[4]))
        rows, ids, wts = ffn_norm_router(streams, norm_ffn[l], w["router_w"], w["router_b"])
        f2 = hier_moe(rows, ids[:, :TOP_K], wts[:, :TOP_K], w["moe_gate"], w["moe_up"], w["moe_down"])
        if not last:
            h_ctx = moe_combine(h_ctx, f2, m_ctx[5], B * T, n_rows)
        h_lat = moe_combine(h_lat, f2, m_lat[5], 0, n_rows)
    return h_lat
```

```python
import functools
import math

import numpy as np
import jax
import jax.numpy as jnp
from jax import lax
from jax.experimental import pallas as pl
from jax.experimental.pallas import tpu as pltpu

F32 = jnp.float32
BF16 = jnp.bfloat16

D_MODEL = 4096
DEPTH = 2
GRID_W = 64
EPS = 1e-6
GROUP_W = 1024

RWKV_HEAD = 64
RWKV_DECAY_RANK = 64
RWKV_ICLR_RANK = 64
RWKV_GATE_RANK = 160
RWKV_GN_EPS = 64e-5
RWKV_COLS = 3 * GROUP_W + 2 * RWKV_DECAY_RANK + 2 * RWKV_ICLR_RANK + RWKV_GATE_RANK
RWKV_PAD_COLS = 3 * GROUP_W + 256 + 256

POOL_WINDOWS = (2, 4, 8, 16)
POOL_GROUP = 256
POOL_HALO = 8

MLSTM_HEADS = 4
MLSTM_DV = 256
MLSTM_DK = 128
MLSTM_COLS = 2 * MLSTM_HEADS * MLSTM_DK + 2 * GROUP_W + 4 * MLSTM_HEADS
MLSTM_PAD_COLS = 2 * MLSTM_HEADS * MLSTM_DK + 2 * GROUP_W + 128
ROPE_BASE = 10000.0

NA_HEADS = 8
NA_HEAD = 128
WIN_H = 8
WIN_W = 16
NA_COLS = 3 * GROUP_W
NA_QROWS = 4
NA_KROWS = 3 * NA_QROWS

N_GROUPS = 4
EXPERTS_PER_GROUP = 4
N_EXPERTS = 16
TOP_K = 2
EXPERT_FF = 1024
MOE_BLOCK = 256
MOE_DMA_UNROLL = 8

LANES = 128
SUBLANES = 8
VMEM_LIMIT = 56 * 1024 * 1024
NEG = -1e30

RWKV_CHUNK = 64
RWKV_PAIRS = 8
MLSTM_CHUNK = 256


def _cp(*sem):
    return pltpu.CompilerParams(dimension_semantics=sem, vmem_limit_bytes=VMEM_LIMIT)


def _dot(a, b):
    return jnp.dot(a.astype(BF16), b.astype(BF16), preferred_element_type=F32)


def _dot_nt(a, b):
    return lax.dot_general(a.astype(BF16), b.astype(BF16), (((1,), (1,)), ((), ())),
                           preferred_element_type=F32)


def _dot_tn(a, b):
    return lax.dot_general(a.astype(BF16), b.astype(BF16), (((0,), (0,)), ((), ())),
                           preferred_element_type=F32)


def _split2(x):
    hi = x.astype(BF16)
    lo = (x - hi.astype(F32)).astype(BF16)
    return hi, lo


def _dot3(a, b, dot=_dot):
    ah, al = _split2(a)
    bh, bl = _split2(b)
    return dot(ah, bh) + (dot(ah, bl) + dot(al, bh))


def _dot_exact_lhs(a_bf16, b):
    b0 = b.astype(BF16)
    r1 = b - b0.astype(F32)
    b1 = r1.astype(BF16)
    b2 = (r1 - b1.astype(F32)).astype(BF16)
    return _dot(a_bf16, b0) + (_dot(a_bf16, b1) + _dot(a_bf16, b2))


def _sigmoid(x):
    return 1.0 / (1.0 + jnp.exp(-x))


def _silu(x):
    return x * _sigmoid(x)


def _log_sigmoid(x):
    return jnp.minimum(x, 0.0) - jnp.log(1.0 + jnp.exp(-jnp.abs(x)))


def _head_sum(x, width):
    r = lax.broadcasted_iota(jnp.int32, (LANES, LANES), 0) // width
    c = lax.broadcasted_iota(jnp.int32, (LANES, LANES), 1) // width
    e = jnp.where(r == c, 1.0, 0.0).astype(BF16)
    outs = []
    for j in range(x.shape[1] // LANES):
        hi, lo = _split2(x[:, j * LANES:(j + 1) * LANES])
        outs.append(_dot(hi, e) + _dot(lo, e))
    return jnp.concatenate(outs, axis=1)


def _ada_kernel(c_ref, w_ref, b_ref, o_ref):
    s = _silu(c_ref[...])
    o_ref[0] = _dot(s, w_ref[0]) + b_ref[0]


def ada_modulation(cvec, ada_w, ada_b):
    L, D, N = ada_w.shape
    tn = 512
    return pl.pallas_call(
        _ada_kernel,
        grid=(L, N // tn),
        in_specs=[pl.BlockSpec((SUBLANES, D), lambda l, j: (0, 0)),
                  pl.BlockSpec((1, D, tn), lambda l, j: (l, 0, j)),
                  pl.BlockSpec((1, 1, tn), lambda l, j: (l, 0, j))],
        out_specs=pl.BlockSpec((1, SUBLANES, tn), lambda l, j: (l, 0, j)),
        out_shape=jax.ShapeDtypeStruct((L, SUBLANES, N), F32),
        compiler_params=_cp("parallel", "parallel"),
        name="ada_modulation",
    )(cvec, ada_w, ada_b.reshape(L, 1, N))


def _normmod_kernel(h_ref, g_ref, sh_ref, sc_ref, o_ref):
    x = h_ref[0]
    y = x * lax.rsqrt(jnp.mean(x * x, axis=-1, keepdims=True) + EPS) * g_ref[...]
    o_ref[0] = (y * (1.0 + sc_ref[0]) + sh_ref[0]).astype(o_ref.dtype)


def norm_modulate(h, g, shift, scale):
    B, T, D = h.shape
    tm = 256
    vec = pl.BlockSpec((1, 1, D), lambda b, i: (b, 0, 0))
    return pl.pallas_call(
        _normmod_kernel,
        grid=(B, T // tm),
        in_specs=[pl.BlockSpec((1, tm, D), lambda b, i: (b, i, 0)),
                  pl.BlockSpec((1, D), lambda b, i: (0, 0)), vec, vec],
        out_specs=pl.BlockSpec((1, tm, D), lambda b, i: (b, i, 0)),
        out_shape=jax.ShapeDtypeStruct((B, T, D), BF16),
        compiler_params=_cp("parallel", "parallel"),
        name="norm_modulate",
    )(h, g.reshape(1, D), shift, scale)


def _mm_kernel(x_ref, w_ref, o_ref):
    o_ref[0] = jnp.dot(x_ref[0], w_ref[...], preferred_element_type=F32).astype(o_ref.dtype)


def _pick_tile(n, cands):
    for t in cands:
        if n % t == 0:
            return t
    raise ValueError(n)


def matmul(x, w, out_dtype=F32):
    B, T, K = x.shape
    N = w.shape[1]
    tm = _pick_tile(T, (512, 256))
    tn = _pick_tile(N, (1024, 896, 768, 640, 512))
    return pl.pallas_call(
        _mm_kernel,
        grid=(B, T // tm, N // tn),
        in_specs=[pl.BlockSpec((1, tm, K), lambda b, i, j: (b, i, 0)),
                  pl.BlockSpec((K, tn), lambda b, i, j: (0, j))],
        out_specs=pl.BlockSpec((1, tm, tn), lambda b, i, j: (b, i, j)),
        out_shape=jax.ShapeDtypeStruct((B, T, N), out_dtype),
        compiler_params=_cp("parallel", "parallel", "arbitrary"),
        name="in_proj",
    )(x, w)


def _outproj_kernel(ya_ref, yb_ref, yc_ref, yd_ref, w_ref, h_ref, gate_ref, o_ref):
    acc = jnp.dot(ya_ref[0], w_ref[0:GROUP_W, :], preferred_element_type=F32)
    acc += jnp.dot(yb_ref[0], w_ref[GROUP_W:2 * GROUP_W, :], preferred_element_type=F32)
    acc += jnp.dot(yc_ref[0], w_ref[2 * GROUP_W:3 * GROUP_W, :], preferred_element_type=F32)
    acc += jnp.dot(yd_ref[0], w_ref[3 * GROUP_W:4 * GROUP_W, :], preferred_element_type=F32)
    o_ref[0] = h_ref[0] + gate_ref[0] * acc


def out_proj_residual(ys, w, h, gate):
    B, T, D = h.shape
    tm = _pick_tile(T, (512, 256))
    tn = 512
    ysp = pl.BlockSpec((1, tm, GROUP_W), lambda b, i, j: (b, i, 0))
    return pl.pallas_call(
        _outproj_kernel,
        grid=(B, T // tm, D // tn),
        in_specs=[ysp, ysp, ysp, ysp,
                  pl.BlockSpec((D, tn), lambda b, i, j: (0, j)),
                  pl.BlockSpec((1, tm, tn), lambda b, i, j: (b, i, j)),
                  pl.BlockSpec((1, 1, tn), lambda b, i, j: (b, 0, j))],
        out_specs=pl.BlockSpec((1, tm, tn), lambda b, i, j: (b, i, j)),
        out_shape=jax.ShapeDtypeStruct((B, T, D), F32),
        compiler_params=_cp("parallel", "parallel", "arbitrary"),
        name="out_proj",
    )(*ys, w, h, gate)


def _halo_specs(tm, width, halo=SUBLANES):
    per = tm // halo

    def prev_map(b, i):
        return (b, jnp.maximum(i * per - 1, 0), 0)

    def next_map(nb):
        return lambda b, i: (b, jnp.minimum((i + 1) * per, nb - 1), 0)

    return prev_map, next_map


def _shift_rows(x, prev_row, next_row):
    tm = x.shape[0]
    row = lax.broadcasted_iota(jnp.int32, x.shape, 0)
    prev = jnp.where(row == 0, prev_row, pltpu.roll(x, 1, axis=0))
    nxt = jnp.where(row == tm - 1, next_row, pltpu.roll(x, tm - 1, axis=0))
    return prev, nxt


def _rwkv_prep_kernel(p_ref, pp_ref, pn_ref, mu_ref, wup_ref, w0_ref, aup_ref, a0_ref, gup_ref,
                      kk_w_ref, ka_w_ref, rk_w_ref,
                      r_ref, kk_ref, v_ref, bonus_ref, g_ref, lw_ref, kd_ref, bb_ref):
    i = pl.program_id(1)
    last = pl.num_programs(1) - 1
    x = p_ref[0]
    prev_row = jnp.where(i == 0, 0.0, pp_ref[0, SUBLANES - 1:SUBLANES, :])
    next_row = jnp.where(i == last, 0.0, pn_ref[0, 0:1, :])
    prev, nxt = _shift_rows(x, prev_row, next_row)
    x = x + mu_ref[0:1, :] * (prev - x) + mu_ref[1:2, :] * (nxt - x)

    r = x[:, 0:GROUP_W]
    k = x[:, GROUP_W:2 * GROUP_W]
    v = x[:, 2 * GROUP_W:3 * GROUP_W]
    low = x[:, 3 * GROUP_W:3 * GROUP_W + 256]
    gd = x[:, 3 * GROUP_W + 256:3 * GROUP_W + 512]

    zw = _dot(jnp.tanh(low[:, 0:LANES]), wup_ref[...]) + w0_ref[...]
    za = _dot(low[:, LANES:2 * LANES], aup_ref[...]) + a0_ref[...]
    kkr = k * kk_w_ref[...]
    kk = kkr * lax.rsqrt(_head_sum(kkr * kkr, RWKV_HEAD) + EPS)
    r_ref[0] = r
    kk_ref[0] = kk
    v_ref[0] = v
    bonus_ref[0] = _head_sum(r * k * rk_w_ref[...], RWKV_HEAD) * v
    g_ref[0] = _dot(_sigmoid(gd), gup_ref[...])
    for d in range(2):
        sl = slice(d * GROUP_W, (d + 1) * GROUP_W)
        lw_ref[d, 0] = (-math.exp(-0.5)) * _sigmoid(zw[:, sl])
        a = _sigmoid(za[:, sl])
        kd_ref[d, 0] = k * (1.0 + (a - 1.0) * ka_w_ref[...])
        bb_ref[d, 0] = kk * a


def rwkv_prep(p, w):
    B, T, W = p.shape
    tm = 128
    prev_map, next_map = _halo_specs(tm, W)
    nb8 = T // SUBLANES
    full = lambda shape: pl.BlockSpec(shape, lambda b, i: tuple(0 for _ in shape))
    tok = pl.BlockSpec((1, tm, GROUP_W), lambda b, i: (b, i, 0))
    tokd = pl.BlockSpec((2, 1, tm, GROUP_W), lambda b, i: (0, b, i, 0))
    sh = jax.ShapeDtypeStruct((B, T, GROUP_W), F32)
    shd = jax.ShapeDtypeStruct((2, B, T, GROUP_W), F32)
    return pl.pallas_call(
        _rwkv_prep_kernel,
        grid=(B, T // tm),
        in_specs=[pl.BlockSpec((1, tm, W), lambda b, i: (b, i, 0)),
                  pl.BlockSpec((1, SUBLANES, W), prev_map),
                  pl.BlockSpec((1, SUBLANES, W), next_map(nb8)),
                  full((2, W)), full((LANES, 2 * GROUP_W)), full((1, 2 * GROUP_W)),
                  full((LANES, 2 * GROUP_W)), full((1, 2 * GROUP_W)), full((256, GROUP_W)),
                  full((1, GROUP_W)), full((1, GROUP_W)), full((1, GROUP_W))],
        out_specs=[tok, tok, tok, tok, tok, tokd, tokd, tokd],
        out_shape=[sh, sh, sh, sh, sh, shd, shd, shd],
        compiler_params=_cp("parallel", "parallel"),
        name="rwkv_prep",
    )(p, p, p, w["mu"], w["w_up"], w["w0"], w["a_up"], w["a0"], w["g_up"],
      w["k_k"], w["k_a"], w["r_k"])


def _dform(x, m0):
    return jnp.concatenate([jnp.where(m0, x, 0.0), jnp.where(m0, 0.0, x)], axis=0)


def _rwkv_chunk(tiles, sts, rev):
    n = len(tiles)
    C = tiles[0][0].shape[0]
    C2 = 2 * C
    sgn = jnp.where(rev, -1, 1)
    rr = lax.broadcasted_iota(jnp.int32, (C, C), 0)
    cc = lax.broadcasted_iota(jnp.int32, (C, C), 1)
    tri = jnp.where((rr - cc) * sgn >= 0, 1.0, 0.0).astype(BF16)
    lane = lax.broadcasted_iota(jnp.int32, (C, LANES), 1)
    m0 = lane < RWKV_HEAD
    rd = lax.broadcasted_iota(jnp.int32, (C2, C2), 0)
    cd = lax.broadcasted_iota(jnp.int32, (C2, C2), 1)
    same = (rd // C) == (cd // C)
    ahead = (rd % C - cd % C) * sgn
    strict = same & (ahead > 0)
    incl = same & (ahead >= 0)
    diag = rd == cd
    each = range(n)

    lins = [_dot_exact_lhs(tri, t[3]) for t in tiles]
    ltots = [jnp.sum(t[3], axis=0, keepdims=True) for t in tiles]
    a_d, r_d, k_d, b_d, v_d, kw_d, bw_d = [], [], [], [], [], [], []
    for (r, kk, v, lw, kd, bb), lin, ltot in zip(tiles, lins, ltots):
        einv = jnp.exp(-lin)
        ew = jnp.exp(ltot - lin)
        a_d.append(_dform(-kk * jnp.exp(lin - lw), m0).astype(BF16))
        r_d.append(_dform(r * jnp.exp(lin), m0).astype(BF16))
        k_d.append(_dform(kd * einv, m0).astype(BF16))
        b_d.append(_dform(bb * einv, m0).astype(BF16))
        v_d.append(_dform(v, m0).astype(BF16))
        kw_d.append(_dform(kd * ew, m0).astype(BF16))
        bw_d.append(_dform(bb * ew, m0).astype(BF16))

    grams = [_dot_nt(jnp.concatenate([a_d[i], r_d[i]], axis=0), jnp.concatenate([k_d[i], b_d[i]], axis=0))
             for i in each]
    a_ak = [jnp.where(strict, g[:C2, :C2], 0.0).astype(BF16) for g in grams]
    a_ab = [jnp.where(strict, g[:C2, C2:], 0.0) for g in grams]
    a_rk = [jnp.where(incl, g[C2:, :C2], 0.0).astype(BF16) for g in grams]
    a_rb = [jnp.where(incl, g[C2:, C2:], 0.0).astype(BF16) for g in grams]

    tinv = [jnp.where(diag, 1.0, 0.0) + a for a in a_ab]
    apow = a_ab
    for _ in range(int(math.log2(C)) - 1):
        apow = [_dot(a, a) for a in apow]
        tinv = [t + _dot(t, a) for t, a in zip(tinv, apow)]

    akv = [_dot(a_ak[i], v_d[i]) for i in each]
    p = [_dot(tinv[i], jnp.concatenate([a_d[i], akv[i].astype(BF16)], axis=1)) for i in each]
    ur = [_dot(jnp.concatenate([p[i][:, :LANES].astype(BF16), r_d[i]], axis=0), sts[i]) for i in each]
    u = [ur[i][:C2] + p[i][:, LANES:] for i in each]
    y2 = [ur[i][C2:] + _dot(a_rk[i], v_d[i]) + _dot(a_rb[i], u[i]) for i in each]
    ys = [y[:C] + y[C:] for y in y2]
    new = []
    for i in each:
        wc = jnp.broadcast_to(jnp.exp(ltots[i]), (LANES, LANES))
        wcol = jnp.sum(jnp.where(diag, wc, 0.0), axis=1, keepdims=True)
        new.append(sts[i] * wcol + _dot_tn(jnp.concatenate([bw_d[i], kw_d[i]], axis=0),
                                           jnp.concatenate([u[i].astype(BF16), v_d[i]], axis=0)))
    return ys, new


def _rwkv_scan_kernel(r_ref, kk_ref, v_ref, lw_ref, kd_ref, bb_ref, s0_ref, y_ref, sT_ref, st_scr,
                      *, npairs):
    d = pl.program_id(0)
    c = pl.program_id(3)
    nc = pl.num_programs(3)

    @pl.when(c == 0)
    def _():
        st_scr[...] = s0_ref[0, 0]

    lanes = [slice(j * LANES, (j + 1) * LANES) for j in range(npairs)]
    tiles = [(r_ref[0, :, sl], kk_ref[0, :, sl], v_ref[0, :, sl],
              lw_ref[0, 0, :, sl], kd_ref[0, 0, :, sl], bb_ref[0, 0, :, sl]) for sl in lanes]
    ys, new = _rwkv_chunk(tiles, [st_scr[j] for j in range(npairs)], d == 1)
    for j, sl in enumerate(lanes):
        y_ref[0, 0, :, sl] = ys[j]
        st_scr[j] = new[j]

    @pl.when(c == nc - 1)
    def _():
        sT_ref[0, 0] = st_scr[...]


def rwkv_scan(f, s0, npairs=RWKV_PAIRS):
    r, kk, v, lw, kd, bb = f["r"], f["kk"], f["v"], f["lw"], f["kd"], f["bb"]
    B, T, _ = r.shape
    C = RWKV_CHUNK
    nc = T // C
    ng = RWKV_PAIRS // npairs
    W = npairs * LANES
    chunk = lambda d, c: c + d * (nc - 1 - 2 * c)
    tok = pl.BlockSpec((1, C, W), lambda d, b, g, c: (b, chunk(d, c), g))
    tokd = pl.BlockSpec((1, 1, C, W), lambda d, b, g, c: (d, b, chunk(d, c), g))
    stsp = pl.BlockSpec((1, 1, npairs, LANES, LANES), lambda d, b, g, c: (d, b, g, 0, 0))
    return pl.pallas_call(
        functools.partial(_rwkv_scan_kernel, npairs=npairs),
        grid=(2, B, ng, nc),
        in_specs=[tok, tok, tok, tokd, tokd, tokd, stsp],
        out_specs=[tokd, stsp],
        out_shape=[jax.ShapeDtypeStruct((2, B, T, GROUP_W), F32),
                   jax.ShapeDtypeStruct(s0.shape, F32)],
        scratch_shapes=[pltpu.VMEM((npairs, LANES, LANES), F32)],
        compiler_params=_cp("parallel", "parallel", "parallel", "arbitrary"),
        name="rwkv_scan",
    )(r, kk, v, lw, kd, bb, s0)


def _rwkv_readout_kernel(yf_ref, yb_ref, bonus_ref, g_ref, ln_ref, o_ref):
    y = yf_ref[0, 0] + yb_ref[0, 0]
    mean = _head_sum(y, RWKV_HEAD) * (1.0 / RWKV_HEAD)
    cen = y - mean
    var = _head_sum(cen * cen, RWKV_HEAD) * (1.0 / RWKV_HEAD)
    yn = cen * lax.rsqrt(var + RWKV_GN_EPS) * ln_ref[0:1, :] + ln_ref[1:2, :]
    o_ref[0] = ((yn + bonus_ref[0]) * g_ref[0]).astype(o_ref.dtype)


def rwkv_readout(y, bonus, g, ln):
    _, B, T, W = y.shape
    tm = 256
    tok = pl.BlockSpec((1, tm, W), lambda b, i: (b, i, 0))
    return pl.pallas_call(
        _rwkv_readout_kernel,
        grid=(B, T // tm),
        in_specs=[pl.BlockSpec((1, 1, tm, W), lambda b, i: (0, b, i, 0)),
                  pl.BlockSpec((1, 1, tm, W), lambda b, i: (1, b, i, 0)),
                  tok, tok, pl.BlockSpec((2, W), lambda b, i: (0, 0))],
        out_specs=tok,
        out_shape=jax.ShapeDtypeStruct((B, T, W), BF16),
        compiler_params=_cp("parallel", "parallel"),
        name="rwkv_readout",
    )(y, y, bonus, g, ln)


def _pool_kernel(p_ref, pp_ref, pn_ref, w_ref, sc_ref, o_ref, *, total):
    i = pl.program_id(1)
    last = pl.num_programs(1) - 1
    x = p_ref[0]
    tm = x.shape[0]
    n = tm + 2 * POOL_HALO
    ext = jnp.concatenate([jnp.where(i == 0, 0.0, pp_ref[0]), x,
                           jnp.where(i == last, 0.0, pn_ref[0])], axis=0)
    t = i * tm + lax.broadcasted_iota(jnp.int32, (tm, 1), 0)
    outs = []
    for gi, win in enumerate(POOL_WINDOWS):
        sl = slice(gi * POOL_GROUP, (gi + 1) * POOL_GROUP)
        e = ext[:, sl]
        s = e + pltpu.roll(e, 1, axis=0)
        step = 1
        while 2 * step < win:
            s = pltpu.roll(s, step, axis=0) + pltpu.roll(s, n - step, axis=0)
            step *= 2
        h = win // 2
        cnt = (jnp.minimum(t + h, total) - jnp.maximum(t - h, 0)).astype(F32)
        z = s[POOL_HALO:POOL_HALO + tm] / cnt - x[:, sl]
        outs.append(_dot(z, w_ref[gi]))
    o_ref[0] = (jnp.concatenate(outs, axis=1) * sc_ref[...]).astype(o_ref.dtype)


def pool_mixer(p, pool_w, pool_scale):
    B, T, W = p.shape
    tm = 256
    prev_map, next_map = _halo_specs(tm, W)
    return pl.pallas_call(
        functools.partial(_pool_kernel, total=T),
        grid=(B, T // tm),
        in_specs=[pl.BlockSpec((1, tm, W), lambda b, i: (b, i, 0)),
                  pl.BlockSpec((1, POOL_HALO, W), prev_map),
                  pl.BlockSpec((1, POOL_HALO, W), next_map(T // POOL_HALO)),
                  pl.BlockSpec(pool_w.shape, lambda b, i: (0, 0, 0)),
                  pl.BlockSpec((1, W), lambda b, i: (0, 0))],
        out_specs=pl.BlockSpec((1, tm, W), lambda b, i: (b, i, 0)),
        out_shape=jax.ShapeDtypeStruct((B, T, W), BF16),
        compiler_params=_cp("parallel", "parallel"),
        name="pool_mixer",
    )(p, p, p, pool_w, pool_scale.reshape(1, W))


def _mlstm_prep_kernel(p_ref, pp_ref, pn_ref, g_ref, cw_ref, gb_ref, *rest, rope):
    if rope:
        cos_ref, sin_ref, q_ref, k_ref, go_ref = rest
    else:
        q_ref, k_ref, go_ref = rest
    i = pl.program_id(1)
    last = pl.num_programs(1) - 1
    x = p_ref[0]
    prev_row = jnp.where(i == 0, 0.0, pp_ref[0, SUBLANES - 1:SUBLANES, :])
    next_row = jnp.where(i == last, 0.0, pn_ref[0, 0:1, :])
    prev, nxt = _shift_rows(x, prev_row, next_row)
    qk = _silu(cw_ref[0:1, :] * prev + cw_ref[1:2, :] * x + cw_ref[2:3, :] * nxt)
    if rope:
        w = qk.shape[1]
        lane = lax.broadcasted_iota(jnp.int32, qk.shape, 1)
        partner = jnp.where(lane % 2 == 0, pltpu.roll(qk, w - 1, axis=1), pltpu.roll(qk, 1, axis=1))
        reps = w // LANES
        cos = jnp.concatenate([cos_ref[...]] * reps, axis=1)
        sin = jnp.concatenate([sin_ref[...]] * reps, axis=1)
        qk = qk * cos + partner * sin
    half = MLSTM_HEADS * MLSTM_DK
    q_ref[0] = qk[:, :half].astype(BF16)
    k_ref[0] = (qk[:, half:] * (MLSTM_DK ** -0.5)).astype(BF16)
    g = g_ref[0] + gb_ref[...]
    lane = lax.broadcasted_iota(jnp.int32, g.shape, 1)
    go_ref[0] = jnp.where((lane // MLSTM_HEADS) % 2 == 1, _log_sigmoid(g), g)


def mlstm_prep(p, conv_w, gate_b, rope_tabs):
    B, T, W = p.shape
    tm = 256
    QK = 2 * MLSTM_HEADS * MLSTM_DK
    prev_map, next_map = _halo_specs(tm, QK)
    rope = rope_tabs is not None
    in_specs = [pl.BlockSpec((1, tm, QK), lambda b, i: (b, i, 0)),
                pl.BlockSpec((1, SUBLANES, QK), prev_map),
                pl.BlockSpec((1, SUBLANES, QK), next_map(T // SUBLANES)),
                pl.BlockSpec((1, tm, LANES), lambda b, i: (b, i, (W - LANES) // LANES)),
                pl.BlockSpec((3, QK), lambda b, i: (0, 0)),
                pl.BlockSpec((1, LANES), lambda b, i: (0, 0))]
    args = [p, p, p, p, conv_w, gate_b]
    if rope:
        in_specs += [pl.BlockSpec((tm, LANES), lambda b, i: (i, 0))] * 2
        args += list(rope_tabs)
    half = MLSTM_HEADS * MLSTM_DK
    return pl.pallas_call(
        functools.partial(_mlstm_prep_kernel, rope=rope),
        grid=(B, T // tm),
        in_specs=in_specs,
        out_specs=[pl.BlockSpec((1, tm, half), lambda b, i: (b, i, 0)),
                   pl.BlockSpec((1, tm, half), lambda b, i: (b, i, 0)),
                   pl.BlockSpec((1, tm, LANES), lambda b, i: (b, i, 0))],
        out_shape=[jax.ShapeDtypeStruct((B, T, half), BF16),
                   jax.ShapeDtypeStruct((B, T, half), BF16),
                   jax.ShapeDtypeStruct((B, T, LANES), F32)],
        compiler_params=_cp("parallel", "parallel"),
        name="mlstm_prep",
    )(*args)


def _mlstm_chunk_kernel(q_ref, k_ref, v_ref, gc_ref, gr_ref, c0_ref, n0_ref, m0_ref,
                        h_ref, cT_ref, nT_ref, mT_ref, c_scr, n_scr, m_scr):
    d = pl.program_id(0)
    hh = pl.program_id(2)
    c = pl.program_id(3)
    nc = pl.num_programs(3)

    @pl.when(c == 0)
    def _():
        c_scr[...] = c0_ref[0, 0, 0]
        n_scr[...] = n0_ref[0, 0, 0]
        m_scr[...] = m0_ref[0, 0, 0]

    rev = d == 1
    q = q_ref[0]
    k = k_ref[0]
    v = v_ref[0]
    L = q.shape[0]
    rr = lax.broadcasted_iota(jnp.int32, (L, L), 0)
    cc = lax.broadcasted_iota(jnp.int32, (L, L), 1)
    sgn = jnp.where(rev, -1, 1)
    seen = (rr - cc) * sgn >= 0
    tri = jnp.where(seen, 1.0, 0.0).astype(BF16)
    tri_t = jnp.where((cc - rr) * sgn >= 0, 1.0, 0.0).astype(BF16)

    gc = gc_ref[0]
    gr = gr_ref[0]
    lane = lax.broadcasted_iota(jnp.int32, gc.shape, 1)
    subl = lax.broadcasted_iota(jnp.int32, gr.shape, 0)
    i_lane = d * (2 * MLSTM_HEADS) + hh
    f_lane = i_lane + MLSTM_HEADS
    pick_c = lambda a, idx: jnp.sum(jnp.where(lane == idx, a, 0.0), axis=1, keepdims=True)
    pick_r = lambda a, idx: jnp.sum(jnp.where(subl == idx, a, 0.0), axis=0, keepdims=True)
    ig_c = pick_c(gc, i_lane)
    lf_c = pick_c(gc, f_lane)
    b_c = pick_c(_dot_exact_lhs(tri, gc), f_lane)
    ig_r = pick_r(gr, i_lane)
    gr_hi = gr.astype(BF16)
    gr_r1 = gr - gr_hi.astype(F32)
    gr_mid = gr_r1.astype(BF16)
    gr_lo = (gr_r1 - gr_mid.astype(F32)).astype(BF16)
    b_r = pick_r(_dot(gr_hi, tri_t) + (_dot(gr_mid, tri_t) + _dot(gr_lo, tri_t)), f_lane)

    m_prev = m_scr[...]
    dlog = jnp.where(seen, b_c - b_r + ig_r, NEG)
    inter = b_c + m_prev
    m_t = jnp.maximum(inter, jnp.max(dlog, axis=1, keepdims=True))
    dmat = jnp.exp(dlog - m_t)
    w_inter = jnp.exp(inter - m_t)
    s = _dot_nt(q, k) * dmat
    cst = c_scr[...]
    nst = n_scr[...]
    num = _dot(s, v) + w_inter * _dot(q, cst)
    den = jnp.sum(s, axis=1, keepdims=True) + w_inter * jnp.sum(q.astype(F32) * nst, axis=1, keepdims=True)
    h_ref[0, 0] = num / jnp.maximum(jnp.abs(den), jnp.exp(-m_t))

    b_end = jnp.sum(lf_c, axis=0, keepdims=True)
    g_s = b_end - b_c + ig_c
    m_new = jnp.maximum(b_end + m_prev, jnp.max(g_s, axis=0, keepdims=True))
    kw = k.astype(F32) * jnp.exp(g_s - m_new)
    decay = jnp.exp(b_end + m_prev - m_new)
    c_scr[...] = decay * cst + _dot_tn(kw, v)
    n_scr[...] = decay * nst + jnp.sum(kw, axis=0, keepdims=True)
    m_scr[...] = m_new

    @pl.when(c == nc - 1)
    def _():
        cT_ref[0, 0, 0] = c_scr[...]
        nT_ref[0, 0, 0] = n_scr[...]
        mT_ref[0, 0, 0] = m_scr[...]


def mlstm_scan(q, k, p, gates, gates_t, state):
    B, T, _ = q.shape
    L = min(MLSTM_CHUNK, T)
    nc = T // L
    H = MLSTM_HEADS
    c0, n0, m0 = state
    chunk = lambda d, c: c + d * (nc - 1 - 2 * c)
    v_off = (2 * H * MLSTM_DK) // MLSTM_DV
    csp = pl.BlockSpec((1, 1, 1, MLSTM_DK, MLSTM_DV), lambda d, b, h, c: (d, b, h, 0, 0))
    nsp = pl.BlockSpec((1, 1, 1, 1, MLSTM_DK), lambda d, b, h, c: (d, b, h, 0, 0))
    msp = pl.BlockSpec((1, 1, 1, 1, 1), lambda d, b, h, c: (d, b, h, 0, 0))
    return pl.pallas_call(
        _mlstm_chunk_kernel,
        grid=(2, B, H, nc),
        in_specs=[pl.BlockSpec((1, L, MLSTM_DK), lambda d, b, h, c: (b, chunk(d, c), h)),
                  pl.BlockSpec((1, L, MLSTM_DK), lambda d, b, h, c: (b, chunk(d, c), h)),
                  pl.BlockSpec((1, L, MLSTM_DV), lambda d, b, h, c: (b, chunk(d, c), v_off + h)),
                  pl.BlockSpec((1, L, LANES), lambda d, b, h, c: (b, chunk(d, c), 0)),
                  pl.BlockSpec((1, 4 * H, L), lambda d, b, h, c: (b, 0, chunk(d, c))),
                  csp, nsp, msp],
        out_specs=[pl.BlockSpec((1, 1, L, MLSTM_DV), lambda d, b, h, c: (d, b, chunk(d, c), h)),
                   csp, nsp, msp],
        out_shape=[jax.ShapeDtypeStruct((2, B, T, GROUP_W), F32),
                   jax.ShapeDtypeStruct(c0.shape, F32),
                   jax.ShapeDtypeStruct(n0.shape, F32),
                   jax.ShapeDtypeStruct(m0.shape, F32)],
        scratch_shapes=[pltpu.VMEM((MLSTM_DK, MLSTM_DV), F32),
                        pltpu.VMEM((1, MLSTM_DK), F32),
                        pltpu.VMEM((1, 1), F32)],
        compiler_params=_cp("parallel", "parallel", "parallel", "arbitrary"),
        name="mlstm_scan",
    )(q, k, p, gates, gates_t, c0, n0, m0)


def _mlstm_readout_kernel(hf_ref, hb_ref, o_ref, nw_ref, out_ref):
    h = hf_ref[0, 0] + hb_ref[0, 0]
    outs = []
    for j in range(MLSTM_HEADS):
        hj = h[:, j * MLSTM_DV:(j + 1) * MLSTM_DV]
        outs.append(hj * lax.rsqrt(jnp.mean(hj * hj, axis=-1, keepdims=True) + EPS))
    hn = jnp.concatenate(outs, axis=1)
    out_ref[0] = (hn * nw_ref[...] * _sigmoid(o_ref[0])).astype(out_ref.dtype)


def mlstm_readout(h, p, norm_w):
    _, B, T, W = h.shape
    tm = 256
    o_blk = (2 * MLSTM_HEADS * MLSTM_DK + GROUP_W) // GROUP_W
    return pl.pallas_call(
        _mlstm_readout_kernel,
        grid=(B, T // tm),
        in_specs=[pl.BlockSpec((1, 1, tm, W), lambda b, i: (0, b, i, 0)),
                  pl.BlockSpec((1, 1, tm, W), lambda b, i: (1, b, i, 0)),
                  pl.BlockSpec((1, tm, W), lambda b, i: (b, i, o_blk)),
                  pl.BlockSpec((1, W), lambda b, i: (0, 0))],
        out_specs=pl.BlockSpec((1, tm, W), lambda b, i: (b, i, 0)),
        out_shape=jax.ShapeDtypeStruct((B, T, W), BF16),
        compiler_params=_cp("parallel", "parallel"),
        name="mlstm_readout",
    )(h, h, p, norm_w.reshape(1, W))


def _na_prep_kernel(p_ref, nw_ref, q_ref, k_ref, v_ref):
    x = p_ref[0]
    for j in range(NA_HEADS):
        sl = slice(j * NA_HEAD, (j + 1) * NA_HEAD)
        for src, dst, row, scale in ((0, q_ref, 0, NA_HEAD ** -0.5), (GROUP_W, k_ref, 1, 1.0)):
            z = x[:, src + j * NA_HEAD:src + (j + 1) * NA_HEAD]
            zn = z * lax.rsqrt(jnp.mean(z * z, axis=-1, keepdims=True) + EPS) * nw_ref[row:row + 1, :]
            dst[0, :, sl] = (zn * scale).astype(BF16)
    v_ref[0] = x[:, 2 * GROUP_W:].astype(BF16)


def na_prep(p, qk_norm):
    B, T, W = p.shape
    tm = 256
    tok = pl.BlockSpec((1, tm, GROUP_W), lambda b, i: (b, i, 0))
    sh = jax.ShapeDtypeStruct((B, T, GROUP_W), BF16)
    return pl.pallas_call(
        _na_prep_kernel,
        grid=(B, T // tm),
        in_specs=[pl.BlockSpec((1, tm, W), lambda b, i: (b, i, 0)),
                  pl.BlockSpec((2, NA_HEAD), lambda b, i: (0, 0))],
        out_specs=[tok, tok, tok],
        out_shape=[sh, sh, sh],
        compiler_params=_cp("parallel", "parallel"),
        name="na_prep",
    )(p, qk_norm)


def _softmax_rows(s):
    m = jnp.max(s, axis=1, keepdims=True)
    e = jnp.exp(s - m)
    return e / jnp.sum(e, axis=1, keepdims=True)


def _na_ctx_kernel(q_ref, k_ref, v_ref, o_ref):
    for j in range(NA_HEADS):
        sl = slice(j * NA_HEAD, (j + 1) * NA_HEAD)
        pr = _softmax_rows(_dot_nt(q_ref[0, :, sl], k_ref[0, :, sl]))
        o_ref[0, :, sl] = _dot(pr, v_ref[0, :, sl]).astype(o_ref.dtype)


def na_ctx_attention(q, k, v):
    B, T, W = q.shape
    tok = pl.BlockSpec((1, T, W), lambda b: (b, 0, 0))
    return pl.pallas_call(
        _na_ctx_kernel,
        grid=(B,),
        in_specs=[tok, tok, tok],
        out_specs=tok,
        out_shape=jax.ShapeDtypeStruct((B, T, W), BF16),
        compiler_params=_cp("parallel"),
        name="na_ctx_attention",
    )(q, k, v)


def _na_lat_kernel(q_ref, k0_ref, k1_ref, k2_ref, v0_ref, v1_ref, v2_ref, kc_ref, vc_ref, bias_ref, o_ref):
    for j in range(NA_HEADS):
        sl = slice(j * NA_HEAD, (j + 1) * NA_HEAD)
        q = q_ref[0, :, sl]
        kcat = jnp.concatenate([k0_ref[0, :, sl], k1_ref[0, :, sl], k2_ref[0, :, sl]], axis=0)
        vcat = jnp.concatenate([v0_ref[0, :, sl], v1_ref[0, :, sl], v2_ref[0, :, sl],
                                vc_ref[0, :, sl]], axis=0)
        s = jnp.concatenate([_dot_nt(q, kcat) + bias_ref[0, j], _dot_nt(q, kc_ref[0, :, sl])], axis=1)
        o_ref[0, :, sl] = _dot(_softmax_rows(s), vcat).astype(o_ref.dtype)


def na_bias_table(rpb, rows):
    nq, nk = NA_QROWS, NA_KROWS
    starts = np.array([0, nq, rows - nq])
    r = starts[:, None] + np.arange(nq)[None, :]
    kr = starts[:, None] - nq + np.arange(nk)[None, :]
    r0 = np.clip(r - WIN_H // 2, 0, rows - WIN_H)
    rv = (kr[:, None, :] >= r0[:, :, None]) & (kr[:, None, :] < r0[:, :, None] + WIN_H)
    dr = kr[:, None, :] - r[:, :, None] + WIN_H - 1
    col = np.arange(GRID_W)
    c0 = np.clip(col - WIN_W // 2, 0, GRID_W - WIN_W)
    cv = (col[None, :] >= c0[:, None]) & (col[None, :] < c0[:, None] + WIN_W)
    pad = GRID_W - WIN_W
    rpb_pad = jnp.pad(rpb, ((0, 0), (0, 0), (pad, pad)))
    colbias = jnp.stack([rpb_pad[:, :, GRID_W - 1 - q:2 * GRID_W - 1 - q] for q in range(GRID_W)], axis=2)
    colbias = jnp.where(cv[None, None], colbias, NEG)
    blank = jnp.full((NA_HEADS, GRID_W, GRID_W), NEG, F32)
    pats = []
    for p in range(3):
        qrows = [jnp.concatenate([colbias[:, int(dr[p, i, j])] if rv[p, i, j] else blank
                                  for j in range(nk)], axis=2) for i in range(nq)]
        pats.append(jnp.concatenate(qrows, axis=1))
    return jnp.stack(pats)


def na_lat_attention(q, k, v, kc, vc, bias):
    B, T, W = q.shape
    tq = NA_QROWS * GRID_W
    nb = T // tq
    ctx_len = kc.shape[1]
    qsp = pl.BlockSpec((1, tq, W), lambda b, i: (b, i, 0))
    prv = pl.BlockSpec((1, tq, W), lambda b, i: (b, jnp.maximum(i - 1, 0), 0))
    nxt = pl.BlockSpec((1, tq, W), lambda b, i: (b, jnp.minimum(i + 1, nb - 1), 0))
    csp = pl.BlockSpec((1, ctx_len, W), lambda b, i: (b, 0, 0))
    pattern = lambda i: jnp.where(i == 0, 0, jnp.where(i == nb - 1, 2, 1))
    return pl.pallas_call(
        _na_lat_kernel,
        grid=(B, nb),
        in_specs=[qsp, prv, qsp, nxt, prv, qsp, nxt, csp, csp,
                  pl.BlockSpec((1, NA_HEADS, tq, NA_KROWS * GRID_W), lambda b, i: (pattern(i), 0, 0, 0))],
        out_specs=qsp,
        out_shape=jax.ShapeDtypeStruct((B, T, W), BF16),
        compiler_params=_cp("parallel", "arbitrary"),
        name="na_lat_attention",
    )(q, k, k, k, v, v, v, kc, vc, bias)


def _ffn_norm_router_body(h_ref, g_ref, sh_ref, sc_ref, rw_ref, rb_ref, v_ref, ids_ref, wts_ref):
    x = h_ref[0]
    y = x * lax.rsqrt(jnp.mean(x * x, axis=-1, keepdims=True) + EPS) * g_ref[...]
    v = y * (1.0 + sc_ref[0]) + sh_ref[0]
    v_ref[...] = v
    vh, vl = _split2(v)
    logits = (_dot(vh, rw_ref[0]) + (_dot(vh, rw_ref[1]) + _dot(vl, rw_ref[0]))) + rb_ref[...]
    lane = lax.broadcasted_iota(jnp.int32, logits.shape, 1).astype(F32)
    first = lambda mask: jnp.min(jnp.where(mask, lane, float(LANES)), axis=1, keepdims=True)

    g_mask = lane < N_GROUPS
    gl = jnp.where(g_mask, logits, NEG)
    gmax = jnp.max(gl, axis=1, keepdims=True)
    g_p = 1.0 / jnp.sum(jnp.exp(gl - gmax), axis=1, keepdims=True)
    g_idx = first(g_mask & (gl == gmax))
    e_lane = lane - N_GROUPS
    e_mask = (e_lane >= 0) & (e_lane < N_EXPERTS) & (jnp.floor(e_lane * (1.0 / EXPERTS_PER_GROUP)) == g_idx)
    el = jnp.where(e_mask, logits, NEG)
    e1 = jnp.max(el, axis=1, keepdims=True)
    i1 = first(e_mask & (el == e1))
    el2 = jnp.where(lane == i1, NEG, el)
    e2 = jnp.max(el2, axis=1, keepdims=True)
    i2 = first(e_mask & (lane != i1) & (el2 == e2))
    x2 = jnp.exp(e2 - e1)
    w1 = g_p / (1.0 + x2)
    w2 = g_p * x2 / (1.0 + x2)
    ids_ref[...] = jnp.where(lane == 0, i1 - N_GROUPS, jnp.where(lane == 1, i2 - N_GROUPS, 0.0)).astype(jnp.int32)
    wts_ref[...] = jnp.where(lane == 0, w1, jnp.where(lane == 1, w2, 0.0))


def _ffn_norm_router_kernel(*refs, bounds):
    ns = len(bounds) - 1
    g_ref, rw_ref, rb_ref, v_ref, ids_ref, wts_ref = refs[3 * ns:]
    r = pl.program_id(0)
    for s in range(ns):
        h_ref, sh_ref, sc_ref = refs[3 * s:3 * s + 3]

        @pl.when((r >= bounds[s]) & (r < bounds[s + 1]))
        def _():
            _ffn_norm_router_body(h_ref, g_ref, sh_ref, sc_ref, rw_ref, rb_ref, v_ref, ids_ref, wts_ref)


def ffn_norm_router(streams, g, rw, rb):
    D = g.shape[0]
    tm = 256
    bounds = [0]
    in_specs, args = [], []
    for h, shift, scale in streams:
        B, T, _ = h.shape
        nt = T // tm
        lo, nblk = bounds[-1], B * nt
        bounds.append(lo + nblk)
        local = lambda r, lo=lo, nblk=nblk: jnp.clip(r - lo, 0, nblk - 1)
        in_specs += [pl.BlockSpec((1, tm, D), lambda r, f=local, nt=nt: (f(r) // nt, f(r) % nt, 0)),
                     pl.BlockSpec((1, 1, D), lambda r, f=local, nt=nt: (f(r) // nt, 0, 0)),
                     pl.BlockSpec((1, 1, D), lambda r, f=local, nt=nt: (f(r) // nt, 0, 0))]
        args += [h, shift, scale]
    n_rows = bounds[-1] * tm
    in_specs += [pl.BlockSpec((1, D), lambda r: (0, 0)),
                 pl.BlockSpec((2, D, LANES), lambda r: (0, 0, 0)),
                 pl.BlockSpec((1, LANES), lambda r: (0, 0))]
    args += [g.reshape(1, D), rw, rb]
    lan = pl.BlockSpec((tm, LANES), lambda r: (r, 0))
    return pl.pallas_call(
        functools.partial(_ffn_norm_router_kernel, bounds=tuple(bounds)),
        grid=(bounds[-1],),
        in_specs=in_specs,
        out_specs=[pl.BlockSpec((tm, D), lambda r: (r, 0)), lan, lan],
        out_shape=[jax.ShapeDtypeStruct((n_rows, D), F32),
                   jax.ShapeDtypeStruct((n_rows, LANES), jnp.int32),
                   jax.ShapeDtypeStruct((n_rows, LANES), F32)],
        compiler_params=_cp("parallel"),
        name="ffn_norm_router",
    )(*args)


def _moe_kernel(be_ref, nv_ref, tok_ref, tokn_ref, dst_ref, sw_ref, v_hbm, wg_hbm, wu_hbm, wd_hbm, out_hbm,
                x0, x1, y0, y1, wg_buf, wu_buf, wd_buf, gsem, ssem, wsem):
    i = pl.program_id(0)
    nb = pl.num_programs(0)
    xs, ys = (x0, x1), (y0, y1)

    def gather_row(idx_ref, s, j):
        return pltpu.make_async_copy(v_hbm.at[pl.ds(idx_ref[0, 0, j], 1)], xs[s].at[pl.ds(j, 1)], gsem.at[s])

    def start_gather_loop(idx_ref, s):
        def body(j, carry):
            gather_row(idx_ref, s, j).start()
            return carry
        lax.fori_loop(0, MOE_BLOCK, body, 0)

    def wait_gather(s):
        pltpu.make_async_copy(v_hbm.at[pl.ds(0, MOE_BLOCK)], xs[s], gsem.at[s]).wait()

    def scatter_row(s, j):
        return pltpu.make_async_copy(ys[s].at[pl.ds(j, 1)], out_hbm.at[pl.ds(dst_ref[0, 0, j], 1)], ssem.at[s])

    def start_scatter(s, n):
        groups = n // MOE_DMA_UNROLL

        def group(g, carry):
            for u in range(MOE_DMA_UNROLL):
                scatter_row(s, g * MOE_DMA_UNROLL + u).start()
            return carry
        lax.fori_loop(0, groups, group, 0)

        def single(j, carry):
            scatter_row(s, j).start()
            return carry
        lax.fori_loop(groups * MOE_DMA_UNROLL, n, single, 0)

    def wait_scatter(s, n):
        @pl.when(n == MOE_BLOCK)
        def _():
            pltpu.make_async_copy(ys[s], out_hbm.at[pl.ds(0, MOE_BLOCK)], ssem.at[s]).wait()

        @pl.when(n < MOE_BLOCK)
        def _():
            def body(j, carry):
                pltpu.make_async_copy(ys[s].at[pl.ds(0, 1)], out_hbm.at[pl.ds(0, 1)], ssem.at[s]).wait()
                return carry
            lax.fori_loop(0, n, body, 0)

    @pl.when(i == 0)
    def _():
        start_gather_loop(tok_ref, 0)

    e = be_ref[i]
    nv = nv_ref[i]

    @pl.when((nv > 0) & ((i == 0) | (e != be_ref[jnp.maximum(i - 1, 0)])))
    def _():
        copies = [pltpu.make_async_copy(wg_hbm.at[e], wg_buf, wsem.at[0]),
                  pltpu.make_async_copy(wu_hbm.at[e], wu_buf, wsem.at[1]),
                  pltpu.make_async_copy(wd_hbm.at[e], wd_buf, wsem.at[2])]
        for cp in copies:
            cp.start()
        for cp in copies:
            cp.wait()

    def step(s):
        wait_gather(s)

        @pl.when(i >= 2)
        def _():
            wait_scatter(s, nv_ref[jnp.maximum(i - 2, 0)])

        @pl.when(nv > 0)
        def _():
            x = xs[s][...].astype(BF16)
            for j in range(MOE_BLOCK):
                gather_row(tokn_ref, 1 - s, j).start()
            hg = jnp.dot(x, wg_buf[...], preferred_element_type=F32)
            hu = jnp.dot(x, wu_buf[...], preferred_element_type=F32)
            ys[s][...] = _dot(_silu(hg) * hu, wd_buf[...]) * sw_ref[...]
            start_scatter(s, nv)

        @pl.when(nv == 0)
        def _():
            start_gather_loop(tokn_ref, 1 - s)

        @pl.when(i == nb - 1)
        def _():
            wait_gather(1 - s)
            wait_scatter(s, nv)

            @pl.when(nb > 1)
            def _():
                wait_scatter(1 - s, nv_ref[jnp.maximum(i - 1, 0)])

    for s in range(2):
        pl.when(i % 2 == s)(functools.partial(step, s))


def moe_experts(v_rows, block_expert, block_valid, slot_tok, slot_dst, slot_w, wg, wu, wd, n_out):
    P = slot_tok.shape[0]
    D = v_rows.shape[1]
    nb = P // MOE_BLOCK
    idx = lambda a: a.reshape(nb, 1, MOE_BLOCK)
    smem = lambda imap: pl.BlockSpec((1, 1, MOE_BLOCK), imap, memory_space=pltpu.SMEM)
    hbm = pl.BlockSpec(memory_space=pl.ANY)
    grid_spec = pltpu.PrefetchScalarGridSpec(
        num_scalar_prefetch=2,
        grid=(nb,),
        in_specs=[smem(lambda i, be, nv: (i, 0, 0)),
                  smem(lambda i, be, nv: (jnp.minimum(i + 1, nb - 1), 0, 0)),
                  smem(lambda i, be, nv: (i, 0, 0)),
                  pl.BlockSpec((MOE_BLOCK, 1), lambda i, be, nv: (i, 0)),
                  hbm, hbm, hbm, hbm],
        out_specs=hbm,
        scratch_shapes=[pltpu.VMEM((MOE_BLOCK, D), F32),
                        pltpu.VMEM((MOE_BLOCK, D), F32),
                        pltpu.VMEM((MOE_BLOCK, D), F32),
                        pltpu.VMEM((MOE_BLOCK, D), F32),
                        pltpu.VMEM((D, EXPERT_FF), BF16),
                        pltpu.VMEM((D, EXPERT_FF), BF16),
                        pltpu.VMEM((EXPERT_FF, D), BF16),
                        pltpu.SemaphoreType.DMA((2,)),
                        pltpu.SemaphoreType.DMA((2,)),
                        pltpu.SemaphoreType.DMA((3,))],
    )
    return pl.pallas_call(
        _moe_kernel,
        grid_spec=grid_spec,
        out_shape=jax.ShapeDtypeStruct((n_out, D), F32),
        compiler_params=_cp("arbitrary"),
        name="moe_experts",
    )(block_expert, block_valid, idx(slot_tok), idx(slot_tok), idx(slot_dst), slot_w.reshape(P, 1),
      v_rows, wg, wu, wd)


def _moe_combine_kernel(h_ref, f0_ref, f1_ref, gate_ref, o_ref):
    o_ref[0] = h_ref[0] + gate_ref[0] * (f0_ref[...] + f1_ref[...])


def moe_combine(h, out2, gate, row_offset, n_rows):
    B, T, D = h.shape
    tm = 256
    nt = T // tm
    blk0 = row_offset // tm
    tok = pl.BlockSpec((1, tm, D), lambda b, i: (b, i, 0))
    return pl.pallas_call(
        _moe_combine_kernel,
        grid=(B, nt),
        in_specs=[tok,
                  pl.BlockSpec((tm, D), lambda b, i: (blk0 + b * nt + i, 0)),
                  pl.BlockSpec((tm, D), lambda b, i: (n_rows // tm + blk0 + b * nt + i, 0)),
                  pl.BlockSpec((1, 1, D), lambda b, i: (b, 0, 0))],
        out_specs=tok,
        out_shape=jax.ShapeDtypeStruct((B, T, D), F32),
        compiler_params=_cp("parallel", "parallel"),
        name="moe_combine",
    )(h, out2, out2, gate)


def moe_dispatch(ids, wts):
    N = ids.shape[0]
    A = N * TOP_K
    e_flat = ids.reshape(A)
    order = jnp.argsort(e_flat).astype(jnp.int32)
    counts = jnp.sum((e_flat[:, None] == jnp.arange(N_EXPERTS)[None, :]).astype(jnp.int32), axis=0)
    padded = (counts + MOE_BLOCK - 1) // MOE_BLOCK * MOE_BLOCK
    ends = jnp.cumsum(padded)
    n_blocks = -(-(A + N_EXPERTS * (MOE_BLOCK - 1)) // MOE_BLOCK)
    block_start = jnp.arange(n_blocks, dtype=jnp.int32) * MOE_BLOCK
    block_expert = jnp.minimum(jnp.searchsorted(ends, block_start, side="right"),
                               N_EXPERTS - 1).astype(jnp.int32)
    into = block_start - (ends - padded)[block_expert]
    b_count = counts[block_expert]
    block_valid = jnp.clip(b_count - into, 0, MOE_BLOCK).astype(jnp.int32)
    local = (into[:, None] + jnp.arange(MOE_BLOCK, dtype=jnp.int32)[None, :])
    real = local < b_count[:, None]
    src = jnp.clip((jnp.cumsum(counts) - counts)[block_expert][:, None] + local, 0, A - 1)
    a = order[src.reshape(-1)]
    real = real.reshape(-1)
    tok = a // TOP_K
    slot_tok = jnp.where(real, tok, 0).astype(jnp.int32)
    slot_dst = jnp.where(real, (a % TOP_K) * N + tok, 0).astype(jnp.int32)
    slot_w = jnp.where(real, wts.reshape(A)[a], 0.0)
    return slot_tok, slot_dst, slot_w, block_expert, block_valid


def hier_moe(v_rows, ids, wts, wg, wu, wd):
    N = v_rows.shape[0]
    slot_tok, slot_dst, slot_w, block_expert, block_valid = moe_dispatch(ids, wts)
    return moe_experts(v_rows, block_expert, block_valid, slot_tok, slot_dst, slot_w, wg, wu, wd, TOP_K * N)


def _layer_weights(l, P):
    w = {}
    w_in = P["w_in"][l]
    o = 0
    a = w_in[:, o:o + RWKV_COLS]; o += RWKV_COLS
    b = w_in[:, o:o + GROUP_W]; o += GROUP_W
    c = w_in[:, o:o + MLSTM_COLS]; o += MLSTM_COLS
    d = w_in[:, o:o + NA_COLS]
    D = w_in.shape[0]
    pad_a = RWKV_PAD_COLS - RWKV_COLS
    w["in_a"] = jnp.concatenate([a, jnp.zeros((D, pad_a), F32)], axis=1).astype(BF16)
    w["in_b"] = b.astype(BF16)
    w["in_c"] = jnp.concatenate([c, jnp.zeros((D, MLSTM_PAD_COLS - MLSTM_COLS), F32)], axis=1).astype(BF16)
    w["in_d"] = d.astype(BF16)
    w["out"] = P["w_out"][l].astype(BF16)

    z = jnp.zeros((RWKV_DECAY_RANK, GROUP_W), F32)
    wup, aup = P["rwkv_w_up"][l], P["rwkv_a_up"][l]
    rw = {
        "mu": jnp.concatenate([P["rwkv_mu"][l], jnp.zeros((2, pad_a), F32)], axis=1),
        "w_up": jnp.concatenate([jnp.concatenate([wup[0], z], axis=1),
                                 jnp.concatenate([z, wup[1]], axis=1)], axis=0).astype(BF16),
        "a_up": jnp.concatenate([jnp.concatenate([aup[0], z], axis=1),
                                 jnp.concatenate([z, aup[1]], axis=1)], axis=0).astype(BF16),
        "w0": P["rwkv_w0"][l].reshape(1, 2 * GROUP_W),
        "a0": P["rwkv_a0"][l].reshape(1, 2 * GROUP_W),
        "g_up": jnp.concatenate([P["rwkv_g_up"][l],
                                 jnp.zeros((256 - RWKV_GATE_RANK, GROUP_W), F32)], axis=0).astype(BF16),
        "k_k": P["rwkv_k_k"][l].reshape(1, GROUP_W),
        "k_a": P["rwkv_k_a"][l].reshape(1, GROUP_W),
        "r_k": P["rwkv_r_k"][l].reshape(1, GROUP_W),
    }
    w["rwkv"] = rw
    w["rwkv_ln"] = P["rwkv_ln"][l]
    w["pool_w"] = P["pool_w"][l].astype(BF16)
    w["pool_scale"] = P["pool_scale"][l]
    w["mlstm_conv"] = P["mlstm_conv"][l]
    gb = P["mlstm_gate_b"][l].reshape(1, 4 * MLSTM_HEADS)
    w["mlstm_gate_b"] = jnp.concatenate([gb, jnp.zeros((1, LANES - 4 * MLSTM_HEADS), F32)], axis=1)
    w["mlstm_norm"] = P["mlstm_norm"][l]
    w["na_qk_norm"] = P["na_qk_norm"][l]
    w["na_rpb"] = P["na_rpb"][l]
    rcat = jnp.concatenate([P["router_g_w"][l], P["router_e_w"][l],
                            jnp.zeros((D, LANES - N_GROUPS - N_EXPERTS), F32)], axis=1)
    rhi = rcat.astype(BF16)
    w["router_w"] = jnp.stack([rhi, (rcat - rhi.astype(F32)).astype(BF16)])
    w["router_b"] = jnp.concatenate([P["router_g_b"][l], P["router_e_b"][l],
                                     jnp.zeros((LANES - N_GROUPS - N_EXPERTS,), F32)]).reshape(1, LANES)
    w["moe_gate"] = P["moe_w_gate"][l].astype(BF16)
    w["moe_up"] = P["moe_w_up"][l].astype(BF16)
    w["moe_down"] = P["moe_w_down"][l].astype(BF16)
    return w


def rope_tables(T):
    t = jnp.arange(T)
    row = (t // GRID_W).astype(F32)
    col = (t % GRID_W).astype(F32)
    n_pairs = MLSTM_DK // 4
    inv = ROPE_BASE ** (-jnp.arange(n_pairs, dtype=F32) / n_pairs)
    ang = jnp.concatenate([row[:, None] * inv, col[:, None] * inv], axis=-1)
    cos = jnp.repeat(jnp.cos(ang), 2, axis=1)
    sin = jnp.repeat(jnp.sin(ang), 2, axis=1) * jnp.tile(jnp.array([-1.0, 1.0], F32), MLSTM_DK // 2)
    return cos, sin


def rwkv_mixer(p_ctx, p_lat, w, need_ctx):
    B = p_lat.shape[0]
    names = ("r", "kk", "v", "bonus", "g", "lw", "kd", "bb")
    fc = dict(zip(names, rwkv_prep(p_ctx, w["rwkv"])))
    fl = dict(zip(names, rwkv_prep(p_lat, w["rwkv"])))
    s0 = jnp.zeros((2, B, RWKV_PAIRS, LANES, LANES), F32)
    y_c, s_ctx = rwkv_scan(fc, s0)
    y_l, _ = rwkv_scan(fl, s_ctx)
    out_l = rwkv_readout(y_l, fl["bonus"], fl["g"], w["rwkv_ln"])
    out_c = rwkv_readout(y_c, fc["bonus"], fc["g"], w["rwkv_ln"]) if need_ctx else None
    return out_c, out_l


def mlstm_mixer(p_ctx, p_lat, w, rope, need_ctx):
    B = p_lat.shape[0]
    H = MLSTM_HEADS
    state = (jnp.zeros((2, B, H, MLSTM_DK, MLSTM_DV), F32),
             jnp.zeros((2, B, H, 1, MLSTM_DK), F32),
             jnp.zeros((2, B, H, 1, 1), F32))
    outs = []
    for p, tabs in ((p_ctx, None), (p_lat, rope)):
        q, k, gates = mlstm_prep(p, w["mlstm_conv"], w["mlstm_gate_b"], tabs)
        gates_t = jnp.swapaxes(gates[:, :, :4 * H], 1, 2)
        h, *state = mlstm_scan(q, k, p, gates, gates_t, tuple(state))
        outs.append(h)
    out_l = mlstm_readout(outs[1], p_lat, w["mlstm_norm"])
    out_c = mlstm_readout(outs[0], p_ctx, w["mlstm_norm"]) if need_ctx else None
    return out_c, out_l


def na_mixer(p_ctx, p_lat, w, need_ctx):
    qc, kc, vc = na_prep(p_ctx, w["na_qk_norm"])
    ql, kl, vl = na_prep(p_lat, w["na_qk_norm"])
    rows = p_lat.shape[1] // GRID_W
    bias = na_bias_table(w["na_rpb"], rows)
    out_l = na_lat_attention(ql, kl, vl, kc, vc, bias)
    out_c = na_ctx_attention(qc, kc, vc) if need_ctx else None
    return out_c, out_l


def token_mixers(u_ctx, u_lat, w, rope, need_ctx):
    outs_c, outs_l = [], []
    pa = (matmul(u_ctx, w["in_a"]), matmul(u_lat, w["in_a"]))
    oc, ol = rwkv_mixer(pa[0], pa[1], w, need_ctx)
    outs_c.append(oc); outs_l.append(ol)
    pb = (matmul(u_ctx, w["in_b"]), matmul(u_lat, w["in_b"]))
    outs_l.append(pool_mixer(pb[1], w["pool_w"], w["pool_scale"]))
    outs_c.append(pool_mixer(pb[0], w["pool_w"], w["pool_scale"]) if need_ctx else None)
    pc = (matmul(u_ctx, w["in_c"]), matmul(u_lat, w["in_c"]))
    oc, ol = mlstm_mixer(pc[0], pc[1], w, rope, need_ctx)
    outs_c.append(oc); outs_l.append(ol)
    pd = (matmul(u_ctx, w["in_d"]), matmul(u_lat, w["in_d"]))
    oc, ol = na_mixer(pd[0], pd[1], w, need_ctx)
    outs_c.append(oc); outs_l.append(ol)
    return outs_c, outs_l


def kernel(x, c, ctx, c_ctx, ada_w, ada_b, norm_mix, norm_ffn, w_in, w_out, rwkv_mu, rwkv_w0, rwkv_w_up, rwkv_a0, rwkv_a_up, rwkv_g_up, rwkv_k_k, rwkv_k_a, rwkv_r_k, rwkv_ln, pool_w, pool_scale, mlstm_conv, mlstm_gate_b, mlstm_norm, na_qk_norm, na_rpb, router_g_w, router_g_b, router_e_w, router_e_b, moe_w_gate, moe_w_up, moe_w_down):
    P = dict(w_in=w_in, w_out=w_out, rwkv_mu=rwkv_mu, rwkv_w0=rwkv_w0, rwkv_w_up=rwkv_w_up,
             rwkv_a0=rwkv_a0, rwkv_a_up=rwkv_a_up, rwkv_g_up=rwkv_g_up, rwkv_k_k=rwkv_k_k,
             rwkv_k_a=rwkv_k_a, rwkv_r_k=rwkv_r_k, rwkv_ln=rwkv_ln, pool_w=pool_w, pool_scale=pool_scale,
             mlstm_conv=mlstm_conv, mlstm_gate_b=mlstm_gate_b, mlstm_norm=mlstm_norm,
             na_qk_norm=na_qk_norm, na_rpb=na_rpb, router_g_w=router_g_w, router_g_b=router_g_b,
             router_e_w=router_e_w, router_e_b=router_e_b, moe_w_gate=moe_w_gate, moe_w_up=moe_w_up,
             moe_w_down=moe_w_down)
    B, T, D = x.shape
    Lc = ctx.shape[1]
    depth = ada_w.shape[0]
    rope = rope_tables(T)

    cvec = jnp.concatenate([c, c_ctx[None, :], jnp.zeros((SUBLANES - B - 1, D), F32)], axis=0)
    mods = ada_modulation(cvec, ada_w, ada_b)

    h_lat, h_ctx = x, ctx
    for l in range(depth):
        last = l == depth - 1
        w = _layer_weights(l, P)
        m = mods[l].reshape(SUBLANES, 6, D)
        m_lat = [m[:B, i][:, None, :] for i in range(6)]
        m_ctx = [jnp.broadcast_to(m[B, i][None, None, :], (B, 1, D)) for i in range(6)]

        u_lat = norm_modulate(h_lat, norm_mix[l], m_lat[0], m_lat[1])
        u_ctx = norm_modulate(h_ctx, norm_mix[l], m_ctx[0], m_ctx[1])
        ys_ctx, ys_lat = token_mixers(u_ctx, u_lat, w, rope, not last)
        h_lat = out_proj_residual(ys_lat, w["out"], h_lat, m_lat[2])
        n_rows = B * T if last else B * (T + Lc)
        streams = [(h_lat, m_lat[3], m_lat[4])]
        if not last:
            h_ctx = out_proj_residual(ys_ctx, w["out"], h_ctx, m_ctx[2])
            streams.append((h_ctx, m_ctx[3], m_ctx[4]))
        rows, ids, wts = ffn_norm_router(streams, norm_ffn[l], w["router_w"], w["router_b"])
        f2 = hier_moe(rows, ids[:, :TOP_K], wts[:, :TOP_K], w["moe_gate"], w["moe_up"], w["moe_down"])
        if not last:
            h_ctx = moe_combine(h_ctx, f2, m_ctx[5], B * T, n_rows)
        h_lat = moe_combine(h_lat, f2, m_lat[5], 0, n_rows)
    return h_lat
```

```python
import functools
import math

import numpy as np
import jax
import jax.numpy as jnp
from jax import lax
from jax.experimental import pallas as pl
from jax.experimental.pallas import tpu as pltpu

F32 = jnp.float32
BF16 = jnp.bfloat16

D_MODEL = 4096
DEPTH = 2
GRID_W = 64
EPS = 1e-6
GROUP_W = 1024

RWKV_HEAD = 64
RWKV_DECAY_RANK = 64
RWKV_ICLR_RANK = 64
RWKV_GATE_RANK = 160
RWKV_GN_EPS = 64e-5
RWKV_COLS = 3 * GROUP_W + 2 * RWKV_DECAY_RANK + 2 * RWKV_ICLR_RANK + RWKV_GATE_RANK
RWKV_PAD_COLS = 3 * GROUP_W + 256 + 256

POOL_WINDOWS = (2, 4, 8, 16)
POOL_GROUP = 256
POOL_HALO = 8

MLSTM_HEADS = 4
MLSTM_DV = 256
MLSTM_DK = 128
MLSTM_COLS = 2 * MLSTM_HEADS * MLSTM_DK + 2 * GROUP_W + 4 * MLSTM_HEADS
MLSTM_PAD_COLS = 2 * MLSTM_HEADS * MLSTM_DK + 2 * GROUP_W + 128
ROPE_BASE = 10000.0

NA_HEADS = 8
NA_HEAD = 128
WIN_H = 8
WIN_W = 16
NA_COLS = 3 * GROUP_W
NA_QROWS = 4
NA_KROWS = 3 * NA_QROWS

N_GROUPS = 4
EXPERTS_PER_GROUP = 4
N_EXPERTS = 16
TOP_K = 2
EXPERT_FF = 1024
MOE_BLOCK = 256
MOE_FF_TILE = 256
MOE_OUT_TILE = 1024

LANES = 128
SUBLANES = 8
VMEM_LIMIT = 56 * 1024 * 1024
NEG = -1e30

RWKV_CHUNK = 64
RWKV_PAIRS = 8
MLSTM_CHUNK = 256


def _cp(*sem):
    return pltpu.CompilerParams(dimension_semantics=sem, vmem_limit_bytes=VMEM_LIMIT)


def _dot(a, b):
    return jnp.dot(a.astype(BF16), b.astype(BF16), preferred_element_type=F32)


def _dot_nt(a, b):
    return lax.dot_general(a.astype(BF16), b.astype(BF16), (((1,), (1,)), ((), ())),
                           preferred_element_type=F32)


def _dot_tn(a, b):
    return lax.dot_general(a.astype(BF16), b.astype(BF16), (((0,), (0,)), ((), ())),
                           preferred_element_type=F32)


def _split2(x):
    hi = x.astype(BF16)
    lo = (x - hi.astype(F32)).astype(BF16)
    return hi, lo


def _dot3(a, b, dot=_dot):
    ah, al = _split2(a)
    bh, bl = _split2(b)
    return dot(ah, bh) + (dot(ah, bl) + dot(al, bh))


def _dot_exact_lhs(a_bf16, b):
    b0 = b.astype(BF16)
    r1 = b - b0.astype(F32)
    b1 = r1.astype(BF16)
    b2 = (r1 - b1.astype(F32)).astype(BF16)
    return _dot(a_bf16, b0) + (_dot(a_bf16, b1) + _dot(a_bf16, b2))


def _sigmoid(x):
    return 1.0 / (1.0 + jnp.exp(-x))


def _silu(x):
    return x * _sigmoid(x)


def _log_sigmoid(x):
    return jnp.minimum(x, 0.0) - jnp.log(1.0 + jnp.exp(-jnp.abs(x)))


def _head_sum(x, width):
    r = lax.broadcasted_iota(jnp.int32, (LANES, LANES), 0) // width
    c = lax.broadcasted_iota(jnp.int32, (LANES, LANES), 1) // width
    e = jnp.where(r == c, 1.0, 0.0).astype(BF16)
    outs = []
    for j in range(x.shape[1] // LANES):
        hi, lo = _split2(x[:, j * LANES:(j + 1) * LANES])
        outs.append(_dot(hi, e) + _dot(lo, e))
    return jnp.concatenate(outs, axis=1)


def _ada_kernel(c_ref, w_ref, b_ref, o_ref):
    s = _silu(c_ref[...])
    o_ref[0] = _dot(s, w_ref[0]) + b_ref[0]


def ada_modulation(cvec, ada_w, ada_b):
    L, D, N = ada_w.shape
    tn = 512
    return pl.pallas_call(
        _ada_kernel,
        grid=(L, N // tn),
        in_specs=[pl.BlockSpec((SUBLANES, D), lambda l, j: (0, 0)),
                  pl.BlockSpec((1, D, tn), lambda l, j: (l, 0, j)),
                  pl.BlockSpec((1, 1, tn), lambda l, j: (l, 0, j))],
        out_specs=pl.BlockSpec((1, SUBLANES, tn), lambda l, j: (l, 0, j)),
        out_shape=jax.ShapeDtypeStruct((L, SUBLANES, N), F32),
        compiler_params=_cp("parallel", "parallel"),
        name="ada_modulation",
    )(cvec, ada_w, ada_b.reshape(L, 1, N))


def _normmod_kernel(h_ref, g_ref, sh_ref, sc_ref, o_ref):
    x = h_ref[0]
    y = x * lax.rsqrt(jnp.mean(x * x, axis=-1, keepdims=True) + EPS) * g_ref[...]
    o_ref[0] = (y * (1.0 + sc_ref[0]) + sh_ref[0]).astype(o_ref.dtype)


def norm_modulate(h, g, shift, scale):
    B, T, D = h.shape
    tm = 256
    vec = pl.BlockSpec((1, 1, D), lambda b, i: (b, 0, 0))
    return pl.pallas_call(
        _normmod_kernel,
        grid=(B, T // tm),
        in_specs=[pl.BlockSpec((1, tm, D), lambda b, i: (b, i, 0)),
                  pl.BlockSpec((1, D), lambda b, i: (0, 0)), vec, vec],
        out_specs=pl.BlockSpec((1, tm, D), lambda b, i: (b, i, 0)),
        out_shape=jax.ShapeDtypeStruct((B, T, D), BF16),
        compiler_params=_cp("parallel", "parallel"),
        name="norm_modulate",
    )(h, g.reshape(1, D), shift, scale)


def _mm_kernel(x_ref, w_ref, o_ref):
    o_ref[0] = jnp.dot(x_ref[0], w_ref[...], preferred_element_type=F32).astype(o_ref.dtype)


def _pick_tile(n, cands):
    for t in cands:
        if n % t == 0:
            return t
    raise ValueError(n)


def matmul(x, w, out_dtype=F32):
    B, T, K = x.shape
    N = w.shape[1]
    tm = _pick_tile(T, (1024, 512, 256))
    tn = _pick_tile(N, (1024, 896, 768, 640, 512))
    return pl.pallas_call(
        _mm_kernel,
        grid=(B, T // tm, N // tn),
        in_specs=[pl.BlockSpec((1, tm, K), lambda b, i, j: (b, i, 0)),
                  pl.BlockSpec((K, tn), lambda b, i, j: (0, j))],
        out_specs=pl.BlockSpec((1, tm, tn), lambda b, i, j: (b, i, j)),
        out_shape=jax.ShapeDtypeStruct((B, T, N), out_dtype),
        compiler_params=_cp("parallel", "parallel", "arbitrary"),
        name="in_proj",
    )(x, w)


def _outproj_kernel(ya_ref, yb_ref, yc_ref, yd_ref, w_ref, h_ref, gate_ref, o_ref):
    acc = jnp.dot(ya_ref[0], w_ref[0:GROUP_W, :], preferred_element_type=F32)
    acc += jnp.dot(yb_ref[0], w_ref[GROUP_W:2 * GROUP_W, :], preferred_element_type=F32)
    acc += jnp.dot(yc_ref[0], w_ref[2 * GROUP_W:3 * GROUP_W, :], preferred_element_type=F32)
    acc += jnp.dot(yd_ref[0], w_ref[3 * GROUP_W:4 * GROUP_W, :], preferred_element_type=F32)
    o_ref[0] = h_ref[0] + gate_ref[0] * acc


def out_proj_residual(ys, w, h, gate):
    B, T, D = h.shape
    tm = _pick_tile(T, (512, 256))
    tn = 1024
    ysp =pl.BlockSpec((1, tm, GROUP_W), lambda b, i, j: (b, i, 0))
    return pl.pallas_call(
        _outproj_kernel,
        grid=(B, T // tm, D // tn),
        in_specs=[ysp, ysp, ysp, ysp,
                  pl.BlockSpec((D, tn), lambda b, i, j: (0, j)),
                  pl.BlockSpec((1, tm, tn), lambda b, i, j: (b, i, j)),
                  pl.BlockSpec((1, 1, tn), lambda b, i, j: (b, 0, j))],
        out_specs=pl.BlockSpec((1, tm, tn), lambda b, i, j: (b, i, j)),
        out_shape=jax.ShapeDtypeStruct((B, T, D), F32),
        compiler_params=_cp("parallel", "parallel", "arbitrary"),
        name="out_proj",
    )(*ys, w, h, gate)


def _halo_specs(tm, width, halo=SUBLANES):
    per = tm // halo

    def prev_map(b, i):
        return (b, jnp.maximum(i * per - 1, 0), 0)

    def next_map(nb):
        return lambda b, i: (b, jnp.minimum((i + 1) * per, nb - 1), 0)

    return prev_map, next_map


def _shift_rows(x, prev_row, next_row):
    tm = x.shape[0]
    row = lax.broadcasted_iota(jnp.int32, x.shape, 0)
    prev = jnp.where(row == 0, prev_row, pltpu.roll(x, 1, axis=0))
    nxt = jnp.where(row == tm - 1, next_row, pltpu.roll(x, tm - 1, axis=0))
    return prev, nxt


def _rwkv_prep_kernel(p_ref, pp_ref, pn_ref, mu_ref, wup_ref, w0_ref, aup_ref, a0_ref, gup_ref,
                      kk_w_ref, ka_w_ref, rk_w_ref,
                      r_ref, kk_ref, v_ref, bonus_ref, g_ref, lw_ref, kd_ref, bb_ref):
    i = pl.program_id(1)
    last = pl.num_programs(1) - 1
    x = p_ref[0]
    prev_row = jnp.where(i == 0, 0.0, pp_ref[0, SUBLANES - 1:SUBLANES, :])
    next_row = jnp.where(i == last, 0.0, pn_ref[0, 0:1, :])
    prev, nxt = _shift_rows(x, prev_row, next_row)
    x = x + mu_ref[0:1, :] * (prev - x) + mu_ref[1:2, :] * (nxt - x)

    r = x[:, 0:GROUP_W]
    k = x[:, GROUP_W:2 * GROUP_W]
    v = x[:, 2 * GROUP_W:3 * GROUP_W]
    low = x[:, 3 * GROUP_W:3 * GROUP_W + 256]
    gd = x[:, 3 * GROUP_W + 256:3 * GROUP_W + 512]

    zw = _dot(jnp.tanh(low[:, 0:LANES]), wup_ref[...]) + w0_ref[...]
    za = _dot(low[:, LANES:2 * LANES], aup_ref[...]) + a0_ref[...]
    kkr = k * kk_w_ref[...]
    kk = kkr * lax.rsqrt(_head_sum(kkr * kkr, RWKV_HEAD) + EPS)
    r_ref[0] = r
    kk_ref[0] = kk
    v_ref[0] = v
    bonus_ref[0] = _head_sum(r * k * rk_w_ref[...], RWKV_HEAD) * v
    g_ref[0] = _dot(_sigmoid(gd), gup_ref[...])
    for d in range(2):
        sl = slice(d * GROUP_W, (d + 1) * GROUP_W)
        lw_ref[d, 0] = (-math.exp(-0.5)) * _sigmoid(zw[:, sl])
        a = _sigmoid(za[:, sl])
        kd_ref[d, 0] = k * (1.0 + (a - 1.0) * ka_w_ref[...])
        bb_ref[d, 0] = kk * a


def rwkv_prep(p, w):
    B, T, W = p.shape
    tm = 128
    prev_map, next_map = _halo_specs(tm, W)
    nb8 = T // SUBLANES
    full = lambda shape: pl.BlockSpec(shape, lambda b, i: tuple(0 for _ in shape))
    tok = pl.BlockSpec((1, tm, GROUP_W), lambda b, i: (b, i, 0))
    tokd = pl.BlockSpec((2, 1, tm, GROUP_W), lambda b, i: (0, b, i, 0))
    sh = jax.ShapeDtypeStruct((B, T, GROUP_W), F32)
    shd = jax.ShapeDtypeStruct((2, B, T, GROUP_W), F32)
    return pl.pallas_call(
        _rwkv_prep_kernel,
        grid=(B, T // tm),
        in_specs=[pl.BlockSpec((1, tm, W), lambda b, i: (b, i, 0)),
                  pl.BlockSpec((1, SUBLANES, W), prev_map),
                  pl.BlockSpec((1, SUBLANES, W), next_map(nb8)),
                  full((2, W)), full((LANES, 2 * GROUP_W)), full((1, 2 * GROUP_W)),
                  full((LANES, 2 * GROUP_W)), full((1, 2 * GROUP_W)), full((256, GROUP_W)),
                  full((1, GROUP_W)), full((1, GROUP_W)), full((1, GROUP_W))],
        out_specs=[tok, tok, tok, tok, tok, tokd, tokd, tokd],
        out_shape=[sh, sh, sh, sh, sh, shd, shd, shd],
        compiler_params=_cp("parallel", "parallel"),
        name="rwkv_prep",
    )(p, p, p, w["mu"], w["w_up"], w["w0"], w["a_up"], w["a0"], w["g_up"],
      w["k_k"], w["k_a"], w["r_k"])


def _dform(x, m0):
    return jnp.concatenate([jnp.where(m0, x, 0.0), jnp.where(m0, 0.0, x)], axis=0)


def _rwkv_chunk(tiles, sts, rev):
    n = len(tiles)
    C = tiles[0][0].shape[0]
    C2 = 2 * C
    sgn = jnp.where(rev, -1, 1)
    rr = lax.broadcasted_iota(jnp.int32, (C, C), 0)
    cc = lax.broadcasted_iota(jnp.int32, (C, C), 1)
    tri = jnp.where((rr - cc) * sgn >= 0, 1.0, 0.0).astype(BF16)
    lane = lax.broadcasted_iota(jnp.int32, (C, LANES), 1)
    m0 = lane < RWKV_HEAD
    rd = lax.broadcasted_iota(jnp.int32, (C2, C2), 0)
    cd = lax.broadcasted_iota(jnp.int32, (C2, C2), 1)
    same = (rd // C) == (cd // C)
    ahead = (rd % C - cd % C) * sgn
    strict = same & (ahead > 0)
    incl = same & (ahead >= 0)
    diag = rd == cd
    each = range(n)

    lins = [_dot_exact_lhs(tri, t[3]) for t in tiles]
    ltots = [jnp.sum(t[3], axis=0, keepdims=True) for t in tiles]
    a_d, r_d, k_d, b_d, v_d, kw_d, bw_d = [], [], [], [], [], [], []
    for (r, kk, v, lw, kd, bb), lin, ltot in zip(tiles, lins, ltots):
        einv = jnp.exp(-lin)
        ew = jnp.exp(ltot - lin)
        a_d.append(_dform(-kk * jnp.exp(lin - lw), m0).astype(BF16))
        r_d.append(_dform(r * jnp.exp(lin), m0).astype(BF16))
        k_d.append(_dform(kd * einv, m0).astype(BF16))
        b_d.append(_dform(bb * einv, m0).astype(BF16))
        v_d.append(_dform(v, m0).astype(BF16))
        kw_d.append(_dform(kd * ew, m0).astype(BF16))
        bw_d.append(_dform(bb * ew, m0).astype(BF16))

    grams = [_dot_nt(jnp.concatenate([a_d[i], r_d[i]], axis=0), jnp.concatenate([k_d[i], b_d[i]], axis=0))
             for i in each]
    a_ak = [jnp.where(strict, g[:C2, :C2], 0.0).astype(BF16) for g in grams]
    a_ab = [jnp.where(strict, g[:C2, C2:], 0.0) for g in grams]
    a_rk = [jnp.where(incl, g[C2:, :C2], 0.0).astype(BF16) for g in grams]
    a_rb = [jnp.where(incl, g[C2:, C2:], 0.0).astype(BF16) for g in grams]

    tinv = [jnp.where(diag, 1.0, 0.0) + a for a in a_ab]
    apow = a_ab
    for _ in range(int(math.log2(C)) - 1):
        apow = [_dot(a, a) for a in apow]
        tinv = [t + _dot(t, a) for t, a in zip(tinv, apow)]

    akv = [_dot(a_ak[i], v_d[i]) for i in each]
    p = [_dot(tinv[i], jnp.concatenate([a_d[i], akv[i].astype(BF16)], axis=1)) for i in each]
    ur = [_dot(jnp.concatenate([p[i][:, :LANES].astype(BF16), r_d[i]], axis=0), sts[i]) for i in each]
    u = [ur[i][:C2] + p[i][:, LANES:] for i in each]
    y2 = [ur[i][C2:] + _dot(a_rk[i], v_d[i]) + _dot(a_rb[i], u[i]) for i in each]
    ys = [y[:C] + y[C:] for y in y2]
    new = []
    for i in each:
        wc = jnp.broadcast_to(jnp.exp(ltots[i]), (LANES, LANES))
        wcol = jnp.sum(jnp.where(diag, wc, 0.0), axis=1, keepdims=True)
        new.append(sts[i] * wcol + _dot_tn(jnp.concatenate([bw_d[i], kw_d[i]], axis=0),
                                           jnp.concatenate([u[i].astype(BF16), v_d[i]], axis=0)))
    return ys, new


def _rwkv_scan_kernel(r_ref, kk_ref, v_ref, lw_ref, kd_ref, bb_ref, s0_ref, y_ref, sT_ref, st_scr,
                      *, npairs):
    d = pl.program_id(0)
    c = pl.program_id(3)
    nc = pl.num_programs(3)

    @pl.when(c == 0)
    def _():
        st_scr[...] = s0_ref[0, 0]

    lanes = [slice(j * LANES, (j + 1) * LANES) for j in range(npairs)]
    tiles = [(r_ref[0, :, sl], kk_ref[0, :, sl], v_ref[0, :, sl],
              lw_ref[0, 0, :, sl], kd_ref[0, 0, :, sl], bb_ref[0, 0, :, sl]) for sl in lanes]
    ys, new = _rwkv_chunk(tiles, [st_scr[j] for j in range(npairs)], d == 1)
    for j, sl in enumerate(lanes):
        y_ref[0, 0, :, sl] = ys[j]
        st_scr[j] = new[j]

    @pl.when(c == nc - 1)
    def _():
        sT_ref[0, 0] = st_scr[...]


def rwkv_scan(f, s0, npairs=RWKV_PAIRS):
    r, kk, v, lw, kd, bb = f["r"], f["kk"], f["v"], f["lw"], f["kd"], f["bb"]
    B, T, _ = r.shape
    C = RWKV_CHUNK
    nc = T // C
    ng = RWKV_PAIRS // npairs
    W = npairs * LANES
    chunk = lambda d, c: c + d * (nc - 1 - 2 * c)
    tok = pl.BlockSpec((1, C, W), lambda d, b, g, c: (b, chunk(d, c), g))
    tokd = pl.BlockSpec((1, 1, C, W), lambda d, b, g, c: (d, b, chunk(d, c), g))
    stsp = pl.BlockSpec((1, 1, npairs, LANES, LANES), lambda d, b, g, c: (d, b, g, 0, 0))
    return pl.pallas_call(
        functools.partial(_rwkv_scan_kernel, npairs=npairs),
        grid=(2, B, ng, nc),
        in_specs=[tok, tok, tok, tokd, tokd, tokd, stsp],
        out_specs=[tokd, stsp],
        out_shape=[jax.ShapeDtypeStruct((2, B, T, GROUP_W), F32),
                   jax.ShapeDtypeStruct(s0.shape, F32)],
        scratch_shapes=[pltpu.VMEM((npairs, LANES, LANES), F32)],
        compiler_params=_cp("parallel", "parallel", "parallel", "arbitrary"),
        name="rwkv_scan",
    )(r, kk, v, lw, kd, bb, s0)


def _rwkv_readout_kernel(yf_ref, yb_ref, bonus_ref, g_ref, ln_ref, o_ref):
    y = yf_ref[0, 0] + yb_ref[0, 0]
    mean = _head_sum(y, RWKV_HEAD) * (1.0 / RWKV_HEAD)
    cen = y - mean
    var = _head_sum(cen * cen, RWKV_HEAD) * (1.0 / RWKV_HEAD)
    yn = cen * lax.rsqrt(var + RWKV_GN_EPS) * ln_ref[0:1, :] + ln_ref[1:2, :]
    o_ref[0] = ((yn + bonus_ref[0]) * g_ref[0]).astype(o_ref.dtype)


def rwkv_readout(y, bonus, g, ln):
    _, B, T, W = y.shape
    tm = 256
    tok = pl.BlockSpec((1, tm, W), lambda b, i: (b, i, 0))
    return pl.pallas_call(
        _rwkv_readout_kernel,
        grid=(B, T // tm),
        in_specs=[pl.BlockSpec((1, 1, tm, W), lambda b, i: (0, b, i, 0)),
                  pl.BlockSpec((1, 1, tm, W), lambda b, i: (1, b, i, 0)),
                  tok, tok, pl.BlockSpec((2, W), lambda b, i: (0, 0))],
        out_specs=tok,
        out_shape=jax.ShapeDtypeStruct((B, T, W), BF16),
        compiler_params=_cp("parallel", "parallel"),
        name="rwkv_readout",
    )(y, y, bonus, g, ln)


def _pool_kernel(p_ref, pp_ref, pn_ref, w_ref, sc_ref, o_ref, *, total):
    i = pl.program_id(1)
    last = pl.num_programs(1) - 1
    x = p_ref[0]
    tm = x.shape[0]
    n = tm + 2 * POOL_HALO
    ext = jnp.concatenate([jnp.where(i == 0, 0.0, pp_ref[0]), x,
                           jnp.where(i == last, 0.0, pn_ref[0])], axis=0)
    t = i * tm + lax.broadcasted_iota(jnp.int32, (tm, 1), 0)
    outs = []
    for gi, win in enumerate(POOL_WINDOWS):
        sl = slice(gi * POOL_GROUP, (gi + 1) * POOL_GROUP)
        e = ext[:, sl]
        s = e + pltpu.roll(e, 1, axis=0)
        step = 1
        while 2 * step < win:
            s = pltpu.roll(s, step, axis=0) + pltpu.roll(s, n - step, axis=0)
            step *= 2
        h = win // 2
        cnt = (jnp.minimum(t + h, total) - jnp.maximum(t - h, 0)).astype(F32)
        z = s[POOL_HALO:POOL_HALO + tm] / cnt - x[:, sl]
        outs.append(_dot(z, w_ref[gi]))
    o_ref[0] = (jnp.concatenate(outs, axis=1) * sc_ref[...]).astype(o_ref.dtype)


def pool_mixer(p, pool_w, pool_scale):
    B, T, W = p.shape
    tm = 256
    prev_map, next_map = _halo_specs(tm, W)
    return pl.pallas_call(
        functools.partial(_pool_kernel, total=T),
        grid=(B, T // tm),
        in_specs=[pl.BlockSpec((1, tm, W), lambda b, i: (b, i, 0)),
                  pl.BlockSpec((1, POOL_HALO, W), prev_map),
                  pl.BlockSpec((1, POOL_HALO, W), next_map(T // POOL_HALO)),
                  pl.BlockSpec(pool_w.shape, lambda b, i: (0, 0, 0)),
                  pl.BlockSpec((1, W), lambda b, i: (0, 0))],
        out_specs=pl.BlockSpec((1, tm, W), lambda b, i: (b, i, 0)),
        out_shape=jax.ShapeDtypeStruct((B, T, W), BF16),
        compiler_params=_cp("parallel", "parallel"),
        name="pool_mixer",
    )(p, p, p, pool_w, pool_scale.reshape(1, W))


def _mlstm_prep_kernel(p_ref, pp_ref, pn_ref, g_ref, cw_ref, gb_ref, *rest, rope):
    if rope:
        cos_ref, sin_ref, q_ref, k_ref, go_ref = rest
    else:
        q_ref, k_ref, go_ref = rest
    i = pl.program_id(1)
    last = pl.num_programs(1) - 1
    x = p_ref[0]
    prev_row = jnp.where(i == 0, 0.0, pp_ref[0, SUBLANES - 1:SUBLANES, :])
    next_row = jnp.where(i == last, 0.0, pn_ref[0, 0:1, :])
    prev, nxt = _shift_rows(x, prev_row, next_row)
    qk = _silu(cw_ref[0:1, :] * prev + cw_ref[1:2, :] * x + cw_ref[2:3, :] * nxt)
    if rope:
        w = qk.shape[1]
        lane = lax.broadcasted_iota(jnp.int32, qk.shape, 1)
        partner = jnp.where(lane % 2 == 0, pltpu.roll(qk, w - 1, axis=1), pltpu.roll(qk, 1, axis=1))
        reps = w // LANES
        cos = jnp.concatenate([cos_ref[...]] * reps, axis=1)
        sin = jnp.concatenate([sin_ref[...]] * reps, axis=1)
        qk = qk * cos + partner * sin
    half = MLSTM_HEADS * MLSTM_DK
    q_ref[0] = qk[:, :half].astype(BF16)
    k_ref[0] = (qk[:, half:] * (MLSTM_DK ** -0.5)).astype(BF16)
    g = g_ref[0] + gb_ref[...]
    lane = lax.broadcasted_iota(jnp.int32, g.shape, 1)
    go_ref[0] = jnp.where((lane // MLSTM_HEADS) % 2 == 1, _log_sigmoid(g), g)


def mlstm_prep(p, conv_w, gate_b, rope_tabs):
    B, T, W = p.shape
    tm = 256
    QK = 2 * MLSTM_HEADS * MLSTM_DK
    prev_map, next_map = _halo_specs(tm, QK)
    rope = rope_tabs is not None
    in_specs = [pl.BlockSpec((1, tm, QK), lambda b, i: (b, i, 0)),
                pl.BlockSpec((1, SUBLANES, QK), prev_map),
                pl.BlockSpec((1, SUBLANES, QK), next_map(T // SUBLANES)),
                pl.BlockSpec((1, tm, LANES), lambda b, i: (b, i, (W - LANES) // LANES)),
                pl.BlockSpec((3, QK), lambda b, i: (0, 0)),
                pl.BlockSpec((1, LANES), lambda b, i: (0, 0))]
    args = [p, p, p, p, conv_w, gate_b]
    if rope:
        in_specs += [pl.BlockSpec((tm, LANES), lambda b, i: (i, 0))] * 2
        args += list(rope_tabs)
    half = MLSTM_HEADS * MLSTM_DK
    return pl.pallas_call(
        functools.partial(_mlstm_prep_kernel, rope=rope),
        grid=(B, T // tm),
        in_specs=in_specs,
        out_specs=[pl.BlockSpec((1, tm, half), lambda b, i: (b, i, 0)),
                   pl.BlockSpec((1, tm, half), lambda b, i: (b, i, 0)),
                   pl.BlockSpec((1, tm, LANES), lambda b, i: (b, i, 0))],
        out_shape=[jax.ShapeDtypeStruct((B, T, half), BF16),
                   jax.ShapeDtypeStruct((B, T, half), BF16),
                   jax.ShapeDtypeStruct((B, T, LANES), F32)],
        compiler_params=_cp("parallel", "parallel"),
        name="mlstm_prep",
    )(*args)


def _mlstm_chunk_kernel(q_ref, k_ref, v_ref, gc_ref, gr_ref, c0_ref, n0_ref, m0_ref,
                        h_ref, cT_ref, nT_ref, mT_ref, c_scr, n_scr, m_scr):
    d = pl.program_id(0)
    c = pl.program_id(2)
    nc = pl.num_programs(2)
    H = MLSTM_HEADS

    @pl.when(c == 0)
    def _():
        c_scr[...] = c0_ref[0, 0]
        n_scr[...] = n0_ref[0, 0]
        m_scr[...] = m0_ref[0, 0]

    rev = d == 1
    L = q_ref.shape[1]
    rr = lax.broadcasted_iota(jnp.int32, (L, L), 0)
    cc = lax.broadcasted_iota(jnp.int32, (L, L), 1)
    sgn = jnp.where(rev, -1, 1)
    seen = (rr - cc) * sgn >= 0
    tri = jnp.where(seen, 1.0, 0.0).astype(BF16)
    tri_t = jnp.where((cc - rr) * sgn >= 0, 1.0, 0.0).astype(BF16)

    gc = gc_ref[0]
    gr = gr_ref[0]
    lane = lax.broadcasted_iota(jnp.int32, gc.shape, 1)
    subl = lax.broadcasted_iota(jnp.int32, gr.shape, 0)
    pick_c = lambda a, idx: jnp.sum(jnp.where(lane == idx, a, 0.0), axis=1, keepdims=True)
    pick_r = lambda a, idx: jnp.sum(jnp.where(subl == idx, a, 0.0), axis=0, keepdims=True)
    bc_all = _dot_exact_lhs(tri, gc)
    gr_hi = gr.astype(BF16)
    gr_r1 = gr - gr_hi.astype(F32)
    gr_mid = gr_r1.astype(BF16)
    gr_lo = (gr_r1 - gr_mid.astype(F32)).astype(BF16)
    br_all = _dot(gr_hi, tri_t) + (_dot(gr_mid, tri_t) + _dot(gr_lo, tri_t))

    heads = range(H)
    i_lane = [d * (2 * H) + j for j in heads]
    f_lane = [d * (2 * H) + H + j for j in heads]
    ig_c = [pick_c(gc, i_lane[j]) for j in heads]
    lf_c = [pick_c(gc, f_lane[j]) for j in heads]
    b_c = [pick_c(bc_all, f_lane[j]) for j in heads]
    ig_r = [pick_r(gr, i_lane[j]) for j in heads]
    b_r = [pick_r(br_all, f_lane[j]) for j in heads]
    m_prev = [m_scr[j] for j in heads]
    q = [q_ref[0, :, j * MLSTM_DK:(j + 1) * MLSTM_DK] for j in heads]
    k = [k_ref[0, :, j * MLSTM_DK:(j + 1) * MLSTM_DK] for j in heads]
    v = [v_ref[0, :, j * MLSTM_DV:(j + 1) * MLSTM_DV].astype(BF16) for j in heads]
    cst = [c_scr[j] for j in heads]
    nst = [n_scr[j] for j in heads]

    qk = [_dot_nt(q[j], k[j]) for j in heads]
    qc = [_dot(q[j], cst[j]) for j in heads]
    dlog = [jnp.where(seen, b_c[j] - b_r[j] + ig_r[j], NEG) for j in heads]
    inter = [b_c[j] + m_prev[j] for j in heads]
    m_t = [jnp.maximum(inter[j], jnp.max(dlog[j], axis=1, keepdims=True)) for j in heads]
    s = [qk[j] * jnp.exp(dlog[j] - m_t[j]) for j in heads]
    w_inter = [jnp.exp(inter[j] - m_t[j]) for j in heads]
    sv = [_dot(s[j], v[j]) for j in heads]
    for j in heads:
        num = sv[j] + w_inter[j] * qc[j]
        den = (jnp.sum(s[j], axis=1, keepdims=True)
               + w_inter[j] * jnp.sum(q[j].astype(F32) * nst[j], axis=1, keepdims=True))
        h_ref[0, 0, :, j * MLSTM_DV:(j + 1) * MLSTM_DV] = num / jnp.maximum(jnp.abs(den), jnp.exp(-m_t[j]))

    b_end = [jnp.sum(lf_c[j], axis=0, keepdims=True) for j in heads]
    g_s = [b_end[j] - b_c[j] + ig_c[j] for j in heads]
    m_new = [jnp.maximum(b_end[j] + m_prev[j], jnp.max(g_s[j], axis=0, keepdims=True)) for j in heads]
    kw = [k[j].astype(F32) * jnp.exp(g_s[j] - m_new[j]) for j in heads]
    kv = [_dot_tn(kw[j], v[j]) for j in heads]
    for j in heads:
        decay = jnp.exp(b_end[j] + m_prev[j] - m_new[j])
        c_scr[j] = decay * cst[j] + kv[j]
        n_scr[j] = decay * nst[j] + jnp.sum(kw[j], axis=0, keepdims=True)
        m_scr[j] = m_new[j]

    @pl.when(c == nc - 1)
    def _():
        cT_ref[0, 0] = c_scr[...]
        nT_ref[0, 0] = n_scr[...]
        mT_ref[0, 0] = m_scr[...]


def mlstm_scan(q, k, p, gates, gates_t, state):
    B, T, _ = q.shape
    L = min(MLSTM_CHUNK, T)
    nc = T // L
    H = MLSTM_HEADS
    QK = H * MLSTM_DK
    c0, n0, m0 = state
    chunk = lambda d, c: c + d * (nc - 1 - 2 * c)
    v_blk = (2 * QK) // GROUP_W
    csp = pl.BlockSpec((1, 1, H, MLSTM_DK, MLSTM_DV), lambda d, b, c: (d, b, 0, 0, 0))
    nsp = pl.BlockSpec((1, 1, H, 1, MLSTM_DK), lambda d, b, c: (d, b, 0, 0, 0))
    msp = pl.BlockSpec((1, 1, H, 1, 1), lambda d, b, c: (d, b, 0, 0, 0))
    return pl.pallas_call(
        _mlstm_chunk_kernel,
        grid=(2, B, nc),
        in_specs=[pl.BlockSpec((1, L, QK), lambda d, b, c: (b, chunk(d, c), 0)),
                  pl.BlockSpec((1, L, QK), lambda d, b, c: (b, chunk(d, c), 0)),
                  pl.BlockSpec((1, L, GROUP_W), lambda d, b, c: (b, chunk(d, c), v_blk)),
                  pl.BlockSpec((1, L, LANES), lambda d, b, c: (b, chunk(d, c), 0)),
                  pl.BlockSpec((1, 4 * H, L), lambda d, b, c: (b, 0, chunk(d, c))),
                  csp, nsp, msp],
        out_specs=[pl.BlockSpec((1, 1, L, GROUP_W), lambda d, b, c: (d, b, chunk(d, c), 0)),
                   csp, nsp, msp],
        out_shape=[jax.ShapeDtypeStruct((2, B, T, GROUP_W), F32),
                   jax.ShapeDtypeStruct(c0.shape, F32),
                   jax.ShapeDtypeStruct(n0.shape, F32),
                   jax.ShapeDtypeStruct(m0.shape, F32)],
        scratch_shapes=[pltpu.VMEM((H, MLSTM_DK, MLSTM_DV), F32),
                        pltpu.VMEM((H, 1, MLSTM_DK), F32),
                        pltpu.VMEM((H, 1, 1), F32)],
        compiler_params=_cp("parallel", "parallel", "arbitrary"),
        name="mlstm_scan",
    )(q, k, p, gates, gates_t, c0, n0, m0)


def _mlstm_readout_kernel(hf_ref, hb_ref, o_ref, nw_ref, out_ref):
    h = hf_ref[0, 0] + hb_ref[0, 0]
    outs = []
    for j in range(MLSTM_HEADS):
        hj = h[:, j * MLSTM_DV:(j + 1) * MLSTM_DV]
        outs.append(hj * lax.rsqrt(jnp.mean(hj * hj, axis=-1, keepdims=True) + EPS))
    hn = jnp.concatenate(outs, axis=1)
    out_ref[0] = (hn * nw_ref[...] * _sigmoid(o_ref[0])).astype(out_ref.dtype)


def mlstm_readout(h, p, norm_w):
    _, B, T, W = h.shape
    tm = 256
    o_blk = (2 * MLSTM_HEADS * MLSTM_DK + GROUP_W) // GROUP_W
    return pl.pallas_call(
        _mlstm_readout_kernel,
        grid=(B, T // tm),
        in_specs=[pl.BlockSpec((1, 1, tm, W), lambda b, i: (0, b, i, 0)),
                  pl.BlockSpec((1, 1, tm, W), lambda b, i: (1, b, i, 0)),
                  pl.BlockSpec((1, tm, W), lambda b, i: (b, i, o_blk)),
                  pl.BlockSpec((1, W), lambda b, i: (0, 0))],
        out_specs=pl.BlockSpec((1, tm, W), lambda b, i: (b, i, 0)),
        out_shape=jax.ShapeDtypeStruct((B, T, W), BF16),
        compiler_params=_cp("parallel", "parallel"),
        name="mlstm_readout",
    )(h, h, p, norm_w.reshape(1, W))


def _na_prep_kernel(p_ref, nw_ref, q_ref, k_ref, v_ref):
    x = p_ref[0]
    for j in range(NA_HEADS):
        sl = slice(j * NA_HEAD, (j + 1) * NA_HEAD)
        for src, dst, row, scale in ((0, q_ref, 0, NA_HEAD ** -0.5), (GROUP_W, k_ref, 1, 1.0)):
            z = x[:, src + j * NA_HEAD:src + (j + 1) * NA_HEAD]
            zn = z * lax.rsqrt(jnp.mean(z * z, axis=-1, keepdims=True) + EPS) * nw_ref[row:row + 1, :]
            dst[0, :, sl] = (zn * scale).astype(BF16)
    v_ref[0] = x[:, 2 * GROUP_W:].astype(BF16)


def na_prep(p, qk_norm):
    B, T, W = p.shape
    tm = 256
    tok = pl.BlockSpec((1, tm, GROUP_W), lambda b, i: (b, i, 0))
    sh = jax.ShapeDtypeStruct((B, T, GROUP_W), BF16)
    return pl.pallas_call(
        _na_prep_kernel,
        grid=(B, T // tm),
        in_specs=[pl.BlockSpec((1, tm, W), lambda b, i: (b, i, 0)),
                  pl.BlockSpec((2, NA_HEAD), lambda b, i: (0, 0))],
        out_specs=[tok, tok, tok],
        out_shape=[sh, sh, sh],
        compiler_params=_cp("parallel", "parallel"),
        name="na_prep",
    )(p, qk_norm)


def _softmax_rows(s):
    m = jnp.max(s, axis=1, keepdims=True)
    e = jnp.exp(s - m)
    return e / jnp.sum(e, axis=1, keepdims=True)


def _na_ctx_kernel(q_ref, k_ref, v_ref, o_ref):
    for j in range(NA_HEADS):
        sl = slice(j * NA_HEAD, (j + 1) * NA_HEAD)
        pr = _softmax_rows(_dot_nt(q_ref[0, :, sl], k_ref[0, :, sl]))
        o_ref[0, :, sl] = _dot(pr, v_ref[0, :, sl]).astype(o_ref.dtype)


def na_ctx_attention(q, k, v):
    B, T, W = q.shape
    tok = pl.BlockSpec((1, T, W), lambda b: (b, 0, 0))
    return pl.pallas_call(
        _na_ctx_kernel,
        grid=(B,),
        in_specs=[tok, tok, tok],
        out_specs=tok,
        out_shape=jax.ShapeDtypeStruct((B, T, W), BF16),
        compiler_params=_cp("parallel"),
        name="na_ctx_attention",
    )(q, k, v)


def _na_lat_kernel(q_ref, k0_ref, k1_ref, k2_ref, v0_ref, v1_ref, v2_ref, kc_ref, vc_ref, bias_ref, o_ref):
    for j in range(NA_HEADS):
        sl = slice(j * NA_HEAD, (j + 1) * NA_HEAD)
        q = q_ref[0, :, sl]
        kcat = jnp.concatenate([k0_ref[0, :, sl], k1_ref[0, :, sl], k2_ref[0, :, sl]], axis=0)
        vcat = jnp.concatenate([v0_ref[0, :, sl], v1_ref[0, :, sl], v2_ref[0, :, sl],
                                vc_ref[0, :, sl]], axis=0)
        s = jnp.concatenate([_dot_nt(q, kcat) + bias_ref[0, j], _dot_nt(q, kc_ref[0, :, sl])], axis=1)
        o_ref[0, :, sl] = _dot(_softmax_rows(s), vcat).astype(o_ref.dtype)


def na_bias_table(rpb, rows):
    nq, nk = NA_QROWS, NA_KROWS
    starts = np.array([0, nq, rows - nq])
    r = starts[:, None] + np.arange(nq)[None, :]
    kr = starts[:, None] - nq + np.arange(nk)[None, :]
    r0 = np.clip(r - WIN_H // 2, 0, rows - WIN_H)
    rv = (kr[:, None, :] >= r0[:, :, None]) & (kr[:, None, :] < r0[:, :, None] + WIN_H)
    dr = kr[:, None, :] - r[:, :, None] + WIN_H - 1
    col = np.arange(GRID_W)
    c0 = np.clip(col - WIN_W // 2, 0, GRID_W - WIN_W)
    cv = (col[None, :] >= c0[:, None]) & (col[None, :] < c0[:, None] + WIN_W)
    pad = GRID_W - WIN_W
    rpb_pad = jnp.pad(rpb, ((0, 0), (0, 0), (pad, pad)))
    colbias = jnp.stack([rpb_pad[:, :, GRID_W - 1 - q:2 * GRID_W - 1 - q] for q in range(GRID_W)], axis=2)
    colbias = jnp.where(cv[None, None], colbias, NEG)
    blank = jnp.full((NA_HEADS, GRID_W, GRID_W), NEG, F32)
    pats = []
    for p in range(3):
        qrows = [jnp.concatenate([colbias[:, int(dr[p, i, j])] if rv[p, i, j] else blank
                                  for j in range(nk)], axis=2) for i in range(nq)]
        pats.append(jnp.concatenate(qrows, axis=1))
    return jnp.stack(pats)


def na_lat_attention(q, k, v, kc, vc, bias):
    B, T, W = q.shape
    tq = NA_QROWS * GRID_W
    nb = T // tq
    ctx_len = kc.shape[1]
    qsp = pl.BlockSpec((1, tq, W), lambda b, i: (b, i, 0))
    prv = pl.BlockSpec((1, tq, W), lambda b, i: (b, jnp.maximum(i - 1, 0), 0))
    nxt = pl.BlockSpec((1, tq, W), lambda b, i: (b, jnp.minimum(i + 1, nb - 1), 0))
    csp = pl.BlockSpec((1, ctx_len, W), lambda b, i: (b, 0, 0))
    pattern = lambda i: jnp.where(i == 0, 0, jnp.where(i == nb - 1, 2, 1))
    return pl.pallas_call(
        _na_lat_kernel,
        grid=(B, nb),
        in_specs=[qsp, prv, qsp, nxt, prv, qsp, nxt, csp, csp,
                  pl.BlockSpec((1, NA_HEADS, tq, NA_KROWS * GRID_W), lambda b, i: (pattern(i), 0, 0, 0))],
        out_specs=qsp,
        out_shape=jax.ShapeDtypeStruct((B, T, W), BF16),
        compiler_params=_cp("parallel", "arbitrary"),
        name="na_lat_attention",
    )(q, k, k, k, v, v, v, kc, vc, bias)


def _ffn_norm_router_body(h_ref, g_ref, sh_ref, sc_ref, rw_ref, rb_ref, v_ref, ids_ref, wts_ref):
    x = h_ref[0]
    y = x * lax.rsqrt(jnp.mean(x * x, axis=-1, keepdims=True) + EPS) * g_ref[...]
    v = y * (1.0 + sc_ref[0]) + sh_ref[0]
    v_ref[...] = v
    vh, vl = _split2(v)
    logits = (_dot(vh, rw_ref[0]) + (_dot(vh, rw_ref[1]) + _dot(vl, rw_ref[0]))) + rb_ref[...]
    lane = lax.broadcasted_iota(jnp.int32, logits.shape, 1).astype(F32)
    first = lambda mask: jnp.min(jnp.where(mask, lane, float(LANES)), axis=1, keepdims=True)

    g_mask = lane < N_GROUPS
    gl = jnp.where(g_mask, logits, NEG)
    gmax = jnp.max(gl, axis=1, keepdims=True)
    g_p = 1.0 / jnp.sum(jnp.exp(gl - gmax), axis=1, keepdims=True)
    g_idx = first(g_mask & (gl == gmax))
    e_lane = lane - N_GROUPS
    e_mask = (e_lane >= 0) & (e_lane < N_EXPERTS) & (jnp.floor(e_lane * (1.0 / EXPERTS_PER_GROUP)) == g_idx)
    el = jnp.where(e_mask, logits, NEG)
    e1 = jnp.max(el, axis=1, keepdims=True)
    i1 = first(e_mask & (el == e1))
    el2 = jnp.where(lane == i1, NEG, el)
    e2 = jnp.max(el2, axis=1, keepdims=True)
    i2 = first(e_mask & (lane != i1) & (el2 == e2))
    x2 = jnp.exp(e2 - e1)
    w1 = g_p / (1.0 + x2)
    w2 = g_p * x2 / (1.0 + x2)
    ids_ref[...] = jnp.where(lane == 0, i1 - N_GROUPS, jnp.where(lane == 1, i2 - N_GROUPS, 0.0)).astype(jnp.int32)
    wts_ref[...] = jnp.where(lane == 0, w1, jnp.where(lane == 1, w2, 0.0))


def _ffn_norm_router_kernel(*refs, bounds):
    ns = len(bounds) - 1
    g_ref, rw_ref, rb_ref, v_ref, ids_ref, wts_ref = refs[3 * ns:]
    r = pl.program_id(0)
    for s in range(ns):
        h_ref, sh_ref, sc_ref = refs[3 * s:3 * s + 3]

        @pl.when((r >= bounds[s]) & (r < bounds[s + 1]))
        def _():
            _ffn_norm_router_body(h_ref, g_ref, sh_ref, sc_ref, rw_ref, rb_ref, v_ref, ids_ref, wts_ref)


def ffn_norm_router(streams, g, rw, rb):
    D = g.shape[0]
    tm = 256
    bounds = [0]
    in_specs, args = [], []
    for h, shift, scale in streams:
        B, T, _ = h.shape
        nt = T // tm
        lo, nblk = bounds[-1], B * nt
        bounds.append(lo + nblk)
        local = lambda r, lo=lo, nblk=nblk: jnp.clip(r - lo, 0, nblk - 1)
        in_specs += [pl.BlockSpec((1, tm, D), lambda r, f=local, nt=nt: (f(r) // nt, f(r) % nt, 0)),
                     pl.BlockSpec((1, 1, D), lambda r, f=local, nt=nt: (f(r) // nt, 0, 0)),
                     pl.BlockSpec((1, 1, D), lambda r, f=local, nt=nt: (f(r) // nt, 0, 0))]
        args += [h, shift, scale]
    n_rows = bounds[-1] * tm
    in_specs += [pl.BlockSpec((1, D), lambda r: (0, 0)),
                 pl.BlockSpec((2, D, LANES), lambda r: (0, 0, 0)),
                 pl.BlockSpec((1, LANES), lambda r: (0, 0))]
    args += [g.reshape(1, D), rw, rb]
    lan = pl.BlockSpec((tm, LANES), lambda r: (r, 0))
    return pl.pallas_call(
        functools.partial(_ffn_norm_router_kernel, bounds=tuple(bounds)),
        grid=(bounds[-1],),
        in_specs=in_specs,
        out_specs=[pl.BlockSpec((tm, D), lambda r: (r, 0)), lan, lan],
        out_shape=[jax.ShapeDtypeStruct((n_rows, D), F32),
                   jax.ShapeDtypeStruct((n_rows, LANES), jnp.int32),
                   jax.ShapeDtypeStruct((n_rows, LANES), F32)],
        compiler_params=_cp("parallel"),
        name="ffn_norm_router",
    )(*args)


def _moe_kernel(be_ref, zero_ref, tok_ref, tokn_ref, dstp_ref, sw_ref, v_hbm, wg_hbm, wu_hbm, wd_hbm, out_hbm,
                x0, x1, y0, y1, wg_buf, wu_buf, wd_buf, gsem, ssem, wsem):
    i = pl.program_id(0)
    nb = pl.num_programs(0) - 1
    xs, ys = (x0, x1), (y0, y1)
    zero = zero_ref[0]

    def gather_row(idx_ref, s, j, after=0):
        return pltpu.make_async_copy(v_hbm.at[pl.ds(idx_ref[0, 0, j] + after, 1)], xs[s].at[pl.ds(j, 1)],
                                     gsem.at[s])

    def wait_gather(s):
        pltpu.make_async_copy(v_hbm.at[pl.ds(0, MOE_BLOCK)], xs[s], gsem.at[s]).wait()

    def scatter_row(s, j, after=0):
        return pltpu.make_async_copy(ys[s].at[pl.ds(j, 1)], out_hbm.at[pl.ds(dstp_ref[0, 0, j] + after, 1)],
                                     ssem.at[s])

    def wait_scatter(s):
        pltpu.make_async_copy(ys[s], out_hbm.at[pl.ds(0, MOE_BLOCK)], ssem.at[s]).wait()

    def after(result):
        return result[0, 0].astype(jnp.int32) * zero

    @pl.when(i == 0)
    def _():
        def body(j, carry):
            gather_row(tok_ref, 0, j).start()
            return carry
        lax.fori_loop(0, MOE_BLOCK, body, 0)

    e = be_ref[jnp.minimum(i, nb - 1)]

    @pl.when((i == 0) | ((i < nb) & (e != be_ref[jnp.maximum(i - 1, 0)])))
    def _():
        copies = [pltpu.make_async_copy(wg_hbm.at[e], wg_buf, wsem.at[0]),
                  pltpu.make_async_copy(wu_hbm.at[e], wu_buf, wsem.at[1]),
                  pltpu.make_async_copy(wd_hbm.at[e], wd_buf, wsem.at[2])]
        for cp in copies:
            cp.start()
        for cp in copies:
            cp.wait()

    def step(s, scatter_prev):
        wait_gather(s)

        @pl.when(i >= 2)
        def _():
            wait_scatter(s)

        x = xs[s][...].astype(BF16)
        n_up = EXPERT_FF // MOE_FF_TILE
        per = MOE_BLOCK // (2 * n_up)
        acts = []
        for c in range(n_up):
            cols = slice(c * MOE_FF_TILE, (c + 1) * MOE_FF_TILE)
            hg = jnp.dot(x, wg_buf[:, cols], preferred_element_type=F32)
            hu = jnp.dot(x, wu_buf[:, cols], preferred_element_type=F32)
            for g, res in enumerate((hg, hu)):
                dep = after(res)
                for j in range((2 * c + g) * per, (2 * c + g + 1) * per):
                    gather_row(tokn_ref, 1 - s, j, dep).start()
            acts.append((_silu(hg) * hu).astype(BF16))
        act = jnp.concatenate(acts, axis=1)
        n_down = x.shape[1] // MOE_OUT_TILE
        per = MOE_BLOCK // n_down
        for c in range(n_down):
            cols = slice(c * MOE_OUT_TILE, (c + 1) * MOE_OUT_TILE)
            y = jnp.dot(act, wd_buf[:, cols], preferred_element_type=F32) * sw_ref[...]
            ys[s][:, cols] = y
            if scatter_prev:
                dep = after(y)
                for j in range(c * per, (c + 1) * per):
                    scatter_row(1 - s, j, dep).start()

    pl.when(i == 0)(functools.partial(step, 0, False))
    for s in range(2):
        pl.when((i > 0) & (i < nb) & (i % 2 == s))(functools.partial(step, s, True))

    @pl.when(i == nb)
    def _():
        def drain(s):
            wait_gather(s)
            wait_scatter(s)

            def body(j, carry):
                scatter_row(1 - s, j).start()
                return carry
            lax.fori_loop(0, MOE_BLOCK, body, 0)
            wait_scatter(1 - s)
        for s in range(2):
            pl.when(i % 2 == s)(functools.partial(drain, s))


def moe_experts(v_rows, block_expert, slot_tok, slot_dst, slot_w, wg, wu, wd):
    P = slot_tok.shape[0]
    D = v_rows.shape[1]
    nb = P // MOE_BLOCK
    assert nb >= 2
    idx = lambda a: a.reshape(nb, 1, MOE_BLOCK)
    smem = lambda imap: pl.BlockSpec((1, 1, MOE_BLOCK), imap, memory_space=pltpu.SMEM)
    hbm = pl.BlockSpec(memory_space=pl.ANY)
    last = nb - 1
    grid_spec = pltpu.PrefetchScalarGridSpec(
        num_scalar_prefetch=2,
        grid=(nb + 1,),
        in_specs=[smem(lambda i, be, z: (jnp.minimum(i, last), 0, 0)),
                  smem(lambda i, be, z: (jnp.minimum(i + 1, last), 0, 0)),
                  smem(lambda i, be, z: (jnp.maximum(i - 1, 0), 0, 0)),
                  pl.BlockSpec((MOE_BLOCK, 1), lambda i, be, z: (jnp.minimum(i, last), 0)),
                  hbm, hbm, hbm, hbm],
        out_specs=hbm,
        scratch_shapes=[pltpu.VMEM((MOE_BLOCK, D), F32),
                        pltpu.VMEM((MOE_BLOCK, D), F32),
                        pltpu.VMEM((MOE_BLOCK, D), F32),
                        pltpu.VMEM((MOE_BLOCK, D), F32),
                        pltpu.VMEM((D, EXPERT_FF), BF16),
                        pltpu.VMEM((D, EXPERT_FF), BF16),
                        pltpu.VMEM((EXPERT_FF, D), BF16),
                        pltpu.SemaphoreType.DMA((2,)),
                        pltpu.SemaphoreType.DMA((2,)),
                        pltpu.SemaphoreType.DMA((3,))],
    )
    return pl.pallas_call(
        _moe_kernel,
        grid_spec=grid_spec,
        out_shape=jax.ShapeDtypeStruct((P, D), F32),
        compiler_params=_cp("arbitrary"),
        name="moe_experts",
    )(block_expert, jnp.zeros((1,), jnp.int32), idx(slot_tok), idx(slot_tok), idx(slot_dst),
      slot_w.reshape(P, 1), v_rows, wg, wu, wd)


def _moe_combine_kernel(h_ref, f0_ref, f1_ref, gate_ref, o_ref):
    o_ref[0] = h_ref[0] + gate_ref[0] * (f0_ref[...] + f1_ref[...])


def moe_combine(h, out2, gate, row_offset, n_rows):
    B, T, D = h.shape
    tm = 256
    nt = T // tm
    blk0 = row_offset // tm
    tok = pl.BlockSpec((1, tm, D), lambda b, i: (b, i, 0))
    return pl.pallas_call(
        _moe_combine_kernel,
        grid=(B, nt),
        in_specs=[tok,
                  pl.BlockSpec((tm, D), lambda b, i: (blk0 + b * nt + i, 0)),
                  pl.BlockSpec((tm, D), lambda b, i: (n_rows // tm + blk0 + b * nt + i, 0)),
                  pl.BlockSpec((1, 1, D), lambda b, i: (b, 0, 0))],
        out_specs=tok,
        out_shape=jax.ShapeDtypeStruct((B, T, D), F32),
        compiler_params=_cp("parallel", "parallel"),
        name="moe_combine",
    )(h, out2, out2, gate)


def moe_dispatch(ids, wts):
    N = ids.shape[0]
    A = N * TOP_K
    e_flat = ids.reshape(A)
    order = jnp.argsort(e_flat).astype(jnp.int32)
    counts = jnp.sum((e_flat[:, None] == jnp.arange(N_EXPERTS)[None, :]).astype(jnp.int32), axis=0)
    padded = (counts + MOE_BLOCK - 1) // MOE_BLOCK * MOE_BLOCK
    ends = jnp.cumsum(padded)
    n_blocks = -(-(A + N_EXPERTS * (MOE_BLOCK - 1)) // MOE_BLOCK)
    block_start = jnp.arange(n_blocks, dtype=jnp.int32) * MOE_BLOCK
    block_expert = jnp.minimum(jnp.searchsorted(ends, block_start, side="right"),
                               N_EXPERTS - 1).astype(jnp.int32)
    into = block_start - (ends - padded)[block_expert]
    b_count = counts[block_expert]
    local = (into[:, None] + jnp.arange(MOE_BLOCK, dtype=jnp.int32)[None, :])
    real = local < b_count[:, None]
    src = jnp.clip((jnp.cumsum(counts) - counts)[block_expert][:, None] + local, 0, A - 1)
    a = order[src.reshape(-1)]
    real = real.reshape(-1)
    tok = a // TOP_K
    slot_tok = jnp.where(real, tok, 0).astype(jnp.int32)
    spare = A + jnp.cumsum(jnp.where(real, 0, 1)) - 1
    slot_dst = jnp.where(real, (a % TOP_K) * N + tok, spare).astype(jnp.int32)
    slot_w = jnp.where(real, wts.reshape(A)[a], 0.0)
    return slot_tok, slot_dst, slot_w, block_expert


def hier_moe(v_rows, ids, wts, wg, wu, wd):
    slot_tok, slot_dst, slot_w, block_expert = moe_dispatch(ids, wts)
    return moe_experts(v_rows, block_expert, slot_tok, slot_dst, slot_w, wg, wu, wd)


def _layer_weights(l, P):
    w = {}
    w_in = P["w_in"][l]
    o = 0
    a = w_in[:, o:o + RWKV_COLS]; o += RWKV_COLS
    b = w_in[:, o:o + GROUP_W]; o += GROUP_W
    c = w_in[:, o:o + MLSTM_COLS]; o += MLSTM_COLS
    d = w_in[:, o:o + NA_COLS]
    D = w_in.shape[0]
    pad_a = RWKV_PAD_COLS - RWKV_COLS
    w["in_a"] = jnp.concatenate([a, jnp.zeros((D, pad_a), F32)], axis=1).astype(BF16)
    w["in_b"] = b.astype(BF16)
    w["in_c"] = jnp.concatenate([c, jnp.zeros((D, MLSTM_PAD_COLS - MLSTM_COLS), F32)], axis=1).astype(BF16)
    w["in_d"] = d.astype(BF16)
    w["out"] = P["w_out"][l].astype(BF16)

    z = jnp.zeros((RWKV_DECAY_RANK, GROUP_W), F32)
    wup, aup = P["rwkv_w_up"][l], P["rwkv_a_up"][l]
    rw = {
        "mu": jnp.concatenate([P["rwkv_mu"][l], jnp.zeros((2, pad_a), F32)], axis=1),
        "w_up": jnp.concatenate([jnp.concatenate([wup[0], z], axis=1),
                                 jnp.concatenate([z, wup[1]], axis=1)], axis=0).astype(BF16),
        "a_up": jnp.concatenate([jnp.concatenate([aup[0], z], axis=1),
                                 jnp.concatenate([z, aup[1]], axis=1)], axis=0).astype(BF16),
        "w0": P["rwkv_w0"][l].reshape(1, 2 * GROUP_W),
        "a0": P["rwkv_a0"][l].reshape(1, 2 * GROUP_W),
        "g_up": jnp.concatenate([P["rwkv_g_up"][l],
                                 jnp.zeros((256 - RWKV_GATE_RANK, GROUP_W), F32)], axis=0).astype(BF16),
        "k_k": P["rwkv_k_k"][l].reshape(1, GROUP_W),
        "k_a": P["rwkv_k_a"][l].reshape(1, GROUP_W),
        "r_k": P["rwkv_r_k"][l].reshape(1, GROUP_W),
    }
    w["rwkv"] = rw
    w["rwkv_ln"] = P["rwkv_ln"][l]
    w["pool_w"] = P["pool_w"][l].astype(BF16)
    w["pool_scale"] = P["pool_scale"][l]
    w["mlstm_conv"] = P["mlstm_conv"][l]
    gb = P["mlstm_gate_b"][l].reshape(1, 4 * MLSTM_HEADS)
    w["mlstm_gate_b"] = jnp.concatenate([gb, jnp.zeros((1, LANES - 4 * MLSTM_HEADS), F32)], axis=1)
    w["mlstm_norm"] = P["mlstm_norm"][l]
    w["na_qk_norm"] = P["na_qk_norm"][l]
    w["na_rpb"] = P["na_rpb"][l]
    rcat = jnp.concatenate([P["router_g_w"][l], P["router_e_w"][l],
                            jnp.zeros((D, LANES - N_GROUPS - N_EXPERTS), F32)], axis=1)
    rhi = rcat.astype(BF16)
    w["router_w"] = jnp.stack([rhi, (rcat - rhi.astype(F32)).astype(BF16)])
    w["router_b"] = jnp.concatenate([P["router_g_b"][l], P["router_e_b"][l],
                                     jnp.zeros((LANES - N_GROUPS - N_EXPERTS,), F32)]).reshape(1, LANES)
    w["moe_gate"] = P["moe_w_gate"][l].astype(BF16)
    w["moe_up"] = P["moe_w_up"][l].astype(BF16)
    w["moe_down"] = P["moe_w_down"][l].astype(BF16)
    return w


def rope_tables(T):
    t = jnp.arange(T)
    row = (t // GRID_W).astype(F32)
    col = (t % GRID_W).astype(F32)
    n_pairs = MLSTM_DK // 4
    inv = ROPE_BASE ** (-jnp.arange(n_pairs, dtype=F32) / n_pairs)
    ang = jnp.concatenate([row[:, None] * inv, col[:, None] * inv], axis=-1)
    cos = jnp.repeat(jnp.cos(ang), 2, axis=1)
    sin = jnp.repeat(jnp.sin(ang), 2, axis=1) * jnp.tile(jnp.array([-1.0, 1.0], F32), MLSTM_DK // 2)
    return cos, sin


def rwkv_mixer(p_ctx, p_lat, w, need_ctx):
    B = p_lat.shape[0]
    names = ("r", "kk", "v", "bonus", "g", "lw", "kd", "bb")
    fc = dict(zip(names, rwkv_prep(p_ctx, w["rwkv"])))
    fl = dict(zip(names, rwkv_prep(p_lat, w["rwkv"])))
    s0 = jnp.zeros((2, B, RWKV_PAIRS, LANES, LANES), F32)
    y_c, s_ctx = rwkv_scan(fc, s0)
    y_l, _ = rwkv_scan(fl, s_ctx)
    out_l = rwkv_readout(y_l, fl["bonus"], fl["g"], w["rwkv_ln"])
    out_c = rwkv_readout(y_c, fc["bonus"], fc["g"], w["rwkv_ln"]) if need_ctx else None
    return out_c, out_l


def mlstm_mixer(p_ctx, p_lat, w, rope, need_ctx):
    B = p_lat.shape[0]
    H = MLSTM_HEADS
    state = (jnp.zeros((2, B, H, MLSTM_DK, MLSTM_DV), F32),
             jnp.zeros((2, B, H, 1, MLSTM_DK), F32),
             jnp.zeros((2, B, H, 1, 1), F32))
    outs = []
    for p, tabs in ((p_ctx, None), (p_lat, rope)):
        q, k, gates = mlstm_prep(p, w["mlstm_conv"], w["mlstm_gate_b"], tabs)
        gates_t = jnp.swapaxes(gates[:, :, :4 * H], 1, 2)
        h, *state = mlstm_scan(q, k, p, gates, gates_t, tuple(state))
        outs.append(h)
    out_l = mlstm_readout(outs[1], p_lat, w["mlstm_norm"])
    out_c = mlstm_readout(outs[0], p_ctx, w["mlstm_norm"]) if need_ctx else None
    return out_c, out_l


def na_mixer(p_ctx, p_lat, w, need_ctx):
    qc, kc, vc = na_prep(p_ctx, w["na_qk_norm"])
    ql, kl, vl = na_prep(p_lat, w["na_qk_norm"])
    rows = p_lat.shape[1] // GRID_W
    bias = na_bias_table(w["na_rpb"], rows)
    out_l = na_lat_attention(ql, kl, vl, kc, vc, bias)
    out_c = na_ctx_attention(qc, kc, vc) if need_ctx else None
    return out_c, out_l


def token_mixers(u_ctx, u_lat, w, rope, need_ctx):
    outs_c, outs_l = [], []
    pa = (matmul(u_ctx, w["in_a"]), matmul(u_lat, w["in_a"]))
    oc, ol = rwkv_mixer(pa[0], pa[1], w, need_ctx)
    outs_c.append(oc); outs_l.append(ol)
    pb = (matmul(u_ctx, w["in_b"]), matmul(u_lat, w["in_b"]))
    outs_l.append(pool_mixer(pb[1], w["pool_w"], w["pool_scale"]))
    outs_c.append(pool_mixer(pb[0], w["pool_w"], w["pool_scale"]) if need_ctx else None)
    pc = (matmul(u_ctx, w["in_c"]), matmul(u_lat, w["in_c"]))
    oc, ol = mlstm_mixer(pc[0], pc[1], w, rope, need_ctx)
    outs_c.append(oc); outs_l.append(ol)
    pd = (matmul(u_ctx, w["in_d"]), matmul(u_lat, w["in_d"]))
    oc, ol = na_mixer(pd[0], pd[1], w, need_ctx)
    outs_c.append(oc); outs_l.append(ol)
    return outs_c, outs_l


def kernel(x, c, ctx, c_ctx, ada_w, ada_b, norm_mix, norm_ffn, w_in, w_out, rwkv_mu, rwkv_w0, rwkv_w_up, rwkv_a0, rwkv_a_up, rwkv_g_up, rwkv_k_k, rwkv_k_a, rwkv_r_k, rwkv_ln, pool_w, pool_scale, mlstm_conv, mlstm_gate_b, mlstm_norm, na_qk_norm, na_rpb, router_g_w, router_g_b, router_e_w, router_e_b, moe_w_gate, moe_w_up, moe_w_down):
    P = dict(w_in=w_in, w_out=w_out, rwkv_mu=rwkv_mu, rwkv_w0=rwkv_w0, rwkv_w_up=rwkv_w_up,
             rwkv_a0=rwkv_a0, rwkv_a_up=rwkv_a_up, rwkv_g_up=rwkv_g_up, rwkv_k_k=rwkv_k_k,
             rwkv_k_a=rwkv_k_a, rwkv_r_k=rwkv_r_k, rwkv_ln=rwkv_ln, pool_w=pool_w, pool_scale=pool_scale,
             mlstm_conv=mlstm_conv, mlstm_gate_b=mlstm_gate_b, mlstm_norm=mlstm_norm,
             na_qk_norm=na_qk_norm, na_rpb=na_rpb, router_g_w=router_g_w, router_g_b=router_g_b,
             router_e_w=router_e_w, router_e_b=router_e_b, moe_w_gate=moe_w_gate, moe_w_up=moe_w_up,
             moe_w_down=moe_w_down)
    B, T, D = x.shape
    Lc = ctx.shape[1]
    depth = ada_w.shape[0]
    rope = rope_tables(T)

    cvec = jnp.concatenate([c, c_ctx[None, :], jnp.zeros((SUBLANES - B - 1, D), F32)], axis=0)
    mods = ada_modulation(cvec, ada_w, ada_b)

    h_lat, h_ctx = x, ctx
    for l in range(depth):
        last = l == depth - 1
        w = _layer_weights(l, P)
        m = mods[l].reshape(SUBLANES, 6, D)
        m_lat = [m[:B, i][:, None, :] for i in range(6)]
        m_ctx = [jnp.broadcast_to(m[B, i][None, None, :], (B, 1, D)) for i in range(6)]

        u_lat = norm_modulate(h_lat, norm_mix[l], m_lat[0], m_lat[1])
        u_ctx = norm_modulate(h_ctx, norm_mix[l], m_ctx[0], m_ctx[1])
        ys_ctx, ys_lat = token_mixers(u_ctx, u_lat, w, rope, not last)
        h_lat = out_proj_residual(ys_lat, w["out"], h_lat, m_lat[2])
        n_rows = B * T if last else B * (T + Lc)
        streams = [(h_lat, m_lat[3], m_lat[4])]
        if not last:
            h_ctx = out_proj_residual(ys_ctx, w["out"], h_ctx, m_ctx[2])
            streams.append((h_ctx, m_ctx[3], m_ctx[4]))
        rows, ids, wts = ffn_norm_router(streams, norm_ffn[l], w["router_w"], w["router_b"])
        f2 = hier_moe(rows, ids[:, :TOP_K], wts[:, :TOP_K], w["moe_gate"], w["moe_up"], w["moe_down"])
        if not last:
            h_ctx = moe_combine(h_ctx, f2, m_ctx[5], B * T, n_rows)
        h_lat = moe_combine(h_lat, f2, m_lat[5], 0, n_rows)
    return h_lat
```

```python
import functools
import math

import numpy as np
import jax
import jax.numpy as jnp
from jax import lax
from jax.experimental import pallas as pl
from jax.experimental.pallas import tpu as pltpu

F32 = jnp.float32
BF16 = jnp.bfloat16

D_MODEL = 4096
DEPTH = 2
GRID_W = 64
EPS = 1e-6
GROUP_W = 1024

RWKV_HEAD = 64
RWKV_DECAY_RANK = 64
RWKV_ICLR_RANK = 64
RWKV_GATE_RANK = 160
RWKV_GN_EPS = 64e-5
RWKV_COLS = 3 * GROUP_W + 2 * RWKV_DECAY_RANK + 2 * RWKV_ICLR_RANK + RWKV_GATE_RANK
RWKV_PAD_COLS = 3 * GROUP_W + 256 + 256

POOL_WINDOWS = (2, 4, 8, 16)
POOL_GROUP = 256
POOL_HALO = 8

MLSTM_HEADS = 4
MLSTM_DV = 256
MLSTM_DK = 128
MLSTM_COLS = 2 * MLSTM_HEADS * MLSTM_DK + 2 * GROUP_W + 4 * MLSTM_HEADS
MLSTM_PAD_COLS = 2 * MLSTM_HEADS * MLSTM_DK + 2 * GROUP_W + 128
ROPE_BASE = 10000.0

NA_HEADS = 8
NA_HEAD = 128
WIN_H = 8
WIN_W = 16
NA_COLS = 3 * GROUP_W
NA_QROWS = 4
NA_KROWS = 3 * NA_QROWS

N_GROUPS = 4
EXPERTS_PER_GROUP = 4
N_EXPERTS = 16
TOP_K = 2
EXPERT_FF = 1024
MOE_BLOCK = 256
MOE_FF_TILE = 256
MOE_OUT_TILE = 1024

LANES = 128
SUBLANES = 8
VMEM_LIMIT = 56 * 1024 * 1024
NEG = -1e30

RWKV_CHUNK = 64
RWKV_PAIRS = 8
MLSTM_CHUNK = 256


def _cp(*sem):
    return pltpu.CompilerParams(dimension_semantics=sem, vmem_limit_bytes=VMEM_LIMIT)


def _dot(a, b):
    return jnp.dot(a.astype(BF16), b.astype(BF16), preferred_element_type=F32)


def _dot_nt(a, b):
    return lax.dot_general(a.astype(BF16), b.astype(BF16), (((1,), (1,)), ((), ())),
                           preferred_element_type=F32)


def _dot_tn(a, b):
    return lax.dot_general(a.astype(BF16), b.astype(BF16), (((0,), (0,)), ((), ())),
                           preferred_element_type=F32)


def _split2(x):
    hi = x.astype(BF16)
    lo = (x - hi.astype(F32)).astype(BF16)
    return hi, lo


def _dot3(a, b, dot=_dot):
    ah, al = _split2(a)
    bh, bl = _split2(b)
    return dot(ah, bh) + (dot(ah, bl) + dot(al, bh))


def _dot_exact_lhs(a_bf16, b):
    b0 = b.astype(BF16)
    r1 = b - b0.astype(F32)
    b1 = r1.astype(BF16)
    b2 = (r1 - b1.astype(F32)).astype(BF16)
    return _dot(a_bf16, b0) + (_dot(a_bf16, b1) + _dot(a_bf16, b2))


def _sigmoid(x):
    return 1.0 / (1.0 + jnp.exp(-x))


def _silu(x):
    return x * _sigmoid(x)


def _log_sigmoid(x):
    return jnp.minimum(x, 0.0) - jnp.log(1.0 + jnp.exp(-jnp.abs(x)))


def _pack_bf16_pair(lo, hi):
    lo_bits = lax.bitcast_convert_type(lo.astype(BF16).astype(F32), jnp.uint32)
    hi_bits = lax.bitcast_convert_type(hi.astype(BF16).astype(F32), jnp.uint32)
    return (lo_bits >> 16) | hi_bits


def _unpack_bf16_pair(w):
    lo = lax.bitcast_convert_type(w << 16, F32)
    hi = lax.bitcast_convert_type(w & jnp.uint32(0xFFFF0000), F32)
    return lo, hi


def _head_sum(x, width):
    r = lax.broadcasted_iota(jnp.int32, (LANES, LANES), 0) // width
    c = lax.broadcasted_iota(jnp.int32, (LANES, LANES), 1) // width
    e = jnp.where(r == c, 1.0, 0.0).astype(BF16)
    outs = []
    for j in range(x.shape[1] // LANES):
        hi, lo = _split2(x[:, j * LANES:(j + 1) * LANES])
        outs.append(_dot(hi, e) + _dot(lo, e))
    return jnp.concatenate(outs, axis=1)


def _ada_kernel(c_ref, w_ref, b_ref, o_ref):
    s = _silu(c_ref[...])
    o_ref[0] = _dot(s, w_ref[0]) + b_ref[0]


def ada_modulation(cvec, ada_w, ada_b):
    L, D, N = ada_w.shape
    tn = 512
    return pl.pallas_call(
        _ada_kernel,
        grid=(L, N // tn),
        in_specs=[pl.BlockSpec((SUBLANES, D), lambda l, j: (0, 0)),
                  pl.BlockSpec((1, D, tn), lambda l, j: (l, 0, j)),
                  pl.BlockSpec((1, 1, tn), lambda l, j: (l, 0, j))],
        out_specs=pl.BlockSpec((1, SUBLANES, tn), lambda l, j: (l, 0, j)),
        out_shape=jax.ShapeDtypeStruct((L, SUBLANES, N), F32),
        compiler_params=_cp("parallel", "parallel"),
        name="ada_modulation",
    )(cvec, ada_w, ada_b.reshape(L, 1, N))


def _normmod_kernel(h_ref, g_ref, sh_ref, sc_ref, o_ref):
    x = h_ref[0]
    y = x * lax.rsqrt(jnp.mean(x * x, axis=-1, keepdims=True) + EPS) * g_ref[...]
    o_ref[0] = (y * (1.0 + sc_ref[0]) + sh_ref[0]).astype(o_ref.dtype)


def norm_modulate(h, g, shift, scale):
    B, T, D = h.shape
    tm = 256
    vec = pl.BlockSpec((1, 1, D), lambda b, i: (b, 0, 0))
    return pl.pallas_call(
        _normmod_kernel,
        grid=(B, T // tm),
        in_specs=[pl.BlockSpec((1, tm, D), lambda b, i: (b, i, 0)),
                  pl.BlockSpec((1, D), lambda b, i: (0, 0)), vec, vec],
        out_specs=pl.BlockSpec((1, tm, D), lambda b, i: (b, i, 0)),
        out_shape=jax.ShapeDtypeStruct((B, T, D), BF16),
        compiler_params=_cp("parallel", "parallel"),
        name="norm_modulate",
    )(h, g.reshape(1, D), shift, scale)


def _mm_kernel(x_ref, w_ref, o_ref):
    o_ref[0] = jnp.dot(x_ref[0], w_ref[...], preferred_element_type=F32).astype(o_ref.dtype)


def _pick_tile(n, cands):
    for t in cands:
        if n % t == 0:
            return t
    raise ValueError(n)


def matmul(x, w, out_dtype=F32):
    B, T, K = x.shape
    N = w.shape[1]
    tm = _pick_tile(T, (1024, 512, 256))
    tn = _pick_tile(N, (1024, 896, 768, 640, 512))
    return pl.pallas_call(
        _mm_kernel,
        grid=(B, T // tm, N // tn),
        in_specs=[pl.BlockSpec((1, tm, K), lambda b, i, j: (b, i, 0)),
                  pl.BlockSpec((K, tn), lambda b, i, j: (0, j))],
        out_specs=pl.BlockSpec((1, tm, tn), lambda b, i, j: (b, i, j)),
        out_shape=jax.ShapeDtypeStruct((B, T, N), out_dtype),
        compiler_params=_cp("parallel", "parallel", "arbitrary"),
        name="in_proj",
    )(x, w)


def _outproj_kernel(ya_ref, yb_ref, yc_ref, yd_ref, w_ref, h_ref, gate_ref, o_ref):
    acc = jnp.dot(ya_ref[0], w_ref[0:GROUP_W, :], preferred_element_type=F32)
    acc += jnp.dot(yb_ref[0], w_ref[GROUP_W:2 * GROUP_W, :], preferred_element_type=F32)
    acc += jnp.dot(yc_ref[0], w_ref[2 * GROUP_W:3 * GROUP_W, :], preferred_element_type=F32)
    acc += jnp.dot(yd_ref[0], w_ref[3 * GROUP_W:4 * GROUP_W, :], preferred_element_type=F32)
    o_ref[0] = h_ref[0] + gate_ref[0] * acc


def out_proj_residual(ys, w, h, gate):
    B, T, D = h.shape
    tm = _pick_tile(T, (512, 256))
    tn = 1024
    ysp =pl.BlockSpec((1, tm, GROUP_W), lambda b, i, j: (b, i, 0))
    return pl.pallas_call(
        _outproj_kernel,
        grid=(B, T // tm, D // tn),
        in_specs=[ysp, ysp, ysp, ysp,
                  pl.BlockSpec((D, tn), lambda b, i, j: (0, j)),
                  pl.BlockSpec((1, tm, tn), lambda b, i, j: (b, i, j)),
                  pl.BlockSpec((1, 1, tn), lambda b, i, j: (b, 0, j))],
        out_specs=pl.BlockSpec((1, tm, tn), lambda b, i, j: (b, i, j)),
        out_shape=jax.ShapeDtypeStruct((B, T, D), F32),
        compiler_params=_cp("parallel", "parallel", "arbitrary"),
        name="out_proj",
    )(*ys, w, h, gate)


def _halo_specs(tm, width, halo=SUBLANES):
    per = tm // halo

    def prev_map(b, i):
        return (b, jnp.maximum(i * per - 1, 0), 0)

    def next_map(nb):
        return lambda b, i: (b, jnp.minimum((i + 1) * per, nb - 1), 0)

    return prev_map, next_map


def _shift_rows(x, prev_row, next_row):
    tm = x.shape[0]
    row = lax.broadcasted_iota(jnp.int32, x.shape, 0)
    prev = jnp.where(row == 0, prev_row, pltpu.roll(x, 1, axis=0))
    nxt = jnp.where(row == tm - 1, next_row, pltpu.roll(x, tm - 1, axis=0))
    return prev, nxt


def _rwkv_prep_kernel(p_ref, pp_ref, pn_ref, mu_ref, wup_ref, w0_ref, aup_ref, a0_ref, gup_ref,
                      kk_w_ref, ka_w_ref, rk_w_ref,
                      r_ref, kk_ref, v_ref, bonus_ref, g_ref, lw_ref, kd_ref, bb_ref):
    i = pl.program_id(1)
    last = pl.num_programs(1) - 1
    x = p_ref[0]
    prev_row = jnp.where(i == 0, 0.0, pp_ref[0, SUBLANES - 1:SUBLANES, :])
    next_row = jnp.where(i == last, 0.0, pn_ref[0, 0:1, :])
    prev, nxt = _shift_rows(x, prev_row, next_row)
    x = x + mu_ref[0:1, :] * (prev - x) + mu_ref[1:2, :] * (nxt - x)

    r = x[:, 0:GROUP_W]
    k = x[:, GROUP_W:2 * GROUP_W]
    v = x[:, 2 * GROUP_W:3 * GROUP_W]
    low = x[:, 3 * GROUP_W:3 * GROUP_W + 256]
    gd = x[:, 3 * GROUP_W + 256:3 * GROUP_W + 512]

    zw = _dot(jnp.tanh(low[:, 0:LANES]), wup_ref[...]) + w0_ref[...]
    za = _dot(low[:, LANES:2 * LANES], aup_ref[...]) + a0_ref[...]
    kkr = k * kk_w_ref[...]
    kk = kkr * lax.rsqrt(_head_sum(kkr * kkr, RWKV_HEAD) + EPS)
    r_ref[0] = r
    kk_ref[0] = kk
    v_ref[0] = v
    bonus_ref[0] = _head_sum(r * k * rk_w_ref[...], RWKV_HEAD) * v
    g_ref[0] = _dot(_sigmoid(gd), gup_ref[...])
    for d in range(2):
        sl = slice(d * GROUP_W, (d + 1) * GROUP_W)
        lw_ref[d, 0] = (-math.exp(-0.5)) * _sigmoid(zw[:, sl])
        a = _sigmoid(za[:, sl])
        kd_ref[d, 0] = k * (1.0 + (a - 1.0) * ka_w_ref[...])
        bb_ref[d, 0] = kk * a


def rwkv_prep(p, w):
    B, T, W = p.shape
    tm = 128
    prev_map, next_map = _halo_specs(tm, W)
    nb8 = T // SUBLANES
    full = lambda shape: pl.BlockSpec(shape, lambda b, i: tuple(0 for _ in shape))
    tok = pl.BlockSpec((1, tm, GROUP_W), lambda b, i: (b, i, 0))
    tokd = pl.BlockSpec((2, 1, tm, GROUP_W), lambda b, i: (0, b, i, 0))
    sh = jax.ShapeDtypeStruct((B, T, GROUP_W), F32)
    shd = jax.ShapeDtypeStruct((2, B, T, GROUP_W), F32)
    return pl.pallas_call(
        _rwkv_prep_kernel,
        grid=(B, T // tm),
        in_specs=[pl.BlockSpec((1, tm, W), lambda b, i: (b, i, 0)),
                  pl.BlockSpec((1, SUBLANES, W), prev_map),
                  pl.BlockSpec((1, SUBLANES, W), next_map(nb8)),
                  full((2, W)), full((LANES, 2 * GROUP_W)), full((1, 2 * GROUP_W)),
                  full((LANES, 2 * GROUP_W)), full((1, 2 * GROUP_W)), full((256, GROUP_W)),
                  full((1, GROUP_W)), full((1, GROUP_W)), full((1, GROUP_W))],
        out_specs=[tok, tok, tok, tok, tok, tokd, tokd, tokd],
        out_shape=[sh, sh, sh, sh, sh, shd, shd, shd],
        compiler_params=_cp("parallel", "parallel"),
        name="rwkv_prep",
    )(p, p, p, w["mu"], w["w_up"], w["w0"], w["a_up"], w["a0"], w["g_up"],
      w["k_k"], w["k_a"], w["r_k"])


def _dform(x, m0):
    return jnp.concatenate([jnp.where(m0, x, 0.0), jnp.where(m0, 0.0, x)], axis=0)


def _rwkv_chunk(tiles, sts, rev):
    n = len(tiles)
    C = tiles[0][0].shape[0]
    C2 = 2 * C
    sgn = jnp.where(rev, -1, 1)
    rr = lax.broadcasted_iota(jnp.int32, (C, C), 0)
    cc = lax.broadcasted_iota(jnp.int32, (C, C), 1)
    tri = jnp.where((rr - cc) * sgn >= 0, 1.0, 0.0).astype(BF16)
    lane = lax.broadcasted_iota(jnp.int32, (C, LANES), 1)
    m0 = lane < RWKV_HEAD
    rd = lax.broadcasted_iota(jnp.int32, (C2, C2), 0)
    cd = lax.broadcasted_iota(jnp.int32, (C2, C2), 1)
    same = (rd // C) == (cd // C)
    ahead = (rd % C - cd % C) * sgn
    strict = same & (ahead > 0)
    incl = same & (ahead >= 0)
    diag = rd == cd
    each = range(n)

    lins = [_dot_exact_lhs(tri, t[3]) for t in tiles]
    ltots = [jnp.sum(t[3], axis=0, keepdims=True) for t in tiles]
    a_d, r_d, k_d, b_d, v_d, kw_d, bw_d = [], [], [], [], [], [], []
    for (r, kk, v, lw, kd, bb), lin, ltot in zip(tiles, lins, ltots):
        einv = jnp.exp(-lin)
        ew = jnp.exp(ltot - lin)
        a_d.append(_dform(-kk * jnp.exp(lin - lw), m0).astype(BF16))
        r_d.append(_dform(r * jnp.exp(lin), m0).astype(BF16))
        k_d.append(_dform(kd * einv, m0).astype(BF16))
        b_d.append(_dform(bb * einv, m0).astype(BF16))
        v_d.append(_dform(v, m0).astype(BF16))
        kw_d.append(_dform(kd * ew, m0).astype(BF16))
        bw_d.append(_dform(bb * ew, m0).astype(BF16))

    grams = [_dot_nt(jnp.concatenate([a_d[i], r_d[i]], axis=0), jnp.concatenate([k_d[i], b_d[i]], axis=0))
             for i in each]
    a_ak = [jnp.where(strict, g[:C2, :C2], 0.0).astype(BF16) for g in grams]
    a_ab = [jnp.where(strict, g[:C2, C2:], 0.0) for g in grams]
    a_rk = [jnp.where(incl, g[C2:, :C2], 0.0).astype(BF16) for g in grams]
    a_rb = [jnp.where(incl, g[C2:, C2:], 0.0).astype(BF16) for g in grams]

    tinv = [jnp.where(diag, 1.0, 0.0) + a for a in a_ab]
    apow = a_ab
    for _ in range(int(math.log2(C)) - 1):
        apow = [_dot(a, a) for a in apow]
        tinv = [t + _dot(t, a) for t, a in zip(tinv, apow)]

    akv = [_dot(a_ak[i], v_d[i]) for i in each]
    p = [_dot(tinv[i], jnp.concatenate([a_d[i], akv[i].astype(BF16)], axis=1)) for i in each]
    ur = [_dot(jnp.concatenate([p[i][:, :LANES].astype(BF16), r_d[i]], axis=0), sts[i]) for i in each]
    u = [ur[i][:C2] + p[i][:, LANES:] for i in each]
    y2 = [ur[i][C2:] + _dot(a_rk[i], v_d[i]) + _dot(a_rb[i], u[i]) for i in each]
    ys = [y[:C] + y[C:] for y in y2]
    new = []
    for i in each:
        wc = jnp.broadcast_to(jnp.exp(ltots[i]), (LANES, LANES))
        wcol = jnp.sum(jnp.where(diag, wc, 0.0), axis=1, keepdims=True)
        new.append(sts[i] * wcol + _dot_tn(jnp.concatenate([bw_d[i], kw_d[i]], axis=0),
                                           jnp.concatenate([u[i].astype(BF16), v_d[i]], axis=0)))
    return ys, new


def _rwkv_scan_kernel(r_ref, kk_ref, v_ref, lw_ref, kd_ref, bb_ref, s0_ref, y_ref, sT_ref, st_scr,
                      *, npairs):
    d = pl.program_id(0)
    c = pl.program_id(3)
    nc = pl.num_programs(3)

    @pl.when(c == 0)
    def _():
        st_scr[...] = s0_ref[0, 0]

    lanes = [slice(j * LANES, (j + 1) * LANES) for j in range(npairs)]
    tiles = [(r_ref[0, :, sl], kk_ref[0, :, sl], v_ref[0, :, sl],
              lw_ref[0, 0, :, sl], kd_ref[0, 0, :, sl], bb_ref[0, 0, :, sl]) for sl in lanes]
    ys, new = _rwkv_chunk(tiles, [st_scr[j] for j in range(npairs)], d == 1)
    for j, sl in enumerate(lanes):
        y_ref[0, 0, :, sl] = ys[j]
        st_scr[j] = new[j]

    @pl.when(c == nc - 1)
    def _():
        sT_ref[0, 0] = st_scr[...]


def rwkv_scan(f, s0, npairs=RWKV_PAIRS):
    r, kk, v, lw, kd, bb = f["r"], f["kk"], f["v"], f["lw"], f["kd"], f["bb"]
    B, T, _ = r.shape
    C = RWKV_CHUNK
    nc = T // C
    ng = RWKV_PAIRS // npairs
    W = npairs * LANES
    chunk = lambda d, c: c + d * (nc - 1 - 2 * c)
    tok = pl.BlockSpec((1, C, W), lambda d, b, g, c: (b, chunk(d, c), g))
    tokd = pl.BlockSpec((1, 1, C, W), lambda d, b, g, c: (d, b, chunk(d, c), g))
    stsp = pl.BlockSpec((1, 1, npairs, LANES, LANES), lambda d, b, g, c: (d, b, g, 0, 0))
    return pl.pallas_call(
        functools.partial(_rwkv_scan_kernel, npairs=npairs),
        grid=(2, B, ng, nc),
        in_specs=[tok, tok, tok, tokd, tokd, tokd, stsp],
        out_specs=[tokd, stsp],
        out_shape=[jax.ShapeDtypeStruct((2, B, T, GROUP_W), F32),
                   jax.ShapeDtypeStruct(s0.shape, F32)],
        scratch_shapes=[pltpu.VMEM((npairs, LANES, LANES), F32)],
        compiler_params=_cp("parallel", "parallel", "parallel", "arbitrary"),
        name="rwkv_scan",
    )(r, kk, v, lw, kd, bb, s0)


def _rwkv_readout_kernel(yf_ref, yb_ref, bonus_ref, g_ref, ln_ref, o_ref):
    y = yf_ref[0, 0] + yb_ref[0, 0]
    mean = _head_sum(y, RWKV_HEAD) * (1.0 / RWKV_HEAD)
    cen = y - mean
    var = _head_sum(cen * cen, RWKV_HEAD) * (1.0 / RWKV_HEAD)
    yn = cen * lax.rsqrt(var + RWKV_GN_EPS) * ln_ref[0:1, :] + ln_ref[1:2, :]
    o_ref[0] = ((yn + bonus_ref[0]) * g_ref[0]).astype(o_ref.dtype)


def rwkv_readout(y, bonus, g, ln):
    _, B, T, W = y.shape
    tm = 256
    tok = pl.BlockSpec((1, tm, W), lambda b, i: (b, i, 0))
    return pl.pallas_call(
        _rwkv_readout_kernel,
        grid=(B, T // tm),
        in_specs=[pl.BlockSpec((1, 1, tm, W), lambda b, i: (0, b, i, 0)),
                  pl.BlockSpec((1, 1, tm, W), lambda b, i: (1, b, i, 0)),
                  tok, tok, pl.BlockSpec((2, W), lambda b, i: (0, 0))],
        out_specs=tok,
        out_shape=jax.ShapeDtypeStruct((B, T, W), BF16),
        compiler_params=_cp("parallel", "parallel"),
        name="rwkv_readout",
    )(y, y, bonus, g, ln)


def _pool_kernel(p_ref, pp_ref, pn_ref, w_ref, sc_ref, o_ref, *, total):
    i = pl.program_id(1)
    last = pl.num_programs(1) - 1
    x = p_ref[0]
    tm = x.shape[0]
    n = tm + 2 * POOL_HALO
    ext = jnp.concatenate([jnp.where(i == 0, 0.0, pp_ref[0]), x,
                           jnp.where(i == last, 0.0, pn_ref[0])], axis=0)
    t = i * tm + lax.broadcasted_iota(jnp.int32, (tm, 1), 0)
    outs = []
    for gi, win in enumerate(POOL_WINDOWS):
        sl = slice(gi * POOL_GROUP, (gi + 1) * POOL_GROUP)
        e = ext[:, sl]
        s = e + pltpu.roll(e, 1, axis=0)
        step = 1
        while 2 * step < win:
            s = pltpu.roll(s, step, axis=0) + pltpu.roll(s, n - step, axis=0)
            step *= 2
        h = win // 2
        cnt = (jnp.minimum(t + h, total) - jnp.maximum(t - h, 0)).astype(F32)
        z = s[POOL_HALO:POOL_HALO + tm] / cnt - x[:, sl]
        outs.append(_dot(z, w_ref[gi]))
    o_ref[0] = (jnp.concatenate(outs, axis=1) * sc_ref[...]).astype(o_ref.dtype)


def pool_mixer(p, pool_w, pool_scale):
    B, T, W = p.shape
    tm = 256
    prev_map, next_map = _halo_specs(tm, W)
    return pl.pallas_call(
        functools.partial(_pool_kernel, total=T),
        grid=(B, T // tm),
        in_specs=[pl.BlockSpec((1, tm, W), lambda b, i: (b, i, 0)),
                  pl.BlockSpec((1, POOL_HALO, W), prev_map),
                  pl.BlockSpec((1, POOL_HALO, W), next_map(T // POOL_HALO)),
                  pl.BlockSpec(pool_w.shape, lambda b, i: (0, 0, 0)),
                  pl.BlockSpec((1, W), lambda b, i: (0, 0))],
        out_specs=pl.BlockSpec((1, tm, W), lambda b, i: (b, i, 0)),
        out_shape=jax.ShapeDtypeStruct((B, T, W), BF16),
        compiler_params=_cp("parallel", "parallel"),
        name="pool_mixer",
    )(p, p, p, pool_w, pool_scale.reshape(1, W))


def _mlstm_prep_kernel(p_ref, pp_ref, pn_ref, g_ref, cw_ref, gb_ref, *rest, rope):
    if rope:
        cos_ref, sin_ref, q_ref, k_ref, go_ref = rest
    else:
        q_ref, k_ref, go_ref = rest
    i = pl.program_id(1)
    last = pl.num_programs(1) - 1
    x = p_ref[0]
    prev_row = jnp.where(i == 0, 0.0, pp_ref[0, SUBLANES - 1:SUBLANES, :])
    next_row = jnp.where(i == last, 0.0, pn_ref[0, 0:1, :])
    prev, nxt = _shift_rows(x, prev_row, next_row)
    qk = _silu(cw_ref[0:1, :] * prev + cw_ref[1:2, :] * x + cw_ref[2:3, :] * nxt)
    if rope:
        w = qk.shape[1]
        lane = lax.broadcasted_iota(jnp.int32, qk.shape, 1)
        partner = jnp.where(lane % 2 == 0, pltpu.roll(qk, w - 1, axis=1), pltpu.roll(qk, 1, axis=1))
        reps = w // LANES
        cos = jnp.concatenate([cos_ref[...]] * reps, axis=1)
        sin = jnp.concatenate([sin_ref[...]] * reps, axis=1)
        qk = qk * cos + partner * sin
    half = MLSTM_HEADS * MLSTM_DK
    q_ref[0] = qk[:, :half].astype(BF16)
    k_ref[0] = (qk[:, half:] * (MLSTM_DK ** -0.5)).astype(BF16)
    g = g_ref[0] + gb_ref[...]
    lane = lax.broadcasted_iota(jnp.int32, g.shape, 1)
    go_ref[0] = jnp.where((lane // MLSTM_HEADS) % 2 == 1, _log_sigmoid(g), g)


def mlstm_prep(p, conv_w, gate_b, rope_tabs):
    B, T, W = p.shape
    tm = 256
    QK = 2 * MLSTM_HEADS * MLSTM_DK
    prev_map, next_map = _halo_specs(tm, QK)
    rope = rope_tabs is not None
    in_specs = [pl.BlockSpec((1, tm, QK), lambda b, i: (b, i, 0)),
                pl.BlockSpec((1, SUBLANES, QK), prev_map),
                pl.BlockSpec((1, SUBLANES, QK), next_map(T // SUBLANES)),
                pl.BlockSpec((1, tm, LANES), lambda b, i: (b, i, (W - LANES) // LANES)),
                pl.BlockSpec((3, QK), lambda b, i: (0, 0)),
                pl.BlockSpec((1, LANES), lambda b, i: (0, 0))]
    args = [p, p, p, p, conv_w, gate_b]
    if rope:
        in_specs += [pl.BlockSpec((tm, LANES), lambda b, i: (i, 0))] * 2
        args += list(rope_tabs)
    half = MLSTM_HEADS * MLSTM_DK
    return pl.pallas_call(
        functools.partial(_mlstm_prep_kernel, rope=rope),
        grid=(B, T // tm),
        in_specs=in_specs,
        out_specs=[pl.BlockSpec((1, tm, half), lambda b, i: (b, i, 0)),
                   pl.BlockSpec((1, tm, half), lambda b, i: (b, i, 0)),
                   pl.BlockSpec((1, tm, LANES), lambda b, i: (b, i, 0))],
        out_shape=[jax.ShapeDtypeStruct((B, T, half), BF16),
                   jax.ShapeDtypeStruct((B, T, half), BF16),
                   jax.ShapeDtypeStruct((B, T, LANES), F32)],
        compiler_params=_cp("parallel", "parallel"),
        name="mlstm_prep",
    )(*args)


def _mlstm_chunk_kernel(q_ref, k_ref, v_ref, gc_ref, gr_ref, c0_ref, n0_ref, m0_ref,
                        h_ref, cT_ref, nT_ref, mT_ref, c_scr, n_scr, m_scr):
    d = pl.program_id(0)
    c = pl.program_id(2)
    nc = pl.num_programs(2)
    H = MLSTM_HEADS

    @pl.when(c == 0)
    def _():
        c_scr[...] = c0_ref[0, 0]
        n_scr[...] = n0_ref[0, 0]
        m_scr[...] = m0_ref[0, 0]

    rev = d == 1
    L = q_ref.shape[1]
    rr = lax.broadcasted_iota(jnp.int32, (L, L), 0)
    cc = lax.broadcasted_iota(jnp.int32, (L, L), 1)
    sgn = jnp.where(rev, -1, 1)
    seen = (rr - cc) * sgn >= 0
    tri = jnp.where(seen, 1.0, 0.0).astype(BF16)
    tri_t = jnp.where((cc - rr) * sgn >= 0, 1.0, 0.0).astype(BF16)

    gc = gc_ref[0]
    gr = gr_ref[0]
    lane = lax.broadcasted_iota(jnp.int32, gc.shape, 1)
    subl = lax.broadcasted_iota(jnp.int32, gr.shape, 0)
    pick_c = lambda a, idx: jnp.sum(jnp.where(lane == idx, a, 0.0), axis=1, keepdims=True)
    pick_r = lambda a, idx: jnp.sum(jnp.where(subl == idx, a, 0.0), axis=0, keepdims=True)
    bc_all = _dot_exact_lhs(tri, gc)
    gr_hi = gr.astype(BF16)
    gr_r1 = gr - gr_hi.astype(F32)
    gr_mid = gr_r1.astype(BF16)
    gr_lo = (gr_r1 - gr_mid.astype(F32)).astype(BF16)
    br_all = _dot(gr_hi, tri_t) + (_dot(gr_mid, tri_t) + _dot(gr_lo, tri_t))

    heads = range(H)
    i_lane = [d * (2 * H) + j for j in heads]
    f_lane = [d * (2 * H) + H + j for j in heads]
    ig_c = [pick_c(gc, i_lane[j]) for j in heads]
    lf_c = [pick_c(gc, f_lane[j]) for j in heads]
    b_c = [pick_c(bc_all, f_lane[j]) for j in heads]
    ig_r = [pick_r(gr, i_lane[j]) for j in heads]
    b_r = [pick_r(br_all, f_lane[j]) for j in heads]
    m_prev = [m_scr[j] for j in heads]
    q = [q_ref[0, :, j * MLSTM_DK:(j + 1) * MLSTM_DK] for j in heads]
    k = [k_ref[0, :, j * MLSTM_DK:(j + 1) * MLSTM_DK] for j in heads]
    v = [v_ref[0, :, j * MLSTM_DV:(j + 1) * MLSTM_DV].astype(BF16) for j in heads]
    cst = [c_scr[j] for j in heads]
    nst = [n_scr[j] for j in heads]

    qk = [_dot_nt(q[j], k[j]) for j in heads]
    qc = [_dot(q[j], cst[j]) for j in heads]
    dlog = [jnp.where(seen, b_c[j] - b_r[j] + ig_r[j], NEG) for j in heads]
    inter = [b_c[j] + m_prev[j] for j in heads]
    m_t = [jnp.maximum(inter[j], jnp.max(dlog[j], axis=1, keepdims=True)) for j in heads]
    s = [qk[j] * jnp.exp(dlog[j] - m_t[j]) for j in heads]
    w_inter = [jnp.exp(inter[j] - m_t[j]) for j in heads]
    sv = [_dot(s[j], v[j]) for j in heads]
    for j in heads:
        num = sv[j] + w_inter[j] * qc[j]
        den = (jnp.sum(s[j], axis=1, keepdims=True)
               + w_inter[j] * jnp.sum(q[j].astype(F32) * nst[j], axis=1, keepdims=True))
        h_ref[0, 0, :, j * MLSTM_DV:(j + 1) * MLSTM_DV] = num / jnp.maximum(jnp.abs(den), jnp.exp(-m_t[j]))

    b_end = [jnp.sum(lf_c[j], axis=0, keepdims=True) for j in heads]
    g_s = [b_end[j] - b_c[j] + ig_c[j] for j in heads]
    m_new = [jnp.maximum(b_end[j] + m_prev[j], jnp.max(g_s[j], axis=0, keepdims=True)) for j in heads]
    kw = [k[j].astype(F32) * jnp.exp(g_s[j] - m_new[j]) for j in heads]
    kv = [_dot_tn(kw[j], v[j]) for j in heads]
    for j in heads:
        decay = jnp.exp(b_end[j] + m_prev[j] - m_new[j])
        c_scr[j] = decay * cst[j] + kv[j]
        n_scr[j] = decay * nst[j] + jnp.sum(kw[j], axis=0, keepdims=True)
        m_scr[j] = m_new[j]

    @pl.when(c == nc - 1)
    def _():
        cT_ref[0, 0] = c_scr[...]
        nT_ref[0, 0] = n_scr[...]
        mT_ref[0, 0] = m_scr[...]


def mlstm_scan(q, k, p, gates, gates_t, state):
    B, T, _ = q.shape
    L = min(MLSTM_CHUNK, T)
    nc = T // L
    H = MLSTM_HEADS
    QK = H * MLSTM_DK
    c0, n0, m0 = state
    chunk = lambda d, c: c + d * (nc - 1 - 2 * c)
    v_blk = (2 * QK) // GROUP_W
    csp = pl.BlockSpec((1, 1, H, MLSTM_DK, MLSTM_DV), lambda d, b, c: (d, b, 0, 0, 0))
    nsp = pl.BlockSpec((1, 1, H, 1, MLSTM_DK), lambda d, b, c: (d, b, 0, 0, 0))
    msp = pl.BlockSpec((1, 1, H, 1, 1), lambda d, b, c: (d, b, 0, 0, 0))
    return pl.pallas_call(
        _mlstm_chunk_kernel,
        grid=(2, B, nc),
        in_specs=[pl.BlockSpec((1, L, QK), lambda d, b, c: (b, chunk(d, c), 0)),
                  pl.BlockSpec((1, L, QK), lambda d, b, c: (b, chunk(d, c), 0)),
                  pl.BlockSpec((1, L, GROUP_W), lambda d, b, c: (b, chunk(d, c), v_blk)),
                  pl.BlockSpec((1, L, LANES), lambda d, b, c: (b, chunk(d, c), 0)),
                  pl.BlockSpec((1, 4 * H, L), lambda d, b, c: (b, 0, chunk(d, c))),
                  csp, nsp, msp],
        out_specs=[pl.BlockSpec((1, 1, L, GROUP_W), lambda d, b, c: (d, b, chunk(d, c), 0)),
                   csp, nsp, msp],
        out_shape=[jax.ShapeDtypeStruct((2, B, T, GROUP_W), F32),
                   jax.ShapeDtypeStruct(c0.shape, F32),
                   jax.ShapeDtypeStruct(n0.shape, F32),
                   jax.ShapeDtypeStruct(m0.shape, F32)],
        scratch_shapes=[pltpu.VMEM((H, MLSTM_DK, MLSTM_DV), F32),
                        pltpu.VMEM((H, 1, MLSTM_DK), F32),
                        pltpu.VMEM((H, 1, 1), F32)],
        compiler_params=_cp("parallel", "parallel", "arbitrary"),
        name="mlstm_scan",
    )(q, k, p, gates, gates_t, c0, n0, m0)


def _mlstm_readout_kernel(hf_ref, hb_ref, o_ref, nw_ref, out_ref):
    h = hf_ref[0, 0] + hb_ref[0, 0]
    outs = []
    for j in range(MLSTM_HEADS):
        hj = h[:, j * MLSTM_DV:(j + 1) * MLSTM_DV]
        outs.append(hj * lax.rsqrt(jnp.mean(hj * hj, axis=-1, keepdims=True) + EPS))
    hn = jnp.concatenate(outs, axis=1)
    out_ref[0] = (hn * nw_ref[...] * _sigmoid(o_ref[0])).astype(out_ref.dtype)


def mlstm_readout(h, p, norm_w):
    _, B, T, W = h.shape
    tm = 256
    o_blk = (2 * MLSTM_HEADS * MLSTM_DK + GROUP_W) // GROUP_W
    return pl.pallas_call(
        _mlstm_readout_kernel,
        grid=(B, T // tm),
        in_specs=[pl.BlockSpec((1, 1, tm, W), lambda b, i: (0, b, i, 0)),
                  pl.BlockSpec((1, 1, tm, W), lambda b, i: (1, b, i, 0)),
                  pl.BlockSpec((1, tm, W), lambda b, i: (b, i, o_blk)),
                  pl.BlockSpec((1, W), lambda b, i: (0, 0))],
        out_specs=pl.BlockSpec((1, tm, W), lambda b, i: (b, i, 0)),
        out_shape=jax.ShapeDtypeStruct((B, T, W), BF16),
        compiler_params=_cp("parallel", "parallel"),
        name="mlstm_readout",
    )(h, h, p, norm_w.reshape(1, W))


def _na_prep_kernel(p_ref, nw_ref, q_ref, k_ref, v_ref):
    x = p_ref[0]
    for j in range(NA_HEADS):
        sl = slice(j * NA_HEAD, (j + 1) * NA_HEAD)
        for src, dst, row, scale in ((0, q_ref, 0, NA_HEAD ** -0.5), (GROUP_W, k_ref, 1, 1.0)):
            z = x[:, src + j * NA_HEAD:src + (j + 1) * NA_HEAD]
            zn = z * lax.rsqrt(jnp.mean(z * z, axis=-1, keepdims=True) + EPS) * nw_ref[row:row + 1, :]
            dst[0, :, sl] = (zn * scale).astype(BF16)
    v_ref[0] = x[:, 2 * GROUP_W:].astype(BF16)


def na_prep(p, qk_norm):
    B, T, W = p.shape
    tm = 256
    tok = pl.BlockSpec((1, tm, GROUP_W), lambda b, i: (b, i, 0))
    sh = jax.ShapeDtypeStruct((B, T, GROUP_W), BF16)
    return pl.pallas_call(
        _na_prep_kernel,
        grid=(B, T // tm),
        in_specs=[pl.BlockSpec((1, tm, W), lambda b, i: (b, i, 0)),
                  pl.BlockSpec((2, NA_HEAD), lambda b, i: (0, 0))],
        out_specs=[tok, tok, tok],
        out_shape=[sh, sh, sh],
        compiler_params=_cp("parallel", "parallel"),
        name="na_prep",
    )(p, qk_norm)


def _softmax_rows(s):
    m = jnp.max(s, axis=1, keepdims=True)
    e = jnp.exp(s - m)
    return e / jnp.sum(e, axis=1, keepdims=True)


def _na_ctx_kernel(q_ref, k_ref, v_ref, o_ref):
    for j in range(NA_HEADS):
        sl = slice(j * NA_HEAD, (j + 1) * NA_HEAD)
        pr = _softmax_rows(_dot_nt(q_ref[0, :, sl], k_ref[0, :, sl]))
        o_ref[0, :, sl] = _dot(pr, v_ref[0, :, sl]).astype(o_ref.dtype)


def na_ctx_attention(q, k, v):
    B, T, W = q.shape
    tok = pl.BlockSpec((1, T, W), lambda b: (b, 0, 0))
    return pl.pallas_call(
        _na_ctx_kernel,
        grid=(B,),
        in_specs=[tok, tok, tok],
        out_specs=tok,
        out_shape=jax.ShapeDtypeStruct((B, T, W), BF16),
        compiler_params=_cp("parallel"),
        name="na_ctx_attention",
    )(q, k, v)


def _na_lat_kernel(q_ref, k0_ref, k1_ref, k2_ref, v0_ref, v1_ref, v2_ref, kc_ref, vc_ref, bias_ref, o_ref):
    for j in range(NA_HEADS):
        sl = slice(j * NA_HEAD, (j + 1) * NA_HEAD)
        q = q_ref[0, :, sl]
        kcat = jnp.concatenate([k0_ref[0, :, sl], k1_ref[0, :, sl], k2_ref[0, :, sl]], axis=0)
        vcat = jnp.concatenate([v0_ref[0, :, sl], v1_ref[0, :, sl], v2_ref[0, :, sl],
                                vc_ref[0, :, sl]], axis=0)
        s = jnp.concatenate([_dot_nt(q, kcat) + bias_ref[0, j], _dot_nt(q, kc_ref[0, :, sl])], axis=1)
        o_ref[0, :, sl] = _dot(_softmax_rows(s), vcat).astype(o_ref.dtype)


def na_bias_table(rpb, rows):
    nq, nk = NA_QROWS, NA_KROWS
    starts = np.array([0, nq, rows - nq])
    r = starts[:, None] + np.arange(nq)[None, :]
    kr = starts[:, None] - nq + np.arange(nk)[None, :]
    r0 = np.clip(r - WIN_H // 2, 0, rows - WIN_H)
    rv = (kr[:, None, :] >= r0[:, :, None]) & (kr[:, None, :] < r0[:, :, None] + WIN_H)
    dr = kr[:, None, :] - r[:, :, None] + WIN_H - 1
    col = np.arange(GRID_W)
    c0 = np.clip(col - WIN_W // 2, 0, GRID_W - WIN_W)
    cv = (col[None, :] >= c0[:, None]) & (col[None, :] < c0[:, None] + WIN_W)
    pad = GRID_W - WIN_W
    rpb_pad = jnp.pad(rpb, ((0, 0), (0, 0), (pad, pad)))
    colbias = jnp.stack([rpb_pad[:, :, GRID_W - 1 - q:2 * GRID_W - 1 - q] for q in range(GRID_W)], axis=2)
    colbias = jnp.where(cv[None, None], colbias, NEG)
    blank = jnp.full((NA_HEADS, GRID_W, GRID_W), NEG, F32)
    pats = []
    for p in range(3):
        qrows = [jnp.concatenate([colbias[:, int(dr[p, i, j])] if rv[p, i, j] else blank
                                  for j in range(nk)], axis=2) for i in range(nq)]
        pats.append(jnp.concatenate(qrows, axis=1))
    return jnp.stack(pats)


def na_lat_attention(q, k, v, kc, vc, bias):
    B, T, W = q.shape
    tq = NA_QROWS * GRID_W
    nb = T // tq
    ctx_len = kc.shape[1]
    qsp = pl.BlockSpec((1, tq, W), lambda b, i: (b, i, 0))
    prv = pl.BlockSpec((1, tq, W), lambda b, i: (b, jnp.maximum(i - 1, 0), 0))
    nxt = pl.BlockSpec((1, tq, W), lambda b, i: (b, jnp.minimum(i + 1, nb - 1), 0))
    csp = pl.BlockSpec((1, ctx_len, W), lambda b, i: (b, 0, 0))
    pattern = lambda i: jnp.where(i == 0, 0, jnp.where(i == nb - 1, 2, 1))
    return pl.pallas_call(
        _na_lat_kernel,
        grid=(B, nb),
        in_specs=[qsp, prv, qsp, nxt, prv, qsp, nxt, csp, csp,
                  pl.BlockSpec((1, NA_HEADS, tq, NA_KROWS * GRID_W), lambda b, i: (pattern(i), 0, 0, 0))],
        out_specs=qsp,
        out_shape=jax.ShapeDtypeStruct((B, T, W), BF16),
        compiler_params=_cp("parallel", "arbitrary"),
        name="na_lat_attention",
    )(q, k, k, k, v, v, v, kc, vc, bias)


def _ffn_norm_router_body(h_ref, g_ref, sh_ref, sc_ref, rw_ref, rb_ref, v_ref, ids_ref, wts_ref):
    x = h_ref[0]
    y = x * lax.rsqrt(jnp.mean(x * x, axis=-1, keepdims=True) + EPS) * g_ref[...]
    v = y * (1.0 + sc_ref[0]) + sh_ref[0]
    half = v.shape[1] // 2
    v_ref[...] = _pack_bf16_pair(v[:, :half], v[:, half:])
    vh, vl = _split2(v)
    logits = (_dot(vh, rw_ref[0]) + (_dot(vh, rw_ref[1]) + _dot(vl, rw_ref[0]))) + rb_ref[...]
    lane = lax.broadcasted_iota(jnp.int32, logits.shape, 1).astype(F32)
    first = lambda mask: jnp.min(jnp.where(mask, lane, float(LANES)), axis=1, keepdims=True)

    g_mask = lane < N_GROUPS
    gl = jnp.where(g_mask, logits, NEG)
    gmax = jnp.max(gl, axis=1, keepdims=True)
    g_p = 1.0 / jnp.sum(jnp.exp(gl - gmax), axis=1, keepdims=True)
    g_idx = first(g_mask & (gl == gmax))
    e_lane = lane - N_GROUPS
    e_mask = (e_lane >= 0) & (e_lane < N_EXPERTS) & (jnp.floor(e_lane * (1.0 / EXPERTS_PER_GROUP)) == g_idx)
    el = jnp.where(e_mask, logits, NEG)
    e1 = jnp.max(el, axis=1, keepdims=True)
    i1 = first(e_mask & (el == e1))
    el2 = jnp.where(lane == i1, NEG, el)
    e2 = jnp.max(el2, axis=1, keepdims=True)
    i2 = first(e_mask & (lane != i1) & (el2 == e2))
    x2 = jnp.exp(e2 - e1)
    w1 = g_p / (1.0 + x2)
    w2 = g_p * x2 / (1.0 + x2)
    ids_ref[...] = jnp.where(lane == 0, i1 - N_GROUPS, jnp.where(lane == 1, i2 - N_GROUPS, 0.0)).astype(jnp.int32)
    wts_ref[...] = jnp.where(lane == 0, w1, jnp.where(lane == 1, w2, 0.0))


def _ffn_norm_router_kernel(*refs, bounds):
    ns = len(bounds) - 1
    g_ref, rw_ref, rb_ref, v_ref, ids_ref, wts_ref = refs[3 * ns:]
    r = pl.program_id(0)
    for s in range(ns):
        h_ref, sh_ref, sc_ref = refs[3 * s:3 * s + 3]

        @pl.when((r >= bounds[s]) & (r < bounds[s + 1]))
        def _():
            _ffn_norm_router_body(h_ref, g_ref, sh_ref, sc_ref, rw_ref, rb_ref, v_ref, ids_ref, wts_ref)


def ffn_norm_router(streams, g, rw, rb):
    D = g.shape[0]
    tm = 256
    bounds = [0]
    in_specs, args = [], []
    for h, shift, scale in streams:
        B, T, _ = h.shape
        nt = T // tm
        lo, nblk = bounds[-1], B * nt
        bounds.append(lo + nblk)
        local = lambda r, lo=lo, nblk=nblk: jnp.clip(r - lo, 0, nblk - 1)
        in_specs += [pl.BlockSpec((1, tm, D), lambda r, f=local, nt=nt: (f(r) // nt, f(r) % nt, 0)),
                     pl.BlockSpec((1, 1, D), lambda r, f=local, nt=nt: (f(r) // nt, 0, 0)),
                     pl.BlockSpec((1, 1, D), lambda r, f=local, nt=nt: (f(r) // nt, 0, 0))]
        args += [h, shift, scale]
    n_rows = bounds[-1] * tm
    in_specs += [pl.BlockSpec((1, D), lambda r: (0, 0)),
                 pl.BlockSpec((2, D, LANES), lambda r: (0, 0, 0)),
                 pl.BlockSpec((1, LANES), lambda r: (0, 0))]
    args += [g.reshape(1, D), rw, rb]
    lan = pl.BlockSpec((tm, LANES), lambda r: (r, 0))
    return pl.pallas_call(
        functools.partial(_ffn_norm_router_kernel, bounds=tuple(bounds)),
        grid=(bounds[-1],),
        in_specs=in_specs,
        out_specs=[pl.BlockSpec((tm, D // 2), lambda r: (r, 0)), lan, lan],
        out_shape=[jax.ShapeDtypeStruct((n_rows, D // 2), jnp.uint32),
                   jax.ShapeDtypeStruct((n_rows, LANES), jnp.int32),
                   jax.ShapeDtypeStruct((n_rows, LANES), F32)],
        compiler_params=_cp("parallel"),
        name="ffn_norm_router",
    )(*args)


def _moe_kernel(be_ref, zero_ref, tok_ref, tokn_ref, dstp_ref, sw_ref, v_hbm, wg_hbm, wu_hbm, wd_hbm, out_hbm,
                x0, x1, y0, y1, wg_buf, wu_buf, wd_buf, gsem, ssem, wsem):
    i = pl.program_id(0)
    nb = pl.num_programs(0) - 1
    xs, ys = (x0, x1), (y0, y1)
    zero = zero_ref[0]

    def gather_row(idx_ref, s, j, after=0):
        return pltpu.make_async_copy(v_hbm.at[pl.ds(idx_ref[0, 0, j] + after, 1)], xs[s].at[pl.ds(j, 1)],
                                     gsem.at[s])

    def wait_gather(s):
        pltpu.make_async_copy(v_hbm.at[pl.ds(0, MOE_BLOCK)], xs[s], gsem.at[s]).wait()

    def scatter_row(s, j, after=0):
        return pltpu.make_async_copy(ys[s].at[pl.ds(j, 1)], out_hbm.at[pl.ds(dstp_ref[0, 0, j] + after, 1)],
                                     ssem.at[s])

    def wait_scatter(s):
        pltpu.make_async_copy(ys[s], out_hbm.at[pl.ds(0, MOE_BLOCK)], ssem.at[s]).wait()

    def after(result):
        return result[0, 0].astype(jnp.int32) * zero

    @pl.when(i == 0)
    def _():
        def body(j, carry):
            gather_row(tok_ref, 0, j).start()
            return carry
        lax.fori_loop(0, MOE_BLOCK, body, 0)

    e = be_ref[jnp.minimum(i, nb - 1)]

    @pl.when((i == 0) | ((i < nb) & (e != be_ref[jnp.maximum(i - 1, 0)])))
    def _():
        copies = [pltpu.make_async_copy(wg_hbm.at[e], wg_buf, wsem.at[0]),
                  pltpu.make_async_copy(wu_hbm.at[e], wu_buf, wsem.at[1]),
                  pltpu.make_async_copy(wd_hbm.at[e], wd_buf, wsem.at[2])]
        for cp in copies:
            cp.start()
        for cp in copies:
            cp.wait()

    def step(s, scatter_prev):
        wait_gather(s)
        x_lo, x_hi = _unpack_bf16_pair(xs[s][...])
        x = jnp.concatenate([x_lo.astype(BF16), x_hi.astype(BF16)], axis=1)
        n_up = EXPERT_FF // MOE_FF_TILE
        per = MOE_BLOCK // (2 * n_up)
        acts = []
        for c in range(n_up):
            cols = slice(c * MOE_FF_TILE, (c + 1) * MOE_FF_TILE)
            hg = jnp.dot(x, wg_buf[:, cols], preferred_element_type=F32)
            hu = jnp.dot(x, wu_buf[:, cols], preferred_element_type=F32)
            for g, res in enumerate((hg, hu)):
                dep = after(res)
                for j in range((2 * c + g) * per, (2 * c + g + 1) * per):
                    gather_row(tokn_ref, 1 - s, j, dep).start()
            acts.append((_silu(hg) * hu).astype(BF16))
        act = jnp.concatenate(acts, axis=1)

        @pl.when(i >= 2)
        def _():
            wait_scatter(s)

        half = x.shape[1] // 2
        n_down = half // MOE_OUT_TILE
        per = MOE_BLOCK // n_down
        for c in range(n_down):
            cols = slice(c * MOE_OUT_TILE, (c + 1) * MOE_OUT_TILE)
            hcols = slice(half + c * MOE_OUT_TILE, half + (c + 1) * MOE_OUT_TILE)
            y_lo = jnp.dot(act, wd_buf[:, cols], preferred_element_type=F32) * sw_ref[...]
            y_hi = jnp.dot(act, wd_buf[:, hcols], preferred_element_type=F32) * sw_ref[...]
            ys[s][:, cols] = _pack_bf16_pair(y_lo, y_hi)
            if scatter_prev:
                dep = after(y_hi)
                for j in range(c * per, (c + 1) * per):
                    scatter_row(1 - s, j, dep).start()

    pl.when(i == 0)(functools.partial(step, 0, False))
    for s in range(2):
        pl.when((i > 0) & (i < nb) & (i % 2 == s))(functools.partial(step, s, True))

    @pl.when(i == nb)
    def _():
        def drain(s):
            wait_gather(s)
            wait_scatter(s)

            def body(j, carry):
                scatter_row(1 - s, j).start()
                return carry
            lax.fori_loop(0, MOE_BLOCK, body, 0)
            wait_scatter(1 - s)
        for s in range(2):
            pl.when(i % 2 == s)(functools.partial(drain, s))


def moe_experts(v_rows, block_expert, slot_tok, slot_dst, slot_w, wg, wu, wd):
    P = slot_tok.shape[0]
    W = v_rows.shape[1]
    D = 2 * W
    nb = P // MOE_BLOCK
    assert nb >= 2
    idx = lambda a: a.reshape(nb, 1, MOE_BLOCK)
    smem = lambda imap: pl.BlockSpec((1, 1, MOE_BLOCK), imap, memory_space=pltpu.SMEM)
    hbm = pl.BlockSpec(memory_space=pl.ANY)
    last = nb - 1
    grid_spec = pltpu.PrefetchScalarGridSpec(
        num_scalar_prefetch=2,
        grid=(nb + 1,),
        in_specs=[smem(lambda i, be, z: (jnp.minimum(i, last), 0, 0)),
                  smem(lambda i, be, z: (jnp.minimum(i + 1, last), 0, 0)),
                  smem(lambda i, be, z: (jnp.maximum(i - 1, 0), 0, 0)),
                  pl.BlockSpec((MOE_BLOCK, 1), lambda i, be, z: (jnp.minimum(i, last), 0)),
                  hbm, hbm, hbm, hbm],
        out_specs=hbm,
        scratch_shapes=[pltpu.VMEM((MOE_BLOCK, W), jnp.uint32),
                        pltpu.VMEM((MOE_BLOCK, W), jnp.uint32),
                        pltpu.VMEM((MOE_BLOCK, W), jnp.uint32),
                        pltpu.VMEM((MOE_BLOCK, W), jnp.uint32),
                        pltpu.VMEM((D, EXPERT_FF), BF16),
                        pltpu.VMEM((D, EXPERT_FF), BF16),
                        pltpu.VMEM((EXPERT_FF, D), BF16),
                        pltpu.SemaphoreType.DMA((2,)),
                        pltpu.SemaphoreType.DMA((2,)),
                        pltpu.SemaphoreType.DMA((3,))],
    )
    return pl.pallas_call(
        _moe_kernel,
        grid_spec=grid_spec,
        out_shape=jax.ShapeDtypeStruct((P, W), jnp.uint32),
        compiler_params=_cp("arbitrary"),
        name="moe_experts",
    )(block_expert, jnp.zeros((1,), jnp.int32), idx(slot_tok), idx(slot_tok), idx(slot_dst),
      slot_w.reshape(P, 1), v_rows, wg, wu, wd)


def _moe_combine_kernel(h_ref, f0_ref, f1_ref, gate_ref, o_ref):
    a_lo, a_hi = _unpack_bf16_pair(f0_ref[...])
    b_lo, b_hi = _unpack_bf16_pair(f1_ref[...])
    half = a_lo.shape[1]
    o_ref[0, :, :half] = h_ref[0, :, :half] + gate_ref[0, :, :half] * (a_lo + b_lo)
    o_ref[0, :, half:] = h_ref[0, :, half:] + gate_ref[0, :, half:] * (a_hi + b_hi)


def moe_combine(h, out2, gate, row_offset, n_rows):
    B, T, D = h.shape
    tm = 256
    nt = T // tm
    blk0 = row_offset // tm
    tok = pl.BlockSpec((1, tm, D), lambda b, i: (b, i, 0))
    return pl.pallas_call(
        _moe_combine_kernel,
        grid=(B, nt),
        in_specs=[tok,
                  pl.BlockSpec((tm, D // 2), lambda b, i: (blk0 + b * nt + i, 0)),
                  pl.BlockSpec((tm, D // 2), lambda b, i: (n_rows // tm + blk0 + b * nt + i, 0)),
                  pl.BlockSpec((1, 1, D), lambda b, i: (b, 0, 0))],
        out_specs=tok,
        out_shape=jax.ShapeDtypeStruct((B, T, D), F32),
        compiler_params=_cp("parallel", "parallel"),
        name="moe_combine",
    )(h, out2, out2, gate)


def moe_dispatch(ids, wts):
    N = ids.shape[0]
    A = N * TOP_K
    e_flat = ids.reshape(A)
    order = jnp.argsort(e_flat).astype(jnp.int32)
    counts = jnp.sum((e_flat[:, None] == jnp.arange(N_EXPERTS)[None, :]).astype(jnp.int32), axis=0)
    padded = (counts + MOE_BLOCK - 1) // MOE_BLOCK * MOE_BLOCK
    ends = jnp.cumsum(padded)
    n_blocks = -(-(A + N_EXPERTS * (MOE_BLOCK - 1)) // MOE_BLOCK)
    block_start = jnp.arange(n_blocks, dtype=jnp.int32) * MOE_BLOCK
    block_expert = jnp.minimum(jnp.searchsorted(ends, block_start, side="right"),
                               N_EXPERTS - 1).astype(jnp.int32)
    into = block_start - (ends - padded)[block_expert]
    b_count = counts[block_expert]
    local = (into[:, None] + jnp.arange(MOE_BLOCK, dtype=jnp.int32)[None, :])
    real = local < b_count[:, None]
    src = jnp.clip((jnp.cumsum(counts) - counts)[block_expert][:, None] + local, 0, A - 1)
    a = order[src.reshape(-1)]
    real = real.reshape(-1)
    tok = a // TOP_K
    slot_tok = jnp.where(real, tok, 0).astype(jnp.int32)
    spare = A + jnp.cumsum(jnp.where(real, 0, 1)) - 1
    slot_dst = jnp.where(real, (a % TOP_K) * N + tok, spare).astype(jnp.int32)
    slot_w = jnp.where(real, wts.reshape(A)[a], 0.0)
    return slot_tok, slot_dst, slot_w, block_expert


def hier_moe(v_rows, ids, wts, wg, wu, wd):
    slot_tok, slot_dst, slot_w, block_expert = moe_dispatch(ids, wts)
    return moe_experts(v_rows, block_expert, slot_tok, slot_dst, slot_w, wg, wu, wd)


def _layer_weights(l, P):
    w = {}
    w_in = P["w_in"][l]
    o = 0
    a = w_in[:, o:o + RWKV_COLS]; o += RWKV_COLS
    b = w_in[:, o:o + GROUP_W]; o += GROUP_W
    c = w_in[:, o:o + MLSTM_COLS]; o += MLSTM_COLS
    d = w_in[:, o:o + NA_COLS]
    D = w_in.shape[0]
    pad_a = RWKV_PAD_COLS - RWKV_COLS
    w["in_a"] = jnp.concatenate([a, jnp.zeros((D, pad_a), F32)], axis=1).astype(BF16)
    w["in_b"] = b.astype(BF16)
    w["in_c"] = jnp.concatenate([c, jnp.zeros((D, MLSTM_PAD_COLS - MLSTM_COLS), F32)], axis=1).astype(BF16)
    w["in_d"] = d.astype(BF16)
    w["out"] = P["w_out"][l].astype(BF16)

    z = jnp.zeros((RWKV_DECAY_RANK, GROUP_W), F32)
    wup, aup = P["rwkv_w_up"][l], P["rwkv_a_up"][l]
    rw = {
        "mu": jnp.concatenate([P["rwkv_mu"][l], jnp.zeros((2, pad_a), F32)], axis=1),
        "w_up": jnp.concatenate([jnp.concatenate([wup[0], z], axis=1),
                                 jnp.concatenate([z, wup[1]], axis=1)], axis=0).astype(BF16),
        "a_up": jnp.concatenate([jnp.concatenate([aup[0], z], axis=1),
                                 jnp.concatenate([z, aup[1]], axis=1)], axis=0).astype(BF16),
        "w0": P["rwkv_w0"][l].reshape(1, 2 * GROUP_W),
        "a0": P["rwkv_a0"][l].reshape(1, 2 * GROUP_W),
        "g_up": jnp.concatenate([P["rwkv_g_up"][l],
                                 jnp.zeros((256 - RWKV_GATE_RANK, GROUP_W), F32)], axis=0).astype(BF16),
        "k_k": P["rwkv_k_k"][l].reshape(1, GROUP_W),
        "k_a": P["rwkv_k_a"][l].reshape(1, GROUP_W),
        "r_k": P["rwkv_r_k"][l].reshape(1, GROUP_W),
    }
    w["rwkv"] = rw
    w["rwkv_ln"] = P["rwkv_ln"][l]
    w["pool_w"] = P["pool_w"][l].astype(BF16)
    w["pool_scale"] = P["pool_scale"][l]
    w["mlstm_conv"] = P["mlstm_conv"][l]
    gb = P["mlstm_gate_b"][l].reshape(1, 4 * MLSTM_HEADS)
    w["mlstm_gate_b"] = jnp.concatenate([gb, jnp.zeros((1, LANES - 4 * MLSTM_HEADS), F32)], axis=1)
    w["mlstm_norm"] = P["mlstm_norm"][l]
    w["na_qk_norm"] = P["na_qk_norm"][l]
    w["na_rpb"] = P["na_rpb"][l]
    rcat = jnp.concatenate([P["router_g_w"][l], P["router_e_w"][l],
                            jnp.zeros((D, LANES - N_GROUPS - N_EXPERTS), F32)], axis=1)
    rhi = rcat.astype(BF16)
    w["router_w"] = jnp.stack([rhi, (rcat - rhi.astype(F32)).astype(BF16)])
    w["router_b"] = jnp.concatenate([P["router_g_b"][l], P["router_e_b"][l],
                                     jnp.zeros((LANES - N_GROUPS - N_EXPERTS,), F32)]).reshape(1, LANES)
    w["moe_gate"] = P["moe_w_gate"][l].astype(BF16)
    w["moe_up"] = P["moe_w_up"][l].astype(BF16)
    w["moe_down"] = P["moe_w_down"][l].astype(BF16)
    return w


def rope_tables(T):
    t = jnp.arange(T)
    row = (t // GRID_W).astype(F32)
    col = (t % GRID_W).astype(F32)
    n_pairs = MLSTM_DK // 4
    inv = ROPE_BASE ** (-jnp.arange(n_pairs, dtype=F32) / n_pairs)
    ang = jnp.concatenate([row[:, None] * inv, col[:, None] * inv], axis=-1)
    cos = jnp.repeat(jnp.cos(ang), 2, axis=1)
    sin = jnp.repeat(jnp.sin(ang), 2, axis=1) * jnp.tile(jnp.array([-1.0, 1.0], F32), MLSTM_DK // 2)
    return cos, sin


def rwkv_mixer(p_ctx, p_lat, w, need_ctx):
    B = p_lat.shape[0]
    names = ("r", "kk", "v", "bonus", "g", "lw", "kd", "bb")
    fc = dict(zip(names, rwkv_prep(p_ctx, w["rwkv"])))
    fl = dict(zip(names, rwkv_prep(p_lat, w["rwkv"])))
    s0 = jnp.zeros((2, B, RWKV_PAIRS, LANES, LANES), F32)
    y_c, s_ctx = rwkv_scan(fc, s0)
    y_l, _ = rwkv_scan(fl, s_ctx)
    out_l = rwkv_readout(y_l, fl["bonus"], fl["g"], w["rwkv_ln"])
    out_c = rwkv_readout(y_c, fc["bonus"], fc["g"], w["rwkv_ln"]) if need_ctx else None
    return out_c, out_l


def mlstm_mixer(p_ctx, p_lat, w, rope, need_ctx):
    B = p_lat.shape[0]
    H = MLSTM_HEADS
    state = (jnp.zeros((2, B, H, MLSTM_DK, MLSTM_DV), F32),
             jnp.zeros((2, B, H, 1, MLSTM_DK), F32),
             jnp.zeros((2, B, H, 1, 1), F32))
    outs = []
    for p, tabs in ((p_ctx, None), (p_lat, rope)):
        q, k, gates = mlstm_prep(p, w["mlstm_conv"], w["mlstm_gate_b"], tabs)
        gates_t = jnp.swapaxes(gates[:, :, :4 * H], 1, 2)
        h, *state = mlstm_scan(q, k, p, gates, gates_t, tuple(state))
        outs.append(h)
    out_l = mlstm_readout(outs[1], p_lat, w["mlstm_norm"])
    out_c = mlstm_readout(outs[0], p_ctx, w["mlstm_norm"]) if need_ctx else None
    return out_c, out_l


def na_mixer(p_ctx, p_lat, w, need_ctx):
    qc, kc, vc = na_prep(p_ctx, w["na_qk_norm"])
    ql, kl, vl = na_prep(p_lat, w["na_qk_norm"])
    rows = p_lat.shape[1] // GRID_W
    bias = na_bias_table(w["na_rpb"], rows)
    out_l = na_lat_attention(ql, kl, vl, kc, vc, bias)
    out_c = na_ctx_attention(qc, kc, vc) if need_ctx else None
    return out_c, out_l


def token_mixers(u_ctx, u_lat, w, rope, need_ctx):
    outs_c, outs_l = [], []
    pa = (matmul(u_ctx, w["in_a"]), matmul(u_lat, w["in_a"]))
    oc, ol = rwkv_mixer(pa[0], pa[1], w, need_ctx)
    outs_c.append(oc); outs_l.append(ol)
    pb = (matmul(u_ctx, w["in_b"]), matmul(u_lat, w["in_b"]))
    outs_l.append(pool_mixer(pb[1], w["pool_w"], w["pool_scale"]))
    outs_c.append(pool_mixer(pb[0], w["pool_w"], w["pool_scale"]) if need_ctx else None)
    pc = (matmul(u_ctx, w["in_c"]), matmul(u_lat, w["in_c"]))
    oc, ol = mlstm_mixer(pc[0], pc[1], w, rope, need_ctx)
    outs_c.append(oc); outs_l.append(ol)
    pd = (matmul(u_ctx, w["in_d"]), matmul(u_lat, w["in_d"]))
    oc, ol = na_mixer(pd[0], pd[1], w, need_ctx)
    outs_c.append(oc); outs_l.append(ol)
    return outs_c, outs_l


def kernel(x, c, ctx, c_ctx, ada_w, ada_b, norm_mix, norm_ffn, w_in, w_out, rwkv_mu, rwkv_w0, rwkv_w_up, rwkv_a0, rwkv_a_up, rwkv_g_up, rwkv_k_k, rwkv_k_a, rwkv_r_k, rwkv_ln, pool_w, pool_scale, mlstm_conv, mlstm_gate_b, mlstm_norm, na_qk_norm, na_rpb, router_g_w, router_g_b, router_e_w, router_e_b, moe_w_gate, moe_w_up, moe_w_down):
    P = dict(w_in=w_in, w_out=w_out, rwkv_mu=rwkv_mu, rwkv_w0=rwkv_w0, rwkv_w_up=rwkv_w_up,
             rwkv_a0=rwkv_a0, rwkv_a_up=rwkv_a_up, rwkv_g_up=rwkv_g_up, rwkv_k_k=rwkv_k_k,
             rwkv_k_a=rwkv_k_a, rwkv_r_k=rwkv_r_k, rwkv_ln=rwkv_ln, pool_w=pool_w, pool_scale=pool_scale,
             mlstm_conv=mlstm_conv, mlstm_gate_b=mlstm_gate_b, mlstm_norm=mlstm_norm,
             na_qk_norm=na_qk_norm, na_rpb=na_rpb, router_g_w=router_g_w, router_g_b=router_g_b,
             router_e_w=router_e_w, router_e_b=router_e_b, moe_w_gate=moe_w_gate, moe_w_up=moe_w_up,
             moe_w_down=moe_w_down)
    B, T, D = x.shape
    Lc = ctx.shape[1]
    depth = ada_w.shape[0]
    rope = rope_tables(T)

    cvec = jnp.concatenate([c, c_ctx[None, :], jnp.zeros((SUBLANES - B - 1, D), F32)], axis=0)
    mods = ada_modulation(cvec, ada_w, ada_b)

    h_lat, h_ctx = x, ctx
    for l in range(depth):
        last = l == depth - 1
        w = _layer_weights(l, P)
        m = mods[l].reshape(SUBLANES, 6, D)
        m_lat = [m[:B, i][:, None, :] for i in range(6)]
        m_ctx = [jnp.broadcast_to(m[B, i][None, None, :], (B, 1, D)) for i in range(6)]

        u_lat = norm_modulate(h_lat, norm_mix[l], m_lat[0], m_lat[1])
        u_ctx = norm_modulate(h_ctx, norm_mix[l], m_ctx[0], m_ctx[1])
        ys_ctx, ys_lat = token_mixers(u_ctx, u_lat, w, rope, not last)
        h_lat = out_proj_residual(ys_lat, w["out"], h_lat, m_lat[2])
        n_rows = B * T if last else B * (T + Lc)
        streams = [(h_lat, m_lat[3], m_lat[4])]
        if not last:
            h_ctx = out_proj_residual(ys_ctx, w["out"], h_ctx, m_ctx[2])
            streams.append((h_ctx, m_ctx[3], m_ctx[4]))
        rows, ids, wts = ffn_norm_router(streams, norm_ffn[l], w["router_w"], w["router_b"])
        f2 = hier_moe(rows, ids[:, :TOP_K], wts[:, :TOP_K], w["moe_gate"], w["moe_up"], w["moe_down"])
        if not last:
            h_ctx = moe_combine(h_ctx, f2, m_ctx[5], B * T, n_rows)
        h_lat = moe_combine(h_lat, f2, m_lat[5], 0, n_rows)
    return h_lat
```

```python
import functools
import math

import numpy as np
import jax
import jax.numpy as jnp
from jax import lax
from jax.experimental import pallas as pl
from jax.experimental.pallas import tpu as pltpu

F32 = jnp.float32
BF16 = jnp.bfloat16

D_MODEL = 4096
DEPTH = 2
GRID_W = 64
EPS = 1e-6
GROUP_W = 1024

RWKV_HEAD = 64
RWKV_DECAY_RANK = 64
RWKV_ICLR_RANK = 64
RWKV_GATE_RANK = 160
RWKV_GN_EPS = 64e-5
RWKV_COLS = 3 * GROUP_W + 2 * RWKV_DECAY_RANK + 2 * RWKV_ICLR_RANK + RWKV_GATE_RANK
RWKV_PAD_COLS = 3 * GROUP_W + 256 + 256

POOL_WINDOWS = (2, 4, 8, 16)
POOL_GROUP = 256
POOL_HALO = 8

MLSTM_HEADS = 4
MLSTM_DV = 256
MLSTM_DK = 128
MLSTM_COLS = 2 * MLSTM_HEADS * MLSTM_DK + 2 * GROUP_W + 4 * MLSTM_HEADS
MLSTM_PAD_COLS = 2 * MLSTM_HEADS * MLSTM_DK + 2 * GROUP_W + 128
ROPE_BASE = 10000.0

NA_HEADS = 8
NA_HEAD = 128
WIN_H = 8
WIN_W = 16
NA_COLS = 3 * GROUP_W
NA_QROWS = 4
NA_KROWS = 3 * NA_QROWS

N_GROUPS = 4
EXPERTS_PER_GROUP = 4
N_EXPERTS = 16
TOP_K = 2
EXPERT_FF = 1024
MOE_BLOCK = 256
MOE_FF_TILE = 256
MOE_OUT_TILE = 1024
MOE_STAGE_BYTES = 2 * 1024 * 1024

LANES = 128
SUBLANES = 8
VMEM_LIMIT = 56 * 1024 * 1024
NEG = -1e30

RWKV_CHUNK = 64
RWKV_PAIRS = 8
MLSTM_CHUNK = 256


def _cp(*sem):
    return pltpu.CompilerParams(dimension_semantics=sem, vmem_limit_bytes=VMEM_LIMIT)


def _dot(a, b):
    return jnp.dot(a.astype(BF16), b.astype(BF16), preferred_element_type=F32)


def _dot_nt(a, b):
    return lax.dot_general(a.astype(BF16), b.astype(BF16), (((1,), (1,)), ((), ())),
                           preferred_element_type=F32)


def _dot_tn(a, b):
    return lax.dot_general(a.astype(BF16), b.astype(BF16), (((0,), (0,)), ((), ())),
                           preferred_element_type=F32)


def _split2(x):
    hi = x.astype(BF16)
    lo = (x - hi.astype(F32)).astype(BF16)
    return hi, lo


def _dot3(a, b, dot=_dot):
    ah, al = _split2(a)
    bh, bl = _split2(b)
    return dot(ah, bh) + (dot(ah, bl) + dot(al, bh))


def _dot_exact_lhs(a_bf16, b):
    b0 = b.astype(BF16)
    r1 = b - b0.astype(F32)
    b1 = r1.astype(BF16)
    b2 = (r1 - b1.astype(F32)).astype(BF16)
    return _dot(a_bf16, b0) + (_dot(a_bf16, b1) + _dot(a_bf16, b2))


def _sigmoid(x):
    return 1.0 / (1.0 + jnp.exp(-x))


def _silu(x):
    return x * _sigmoid(x)


def _log_sigmoid(x):
    return jnp.minimum(x, 0.0) - jnp.log(1.0 + jnp.exp(-jnp.abs(x)))


def _pack_bf16_pair(lo, hi):
    lo_bits = lax.bitcast_convert_type(lo.astype(BF16).astype(F32), jnp.uint32)
    hi_bits = lax.bitcast_convert_type(hi.astype(BF16).astype(F32), jnp.uint32)
    return (lo_bits >> 16) | hi_bits


def _unpack_bf16_pair(w):
    lo = lax.bitcast_convert_type(w << 16, F32)
    hi = lax.bitcast_convert_type(w & jnp.uint32(0xFFFF0000), F32)
    return lo, hi


def _head_sum(x, width):
    r = lax.broadcasted_iota(jnp.int32, (LANES, LANES), 0) // width
    c = lax.broadcasted_iota(jnp.int32, (LANES, LANES), 1) // width
    e = jnp.where(r == c, 1.0, 0.0).astype(BF16)
    outs = []
    for j in range(x.shape[1] // LANES):
        hi, lo = _split2(x[:, j * LANES:(j + 1) * LANES])
        outs.append(_dot(hi, e) + _dot(lo, e))
    return jnp.concatenate(outs, axis=1)


def _ada_kernel(c_ref, w_ref, b_ref, o_ref):
    s = _silu(c_ref[...])
    o_ref[0] = _dot(s, w_ref[0]) + b_ref[0]


def ada_modulation(cvec, ada_w, ada_b):
    L, D, N = ada_w.shape
    tn = 512
    return pl.pallas_call(
        _ada_kernel,
        grid=(L, N // tn),
        in_specs=[pl.BlockSpec((SUBLANES, D), lambda l, j: (0, 0)),
                  pl.BlockSpec((1, D, tn), lambda l, j: (l, 0, j)),
                  pl.BlockSpec((1, 1, tn), lambda l, j: (l, 0, j))],
        out_specs=pl.BlockSpec((1, SUBLANES, tn), lambda l, j: (l, 0, j)),
        out_shape=jax.ShapeDtypeStruct((L, SUBLANES, N), F32),
        compiler_params=_cp("parallel", "parallel"),
        name="ada_modulation",
    )(cvec, ada_w, ada_b.reshape(L, 1, N))


def _normmod_kernel(h_ref, g_ref, sh_ref, sc_ref, o_ref):
    x = h_ref[0]
    y = x * lax.rsqrt(jnp.mean(x * x, axis=-1, keepdims=True) + EPS) * g_ref[...]
    o_ref[0] = (y * (1.0 + sc_ref[0]) + sh_ref[0]).astype(o_ref.dtype)


def norm_modulate(h, g, shift, scale):
    B, T, D = h.shape
    tm = 256
    vec = pl.BlockSpec((1, 1, D), lambda b, i: (b, 0, 0))
    return pl.pallas_call(
        _normmod_kernel,
        grid=(B, T // tm),
        in_specs=[pl.BlockSpec((1, tm, D), lambda b, i: (b, i, 0)),
                  pl.BlockSpec((1, D), lambda b, i: (0, 0)), vec, vec],
        out_specs=pl.BlockSpec((1, tm, D), lambda b, i: (b, i, 0)),
        out_shape=jax.ShapeDtypeStruct((B, T, D), BF16),
        compiler_params=_cp("parallel", "parallel"),
        name="norm_modulate",
    )(h, g.reshape(1, D), shift, scale)


def _mm_kernel(x_ref, w_ref, o_ref):
    o_ref[0] = jnp.dot(x_ref[0], w_ref[...], preferred_element_type=F32).astype(o_ref.dtype)


def _pick_tile(n, cands):
    for t in cands:
        if n % t == 0:
            return t
    raise ValueError(n)


def matmul(x, w, out_dtype=F32):
    B, T, K = x.shape
    N = w.shape[1]
    tm = _pick_tile(T, (1024, 512, 256))
    tn = _pick_tile(N, (1024, 896, 768, 640, 512))
    return pl.pallas_call(
        _mm_kernel,
        grid=(B, T // tm, N // tn),
        in_specs=[pl.BlockSpec((1, tm, K), lambda b, i, j: (b, i, 0)),
                  pl.BlockSpec((K, tn), lambda b, i, j: (0, j))],
        out_specs=pl.BlockSpec((1, tm, tn), lambda b, i, j: (b, i, j)),
        out_shape=jax.ShapeDtypeStruct((B, T, N), out_dtype),
        compiler_params=_cp("parallel", "parallel", "arbitrary"),
        name="in_proj",
    )(x, w)


def _outproj_kernel(ya_ref, yb_ref, yc_ref, yd_ref, w_ref, h_ref, gate_ref, o_ref):
    acc = jnp.dot(ya_ref[0], w_ref[0:GROUP_W, :], preferred_element_type=F32)
    acc += jnp.dot(yb_ref[0], w_ref[GROUP_W:2 * GROUP_W, :], preferred_element_type=F32)
    acc += jnp.dot(yc_ref[0], w_ref[2 * GROUP_W:3 * GROUP_W, :], preferred_element_type=F32)
    acc += jnp.dot(yd_ref[0], w_ref[3 * GROUP_W:4 * GROUP_W, :], preferred_element_type=F32)
    o_ref[0] = h_ref[0] + gate_ref[0] * acc


def out_proj_residual(ys, w, h, gate):
    B, T, D = h.shape
    tm = _pick_tile(T, (512, 256))
    tn = 1024
    ysp = pl.BlockSpec((1, tm, GROUP_W), lambda b, i, j: (b, i, 0))
    return pl.pallas_call(
        _outproj_kernel,
        grid=(B, T // tm, D // tn),
        in_specs=[ysp, ysp, ysp, ysp,
                  pl.BlockSpec((D, tn), lambda b, i, j: (0, j)),
                  pl.BlockSpec((1, tm, tn), lambda b, i, j: (b, i, j)),
                  pl.BlockSpec((1, 1, tn), lambda b, i, j: (b, 0, j))],
        out_specs=pl.BlockSpec((1, tm, tn), lambda b, i, j: (b, i, j)),
        out_shape=jax.ShapeDtypeStruct((B, T, D), F32),
        compiler_params=_cp("parallel", "parallel", "arbitrary"),
        name="out_proj",
    )(*ys, w, h, gate)


def _halo_specs(tm, width, halo=SUBLANES):
    per = tm // halo

    def prev_map(b, i):
        return (b, jnp.maximum(i * per - 1, 0), 0)

    def next_map(nb):
        return lambda b, i: (b, jnp.minimum((i + 1) * per, nb - 1), 0)

    return prev_map, next_map


def _shift_rows(x, prev_row, next_row):
    tm = x.shape[0]
    row = lax.broadcasted_iota(jnp.int32, x.shape, 0)
    prev = jnp.where(row == 0, prev_row, pltpu.roll(x, 1, axis=0))
    nxt = jnp.where(row == tm - 1, next_row, pltpu.roll(x, tm - 1, axis=0))
    return prev, nxt


def _rwkv_prep_kernel(p_ref, pp_ref, pn_ref, mu_ref, wup_ref, w0_ref, aup_ref, a0_ref, gup_ref,
                      kk_w_ref, ka_w_ref, rk_w_ref,
                      r_ref, kk_ref, v_ref, bonus_ref, g_ref, lw_ref, kd_ref, bb_ref):
    i = pl.program_id(1)
    last = pl.num_programs(1) - 1
    x = p_ref[0]
    prev_row = jnp.where(i == 0, 0.0, pp_ref[0, SUBLANES - 1:SUBLANES, :])
    next_row = jnp.where(i == last, 0.0, pn_ref[0, 0:1, :])
    prev, nxt = _shift_rows(x, prev_row, next_row)
    x = x + mu_ref[0:1, :] * (prev - x) + mu_ref[1:2, :] * (nxt - x)

    r = x[:, 0:GROUP_W]
    k = x[:, GROUP_W:2 * GROUP_W]
    v = x[:, 2 * GROUP_W:3 * GROUP_W]
    low = x[:, 3 * GROUP_W:3 * GROUP_W + 256]
    gd = x[:, 3 * GROUP_W + 256:3 * GROUP_W + 512]

    zw = _dot(jnp.tanh(low[:, 0:LANES]), wup_ref[...]) + w0_ref[...]
    za = _dot(low[:, LANES:2 * LANES], aup_ref[...]) + a0_ref[...]
    kkr = k * kk_w_ref[...]
    kk = kkr * lax.rsqrt(_head_sum(kkr * kkr, RWKV_HEAD) + EPS)
    r_ref[0] = r.astype(BF16)
    kk_ref[0] = kk.astype(BF16)
    v_ref[0] = v.astype(BF16)
    bonus_ref[0] = (_head_sum(r * k * rk_w_ref[...], RWKV_HEAD) * v).astype(BF16)
    g_ref[0] = _dot(_sigmoid(gd), gup_ref[...]).astype(BF16)
    for d in range(2):
        sl = slice(d * GROUP_W, (d + 1) * GROUP_W)
        lw_ref[d, 0] = (-math.exp(-0.5)) * _sigmoid(zw[:, sl])
        a = _sigmoid(za[:, sl])
        kd_ref[d, 0] = (k * (1.0 + (a - 1.0) * ka_w_ref[...])).astype(BF16)
        bb_ref[d, 0] = (kk * a).astype(BF16)


def rwkv_prep(p, w):
    B, T, W = p.shape
    tm = 128
    prev_map, next_map = _halo_specs(tm, W)
    nb8 = T // SUBLANES
    full = lambda shape: pl.BlockSpec(shape, lambda b, i: tuple(0 for _ in shape))
    tok = pl.BlockSpec((1, tm, GROUP_W), lambda b, i: (b, i, 0))
    tokd = pl.BlockSpec((2, 1, tm, GROUP_W), lambda b, i: (0, b, i, 0))
    sh = jax.ShapeDtypeStruct((B, T, GROUP_W), BF16)
    shd = jax.ShapeDtypeStruct((2, B, T, GROUP_W), BF16)
    shd_f32 = jax.ShapeDtypeStruct((2, B, T, GROUP_W), F32)
    return pl.pallas_call(
        _rwkv_prep_kernel,
        grid=(B, T // tm),
        in_specs=[pl.BlockSpec((1, tm, W), lambda b, i: (b, i, 0)),
                  pl.BlockSpec((1, SUBLANES, W), prev_map),
                  pl.BlockSpec((1, SUBLANES, W), next_map(nb8)),
                  full((2, W)), full((LANES, 2 * GROUP_W)), full((1, 2 * GROUP_W)),
                  full((LANES, 2 * GROUP_W)), full((1, 2 * GROUP_W)), full((256, GROUP_W)),
                  full((1, GROUP_W)), full((1, GROUP_W)), full((1, GROUP_W))],
        out_specs=[tok, tok, tok, tok, tok, tokd, tokd, tokd],
        out_shape=[sh, sh, sh, sh, sh, shd_f32, shd, shd],
        compiler_params=_cp("parallel", "parallel"),
        name="rwkv_prep",
    )(p, p, p, w["mu"], w["w_up"], w["w0"], w["a_up"], w["a0"], w["g_up"],
      w["k_k"], w["k_a"], w["r_k"])


def _dform(x, m0):
    return jnp.concatenate([jnp.where(m0, x, 0.0), jnp.where(m0, 0.0, x)], axis=0)


def _rwkv_chunk(tiles, sts, rev):
    n = len(tiles)
    C = tiles[0][0].shape[0]
    C2 = 2 * C
    sgn = jnp.where(rev, -1, 1)
    rr = lax.broadcasted_iota(jnp.int32, (C, C), 0)
    cc = lax.broadcasted_iota(jnp.int32, (C, C), 1)
    tri = jnp.where((rr - cc) * sgn >= 0, 1.0, 0.0).astype(BF16)
    lane = lax.broadcasted_iota(jnp.int32, (C, LANES), 1)
    m0 = lane < RWKV_HEAD
    rd = lax.broadcasted_iota(jnp.int32, (C2, C2), 0)
    cd = lax.broadcasted_iota(jnp.int32, (C2, C2), 1)
    same = (rd // C) == (cd // C)
    ahead = (rd % C - cd % C) * sgn
    strict = same & (ahead > 0)
    incl = same & (ahead >= 0)
    diag = rd == cd
    each = range(n)

    lins = [_dot_exact_lhs(tri, t[3]) for t in tiles]
    ltots = [jnp.sum(t[3], axis=0, keepdims=True) for t in tiles]
    a_d, r_d, k_d, b_d, v_d, kw_d, bw_d = [], [], [], [], [], [], []
    for (r, kk, v, lw, kd, bb), lin, ltot in zip(tiles, lins, ltots):
        einv = jnp.exp(-lin)
        ew = jnp.exp(ltot - lin)
        a_d.append(_dform(-kk * jnp.exp(lin - lw), m0).astype(BF16))
        r_d.append(_dform(r * jnp.exp(lin), m0).astype(BF16))
        k_d.append(_dform(kd * einv, m0).astype(BF16))
        b_d.append(_dform(bb * einv, m0).astype(BF16))
        v_d.append(_dform(v, m0).astype(BF16))
        kw_d.append(_dform(kd * ew, m0).astype(BF16))
        bw_d.append(_dform(bb * ew, m0).astype(BF16))

    grams = [_dot_nt(jnp.concatenate([a_d[i], r_d[i]], axis=0), jnp.concatenate([k_d[i], b_d[i]], axis=0))
             for i in each]
    a_ak = [jnp.where(strict, g[:C2, :C2], 0.0).astype(BF16) for g in grams]
    a_ab = [jnp.where(strict, g[:C2, C2:], 0.0) for g in grams]
    a_rk = [jnp.where(incl, g[C2:, :C2], 0.0).astype(BF16) for g in grams]
    a_rb = [jnp.where(incl, g[C2:, C2:], 0.0).astype(BF16) for g in grams]

    tinv = [jnp.where(diag, 1.0, 0.0) + a for a in a_ab]
    apow = a_ab
    for _ in range(int(math.log2(C)) - 1):
        apow = [_dot(a, a) for a in apow]
        tinv = [t + _dot(t, a) for t, a in zip(tinv, apow)]

    akv = [_dot(a_ak[i], v_d[i]) for i in each]
    p = [_dot(tinv[i], jnp.concatenate([a_d[i], akv[i].astype(BF16)], axis=1)) for i in each]
    ur = [_dot(jnp.concatenate([p[i][:, :LANES].astype(BF16), r_d[i]], axis=0), sts[i]) for i in each]
    u = [ur[i][:C2] + p[i][:, LANES:] for i in each]
    y2 = [ur[i][C2:] + _dot(a_rk[i], v_d[i]) + _dot(a_rb[i], u[i]) for i in each]
    ys = [y[:C] + y[C:] for y in y2]
    new = []
    for i in each:
        wc = jnp.broadcast_to(jnp.exp(ltots[i]), (LANES, LANES))
        wcol = jnp.sum(jnp.where(diag, wc, 0.0), axis=1, keepdims=True)
        new.append(sts[i] * wcol + _dot_tn(jnp.concatenate([bw_d[i], kw_d[i]], axis=0),
                                           jnp.concatenate([u[i].astype(BF16), v_d[i]], axis=0)))
    return ys, new


def _rwkv_scan_kernel(r_ref, kk_ref, v_ref, lw_ref, kd_ref, bb_ref, s0_ref, y_ref, sT_ref, st_scr,
                      *, npairs):
    d = pl.program_id(0)
    c = pl.program_id(3)
    nc = pl.num_programs(3)

    @pl.when(c == 0)
    def _():
        st_scr[...] = s0_ref[0, 0]

    lanes = [slice(j * LANES, (j + 1) * LANES) for j in range(npairs)]
    f32 = lambda a: a.astype(F32)
    tiles = [(f32(r_ref[0, :, sl]), f32(kk_ref[0, :, sl]), f32(v_ref[0, :, sl]),
              lw_ref[0, 0, :, sl], f32(kd_ref[0, 0, :, sl]), f32(bb_ref[0, 0, :, sl])) for sl in lanes]
    ys, new = _rwkv_chunk(tiles, [st_scr[j] for j in range(npairs)], d == 1)
    for j, sl in enumerate(lanes):
        y_ref[0, 0, :, sl] = ys[j]
        st_scr[j] = new[j]

    @pl.when(c == nc - 1)
    def _():
        sT_ref[0, 0] = st_scr[...]


def rwkv_scan(f, s0, npairs=RWKV_PAIRS):
    r, kk, v, lw, kd, bb = f["r"], f["kk"], f["v"], f["lw"], f["kd"], f["bb"]
    B, T, _ = r.shape
    C = RWKV_CHUNK
    nc = T // C
    ng = RWKV_PAIRS // npairs
    W = npairs * LANES
    chunk = lambda d, c: c + d * (nc - 1 - 2 * c)
    tok = pl.BlockSpec((1, C, W), lambda d, b, g, c: (b, chunk(d, c), g))
    tokd = pl.BlockSpec((1, 1, C, W), lambda d, b, g, c: (d, b, chunk(d, c), g))
    stsp = pl.BlockSpec((1, 1, npairs, LANES, LANES), lambda d, b, g, c: (d, b, g, 0, 0))
    return pl.pallas_call(
        functools.partial(_rwkv_scan_kernel, npairs=npairs),
        grid=(2, B, ng, nc),
        in_specs=[tok, tok, tok, tokd, tokd, tokd, stsp],
        out_specs=[tokd, stsp],
        out_shape=[jax.ShapeDtypeStruct((2, B, T, GROUP_W), F32),
                   jax.ShapeDtypeStruct(s0.shape, F32)],
        scratch_shapes=[pltpu.VMEM((npairs, LANES, LANES), F32)],
        compiler_params=_cp("parallel", "parallel", "parallel", "arbitrary"),
        name="rwkv_scan",
    )(r, kk, v, lw, kd, bb, s0)


def _rwkv_readout_kernel(yf_ref, yb_ref, bonus_ref, g_ref, ln_ref, o_ref):
    y = yf_ref[0, 0] + yb_ref[0, 0]
    mean = _head_sum(y, RWKV_HEAD) * (1.0 / RWKV_HEAD)
    cen = y - mean
    var = _head_sum(cen * cen, RWKV_HEAD) * (1.0 / RWKV_HEAD)
    yn = cen * lax.rsqrt(var + RWKV_GN_EPS) * ln_ref[0:1, :] + ln_ref[1:2, :]
    o_ref[0] = ((yn + bonus_ref[0].astype(F32)) * g_ref[0].astype(F32)).astype(o_ref.dtype)


def rwkv_readout(y, bonus, g, ln):
    _, B, T, W = y.shape
    tm = 256
    tok = pl.BlockSpec((1, tm, W), lambda b, i: (b, i, 0))
    return pl.pallas_call(
        _rwkv_readout_kernel,
        grid=(B, T // tm),
        in_specs=[pl.BlockSpec((1, 1, tm, W), lambda b, i: (0, b, i, 0)),
                  pl.BlockSpec((1, 1, tm, W), lambda b, i: (1, b, i, 0)),
                  tok, tok, pl.BlockSpec((2, W), lambda b, i: (0, 0))],
        out_specs=tok,
        out_shape=jax.ShapeDtypeStruct((B, T, W), BF16),
        compiler_params=_cp("parallel", "parallel"),
        name="rwkv_readout",
    )(y, y, bonus, g, ln)


def _pool_kernel(p_ref, pp_ref, pn_ref, w_ref, sc_ref, o_ref, *, total):
    i = pl.program_id(1)
    last = pl.num_programs(1) - 1
    x = p_ref[0]
    tm = x.shape[0]
    n = tm + 2 * POOL_HALO
    ext = jnp.concatenate([jnp.where(i == 0, 0.0, pp_ref[0]), x,
                           jnp.where(i == last, 0.0, pn_ref[0])], axis=0)
    t = i * tm + lax.broadcasted_iota(jnp.int32, (tm, 1), 0)
    outs = []
    for gi, win in enumerate(POOL_WINDOWS):
        sl = slice(gi * POOL_GROUP, (gi + 1) * POOL_GROUP)
        e = ext[:, sl]
        s = e + pltpu.roll(e, 1, axis=0)
        step = 1
        while 2 * step < win:
            s = pltpu.roll(s, step, axis=0) + pltpu.roll(s, n - step, axis=0)
            step *= 2
        h = win // 2
        cnt = (jnp.minimum(t + h, total) - jnp.maximum(t - h, 0)).astype(F32)
        z = s[POOL_HALO:POOL_HALO + tm] / cnt - x[:, sl]
        outs.append(_dot(z, w_ref[gi]))
    o_ref[0] = (jnp.concatenate(outs, axis=1) * sc_ref[...]).astype(o_ref.dtype)


def pool_mixer(p, pool_w, pool_scale):
    B, T, W = p.shape
    tm = 256
    prev_map, next_map = _halo_specs(tm, W)
    return pl.pallas_call(
        functools.partial(_pool_kernel, total=T),
        grid=(B, T // tm),
        in_specs=[pl.BlockSpec((1, tm, W), lambda b, i: (b, i, 0)),
                  pl.BlockSpec((1, POOL_HALO, W), prev_map),
                  pl.BlockSpec((1, POOL_HALO, W), next_map(T // POOL_HALO)),
                  pl.BlockSpec(pool_w.shape, lambda b, i: (0, 0, 0)),
                  pl.BlockSpec((1, W), lambda b, i: (0, 0))],
        out_specs=pl.BlockSpec((1, tm, W), lambda b, i: (b, i, 0)),
        out_shape=jax.ShapeDtypeStruct((B, T, W), BF16),
        compiler_params=_cp("parallel", "parallel"),
        name="pool_mixer",
    )(p, p, p, pool_w, pool_scale.reshape(1, W))


def _mlstm_prep_kernel(p_ref, pp_ref, pn_ref, g_ref, cw_ref, gb_ref, *rest, rope):
    if rope:
        cos_ref, sin_ref, q_ref, k_ref, go_ref = rest
    else:
        q_ref, k_ref, go_ref = rest
    i = pl.program_id(1)
    last = pl.num_programs(1) - 1
    x = p_ref[0]
    prev_row = jnp.where(i == 0, 0.0, pp_ref[0, SUBLANES - 1:SUBLANES, :])
    next_row = jnp.where(i == last, 0.0, pn_ref[0, 0:1, :])
    prev, nxt = _shift_rows(x, prev_row, next_row)
    qk = _silu(cw_ref[0:1, :] * prev + cw_ref[1:2, :] * x + cw_ref[2:3, :] * nxt)
    if rope:
        w = qk.shape[1]
        lane = lax.broadcasted_iota(jnp.int32, qk.shape, 1)
        partner = jnp.where(lane % 2 == 0, pltpu.roll(qk, w - 1, axis=1), pltpu.roll(qk, 1, axis=1))
        reps = w // LANES
        cos = jnp.concatenate([cos_ref[...]] * reps, axis=1)
        sin = jnp.concatenate([sin_ref[...]] * reps, axis=1)
        qk = qk * cos + partner * sin
    half = MLSTM_HEADS * MLSTM_DK
    q_ref[0] = qk[:, :half].astype(BF16)
    k_ref[0] = (qk[:, half:] * (MLSTM_DK ** -0.5)).astype(BF16)
    g = g_ref[0] + gb_ref[...]
    lane = lax.broadcasted_iota(jnp.int32, g.shape, 1)
    go_ref[0] = jnp.where((lane // MLSTM_HEADS) % 2 == 1, _log_sigmoid(g), g)


def mlstm_prep(p, conv_w, gate_b, rope_tabs):
    B, T, W = p.shape
    tm = 256
    QK = 2 * MLSTM_HEADS * MLSTM_DK
    prev_map, next_map = _halo_specs(tm, QK)
    rope = rope_tabs is not None
    in_specs = [pl.BlockSpec((1, tm, QK), lambda b, i: (b, i, 0)),
                pl.BlockSpec((1, SUBLANES, QK), prev_map),
                pl.BlockSpec((1, SUBLANES, QK), next_map(T // SUBLANES)),
                pl.BlockSpec((1, tm, LANES), lambda b, i: (b, i, (W - LANES) // LANES)),
                pl.BlockSpec((3, QK), lambda b, i: (0, 0)),
                pl.BlockSpec((1, LANES), lambda b, i: (0, 0))]
    args = [p, p, p, p, conv_w, gate_b]
    if rope:
        in_specs += [pl.BlockSpec((tm, LANES), lambda b, i: (i, 0))] * 2
        args += list(rope_tabs)
    half = MLSTM_HEADS * MLSTM_DK
    return pl.pallas_call(
        functools.partial(_mlstm_prep_kernel, rope=rope),
        grid=(B, T // tm),
        in_specs=in_specs,
        out_specs=[pl.BlockSpec((1, tm, half), lambda b, i: (b, i, 0)),
                   pl.BlockSpec((1, tm, half), lambda b, i: (b, i, 0)),
                   pl.BlockSpec((1, tm, LANES), lambda b, i: (b, i, 0))],
        out_shape=[jax.ShapeDtypeStruct((B, T, half), BF16),
                   jax.ShapeDtypeStruct((B, T, half), BF16),
                   jax.ShapeDtypeStruct((B, T, LANES), F32)],
        compiler_params=_cp("parallel", "parallel"),
        name="mlstm_prep",
    )(*args)


def _mlstm_chunk_kernel(q_ref, k_ref, v_ref, gc_ref, gr_ref, c0_ref, n0_ref, m0_ref,
                        h_ref, cT_ref, nT_ref, mT_ref, c_scr, n_scr, m_scr):
    d = pl.program_id(0)
    c = pl.program_id(2)
    nc = pl.num_programs(2)
    H = MLSTM_HEADS

    @pl.when(c == 0)
    def _():
        c_scr[...] = c0_ref[0, 0]
        n_scr[...] = n0_ref[0, 0]
        m_scr[...] = m0_ref[0, 0]

    rev = d == 1
    L = q_ref.shape[1]
    rr = lax.broadcasted_iota(jnp.int32, (L, L), 0)
    cc = lax.broadcasted_iota(jnp.int32, (L, L), 1)
    sgn = jnp.where(rev, -1, 1)
    seen = (rr - cc) * sgn >= 0
    tri = jnp.where(seen, 1.0, 0.0).astype(BF16)
    tri_t = jnp.where((cc - rr) * sgn >= 0, 1.0, 0.0).astype(BF16)

    gc = gc_ref[0]
    gr = gr_ref[0]
    lane = lax.broadcasted_iota(jnp.int32, gc.shape, 1)
    subl = lax.broadcasted_iota(jnp.int32, gr.shape, 0)
    pick_c = lambda a, idx: jnp.sum(jnp.where(lane == idx, a, 0.0), axis=1, keepdims=True)
    pick_r = lambda a, idx: jnp.sum(jnp.where(subl == idx, a, 0.0), axis=0, keepdims=True)
    bc_all = _dot_exact_lhs(tri, gc)
    gr_hi = gr.astype(BF16)
    gr_r1 = gr - gr_hi.astype(F32)
    gr_mid = gr_r1.astype(BF16)
    gr_lo = (gr_r1 - gr_mid.astype(F32)).astype(BF16)
    br_all = _dot(gr_hi, tri_t) + (_dot(gr_mid, tri_t) + _dot(gr_lo, tri_t))

    heads = range(H)
    i_lane = [d * (2 * H) + j for j in heads]
    f_lane = [d * (2 * H) + H + j for j in heads]
    ig_c = [pick_c(gc, i_lane[j]) for j in heads]
    lf_c = [pick_c(gc, f_lane[j]) for j in heads]
    b_c = [pick_c(bc_all, f_lane[j]) for j in heads]
    ig_r = [pick_r(gr, i_lane[j]) for j in heads]
    b_r = [pick_r(br_all, f_lane[j]) for j in heads]
    m_prev = [m_scr[j] for j in heads]
    q = [q_ref[0, :, j * MLSTM_DK:(j + 1) * MLSTM_DK] for j in heads]
    k = [k_ref[0, :, j * MLSTM_DK:(j + 1) * MLSTM_DK] for j in heads]
    v = [v_ref[0, :, j * MLSTM_DV:(j + 1) * MLSTM_DV].astype(BF16) for j in heads]
    cst = [c_scr[j] for j in heads]
    nst = [n_scr[j] for j in heads]

    qk = [_dot_nt(q[j], k[j]) for j in heads]
    qc = [_dot(q[j], cst[j]) for j in heads]
    dlog = [jnp.where(seen, b_c[j] - b_r[j] + ig_r[j], NEG) for j in heads]
    inter = [b_c[j] + m_prev[j] for j in heads]
    m_t = [jnp.maximum(inter[j], jnp.max(dlog[j], axis=1, keepdims=True)) for j in heads]
    s = [qk[j] * jnp.exp(dlog[j] - m_t[j]) for j in heads]
    w_inter = [jnp.exp(inter[j] - m_t[j]) for j in heads]
    sv = [_dot(s[j], v[j]) for j in heads]
    for j in heads:
        num = sv[j] + w_inter[j] * qc[j]
        den = (jnp.sum(s[j], axis=1, keepdims=True)
               + w_inter[j] * jnp.sum(q[j].astype(F32) * nst[j], axis=1, keepdims=True))
        h_ref[0, 0, :, j * MLSTM_DV:(j + 1) * MLSTM_DV] = num / jnp.maximum(jnp.abs(den), jnp.exp(-m_t[j]))

    b_end = [jnp.sum(lf_c[j], axis=0, keepdims=True) for j in heads]
    g_s = [b_end[j] - b_c[j] + ig_c[j] for j in heads]
    m_new = [jnp.maximum(b_end[j] + m_prev[j], jnp.max(g_s[j], axis=0, keepdims=True)) for j in heads]
    kw = [k[j].astype(F32) * jnp.exp(g_s[j] - m_new[j]) for j in heads]
    kv = [_dot_tn(kw[j], v[j]) for j in heads]
    for j in heads:
        decay = jnp.exp(b_end[j] + m_prev[j] - m_new[j])
        c_scr[j] = decay * cst[j] + kv[j]
        n_scr[j] = decay * nst[j] + jnp.sum(kw[j], axis=0, keepdims=True)
        m_scr[j] = m_new[j]

    @pl.when(c == nc - 1)
    def _():
        cT_ref[0, 0] = c_scr[...]
        nT_ref[0, 0] = n_scr[...]
        mT_ref[0, 0] = m_scr[...]


def mlstm_scan(q, k, p, gates, gates_t, state):
    B, T, _ = q.shape
    L = min(MLSTM_CHUNK, T)
    nc = T // L
    H = MLSTM_HEADS
    QK = H * MLSTM_DK
    c0, n0, m0 = state
    chunk = lambda d, c: c + d * (nc - 1 - 2 * c)
    v_blk = (2 * QK) // GROUP_W
    csp = pl.BlockSpec((1, 1, H, MLSTM_DK, MLSTM_DV), lambda d, b, c: (d, b, 0, 0, 0))
    nsp = pl.BlockSpec((1, 1, H, 1, MLSTM_DK), lambda d, b, c: (d, b, 0, 0, 0))
    msp = pl.BlockSpec((1, 1, H, 1, 1), lambda d, b, c: (d, b, 0, 0, 0))
    return pl.pallas_call(
        _mlstm_chunk_kernel,
        grid=(2, B, nc),
        in_specs=[pl.BlockSpec((1, L, QK), lambda d, b, c: (b, chunk(d, c), 0)),
                  pl.BlockSpec((1, L, QK), lambda d, b, c: (b, chunk(d, c), 0)),
                  pl.BlockSpec((1, L, GROUP_W), lambda d, b, c: (b, chunk(d, c), v_blk)),
                  pl.BlockSpec((1, L, LANES), lambda d, b, c: (b, chunk(d, c), 0)),
                  pl.BlockSpec((1, 4 * H, L), lambda d, b, c: (b, 0, chunk(d, c))),
                  csp, nsp, msp],
        out_specs=[pl.BlockSpec((1, 1, L, GROUP_W), lambda d, b, c: (d, b, chunk(d, c), 0)),
                   csp, nsp, msp],
        out_shape=[jax.ShapeDtypeStruct((2, B, T, GROUP_W), F32),
                   jax.ShapeDtypeStruct(c0.shape, F32),
                   jax.ShapeDtypeStruct(n0.shape, F32),
                   jax.ShapeDtypeStruct(m0.shape, F32)],
        scratch_shapes=[pltpu.VMEM((H, MLSTM_DK, MLSTM_DV), F32),
                        pltpu.VMEM((H, 1, MLSTM_DK), F32),
                        pltpu.VMEM((H, 1, 1), F32)],
        compiler_params=_cp("parallel", "parallel", "arbitrary"),
        name="mlstm_scan",
    )(q, k, p, gates, gates_t, c0, n0, m0)


def _mlstm_readout_kernel(hf_ref, hb_ref, o_ref, nw_ref, out_ref):
    h = hf_ref[0, 0] + hb_ref[0, 0]
    outs = []
    for j in range(MLSTM_HEADS):
        hj = h[:, j * MLSTM_DV:(j + 1) * MLSTM_DV]
        outs.append(hj * lax.rsqrt(jnp.mean(hj * hj, axis=-1, keepdims=True) + EPS))
    hn = jnp.concatenate(outs, axis=1)
    out_ref[0] = (hn * nw_ref[...] * _sigmoid(o_ref[0])).astype(out_ref.dtype)


def mlstm_readout(h, p, norm_w):
    _, B, T, W = h.shape
    tm = 256
    o_blk = (2 * MLSTM_HEADS * MLSTM_DK + GROUP_W) // GROUP_W
    return pl.pallas_call(
        _mlstm_readout_kernel,
        grid=(B, T // tm),
        in_specs=[pl.BlockSpec((1, 1, tm, W), lambda b, i: (0, b, i, 0)),
                  pl.BlockSpec((1, 1, tm, W), lambda b, i: (1, b, i, 0)),
                  pl.BlockSpec((1, tm, W), lambda b, i: (b, i, o_blk)),
                  pl.BlockSpec((1, W), lambda b, i: (0, 0))],
        out_specs=pl.BlockSpec((1, tm, W), lambda b, i: (b, i, 0)),
        out_shape=jax.ShapeDtypeStruct((B, T, W), BF16),
        compiler_params=_cp("parallel", "parallel"),
        name="mlstm_readout",
    )(h, h, p, norm_w.reshape(1, W))


def _na_prep_kernel(p_ref, nw_ref, q_ref, k_ref, v_ref):
    x = p_ref[0]
    for j in range(NA_HEADS):
        sl = slice(j * NA_HEAD, (j + 1) * NA_HEAD)
        for src, dst, row, scale in ((0, q_ref, 0, NA_HEAD ** -0.5), (GROUP_W, k_ref, 1, 1.0)):
            z = x[:, src + j * NA_HEAD:src + (j + 1) * NA_HEAD]
            zn = z * lax.rsqrt(jnp.mean(z * z, axis=-1, keepdims=True) + EPS) * nw_ref[row:row + 1, :]
            dst[0, :, sl] = (zn * scale).astype(BF16)
    v_ref[0] = x[:, 2 * GROUP_W:].astype(BF16)


def na_prep(p, qk_norm):
    B, T, W = p.shape
    tm = 256
    tok = pl.BlockSpec((1, tm, GROUP_W), lambda b, i: (b, i, 0))
    sh = jax.ShapeDtypeStruct((B, T, GROUP_W), BF16)
    return pl.pallas_call(
        _na_prep_kernel,
        grid=(B, T // tm),
        in_specs=[pl.BlockSpec((1, tm, W), lambda b, i: (b, i, 0)),
                  pl.BlockSpec((2, NA_HEAD), lambda b, i: (0, 0))],
        out_specs=[tok, tok, tok],
        out_shape=[sh, sh, sh],
        compiler_params=_cp("parallel", "parallel"),
        name="na_prep",
    )(p, qk_norm)


def _softmax_rows(s):
    m = jnp.max(s, axis=1, keepdims=True)
    e = jnp.exp(s - m)
    return e / jnp.sum(e, axis=1, keepdims=True)


def _na_ctx_kernel(q_ref, k_ref, v_ref, o_ref):
    for j in range(NA_HEADS):
        sl = slice(j * NA_HEAD, (j + 1) * NA_HEAD)
        pr = _softmax_rows(_dot_nt(q_ref[0, :, sl], k_ref[0, :, sl]))
        o_ref[0, :, sl] = _dot(pr, v_ref[0, :, sl]).astype(o_ref.dtype)


def na_ctx_attention(q, k, v):
    B, T, W = q.shape
    tok = pl.BlockSpec((1, T, W), lambda b: (b, 0, 0))
    return pl.pallas_call(
        _na_ctx_kernel,
        grid=(B,),
        in_specs=[tok, tok, tok],
        out_specs=tok,
        out_shape=jax.ShapeDtypeStruct((B, T, W), BF16),
        compiler_params=_cp("parallel"),
        name="na_ctx_attention",
    )(q, k, v)


def _na_lat_kernel(q_ref, k0_ref, k1_ref, k2_ref, v0_ref, v1_ref, v2_ref, kc_ref, vc_ref, bias_ref, o_ref):
    for j in range(NA_HEADS):
        sl = slice(j * NA_HEAD, (j + 1) * NA_HEAD)
        q = q_ref[0, :, sl]
        kcat = jnp.concatenate([k0_ref[0, :, sl], k1_ref[0, :, sl], k2_ref[0, :, sl]], axis=0)
        vcat = jnp.concatenate([v0_ref[0, :, sl], v1_ref[0, :, sl], v2_ref[0, :, sl],
                                vc_ref[0, :, sl]], axis=0)
        s = jnp.concatenate([_dot_nt(q, kcat) + bias_ref[0, j], _dot_nt(q, kc_ref[0, :, sl])], axis=1)
        o_ref[0, :, sl] = _dot(_softmax_rows(s), vcat).astype(o_ref.dtype)


def na_bias_table(rpb, rows):
    nq, nk = NA_QROWS, NA_KROWS
    starts = np.array([0, nq, rows - nq])
    r = starts[:, None] + np.arange(nq)[None, :]
    kr = starts[:, None] - nq + np.arange(nk)[None, :]
    r0 = np.clip(r - WIN_H // 2, 0, rows - WIN_H)
    rv = (kr[:, None, :] >= r0[:, :, None]) & (kr[:, None, :] < r0[:, :, None] + WIN_H)
    dr = kr[:, None, :] - r[:, :, None] + WIN_H - 1
    col = np.arange(GRID_W)
    c0 = np.clip(col - WIN_W // 2, 0, GRID_W - WIN_W)
    cv = (col[None, :] >= c0[:, None]) & (col[None, :] < c0[:, None] + WIN_W)
    pad = GRID_W - WIN_W
    rpb_pad = jnp.pad(rpb, ((0, 0), (0, 0), (pad, pad)))
    colbias = jnp.stack([rpb_pad[:, :, GRID_W - 1 - q:2 * GRID_W - 1 - q] for q in range(GRID_W)], axis=2)
    colbias = jnp.where(cv[None, None], colbias, NEG)
    blank = jnp.full((NA_HEADS, GRID_W, GRID_W), NEG, F32)
    pats = []
    for p in range(3):
        qrows = [jnp.concatenate([colbias[:, int(dr[p, i, j])] if rv[p, i, j] else blank
                                  for j in range(nk)], axis=2) for i in range(nq)]
        pats.append(jnp.concatenate(qrows, axis=1))
    return jnp.stack(pats)


def na_lat_attention(q, k, v, kc, vc, bias):
    B, T, W = q.shape
    tq = NA_QROWS * GRID_W
    nb = T // tq
    ctx_len = kc.shape[1]
    qsp = pl.BlockSpec((1, tq, W), lambda b, i: (b, i, 0))
    prv = pl.BlockSpec((1, tq, W), lambda b, i: (b, jnp.maximum(i - 1, 0), 0))
    nxt = pl.BlockSpec((1, tq, W), lambda b, i: (b, jnp.minimum(i + 1, nb - 1), 0))
    csp = pl.BlockSpec((1, ctx_len, W), lambda b, i: (b, 0, 0))
    pattern = lambda i: jnp.where(i == 0, 0, jnp.where(i == nb - 1, 2, 1))
    return pl.pallas_call(
        _na_lat_kernel,
        grid=(B, nb),
        in_specs=[qsp, prv, qsp, nxt, prv, qsp, nxt, csp, csp,
                  pl.BlockSpec((1, NA_HEADS, tq, NA_KROWS * GRID_W), lambda b, i: (pattern(i), 0, 0, 0))],
        out_specs=qsp,
        out_shape=jax.ShapeDtypeStruct((B, T, W), BF16),
        compiler_params=_cp("parallel", "arbitrary"),
        name="na_lat_attention",
    )(q, k, k, k, v, v, v, kc, vc, bias)


def _ffn_norm_router_body(h_ref, g_ref, sh_ref, sc_ref, rw_ref, rb_ref, v_ref, ids_ref, wts_ref):
    x = h_ref[0]
    y = x * lax.rsqrt(jnp.mean(x * x, axis=-1, keepdims=True) + EPS) * g_ref[...]
    v = y * (1.0 + sc_ref[0]) + sh_ref[0]
    half = v.shape[1] // 2
    v_ref[...] = _pack_bf16_pair(v[:, :half], v[:, half:])
    vh, vl = _split2(v)
    logits = (_dot(vh, rw_ref[0]) + (_dot(vh, rw_ref[1]) + _dot(vl, rw_ref[0]))) + rb_ref[...]
    lane = lax.broadcasted_iota(jnp.int32, logits.shape, 1).astype(F32)
    first = lambda mask: jnp.min(jnp.where(mask, lane, float(LANES)), axis=1, keepdims=True)

    g_mask = lane < N_GROUPS
    gl = jnp.where(g_mask, logits, NEG)
    gmax = jnp.max(gl, axis=1, keepdims=True)
    g_p = 1.0 / jnp.sum(jnp.exp(gl - gmax), axis=1, keepdims=True)
    g_idx = first(g_mask & (gl == gmax))
    e_lane = lane - N_GROUPS
    e_mask = (e_lane >= 0) & (e_lane < N_EXPERTS) & (jnp.floor(e_lane * (1.0 / EXPERTS_PER_GROUP)) == g_idx)
    el = jnp.where(e_mask, logits, NEG)
    e1 = jnp.max(el, axis=1, keepdims=True)
    i1 = first(e_mask & (el == e1))
    el2 = jnp.where(lane == i1, NEG, el)
    e2 = jnp.max(el2, axis=1, keepdims=True)
    i2 = first(e_mask & (lane != i1) & (el2 == e2))
    x2 = jnp.exp(e2 - e1)
    w1 = g_p / (1.0 + x2)
    w2 = g_p * x2 / (1.0 + x2)
    ids_ref[...] = jnp.where(lane == 0, i1 - N_GROUPS, jnp.where(lane == 1, i2 - N_GROUPS, 0.0)).astype(jnp.int32)
    wts_ref[...] = jnp.where(lane == 0, w1, jnp.where(lane == 1, w2, 0.0))


def _ffn_norm_router_kernel(*refs, bounds):
    ns = len(bounds) - 1
    g_ref, rw_ref, rb_ref, v_ref, ids_ref, wts_ref = refs[3 * ns:]
    r = pl.program_id(0)
    for s in range(ns):
        h_ref, sh_ref, sc_ref = refs[3 * s:3 * s + 3]

        @pl.when((r >= bounds[s]) & (r < bounds[s + 1]))
        def _():
            _ffn_norm_router_body(h_ref, g_ref, sh_ref, sc_ref, rw_ref, rb_ref, v_ref, ids_ref, wts_ref)


def ffn_norm_router(streams, g, rw, rb):
    D = g.shape[0]
    tm = 256
    bounds = [0]
    in_specs, args = [], []
    for h, shift, scale in streams:
        B, T, _ = h.shape
        nt = T // tm
        lo, nblk = bounds[-1], B * nt
        bounds.append(lo + nblk)
        local = lambda r, lo=lo, nblk=nblk: jnp.clip(r - lo, 0, nblk - 1)
        in_specs += [pl.BlockSpec((1, tm, D), lambda r, f=local, nt=nt: (f(r) // nt, f(r) % nt, 0)),
                     pl.BlockSpec((1, 1, D), lambda r, f=local, nt=nt: (f(r) // nt, 0, 0)),
                     pl.BlockSpec((1, 1, D), lambda r, f=local, nt=nt: (f(r) // nt, 0, 0))]
        args += [h, shift, scale]
    n_rows = bounds[-1] * tm
    in_specs += [pl.BlockSpec((1, D), lambda r: (0, 0)),
                 pl.BlockSpec((2, D, LANES), lambda r: (0, 0, 0)),
                 pl.BlockSpec((1, LANES), lambda r: (0, 0))]
    args += [g.reshape(1, D), rw, rb]
    lan = pl.BlockSpec((tm, LANES), lambda r: (r, 0))
    return pl.pallas_call(
        functools.partial(_ffn_norm_router_kernel, bounds=tuple(bounds)),
        grid=(bounds[-1],),
        in_specs=in_specs,
        out_specs=[pl.BlockSpec((tm, D // 2), lambda r: (r, 0)), lan, lan],
        out_shape=[jax.ShapeDtypeStruct((n_rows, D // 2), jnp.uint32),
                   jax.ShapeDtypeStruct((n_rows, LANES), jnp.int32),
                   jax.ShapeDtypeStruct((n_rows, LANES), F32)],
        compiler_params=_cp("parallel"),
        name="ffn_norm_router",
    )(*args)


def _moe_kernel(be_ref, zero_ref, tok_ref, tokn_ref, dstp_ref, sw_ref, v_hbm, wg_hbm, wu_hbm, wd_hbm, out_hbm,
                x0, x1, y0, y1, wg_buf, wu_buf, wd_buf, stage_up, stage_down, gsem, ssem, wsem, *, layer):
    i = pl.program_id(0)
    nb = pl.num_programs(0) - 1
    xs, ys = (x0, x1), (y0, y1)
    zero = zero_ref[0]

    def gather_row(idx_ref, s, j, after=0):
        return pltpu.make_async_copy(v_hbm.at[pl.ds(idx_ref[0, 0, j] + after, 1)], xs[s].at[pl.ds(j, 1)],
                                     gsem.at[s])

    def wait_gather(s):
        pltpu.make_async_copy(v_hbm.at[pl.ds(0, MOE_BLOCK)], xs[s], gsem.at[s]).wait()

    def scatter_row(s, j, after=0):
        return pltpu.make_async_copy(ys[s].at[pl.ds(j, 1)], out_hbm.at[pl.ds(dstp_ref[0, 0, j] + after, 1)],
                                     ssem.at[s])

    def wait_scatter(s):
        pltpu.make_async_copy(ys[s], out_hbm.at[pl.ds(0, MOE_BLOCK)], ssem.at[s]).wait()

    def after(result):
        return result[0, 0].astype(jnp.int32) * zero

    @pl.when(i == 0)
    def _():
        def body(j, carry):
            gather_row(tok_ref, 0, j).start()
            return carry
        lax.fori_loop(0, MOE_BLOCK, body, 0)

    e = be_ref[jnp.minimum(i, nb - 1)]

    @pl.when((i == 0) | ((i < nb) & (e != be_ref[jnp.maximum(i - 1, 0)])))
    def _():
        chunks = []
        for src, dst, stage in ((wg_hbm, wg_buf, stage_up), (wu_hbm, wu_buf, stage_up),
                                (wd_hbm, wd_buf, stage_down)):
            rows = stage.shape[1]
            for r in range(dst.shape[0] // rows):
                k = len(chunks)
                cp = pltpu.make_async_copy(src.at[layer, e, pl.ds(r * rows, rows)], stage.at[k % 2],
                                           wsem.at[k % 2])
                chunks.append((cp, stage, dst, r * rows, rows))
        chunks[0][0].start()
        for k, (cp, stage, dst, r0, rows) in enumerate(chunks):
            if k + 1 < len(chunks):
                chunks[k + 1][0].start()
            cp.wait()
            dst[r0:r0 + rows, :] = stage[k % 2].astype(BF16)

    def step(s, scatter_prev):
        wait_gather(s)
        x_lo, x_hi = _unpack_bf16_pair(xs[s][...])
        x = jnp.concatenate([x_lo.astype(BF16), x_hi.astype(BF16)], axis=1)
        n_up = EXPERT_FF // MOE_FF_TILE
        per = MOE_BLOCK // (2 * n_up)
        acts = []
        for c in range(n_up):
            cols = slice(c * MOE_FF_TILE, (c + 1) * MOE_FF_TILE)
            hg = jnp.dot(x, wg_buf[:, cols], preferred_element_type=F32)
            hu = jnp.dot(x, wu_buf[:, cols], preferred_element_type=F32)
            for g, res in enumerate((hg, hu)):
                dep = after(res)
                for j in range((2 * c + g) * per, (2 * c + g + 1) * per):
                    gather_row(tokn_ref, 1 - s, j, dep).start()
            acts.append((_silu(hg) * hu).astype(BF16))
        act = jnp.concatenate(acts, axis=1)

        @pl.when(i >= 2)
        def _():
            wait_scatter(s)

        half = x.shape[1] // 2
        n_down = half // MOE_OUT_TILE
        per = MOE_BLOCK // n_down
        for c in range(n_down):
            cols = slice(c * MOE_OUT_TILE, (c + 1) * MOE_OUT_TILE)
            hcols = slice(half + c * MOE_OUT_TILE, half + (c + 1) * MOE_OUT_TILE)
            y_lo = jnp.dot(act, wd_buf[:, cols], preferred_element_type=F32) * sw_ref[...]
            y_hi = jnp.dot(act, wd_buf[:, hcols], preferred_element_type=F32) * sw_ref[...]
            ys[s][:, cols] = _pack_bf16_pair(y_lo, y_hi)
            if scatter_prev:
                dep = after(y_hi)
                for j in range(c * per, (c + 1) * per):
                    scatter_row(1 - s, j, dep).start()

    pl.when(i == 0)(functools.partial(step, 0, False))
    for s in range(2):
        pl.when((i > 0) & (i < nb) & (i % 2 == s))(functools.partial(step, s, True))

    @pl.when(i == nb)
    def _():
        def drain(s):
            wait_gather(s)
            wait_scatter(s)

            def body(j, carry):
                scatter_row(1 - s, j).start()
                return carry
            lax.fori_loop(0, MOE_BLOCK, body, 0)
            wait_scatter(1 - s)
        for s in range(2):
            pl.when(i % 2 == s)(functools.partial(drain, s))


def moe_experts(v_rows, block_expert, slot_tok, slot_dst, slot_w, wg, wu, wd, layer):
    P = slot_tok.shape[0]
    W = v_rows.shape[1]
    D = 2 * W
    nb = P // MOE_BLOCK
    assert nb >= 2
    idx = lambda a: a.reshape(nb, 1, MOE_BLOCK)
    smem = lambda imap: pl.BlockSpec((1, 1, MOE_BLOCK), imap, memory_space=pltpu.SMEM)
    hbm = pl.BlockSpec(memory_space=pl.ANY)
    last = nb - 1
    grid_spec = pltpu.PrefetchScalarGridSpec(
        num_scalar_prefetch=2,
        grid=(nb + 1,),
        in_specs=[smem(lambda i, be, z: (jnp.minimum(i, last), 0, 0)),
                  smem(lambda i, be, z: (jnp.minimum(i + 1, last), 0, 0)),
                  smem(lambda i, be, z: (jnp.maximum(i - 1, 0), 0, 0)),
                  pl.BlockSpec((MOE_BLOCK, 1), lambda i, be, z: (jnp.minimum(i, last), 0)),
                  hbm, hbm, hbm, hbm],
        out_specs=hbm,
        scratch_shapes=[pltpu.VMEM((MOE_BLOCK, W), jnp.uint32),
                        pltpu.VMEM((MOE_BLOCK, W), jnp.uint32),
                        pltpu.VMEM((MOE_BLOCK, W), jnp.uint32),
                        pltpu.VMEM((MOE_BLOCK, W), jnp.uint32),
                        pltpu.VMEM((D, EXPERT_FF), BF16),
                        pltpu.VMEM((D, EXPERT_FF), BF16),
                        pltpu.VMEM((EXPERT_FF, D), BF16),
                        pltpu.VMEM((2, MOE_STAGE_BYTES // (4 * EXPERT_FF), EXPERT_FF), F32),
                        pltpu.VMEM((2, MOE_STAGE_BYTES // (4 * D), D), F32),
                        pltpu.SemaphoreType.DMA((2,)),
                        pltpu.SemaphoreType.DMA((2,)),
                        pltpu.SemaphoreType.DMA((2,))],
    )
    return pl.pallas_call(
        functools.partial(_moe_kernel, layer=layer),
        grid_spec=grid_spec,
        out_shape=jax.ShapeDtypeStruct((P, W), jnp.uint32),
        compiler_params=_cp("arbitrary"),
        name="moe_experts",
    )(block_expert, jnp.zeros((1,), jnp.int32), idx(slot_tok), idx(slot_tok), idx(slot_dst),
      slot_w.reshape(P, 1), v_rows, wg, wu, wd)


def _moe_combine_kernel(h_ref, f0_ref, f1_ref, gate_ref, o_ref):
    a_lo, a_hi = _unpack_bf16_pair(f0_ref[...])
    b_lo, b_hi = _unpack_bf16_pair(f1_ref[...])
    half = a_lo.shape[1]
    o_ref[0, :, :half] = h_ref[0, :, :half] + gate_ref[0, :, :half] * (a_lo + b_lo)
    o_ref[0, :, half:] = h_ref[0, :, half:] + gate_ref[0, :, half:] * (a_hi + b_hi)


def moe_combine(h, out2, gate, row_offset, n_rows):
    B, T, D = h.shape
    tm = 256
    nt = T // tm
    blk0 = row_offset // tm
    tok = pl.BlockSpec((1, tm, D), lambda b, i: (b, i, 0))
    return pl.pallas_call(
        _moe_combine_kernel,
        grid=(B, nt),
        in_specs=[tok,
                  pl.BlockSpec((tm, D // 2), lambda b, i: (blk0 + b * nt + i, 0)),
                  pl.BlockSpec((tm, D // 2), lambda b, i: (n_rows // tm + blk0 + b * nt + i, 0)),
                  pl.BlockSpec((1, 1, D), lambda b, i: (b, 0, 0))],
        out_specs=tok,
        out_shape=jax.ShapeDtypeStruct((B, T, D), F32),
        compiler_params=_cp("parallel", "parallel"),
        name="moe_combine",
    )(h, out2, out2, gate)


def moe_dispatch(ids, wts):
    N = ids.shape[0]
    A = N * TOP_K
    e_flat = ids.reshape(A)
    order = jnp.argsort(e_flat).astype(jnp.int32)
    counts = jnp.sum((e_flat[:, None] == jnp.arange(N_EXPERTS)[None, :]).astype(jnp.int32), axis=0)
    padded = (counts + MOE_BLOCK - 1) // MOE_BLOCK * MOE_BLOCK
    ends = jnp.cumsum(padded)
    n_blocks = -(-(A + N_EXPERTS * (MOE_BLOCK - 1)) // MOE_BLOCK)
    block_start = jnp.arange(n_blocks, dtype=jnp.int32) * MOE_BLOCK
    block_expert = jnp.minimum(jnp.searchsorted(ends, block_start, side="right"),
                               N_EXPERTS - 1).astype(jnp.int32)
    into = block_start - (ends - padded)[block_expert]
    b_count = counts[block_expert]
    local = (into[:, None] + jnp.arange(MOE_BLOCK, dtype=jnp.int32)[None, :])
    real = local < b_count[:, None]
    src = jnp.clip((jnp.cumsum(counts) - counts)[block_expert][:, None] + local, 0, A - 1)
    a = order[src.reshape(-1)]
    real = real.reshape(-1)
    tok = a // TOP_K
    slot_tok = jnp.where(real, tok, 0).astype(jnp.int32)
    spare = A + jnp.cumsum(jnp.where(real, 0, 1)) - 1
    slot_dst = jnp.where(real, (a % TOP_K) * N + tok, spare).astype(jnp.int32)
    slot_w = jnp.where(real, wts.reshape(A)[a], 0.0)
    return slot_tok, slot_dst, slot_w, block_expert


def hier_moe(v_rows, ids, wts, wg, wu, wd, layer):
    slot_tok, slot_dst, slot_w, block_expert = moe_dispatch(ids, wts)
    return moe_experts(v_rows, block_expert, slot_tok, slot_dst, slot_w, wg, wu, wd, layer)


def _layer_weights(l, P):
    w = {}
    w_in = P["w_in"][l]
    o = 0
    a = w_in[:, o:o + RWKV_COLS]; o += RWKV_COLS
    b = w_in[:, o:o + GROUP_W]; o += GROUP_W
    c = w_in[:, o:o + MLSTM_COLS]; o += MLSTM_COLS
    d = w_in[:, o:o + NA_COLS]
    D = w_in.shape[0]
    pad_a = RWKV_PAD_COLS - RWKV_COLS
    w["in_a"] = jnp.concatenate([a, jnp.zeros((D, pad_a), F32)], axis=1).astype(BF16)
    w["in_b"] = b.astype(BF16)
    w["in_c"] = jnp.concatenate([c, jnp.zeros((D, MLSTM_PAD_COLS - MLSTM_COLS), F32)], axis=1).astype(BF16)
    w["in_d"] = d.astype(BF16)
    w["out"] = P["w_out"][l].astype(BF16)

    z = jnp.zeros((RWKV_DECAY_RANK, GROUP_W), F32)
    wup, aup = P["rwkv_w_up"][l], P["rwkv_a_up"][l]
    rw = {
        "mu": jnp.concatenate([P["rwkv_mu"][l], jnp.zeros((2, pad_a), F32)], axis=1),
        "w_up": jnp.concatenate([jnp.concatenate([wup[0], z], axis=1),
                                 jnp.concatenate([z, wup[1]], axis=1)], axis=0).astype(BF16),
        "a_up": jnp.concatenate([jnp.concatenate([aup[0], z], axis=1),
                                 jnp.concatenate([z, aup[1]], axis=1)], axis=0).astype(BF16),
        "w0": P["rwkv_w0"][l].reshape(1, 2 * GROUP_W),
        "a0": P["rwkv_a0"][l].reshape(1, 2 * GROUP_W),
        "g_up": jnp.concatenate([P["rwkv_g_up"][l],
                                 jnp.zeros((256 - RWKV_GATE_RANK, GROUP_W), F32)], axis=0).astype(BF16),
        "k_k": P["rwkv_k_k"][l].reshape(1, GROUP_W),
        "k_a": P["rwkv_k_a"][l].reshape(1, GROUP_W),
        "r_k": P["rwkv_r_k"][l].reshape(1, GROUP_W),
    }
    w["rwkv"] = rw
    w["rwkv_ln"] = P["rwkv_ln"][l]
    w["pool_w"] = P["pool_w"][l].astype(BF16)
    w["pool_scale"] = P["pool_scale"][l]
    w["mlstm_conv"] = P["mlstm_conv"][l]
    gb = P["mlstm_gate_b"][l].reshape(1, 4 * MLSTM_HEADS)
    w["mlstm_gate_b"] = jnp.concatenate([gb, jnp.zeros((1, LANES - 4 * MLSTM_HEADS), F32)], axis=1)
    w["mlstm_norm"] = P["mlstm_norm"][l]
    w["na_qk_norm"] = P["na_qk_norm"][l]
    w["na_rpb"] = P["na_rpb"][l]
    rcat = jnp.concatenate([P["router_g_w"][l], P["router_e_w"][l],
                            jnp.zeros((D, LANES - N_GROUPS - N_EXPERTS), F32)], axis=1)
    rhi = rcat.astype(BF16)
    w["router_w"] = jnp.stack([rhi, (rcat - rhi.astype(F32)).astype(BF16)])
    w["router_b"] = jnp.concatenate([P["router_g_b"][l], P["router_e_b"][l],
                                     jnp.zeros((LANES - N_GROUPS - N_EXPERTS,), F32)]).reshape(1, LANES)
    return w


def rope_tables(T):
    t = jnp.arange(T)
    row = (t // GRID_W).astype(F32)
    col = (t % GRID_W).astype(F32)
    n_pairs = MLSTM_DK // 4
    inv = ROPE_BASE ** (-jnp.arange(n_pairs, dtype=F32) / n_pairs)
    ang = jnp.concatenate([row[:, None] * inv, col[:, None] * inv], axis=-1)
    cos = jnp.repeat(jnp.cos(ang), 2, axis=1)
    sin = jnp.repeat(jnp.sin(ang), 2, axis=1) * jnp.tile(jnp.array([-1.0, 1.0], F32), MLSTM_DK // 2)
    return cos, sin


def rwkv_mixer(p_ctx, p_lat, w, need_ctx):
    B = p_lat.shape[0]
    names = ("r", "kk", "v", "bonus", "g", "lw", "kd", "bb")
    fc = dict(zip(names, rwkv_prep(p_ctx, w["rwkv"])))
    fl = dict(zip(names, rwkv_prep(p_lat, w["rwkv"])))
    s0 = jnp.zeros((2, B, RWKV_PAIRS, LANES, LANES), F32)
    y_c, s_ctx = rwkv_scan(fc, s0)
    y_l, _ = rwkv_scan(fl, s_ctx)
    out_l = rwkv_readout(y_l, fl["bonus"], fl["g"], w["rwkv_ln"])
    out_c = rwkv_readout(y_c, fc["bonus"], fc["g"], w["rwkv_ln"]) if need_ctx else None
    return out_c, out_l


def mlstm_mixer(p_ctx, p_lat, w, rope, need_ctx):
    B = p_lat.shape[0]
    H = MLSTM_HEADS
    state = (jnp.zeros((2, B, H, MLSTM_DK, MLSTM_DV), F32),
             jnp.zeros((2, B, H, 1, MLSTM_DK), F32),
             jnp.zeros((2, B, H, 1, 1), F32))
    outs = []
    for p, tabs in ((p_ctx, None), (p_lat, rope)):
        q, k, gates = mlstm_prep(p, w["mlstm_conv"], w["mlstm_gate_b"], tabs)
        gates_t = jnp.swapaxes(gates[:, :, :4 * H], 1, 2)
        h, *state = mlstm_scan(q, k, p, gates, gates_t, tuple(state))
        outs.append(h)
    out_l = mlstm_readout(outs[1], p_lat, w["mlstm_norm"])
    out_c = mlstm_readout(outs[0], p_ctx, w["mlstm_norm"]) if need_ctx else None
    return out_c, out_l


def na_mixer(p_ctx, p_lat, w, need_ctx):
    qc, kc, vc = na_prep(p_ctx, w["na_qk_norm"])
    ql, kl, vl = na_prep(p_lat, w["na_qk_norm"])
    rows = p_lat.shape[1] // GRID_W
    bias = na_bias_table(w["na_rpb"], rows)
    out_l = na_lat_attention(ql, kl, vl, kc, vc, bias)
    out_c = na_ctx_attention(qc, kc, vc) if need_ctx else None
    return out_c, out_l


def token_mixers(u_ctx, u_lat, w, rope, need_ctx):
    outs_c, outs_l = [], []
    pa = (matmul(u_ctx, w["in_a"]), matmul(u_lat, w["in_a"]))
    oc, ol = rwkv_mixer(pa[0], pa[1], w, need_ctx)
    outs_c.append(oc); outs_l.append(ol)
    pb = (matmul(u_ctx, w["in_b"]), matmul(u_lat, w["in_b"]))
    outs_l.append(pool_mixer(pb[1], w["pool_w"], w["pool_scale"]))
    outs_c.append(pool_mixer(pb[0], w["pool_w"], w["pool_scale"]) if need_ctx else None)
    pc = (matmul(u_ctx, w["in_c"]), matmul(u_lat, w["in_c"]))
    oc, ol = mlstm_mixer(pc[0], pc[1], w, rope, need_ctx)
    outs_c.append(oc); outs_l.append(ol)
    pd = (matmul(u_ctx, w["in_d"]), matmul(u_lat, w["in_d"]))
    oc, ol = na_mixer(pd[0], pd[1], w, need_ctx)
    outs_c.append(oc); outs_l.append(ol)
    return outs_c, outs_l


def kernel(x, c, ctx, c_ctx, ada_w, ada_b, norm_mix, norm_ffn, w_in, w_out, rwkv_mu, rwkv_w0, rwkv_w_up, rwkv_a0, rwkv_a_up, rwkv_g_up, rwkv_k_k, rwkv_k_a, rwkv_r_k, rwkv_ln, pool_w, pool_scale, mlstm_conv, mlstm_gate_b, mlstm_norm, na_qk_norm, na_rpb, router_g_w, router_g_b, router_e_w, router_e_b, moe_w_gate, moe_w_up, moe_w_down):
    P = dict(w_in=w_in, w_out=w_out, rwkv_mu=rwkv_mu, rwkv_w0=rwkv_w0, rwkv_w_up=rwkv_w_up,
             rwkv_a0=rwkv_a0, rwkv_a_up=rwkv_a_up, rwkv_g_up=rwkv_g_up, rwkv_k_k=rwkv_k_k,
             rwkv_k_a=rwkv_k_a, rwkv_r_k=rwkv_r_k, rwkv_ln=rwkv_ln, pool_w=pool_w, pool_scale=pool_scale,
             mlstm_conv=mlstm_conv, mlstm_gate_b=mlstm_gate_b, mlstm_norm=mlstm_norm,
             na_qk_norm=na_qk_norm, na_rpb=na_rpb, router_g_w=router_g_w, router_g_b=router_g_b,
             router_e_w=router_e_w, router_e_b=router_e_b, moe_w_gate=moe_w_gate, moe_w_up=moe_w_up,
             moe_w_down=moe_w_down)
    B, T, D = x.shape
    Lc = ctx.shape[1]
    depth = ada_w.shape[0]
    rope = rope_tables(T)
    moe_w = (moe_w_gate, moe_w_up, moe_w_down)

    cvec = jnp.concatenate([c, c_ctx[None, :], jnp.zeros((SUBLANES - B - 1, D), F32)], axis=0)
    mods = ada_modulation(cvec, ada_w, ada_b)

    h_lat, h_ctx = x, ctx
    for l in range(depth):
        last = l == depth - 1
        w = _layer_weights(l, P)
        m = mods[l].reshape(SUBLANES, 6, D)
        m_lat = [m[:B, i][:, None, :] for i in range(6)]
        m_ctx = [jnp.broadcast_to(m[B, i][None, None, :], (B, 1, D)) for i in range(6)]

        u_lat = norm_modulate(h_lat, norm_mix[l], m_lat[0], m_lat[1])
        u_ctx = norm_modulate(h_ctx, norm_mix[l], m_ctx[0], m_ctx[1])
        ys_ctx, ys_lat = token_mixers(u_ctx, u_lat, w, rope, not last)
        h_lat = out_proj_residual(ys_lat, w["out"], h_lat, m_lat[2])
        n_rows = B * T if last else B * (T + Lc)
        streams = [(h_lat, m_lat[3], m_lat[4])]
        if not last:
            h_ctx = out_proj_residual(ys_ctx, w["out"], h_ctx, m_ctx[2])
            streams.append((h_ctx, m_ctx[3], m_ctx[4]))
        rows, ids, wts = ffn_norm_router(streams, norm_ffn[l], w["router_w"], w["router_b"])
        f2 = hier_moe(rows, ids[:, :TOP_K], wts[:, :TOP_K], *moe_w, l)
        if not last:
            h_ctx = moe_combine(h_ctx, f2, m_ctx[5], B * T, n_rows)
        h_lat = moe_combine(h_lat, f2, m_lat[5], 0, n_rows)
    return h_lat
```

```python
import functools
import math

import numpy as np
import jax
import jax.numpy as jnp
from jax import lax
from jax.experimental import pallas as pl
from jax.experimental.pallas import tpu as pltpu

F32 = jnp.float32
BF16 = jnp.bfloat16

D_MODEL = 4096
DEPTH = 2
GRID_W = 64
EPS = 1e-6
GROUP_W = 1024

RWKV_HEAD = 64
RWKV_DECAY_RANK = 64
RWKV_ICLR_RANK = 64
RWKV_GATE_RANK = 160
RWKV_GN_EPS = 64e-5
RWKV_COLS = 3 * GROUP_W + 2 * RWKV_DECAY_RANK + 2 * RWKV_ICLR_RANK + RWKV_GATE_RANK
RWKV_PAD_COLS = 3 * GROUP_W + 256 + 256

POOL_WINDOWS = (2, 4, 8, 16)
POOL_GROUP = 256
POOL_HALO = 8

MLSTM_HEADS = 4
MLSTM_DV = 256
MLSTM_DK = 128
MLSTM_COLS = 2 * MLSTM_HEADS * MLSTM_DK + 2 * GROUP_W + 4 * MLSTM_HEADS
MLSTM_PAD_COLS = 2 * MLSTM_HEADS * MLSTM_DK + 2 * GROUP_W + 128
ROPE_BASE = 10000.0

NA_HEADS = 8
NA_HEAD = 128
WIN_H = 8
WIN_W = 16
NA_COLS = 3 * GROUP_W
NA_QROWS = 4
NA_KROWS = 3 * NA_QROWS

N_GROUPS = 4
EXPERTS_PER_GROUP = 4
N_EXPERTS = 16
TOP_K = 2
EXPERT_FF = 1024
MOE_BLOCK = 256
MOE_FF_TILE = 256
MOE_OUT_TILE = 1024
MOE_STAGE_BYTES = 2 * 1024 * 1024

LANES = 128
SUBLANES = 8
VMEM_LIMIT = 56 * 1024 * 1024
NEG = -1e30

RWKV_CHUNK = 64
RWKV_PAIRS = 8
MLSTM_CHUNK = 256


def _cp(*sem):
    return pltpu.CompilerParams(dimension_semantics=sem, vmem_limit_bytes=VMEM_LIMIT)


def _dot(a, b):
    return jnp.dot(a.astype(BF16), b.astype(BF16), preferred_element_type=F32)


def _dot_nt(a, b):
    return lax.dot_general(a.astype(BF16), b.astype(BF16), (((1,), (1,)), ((), ())),
                           preferred_element_type=F32)


def _dot_tn(a, b):
    return lax.dot_general(a.astype(BF16), b.astype(BF16), (((0,), (0,)), ((), ())),
                           preferred_element_type=F32)


def _split2(x):
    hi = x.astype(BF16)
    lo = (x - hi.astype(F32)).astype(BF16)
    return hi, lo


def _dot3(a, b, dot=_dot):
    ah, al = _split2(a)
    bh, bl = _split2(b)
    return dot(ah, bh) + (dot(ah, bl) + dot(al, bh))


def _dot_exact_lhs(a_bf16, b):
    b0 = b.astype(BF16)
    r1 = b - b0.astype(F32)
    b1 = r1.astype(BF16)
    b2 = (r1 - b1.astype(F32)).astype(BF16)
    return _dot(a_bf16, b0) + (_dot(a_bf16, b1) + _dot(a_bf16, b2))


def _sigmoid(x):
    return 1.0 / (1.0 + jnp.exp(-x))


def _silu(x):
    return x * _sigmoid(x)


def _log_sigmoid(x):
    return jnp.minimum(x, 0.0) - jnp.log(1.0 + jnp.exp(-jnp.abs(x)))


def _pack_bf16_pair(lo, hi):
    lo_bits = lax.bitcast_convert_type(lo.astype(BF16).astype(F32), jnp.uint32)
    hi_bits = lax.bitcast_convert_type(hi.astype(BF16).astype(F32), jnp.uint32)
    return (lo_bits >> 16) | hi_bits


def _unpack_bf16_pair(w):
    lo = lax.bitcast_convert_type(w << 16, F32)
    hi = lax.bitcast_convert_type(w & jnp.uint32(0xFFFF0000), F32)
    return lo, hi


def _head_sum(x, width):
    r = lax.broadcasted_iota(jnp.int32, (LANES, LANES), 0) // width
    c = lax.broadcasted_iota(jnp.int32, (LANES, LANES), 1) // width
    e = jnp.where(r == c, 1.0, 0.0).astype(BF16)
    outs = []
    for j in range(x.shape[1] // LANES):
        hi, lo = _split2(x[:, j * LANES:(j + 1) * LANES])
        outs.append(_dot(hi, e) + _dot(lo, e))
    return jnp.concatenate(outs, axis=1)


def _ada_kernel(c_ref, w_ref, b_ref, o_ref):
    s = _silu(c_ref[...])
    o_ref[0] = _dot(s, w_ref[0]) + b_ref[0]


def ada_modulation(cvec, ada_w, ada_b):
    L, D, N = ada_w.shape
    tn = 1024
    return pl.pallas_call(
        _ada_kernel,
        grid=(L, N // tn),
        in_specs=[pl.BlockSpec((SUBLANES, D), lambda l, j: (0, 0)),
                  pl.BlockSpec((1, D, tn), lambda l, j: (l, 0, j)),
                  pl.BlockSpec((1, 1, tn), lambda l, j: (l, 0, j))],
        out_specs=pl.BlockSpec((1, SUBLANES, tn), lambda l, j: (l, 0, j)),
        out_shape=jax.ShapeDtypeStruct((L, SUBLANES, N), F32),
        compiler_params=_cp("parallel", "parallel"),
        name="ada_modulation",
    )(cvec, ada_w, ada_b.reshape(L, 1, N))


def _normmod_kernel(h_ref, g_ref, sh_ref, sc_ref, o_ref):
    x = h_ref[0]
    y = x * lax.rsqrt(jnp.mean(x * x, axis=-1, keepdims=True) + EPS) * g_ref[...]
    o_ref[0] = (y * (1.0 + sc_ref[0]) + sh_ref[0]).astype(o_ref.dtype)


def norm_modulate(h, g, shift, scale):
    B, T, D = h.shape
    tm = 256
    vec = pl.BlockSpec((1, 1, D), lambda b, i: (b, 0, 0))
    return pl.pallas_call(
        _normmod_kernel,
        grid=(B, T // tm),
        in_specs=[pl.BlockSpec((1, tm, D), lambda b, i: (b, i, 0)),
                  pl.BlockSpec((1, D), lambda b, i: (0, 0)), vec, vec],
        out_specs=pl.BlockSpec((1, tm, D), lambda b, i: (b, i, 0)),
        out_shape=jax.ShapeDtypeStruct((B, T, D), BF16),
        compiler_params=_cp("parallel", "parallel"),
        name="norm_modulate",
    )(h, g.reshape(1, D), shift, scale)


def _mm_kernel(x_ref, w_ref, o_ref):
    o_ref[0] = jnp.dot(x_ref[0], w_ref[...], preferred_element_type=F32).astype(o_ref.dtype)


def _pick_tile(n, cands):
    for t in cands:
        if n % t == 0:
            return t
    raise ValueError(n)


def matmul(x, w, out_dtype=F32):
    B, T, K = x.shape
    N = w.shape[1]
    tm = _pick_tile(T, (1024, 512, 256))
    tn = _pick_tile(N, (1024, 896, 768, 640, 512))
    return pl.pallas_call(
        _mm_kernel,
        grid=(B, T // tm, N // tn),
        in_specs=[pl.BlockSpec((1, tm, K), lambda b, i, j: (b, i, 0)),
                  pl.BlockSpec((K, tn), lambda b, i, j: (0, j))],
        out_specs=pl.BlockSpec((1, tm, tn), lambda b, i, j: (b, i, j)),
        out_shape=jax.ShapeDtypeStruct((B, T, N), out_dtype),
        compiler_params=_cp("parallel", "parallel", "arbitrary"),
        name="in_proj",
    )(x, w)


def _outproj_kernel(ya_ref, yb_ref, yc_ref, yd_ref, w_ref, h_ref, gate_ref, o_ref):
    acc = jnp.dot(ya_ref[0], w_ref[0:GROUP_W, :], preferred_element_type=F32)
    acc += jnp.dot(yb_ref[0], w_ref[GROUP_W:2 * GROUP_W, :], preferred_element_type=F32)
    acc += jnp.dot(yc_ref[0], w_ref[2 * GROUP_W:3 * GROUP_W, :], preferred_element_type=F32)
    acc += jnp.dot(yd_ref[0], w_ref[3 * GROUP_W:4 * GROUP_W, :], preferred_element_type=F32)
    o_ref[0] = h_ref[0] + gate_ref[0] * acc


def out_proj_residual(ys, w, h, gate):
    B, T, D = h.shape
    tm = _pick_tile(T, (512, 256))
    tn = 1024
    ysp = pl.BlockSpec((1, tm, GROUP_W), lambda b, i, j: (b, i, 0))
    return pl.pallas_call(
        _outproj_kernel,
        grid=(B, T // tm, D // tn),
        in_specs=[ysp, ysp, ysp, ysp,
                  pl.BlockSpec((D, tn), lambda b, i, j: (0, j)),
                  pl.BlockSpec((1, tm, tn), lambda b, i, j: (b, i, j)),
                  pl.BlockSpec((1, 1, tn), lambda b, i, j: (b, 0, j))],
        out_specs=pl.BlockSpec((1, tm, tn), lambda b, i, j: (b, i, j)),
        out_shape=jax.ShapeDtypeStruct((B, T, D), F32),
        compiler_params=_cp("parallel", "parallel", "arbitrary"),
        name="out_proj",
    )(*ys, w, h, gate)


def _halo_specs(tm, width, halo=SUBLANES):
    per = tm // halo

    def prev_map(b, i):
        return (b, jnp.maximum(i * per - 1, 0), 0)

    def next_map(nb):
        return lambda b, i: (b, jnp.minimum((i + 1) * per, nb - 1), 0)

    return prev_map, next_map


def _shift_rows(x, prev_row, next_row):
    tm = x.shape[0]
    row = lax.broadcasted_iota(jnp.int32, x.shape, 0)
    prev = jnp.where(row == 0, prev_row, pltpu.roll(x, 1, axis=0))
    nxt = jnp.where(row == tm - 1, next_row, pltpu.roll(x, tm - 1, axis=0))
    return prev, nxt


def _rwkv_prep_kernel(p_ref, pp_ref, pn_ref, mu_ref, wup_ref, w0_ref, aup_ref, a0_ref, gup_ref,
                      kk_w_ref, ka_w_ref, rk_w_ref,
                      r_ref, kk_ref, v_ref, bonus_ref, g_ref, lw_ref, kd_ref, bb_ref):
    i = pl.program_id(1)
    last = pl.num_programs(1) - 1
    x = p_ref[0]
    prev_row = jnp.where(i == 0, 0.0, pp_ref[0, SUBLANES - 1:SUBLANES, :])
    next_row = jnp.where(i == last, 0.0, pn_ref[0, 0:1, :])
    prev, nxt = _shift_rows(x, prev_row, next_row)
    x = x + mu_ref[0:1, :] * (prev - x) + mu_ref[1:2, :] * (nxt - x)

    r = x[:, 0:GROUP_W]
    k = x[:, GROUP_W:2 * GROUP_W]
    v = x[:, 2 * GROUP_W:3 * GROUP_W]
    low = x[:, 3 * GROUP_W:3 * GROUP_W + 256]
    gd = x[:, 3 * GROUP_W + 256:3 * GROUP_W + 512]

    zw = _dot(jnp.tanh(low[:, 0:LANES]), wup_ref[...]) + w0_ref[...]
    za = _dot(low[:, LANES:2 * LANES], aup_ref[...]) + a0_ref[...]
    kkr = k * kk_w_ref[...]
    kk = kkr * lax.rsqrt(_head_sum(kkr * kkr, RWKV_HEAD) + EPS)
    r_ref[0] = r.astype(BF16)
    kk_ref[0] = kk.astype(BF16)
    v_ref[0] = v.astype(BF16)
    bonus_ref[0] = (_head_sum(r * k * rk_w_ref[...], RWKV_HEAD) * v).astype(BF16)
    g_ref[0] = _dot(_sigmoid(gd), gup_ref[...]).astype(BF16)
    for d in range(2):
        sl = slice(d * GROUP_W, (d + 1) * GROUP_W)
        lw_ref[d, 0] = (-math.exp(-0.5)) * _sigmoid(zw[:, sl])
        a = _sigmoid(za[:, sl])
        kd_ref[d, 0] = (k * (1.0 + (a - 1.0) * ka_w_ref[...])).astype(BF16)
        bb_ref[d, 0] = (kk * a).astype(BF16)


def rwkv_prep(p, w):
    B, T, W = p.shape
    tm = 128
    prev_map, next_map = _halo_specs(tm, W)
    nb8 = T // SUBLANES
    full = lambda shape: pl.BlockSpec(shape, lambda b, i: tuple(0 for _ in shape))
    tok = pl.BlockSpec((1, tm, GROUP_W), lambda b, i: (b, i, 0))
    tokd = pl.BlockSpec((2, 1, tm, GROUP_W), lambda b, i: (0, b, i, 0))
    sh = jax.ShapeDtypeStruct((B, T, GROUP_W), BF16)
    shd = jax.ShapeDtypeStruct((2, B, T, GROUP_W), BF16)
    shd_f32 = jax.ShapeDtypeStruct((2, B, T, GROUP_W), F32)
    return pl.pallas_call(
        _rwkv_prep_kernel,
        grid=(B, T // tm),
        in_specs=[pl.BlockSpec((1, tm, W), lambda b, i: (b, i, 0)),
                  pl.BlockSpec((1, SUBLANES, W), prev_map),
                  pl.BlockSpec((1, SUBLANES, W), next_map(nb8)),
                  full((2, W)), full((LANES, 2 * GROUP_W)), full((1, 2 * GROUP_W)),
                  full((LANES, 2 * GROUP_W)), full((1, 2 * GROUP_W)), full((256, GROUP_W)),
                  full((1, GROUP_W)), full((1, GROUP_W)), full((1, GROUP_W))],
        out_specs=[tok, tok, tok, tok, tok, tokd, tokd, tokd],
        out_shape=[sh, sh, sh, sh, sh, shd_f32, shd, shd],
        compiler_params=_cp("parallel", "parallel"),
        name="rwkv_prep",
    )(p, p, p, w["mu"], w["w_up"], w["w0"], w["a_up"], w["a0"], w["g_up"],
      w["k_k"], w["k_a"], w["r_k"])


def _dform(x, m0):
    return jnp.concatenate([jnp.where(m0, x, 0.0), jnp.where(m0, 0.0, x)], axis=0)


def _rwkv_chunk(tiles, sts, rev):
    n = len(tiles)
    C = tiles[0][0].shape[0]
    C2 = 2 * C
    sgn = jnp.where(rev, -1, 1)
    rr = lax.broadcasted_iota(jnp.int32, (C, C), 0)
    cc = lax.broadcasted_iota(jnp.int32, (C, C), 1)
    tri = jnp.where((rr - cc) * sgn >= 0, 1.0, 0.0).astype(BF16)
    lane = lax.broadcasted_iota(jnp.int32, (C, LANES), 1)
    m0 = lane < RWKV_HEAD
    rd = lax.broadcasted_iota(jnp.int32, (C2, C2), 0)
    cd = lax.broadcasted_iota(jnp.int32, (C2, C2), 1)
    same = (rd // C) == (cd // C)
    ahead = (rd % C - cd % C) * sgn
    strict = same & (ahead > 0)
    incl = same & (ahead >= 0)
    diag = rd == cd
    each = range(n)

    lins = [_dot_exact_lhs(tri, t[3]) for t in tiles]
    ltots = [jnp.sum(t[3], axis=0, keepdims=True) for t in tiles]
    a_d, r_d, k_d, b_d, v_d, kw_d, bw_d = [], [], [], [], [], [], []
    for (r, kk, v, lw, kd, bb), lin, ltot in zip(tiles, lins, ltots):
        einv = jnp.exp(-lin)
        ew = jnp.exp(ltot - lin)
        a_d.append(_dform(-kk * jnp.exp(lin - lw), m0).astype(BF16))
        r_d.append(_dform(r * jnp.exp(lin), m0).astype(BF16))
        k_d.append(_dform(kd * einv, m0).astype(BF16))
        b_d.append(_dform(bb * einv, m0).astype(BF16))
        v_d.append(_dform(v, m0).astype(BF16))
        kw_d.append(_dform(kd * ew, m0).astype(BF16))
        bw_d.append(_dform(bb * ew, m0).astype(BF16))

    grams = [_dot_nt(jnp.concatenate([a_d[i], r_d[i]], axis=0), jnp.concatenate([k_d[i], b_d[i]], axis=0))
             for i in each]
    a_ak = [jnp.where(strict, g[:C2, :C2], 0.0).astype(BF16) for g in grams]
    a_ab = [jnp.where(strict, g[:C2, C2:], 0.0) for g in grams]
    a_rk = [jnp.where(incl, g[C2:, :C2], 0.0).astype(BF16) for g in grams]
    a_rb = [jnp.where(incl, g[C2:, C2:], 0.0).astype(BF16) for g in grams]

    tinv = [jnp.where(diag, 1.0, 0.0) + a for a in a_ab]
    apow = a_ab
    for _ in range(int(math.log2(C)) - 1):
        apow = [_dot(a, a) for a in apow]
        tinv = [t + _dot(t, a) for t, a in zip(tinv, apow)]

    akv = [_dot(a_ak[i], v_d[i]) for i in each]
    p = [_dot(tinv[i], jnp.concatenate([a_d[i], akv[i].astype(BF16)], axis=1)) for i in each]
    ur = [_dot(jnp.concatenate([p[i][:, :LANES].astype(BF16), r_d[i]], axis=0), sts[i]) for i in each]
    u = [ur[i][:C2] + p[i][:, LANES:] for i in each]
    y2 = [ur[i][C2:] + _dot(a_rk[i], v_d[i]) + _dot(a_rb[i], u[i]) for i in each]
    ys = [y[:C] + y[C:] for y in y2]
    new = []
    for i in each:
        wc = jnp.broadcast_to(jnp.exp(ltots[i]), (LANES, LANES))
        wcol = jnp.sum(jnp.where(diag, wc, 0.0), axis=1, keepdims=True)
        new.append(sts[i] * wcol + _dot_tn(jnp.concatenate([bw_d[i], kw_d[i]], axis=0),
                                           jnp.concatenate([u[i].astype(BF16), v_d[i]], axis=0)))
    return ys, new


def _rwkv_scan_kernel(r_ref, kk_ref, v_ref, lw_ref, kd_ref, bb_ref, s0_ref, y_ref, sT_ref, st_scr,
                      *, npairs):
    d = pl.program_id(0)
    c = pl.program_id(3)
    nc = pl.num_programs(3)

    @pl.when(c == 0)
    def _():
        st_scr[...] = s0_ref[0, 0]

    lanes = [slice(j * LANES, (j + 1) * LANES) for j in range(npairs)]
    f32 = lambda a: a.astype(F32)
    tiles = [(f32(r_ref[0, :, sl]), f32(kk_ref[0, :, sl]), f32(v_ref[0, :, sl]),
              lw_ref[0, 0, :, sl], f32(kd_ref[0, 0, :, sl]), f32(bb_ref[0, 0, :, sl])) for sl in lanes]
    ys, new = _rwkv_chunk(tiles, [st_scr[j] for j in range(npairs)], d == 1)
    for j, sl in enumerate(lanes):
        y_ref[0, 0, :, sl] = ys[j]
        st_scr[j] = new[j]

    @pl.when(c == nc - 1)
    def _():
        sT_ref[0, 0] = st_scr[...]


def rwkv_scan(f, s0, npairs=RWKV_PAIRS):
    r, kk, v, lw, kd, bb = f["r"], f["kk"], f["v"], f["lw"], f["kd"], f["bb"]
    B, T, _ = r.shape
    C = RWKV_CHUNK
    nc = T // C
    ng = RWKV_PAIRS // npairs
    W = npairs * LANES
    chunk = lambda d, c: c + d * (nc - 1 - 2 * c)
    tok = pl.BlockSpec((1, C, W), lambda d, b, g, c: (b, chunk(d, c), g))
    tokd = pl.BlockSpec((1, 1, C, W), lambda d, b, g, c: (d, b, chunk(d, c), g))
    stsp = pl.BlockSpec((1, 1, npairs, LANES, LANES), lambda d, b, g, c: (d, b, g, 0, 0))
    return pl.pallas_call(
        functools.partial(_rwkv_scan_kernel, npairs=npairs),
        grid=(2, B, ng, nc),
        in_specs=[tok, tok, tok, tokd, tokd, tokd, stsp],
        out_specs=[tokd, stsp],
        out_shape=[jax.ShapeDtypeStruct((2, B, T, GROUP_W), F32),
                   jax.ShapeDtypeStruct(s0.shape, F32)],
        scratch_shapes=[pltpu.VMEM((npairs, LANES, LANES), F32)],
        compiler_params=_cp("parallel", "parallel", "parallel", "arbitrary"),
        name="rwkv_scan",
    )(r, kk, v, lw, kd, bb, s0)


def _rwkv_readout_kernel(yf_ref, yb_ref, bonus_ref, g_ref, ln_ref, o_ref):
    y = yf_ref[0, 0] + yb_ref[0, 0]
    mean = _head_sum(y, RWKV_HEAD) * (1.0 / RWKV_HEAD)
    cen = y - mean
    var = _head_sum(cen * cen, RWKV_HEAD) * (1.0 / RWKV_HEAD)
    yn = cen * lax.rsqrt(var + RWKV_GN_EPS) * ln_ref[0:1, :] + ln_ref[1:2, :]
    o_ref[0] = ((yn + bonus_ref[0].astype(F32)) * g_ref[0].astype(F32)).astype(o_ref.dtype)


def rwkv_readout(y, bonus, g, ln):
    _, B, T, W = y.shape
    tm = 256
    tok = pl.BlockSpec((1, tm, W), lambda b, i: (b, i, 0))
    return pl.pallas_call(
        _rwkv_readout_kernel,
        grid=(B, T // tm),
        in_specs=[pl.BlockSpec((1, 1, tm, W), lambda b, i: (0, b, i, 0)),
                  pl.BlockSpec((1, 1, tm, W), lambda b, i: (1, b, i, 0)),
                  tok, tok, pl.BlockSpec((2, W), lambda b, i: (0, 0))],
        out_specs=tok,
        out_shape=jax.ShapeDtypeStruct((B, T, W), BF16),
        compiler_params=_cp("parallel", "parallel"),
        name="rwkv_readout",
    )(y, y, bonus, g, ln)


def _pool_kernel(p_ref, pp_ref, pn_ref, w_ref, sc_ref, o_ref, *, total):
    i = pl.program_id(1)
    last = pl.num_programs(1) - 1
    x = p_ref[0]
    tm = x.shape[0]
    n = tm + 2 * POOL_HALO
    ext = jnp.concatenate([jnp.where(i == 0, 0.0, pp_ref[0]), x,
                           jnp.where(i == last, 0.0, pn_ref[0])], axis=0)
    t = i * tm + lax.broadcasted_iota(jnp.int32, (tm, 1), 0)
    outs = []
    for gi, win in enumerate(POOL_WINDOWS):
        sl = slice(gi * POOL_GROUP, (gi + 1) * POOL_GROUP)
        e = ext[:, sl]
        s = e + pltpu.roll(e, 1, axis=0)
        step = 1
        while 2 * step < win:
            s = pltpu.roll(s, step, axis=0) + pltpu.roll(s, n - step, axis=0)
            step *= 2
        h = win // 2
        cnt = (jnp.minimum(t + h, total) - jnp.maximum(t - h, 0)).astype(F32)
        z = s[POOL_HALO:POOL_HALO + tm] / cnt - x[:, sl]
        outs.append(_dot(z, w_ref[gi]))
    o_ref[0] = (jnp.concatenate(outs, axis=1) * sc_ref[...]).astype(o_ref.dtype)


def pool_mixer(p, pool_w, pool_scale):
    B, T, W = p.shape
    tm = 256
    prev_map, next_map = _halo_specs(tm, W)
    return pl.pallas_call(
        functools.partial(_pool_kernel, total=T),
        grid=(B, T // tm),
        in_specs=[pl.BlockSpec((1, tm, W), lambda b, i: (b, i, 0)),
                  pl.BlockSpec((1, POOL_HALO, W), prev_map),
                  pl.BlockSpec((1, POOL_HALO, W), next_map(T // POOL_HALO)),
                  pl.BlockSpec(pool_w.shape, lambda b, i: (0, 0, 0)),
                  pl.BlockSpec((1, W), lambda b, i: (0, 0))],
        out_specs=pl.BlockSpec((1, tm, W), lambda b, i: (b, i, 0)),
        out_shape=jax.ShapeDtypeStruct((B, T, W), BF16),
        compiler_params=_cp("parallel", "parallel"),
        name="pool_mixer",
    )(p, p, p, pool_w, pool_scale.reshape(1, W))


def _mlstm_prep_kernel(p_ref, pp_ref, pn_ref, g_ref, cw_ref, gb_ref, *rest, rope):
    if rope:
        cos_ref, sin_ref, q_ref, k_ref, go_ref = rest
    else:
        q_ref, k_ref, go_ref = rest
    i = pl.program_id(1)
    last = pl.num_programs(1) - 1
    x = p_ref[0]
    prev_row = jnp.where(i == 0, 0.0, pp_ref[0, SUBLANES - 1:SUBLANES, :])
    next_row = jnp.where(i == last, 0.0, pn_ref[0, 0:1, :])
    prev, nxt = _shift_rows(x, prev_row, next_row)
    qk = _silu(cw_ref[0:1, :] * prev + cw_ref[1:2, :] * x + cw_ref[2:3, :] * nxt)
    if rope:
        w = qk.shape[1]
        lane = lax.broadcasted_iota(jnp.int32, qk.shape, 1)
        partner = jnp.where(lane % 2 == 0, pltpu.roll(qk, w - 1, axis=1), pltpu.roll(qk, 1, axis=1))
        reps = w // LANES
        cos = jnp.concatenate([cos_ref[...]] * reps, axis=1)
        sin = jnp.concatenate([sin_ref[...]] * reps, axis=1)
        qk = qk * cos + partner * sin
    half = MLSTM_HEADS * MLSTM_DK
    q_ref[0] = qk[:, :half].astype(BF16)
    k_ref[0] = (qk[:, half:] * (MLSTM_DK ** -0.5)).astype(BF16)
    g = g_ref[0] + gb_ref[...]
    lane = lax.broadcasted_iota(jnp.int32, g.shape, 1)
    go_ref[0] = jnp.where((lane // MLSTM_HEADS) % 2 == 1, _log_sigmoid(g), g)


def mlstm_prep(p, conv_w, gate_b, rope_tabs):
    B, T, W = p.shape
    tm = 256
    QK = 2 * MLSTM_HEADS * MLSTM_DK
    prev_map, next_map = _halo_specs(tm, QK)
    rope = rope_tabs is not None
    in_specs = [pl.BlockSpec((1, tm, QK), lambda b, i: (b, i, 0)),
                pl.BlockSpec((1, SUBLANES, QK), prev_map),
                pl.BlockSpec((1, SUBLANES, QK), next_map(T // SUBLANES)),
                pl.BlockSpec((1, tm, LANES), lambda b, i: (b, i, (W - LANES) // LANES)),
                pl.BlockSpec((3, QK), lambda b, i: (0, 0)),
                pl.BlockSpec((1, LANES), lambda b, i: (0, 0))]
    args = [p, p, p, p, conv_w, gate_b]
    if rope:
        in_specs += [pl.BlockSpec((tm, LANES), lambda b, i: (i, 0))] * 2
        args += list(rope_tabs)
    half = MLSTM_HEADS * MLSTM_DK
    return pl.pallas_call(
        functools.partial(_mlstm_prep_kernel, rope=rope),
        grid=(B, T // tm),
        in_specs=in_specs,
        out_specs=[pl.BlockSpec((1, tm, half), lambda b, i: (b, i, 0)),
                   pl.BlockSpec((1, tm, half), lambda b, i: (b, i, 0)),
                   pl.BlockSpec((1, tm, LANES), lambda b, i: (b, i, 0))],
        out_shape=[jax.ShapeDtypeStruct((B, T, half), BF16),
                   jax.ShapeDtypeStruct((B, T, half), BF16),
                   jax.ShapeDtypeStruct((B, T, LANES), F32)],
        compiler_params=_cp("parallel", "parallel"),
        name="mlstm_prep",
    )(*args)


def _mlstm_chunk_kernel(q_ref, k_ref, v_ref, gc_ref, gr_ref, c0_ref, n0_ref, m0_ref,
                        h_ref, cT_ref, nT_ref, mT_ref, c_scr, n_scr, m_scr):
    d = pl.program_id(0)
    c = pl.program_id(2)
    nc = pl.num_programs(2)
    H = MLSTM_HEADS

    @pl.when(c == 0)
    def _():
        c_scr[...] = c0_ref[0, 0]
        n_scr[...] = n0_ref[0, 0]
        m_scr[...] = m0_ref[0, 0]

    rev = d == 1
    L = q_ref.shape[1]
    rr = lax.broadcasted_iota(jnp.int32, (L, L), 0)
    cc = lax.broadcasted_iota(jnp.int32, (L, L), 1)
    sgn = jnp.where(rev, -1, 1)
    seen = (rr - cc) * sgn >= 0
    tri = jnp.where(seen, 1.0, 0.0).astype(BF16)
    tri_t = jnp.where((cc - rr) * sgn >= 0, 1.0, 0.0).astype(BF16)

    gc = gc_ref[0]
    gr = gr_ref[0]
    lane = lax.broadcasted_iota(jnp.int32, gc.shape, 1)
    subl = lax.broadcasted_iota(jnp.int32, gr.shape, 0)
    pick_c = lambda a, idx: jnp.sum(jnp.where(lane == idx, a, 0.0), axis=1, keepdims=True)
    pick_r = lambda a, idx: jnp.sum(jnp.where(subl == idx, a, 0.0), axis=0, keepdims=True)
    bc_all = _dot_exact_lhs(tri, gc)
    gr_hi = gr.astype(BF16)
    gr_r1 = gr - gr_hi.astype(F32)
    gr_mid = gr_r1.astype(BF16)
    gr_lo = (gr_r1 - gr_mid.astype(F32)).astype(BF16)
    br_all = _dot(gr_hi, tri_t) + (_dot(gr_mid, tri_t) + _dot(gr_lo, tri_t))

    heads = range(H)
    i_lane = [d * (2 * H) + j for j in heads]
    f_lane = [d * (2 * H) + H + j for j in heads]
    ig_c = [pick_c(gc, i_lane[j]) for j in heads]
    lf_c = [pick_c(gc, f_lane[j]) for j in heads]
    b_c = [pick_c(bc_all, f_lane[j]) for j in heads]
    ig_r = [pick_r(gr, i_lane[j]) for j in heads]
    b_r = [pick_r(br_all, f_lane[j]) for j in heads]
    m_prev = [m_scr[j] for j in heads]
    q = [q_ref[0, :, j * MLSTM_DK:(j + 1) * MLSTM_DK] for j in heads]
    k = [k_ref[0, :, j * MLSTM_DK:(j + 1) * MLSTM_DK] for j in heads]
    v = [v_ref[0, :, j * MLSTM_DV:(j + 1) * MLSTM_DV].astype(BF16) for j in heads]
    cst = [c_scr[j] for j in heads]
    nst = [n_scr[j] for j in heads]

    qk = [_dot_nt(q[j], k[j]) for j in heads]
    qc = [_dot(q[j], cst[j]) for j in heads]
    dlog = [jnp.where(seen, b_c[j] - b_r[j] + ig_r[j], NEG) for j in heads]
    inter = [b_c[j] + m_prev[j] for j in heads]
    m_t = [jnp.maximum(inter[j], jnp.max(dlog[j], axis=1, keepdims=True)) for j in heads]
    s = [qk[j] * jnp.exp(dlog[j] - m_t[j]) for j in heads]
    w_inter = [jnp.exp(inter[j] - m_t[j]) for j in heads]
    sv = [_dot(s[j], v[j]) for j in heads]
    for j in heads:
        num = sv[j] + w_inter[j] * qc[j]
        den = (jnp.sum(s[j], axis=1, keepdims=True)
               + w_inter[j] * jnp.sum(q[j].astype(F32) * nst[j], axis=1, keepdims=True))
        h_ref[0, 0, :, j * MLSTM_DV:(j + 1) * MLSTM_DV] = num / jnp.maximum(jnp.abs(den), jnp.exp(-m_t[j]))

    b_end = [jnp.sum(lf_c[j], axis=0, keepdims=True) for j in heads]
    g_s = [b_end[j] - b_c[j] + ig_c[j] for j in heads]
    m_new = [jnp.maximum(b_end[j] + m_prev[j], jnp.max(g_s[j], axis=0, keepdims=True)) for j in heads]
    kw = [k[j].astype(F32) * jnp.exp(g_s[j] - m_new[j]) for j in heads]
    kv = [_dot_tn(kw[j], v[j]) for j in heads]
    for j in heads:
        decay = jnp.exp(b_end[j] + m_prev[j] - m_new[j])
        c_scr[j] = decay * cst[j] + kv[j]
        n_scr[j] = decay * nst[j] + jnp.sum(kw[j], axis=0, keepdims=True)
        m_scr[j] = m_new[j]

    @pl.when(c == nc - 1)
    def _():
        cT_ref[0, 0] = c_scr[...]
        nT_ref[0, 0] = n_scr[...]
        mT_ref[0, 0] = m_scr[...]


def mlstm_scan(q, k, p, gates, gates_t, state):
    B, T, _ = q.shape
    L = min(MLSTM_CHUNK, T)
    nc = T // L
    H = MLSTM_HEADS
    QK = H * MLSTM_DK
    c0, n0, m0 = state
    chunk = lambda d, c: c + d * (nc - 1 - 2 * c)
    v_blk = (2 * QK) // GROUP_W
    csp = pl.BlockSpec((1, 1, H, MLSTM_DK, MLSTM_DV), lambda d, b, c: (d, b, 0, 0, 0))
    nsp = pl.BlockSpec((1, 1, H, 1, MLSTM_DK), lambda d, b, c: (d, b, 0, 0, 0))
    msp = pl.BlockSpec((1, 1, H, 1, 1), lambda d, b, c: (d, b, 0, 0, 0))
    return pl.pallas_call(
        _mlstm_chunk_kernel,
        grid=(2, B, nc),
        in_specs=[pl.BlockSpec((1, L, QK), lambda d, b, c: (b, chunk(d, c), 0)),
                  pl.BlockSpec((1, L, QK), lambda d, b, c: (b, chunk(d, c), 0)),
                  pl.BlockSpec((1, L, GROUP_W), lambda d, b, c: (b, chunk(d, c), v_blk)),
                  pl.BlockSpec((1, L, LANES), lambda d, b, c: (b, chunk(d, c), 0)),
                  pl.BlockSpec((1, 4 * H, L), lambda d, b, c: (b, 0, chunk(d, c))),
                  csp, nsp, msp],
        out_specs=[pl.BlockSpec((1, 1, L, GROUP_W), lambda d, b, c: (d, b, chunk(d, c), 0)),
                   csp, nsp, msp],
        out_shape=[jax.ShapeDtypeStruct((2, B, T, GROUP_W), F32),
                   jax.ShapeDtypeStruct(c0.shape, F32),
                   jax.ShapeDtypeStruct(n0.shape, F32),
                   jax.ShapeDtypeStruct(m0.shape, F32)],
        scratch_shapes=[pltpu.VMEM((H, MLSTM_DK, MLSTM_DV), F32),
                        pltpu.VMEM((H, 1, MLSTM_DK), F32),
                        pltpu.VMEM((H, 1, 1), F32)],
        compiler_params=_cp("parallel", "parallel", "arbitrary"),
        name="mlstm_scan",
    )(q, k, p, gates, gates_t, c0, n0, m0)


def _mlstm_readout_kernel(hf_ref, hb_ref, o_ref, nw_ref, out_ref):
    h = hf_ref[0, 0] + hb_ref[0, 0]
    outs = []
    for j in range(MLSTM_HEADS):
        hj = h[:, j * MLSTM_DV:(j + 1) * MLSTM_DV]
        outs.append(hj * lax.rsqrt(jnp.mean(hj * hj, axis=-1, keepdims=True) + EPS))
    hn = jnp.concatenate(outs, axis=1)
    out_ref[0] = (hn * nw_ref[...] * _sigmoid(o_ref[0])).astype(out_ref.dtype)


def mlstm_readout(h, p, norm_w):
    _, B, T, W = h.shape
    tm = 256
    o_blk = (2 * MLSTM_HEADS * MLSTM_DK + GROUP_W) // GROUP_W
    return pl.pallas_call(
        _mlstm_readout_kernel,
        grid=(B, T // tm),
        in_specs=[pl.BlockSpec((1, 1, tm, W), lambda b, i: (0, b, i, 0)),
                  pl.BlockSpec((1, 1, tm, W), lambda b, i: (1, b, i, 0)),
                  pl.BlockSpec((1, tm, W), lambda b, i: (b, i, o_blk)),
                  pl.BlockSpec((1, W), lambda b, i: (0, 0))],
        out_specs=pl.BlockSpec((1, tm, W), lambda b, i: (b, i, 0)),
        out_shape=jax.ShapeDtypeStruct((B, T, W), BF16),
        compiler_params=_cp("parallel", "parallel"),
        name="mlstm_readout",
    )(h, h, p, norm_w.reshape(1, W))


def _na_prep_kernel(p_ref, nw_ref, q_ref, k_ref, v_ref):
    x = p_ref[0]
    for j in range(NA_HEADS):
        sl = slice(j * NA_HEAD, (j + 1) * NA_HEAD)
        for src, dst, row, scale in ((0, q_ref, 0, NA_HEAD ** -0.5), (GROUP_W, k_ref, 1, 1.0)):
            z = x[:, src + j * NA_HEAD:src + (j + 1) * NA_HEAD]
            zn = z * lax.rsqrt(jnp.mean(z * z, axis=-1, keepdims=True) + EPS) * nw_ref[row:row + 1, :]
            dst[0, :, sl] = (zn * scale).astype(BF16)
    v_ref[0] = x[:, 2 * GROUP_W:].astype(BF16)


def na_prep(p, qk_norm):
    B, T, W = p.shape
    tm = 256
    tok = pl.BlockSpec((1, tm, GROUP_W), lambda b, i: (b, i, 0))
    sh = jax.ShapeDtypeStruct((B, T, GROUP_W), BF16)
    return pl.pallas_call(
        _na_prep_kernel,
        grid=(B, T // tm),
        in_specs=[pl.BlockSpec((1, tm, W), lambda b, i: (b, i, 0)),
                  pl.BlockSpec((2, NA_HEAD), lambda b, i: (0, 0))],
        out_specs=[tok, tok, tok],
        out_shape=[sh, sh, sh],
        compiler_params=_cp("parallel", "parallel"),
        name="na_prep",
    )(p, qk_norm)


def _softmax_av(s, v):
    m = jnp.max(s, axis=1, keepdims=True)
    e = jnp.exp(s - m)
    return _dot(e, v) * (1.0 / jnp.sum(e, axis=1, keepdims=True))


def _na_ctx_kernel(q_ref, k_ref, v_ref, o_ref):
    for j in range(NA_HEADS):
        sl = slice(j * NA_HEAD, (j + 1) * NA_HEAD)
        s = _dot_nt(q_ref[0, :, sl], k_ref[0, :, sl])
        o_ref[0, :, sl] = _softmax_av(s, v_ref[0, :, sl]).astype(o_ref.dtype)


def na_ctx_attention(q, k, v):
    B, T, W = q.shape
    tok = pl.BlockSpec((1, T, W), lambda b: (b, 0, 0))
    return pl.pallas_call(
        _na_ctx_kernel,
        grid=(B,),
        in_specs=[tok, tok, tok],
        out_specs=tok,
        out_shape=jax.ShapeDtypeStruct((B, T, W), BF16),
        compiler_params=_cp("parallel"),
        name="na_ctx_attention",
    )(q, k, v)


def _na_lat_kernel(q_ref, k0_ref, k1_ref, k2_ref, v0_ref, v1_ref, v2_ref, kc_ref, vc_ref, bias_ref, o_ref):
    for j in range(NA_HEADS):
        sl = slice(j * NA_HEAD, (j + 1) * NA_HEAD)
        q = q_ref[0, :, sl]
        kcat = jnp.concatenate([k0_ref[0, :, sl], k1_ref[0, :, sl], k2_ref[0, :, sl]], axis=0)
        vcat = jnp.concatenate([v0_ref[0, :, sl], v1_ref[0, :, sl], v2_ref[0, :, sl],
                                vc_ref[0, :, sl]], axis=0)
        s = jnp.concatenate([_dot_nt(q, kcat) + bias_ref[0, j], _dot_nt(q, kc_ref[0, :, sl])], axis=1)
        o_ref[0, :, sl] = _softmax_av(s, vcat).astype(o_ref.dtype)


def na_bias_table(rpb, rows):
    nq, nk = NA_QROWS, NA_KROWS
    starts = np.array([0, nq, rows - nq])
    r = starts[:, None] + np.arange(nq)[None, :]
    kr = starts[:, None] - nq + np.arange(nk)[None, :]
    r0 = np.clip(r - WIN_H // 2, 0, rows - WIN_H)
    rv = (kr[:, None, :] >= r0[:, :, None]) & (kr[:, None, :] < r0[:, :, None] + WIN_H)
    dr = kr[:, None, :] - r[:, :, None] + WIN_H - 1
    col = np.arange(GRID_W)
    c0 = np.clip(col - WIN_W // 2, 0, GRID_W - WIN_W)
    cv = (col[None, :] >= c0[:, None]) & (col[None, :] < c0[:, None] + WIN_W)
    pad = GRID_W - WIN_W
    rpb_pad = jnp.pad(rpb, ((0, 0), (0, 0), (pad, pad)))
    colbias = jnp.stack([rpb_pad[:, :, GRID_W - 1 - q:2 * GRID_W - 1 - q] for q in range(GRID_W)], axis=2)
    colbias = jnp.where(cv[None, None], colbias, NEG)
    blank = jnp.full((NA_HEADS, GRID_W, GRID_W), NEG, F32)
    pats = []
    for p in range(3):
        qrows = [jnp.concatenate([colbias[:, int(dr[p, i, j])] if rv[p, i, j] else blank
                                  for j in range(nk)], axis=2) for i in range(nq)]
        pats.append(jnp.concatenate(qrows, axis=1))
    return jnp.stack(pats)


def na_lat_attention(q, k, v, kc, vc, bias):
    B, T, W = q.shape
    tq = NA_QROWS * GRID_W
    nb = T // tq
    ctx_len = kc.shape[1]
    qsp = pl.BlockSpec((1, tq, W), lambda b, i: (b, i, 0))
    prv = pl.BlockSpec((1, tq, W), lambda b, i: (b, jnp.maximum(i - 1, 0), 0))
    nxt = pl.BlockSpec((1, tq, W), lambda b, i: (b, jnp.minimum(i + 1, nb - 1), 0))
    csp = pl.BlockSpec((1, ctx_len, W), lambda b, i: (b, 0, 0))
    pattern = lambda i: jnp.where(i == 0, 0, jnp.where(i == nb - 1, 2, 1))
    return pl.pallas_call(
        _na_lat_kernel,
        grid=(B, nb),
        in_specs=[qsp, prv, qsp, nxt, prv, qsp, nxt, csp, csp,
                  pl.BlockSpec((1, NA_HEADS, tq, NA_KROWS * GRID_W), lambda b, i: (pattern(i), 0, 0, 0))],
        out_specs=qsp,
        out_shape=jax.ShapeDtypeStruct((B, T, W), BF16),
        compiler_params=_cp("parallel", "arbitrary"),
        name="na_lat_attention",
    )(q, k, k, k, v, v, v, kc, vc, bias)


def _ffn_norm_router_body(h_ref, g_ref, sh_ref, sc_ref, rw_ref, rb_ref, v_ref, ids_ref, wts_ref):
    x = h_ref[0]
    y = x * lax.rsqrt(jnp.mean(x * x, axis=-1, keepdims=True) + EPS) * g_ref[...]
    v = y * (1.0 + sc_ref[0]) + sh_ref[0]
    half = v.shape[1] // 2
    v_ref[...] = _pack_bf16_pair(v[:, :half], v[:, half:])
    vh, vl = _split2(v)
    logits = (_dot(vh, rw_ref[0]) + (_dot(vh, rw_ref[1]) + _dot(vl, rw_ref[0]))) + rb_ref[...]
    lane = lax.broadcasted_iota(jnp.int32, logits.shape, 1).astype(F32)
    first = lambda mask: jnp.min(jnp.where(mask, lane, float(LANES)), axis=1, keepdims=True)

    g_mask = lane < N_GROUPS
    gl = jnp.where(g_mask, logits, NEG)
    gmax = jnp.max(gl, axis=1, keepdims=True)
    g_p = 1.0 / jnp.sum(jnp.exp(gl - gmax), axis=1, keepdims=True)
    g_idx = first(g_mask & (gl == gmax))
    e_lane = lane - N_GROUPS
    e_mask = (e_lane >= 0) & (e_lane < N_EXPERTS) & (jnp.floor(e_lane * (1.0 / EXPERTS_PER_GROUP)) == g_idx)
    el = jnp.where(e_mask, logits, NEG)
    e1 = jnp.max(el, axis=1, keepdims=True)
    i1 = first(e_mask & (el == e1))
    el2 = jnp.where(lane == i1, NEG, el)
    e2 = jnp.max(el2, axis=1, keepdims=True)
    i2 = first(e_mask & (lane != i1) & (el2 == e2))
    x2 = jnp.exp(e2 - e1)
    w1 = g_p / (1.0 + x2)
    w2 = g_p * x2 / (1.0 + x2)
    ids_ref[...] = jnp.where(lane == 0, i1 - N_GROUPS, jnp.where(lane == 1, i2 - N_GROUPS, 0.0)).astype(jnp.int32)
    wts_ref[...] = jnp.where(lane == 0, w1, jnp.where(lane == 1, w2, 0.0))


def _ffn_norm_router_kernel(*refs, bounds):
    ns = len(bounds) - 1
    g_ref, rw_ref, rb_ref, v_ref, ids_ref, wts_ref = refs[3 * ns:]
    r = pl.program_id(0)
    for s in range(ns):
        h_ref, sh_ref, sc_ref = refs[3 * s:3 * s + 3]

        @pl.when((r >= bounds[s]) & (r < bounds[s + 1]))
        def _():
            _ffn_norm_router_body(h_ref, g_ref, sh_ref, sc_ref, rw_ref, rb_ref, v_ref, ids_ref, wts_ref)


def ffn_norm_router(streams, g, rw, rb):
    D = g.shape[0]
    tm = 256
    bounds = [0]
    in_specs, args = [], []
    for h, shift, scale in streams:
        B, T, _ = h.shape
        nt = T // tm
        lo, nblk = bounds[-1], B * nt
        bounds.append(lo + nblk)
        local = lambda r, lo=lo, nblk=nblk: jnp.clip(r - lo, 0, nblk - 1)
        in_specs += [pl.BlockSpec((1, tm, D), lambda r, f=local, nt=nt: (f(r) // nt, f(r) % nt, 0)),
                     pl.BlockSpec((1, 1, D), lambda r, f=local, nt=nt: (f(r) // nt, 0, 0)),
                     pl.BlockSpec((1, 1, D), lambda r, f=local, nt=nt: (f(r) // nt, 0, 0))]
        args += [h, shift, scale]
    n_rows = bounds[-1] * tm
    in_specs += [pl.BlockSpec((1, D), lambda r: (0, 0)),
                 pl.BlockSpec((2, D, LANES), lambda r: (0, 0, 0)),
                 pl.BlockSpec((1, LANES), lambda r: (0, 0))]
    args += [g.reshape(1, D), rw, rb]
    lan = pl.BlockSpec((tm, LANES), lambda r: (r, 0))
    return pl.pallas_call(
        functools.partial(_ffn_norm_router_kernel, bounds=tuple(bounds)),
        grid=(bounds[-1],),
        in_specs=in_specs,
        out_specs=[pl.BlockSpec((tm, D // 2), lambda r: (r, 0)), lan, lan],
        out_shape=[jax.ShapeDtypeStruct((n_rows, D // 2), jnp.uint32),
                   jax.ShapeDtypeStruct((n_rows, LANES), jnp.int32),
                   jax.ShapeDtypeStruct((n_rows, LANES), F32)],
        compiler_params=_cp("parallel"),
        name="ffn_norm_router",
    )(*args)


def _moe_kernel(be_ref, zero_ref, tok_ref, tokn_ref, dstp_ref, sw_ref, v_hbm, wg_hbm, wu_hbm, wd_hbm, out_hbm,
                x0, x1, y0, y1, wg_buf, wu_buf, wd_buf, stage_up, stage_down, gsem, ssem, wsem, *, layer):
    i = pl.program_id(0)
    nb = pl.num_programs(0) - 1
    xs, ys = (x0, x1), (y0, y1)
    zero = zero_ref[0]
    used = zero_ref[1]

    def gather_row(idx_ref, s, j, after=0):
        return pltpu.make_async_copy(v_hbm.at[pl.ds(idx_ref[0, 0, j] + after, 1)], xs[s].at[pl.ds(j, 1)],
                                     gsem.at[s])

    def wait_gather(s):
        pltpu.make_async_copy(v_hbm.at[pl.ds(0, MOE_BLOCK)], xs[s], gsem.at[s]).wait()

    def scatter_row(s, j, after=0):
        return pltpu.make_async_copy(ys[s].at[pl.ds(j, 1)], out_hbm.at[pl.ds(dstp_ref[0, 0, j] + after, 1)],
                                     ssem.at[s])

    def wait_scatter(s):
        pltpu.make_async_copy(ys[s], out_hbm.at[pl.ds(0, MOE_BLOCK)], ssem.at[s]).wait()

    def after(result):
        return result[0, 0].astype(jnp.int32) * zero

    @pl.when(i == 0)
    def _():
        def body(j, carry):
            gather_row(tok_ref, 0, j).start()
            return carry
        lax.fori_loop(0, MOE_BLOCK, body, 0)

    e = be_ref[jnp.minimum(i, nb - 1)]

    @pl.when((i == 0) | ((i < used) & (e != be_ref[jnp.maximum(i - 1, 0)])))
    def _():
        chunks = []
        for src, dst, stage in ((wg_hbm, wg_buf, stage_up), (wu_hbm, wu_buf, stage_up),
                                (wd_hbm, wd_buf, stage_down)):
            rows = stage.shape[1]
            for r in range(dst.shape[0] // rows):
                k = len(chunks)
                cp = pltpu.make_async_copy(src.at[layer, e, pl.ds(r * rows, rows)], stage.at[k % 2],
                                           wsem.at[k % 2])
                chunks.append((cp, stage, dst, r * rows, rows))
        chunks[0][0].start()
        for k, (cp, stage, dst, r0, rows) in enumerate(chunks):
            if k + 1 < len(chunks):
                chunks[k + 1][0].start()
            cp.wait()
            dst[r0:r0 + rows, :] = stage[k % 2].astype(BF16)

    def step(s, scatter_prev):
        wait_gather(s)
        x_lo, x_hi = _unpack_bf16_pair(xs[s][...])
        x = jnp.concatenate([x_lo.astype(BF16), x_hi.astype(BF16)], axis=1)
        n_up = EXPERT_FF // MOE_FF_TILE
        per = MOE_BLOCK // (2 * n_up)
        acts = []
        for c in range(n_up):
            cols = slice(c * MOE_FF_TILE, (c + 1) * MOE_FF_TILE)
            hg = jnp.dot(x, wg_buf[:, cols], preferred_element_type=F32)
            hu = jnp.dot(x, wu_buf[:, cols], preferred_element_type=F32)
            for g, res in enumerate((hg, hu)):
                dep = after(res)
                for j in range((2 * c + g) * per, (2 * c + g + 1) * per):
                    gather_row(tokn_ref, 1 - s, j, dep).start()
            acts.append((_silu(hg) * hu).astype(BF16))
        act = jnp.concatenate(acts, axis=1)

        @pl.when(i >= 2)
        def _():
            wait_scatter(s)

        half = x.shape[1] // 2
        n_down = half // MOE_OUT_TILE
        per = MOE_BLOCK // n_down
        for c in range(n_down):
            cols = slice(c * MOE_OUT_TILE, (c + 1) * MOE_OUT_TILE)
            hcols = slice(half + c * MOE_OUT_TILE, half + (c + 1) * MOE_OUT_TILE)
            y_lo = jnp.dot(act, wd_buf[:, cols], preferred_element_type=F32) * sw_ref[...]
            y_hi = jnp.dot(act, wd_buf[:, hcols], preferred_element_type=F32) * sw_ref[...]
            ys[s][:, cols] = _pack_bf16_pair(y_lo, y_hi)
            if scatter_prev:
                dep = after(y_hi)
                for j in range(c * per, (c + 1) * per):
                    scatter_row(1 - s, j, dep).start()

    def idle(s):
        wait_gather(s)

        def fetch(j, carry):
            gather_row(tokn_ref, 1 - s, j).start()
            return carry
        lax.fori_loop(0, MOE_BLOCK, fetch, 0)
        wait_scatter(s)

        def send(j, carry):
            scatter_row(1 - s, j).start()
            return carry
        lax.fori_loop(0, MOE_BLOCK, send, 0)

    pl.when(i == 0)(functools.partial(step, 0, False))
    for s in range(2):
        pl.when((i > 0) & (i < used) & (i % 2 == s))(functools.partial(step, s, True))
        pl.when((i >= used) & (i < nb) & (i % 2 == s))(functools.partial(idle, s))

    @pl.when(i == nb)
    def _():
        def drain(s):
            wait_gather(s)
            wait_scatter(s)

            def body(j, carry):
                scatter_row(1 - s, j).start()
                return carry
            lax.fori_loop(0, MOE_BLOCK, body, 0)
            wait_scatter(1 - s)
        for s in range(2):
            pl.when(i % 2 == s)(functools.partial(drain, s))


def moe_experts(v_rows, block_expert, used, slot_tok, slot_dst, slot_w, wg, wu, wd, layer):
    P = slot_tok.shape[0]
    W = v_rows.shape[1]
    D = 2 * W
    nb = P // MOE_BLOCK
    assert nb >= 2
    idx = lambda a: a.reshape(nb, 1, MOE_BLOCK)
    smem = lambda imap: pl.BlockSpec((1, 1, MOE_BLOCK), imap, memory_space=pltpu.SMEM)
    hbm = pl.BlockSpec(memory_space=pl.ANY)
    last = nb - 1
    grid_spec = pltpu.PrefetchScalarGridSpec(
        num_scalar_prefetch=2,
        grid=(nb + 1,),
        in_specs=[smem(lambda i, be, z: (jnp.minimum(i, last), 0, 0)),
                  smem(lambda i, be, z: (jnp.minimum(i + 1, last), 0, 0)),
                  smem(lambda i, be, z: (jnp.maximum(i - 1, 0), 0, 0)),
                  pl.BlockSpec((MOE_BLOCK, 1), lambda i, be, z: (jnp.minimum(i, last), 0)),
                  hbm, hbm, hbm, hbm],
        out_specs=hbm,
        scratch_shapes=[pltpu.VMEM((MOE_BLOCK, W), jnp.uint32),
                        pltpu.VMEM((MOE_BLOCK, W), jnp.uint32),
                        pltpu.VMEM((MOE_BLOCK, W), jnp.uint32),
                        pltpu.VMEM((MOE_BLOCK, W), jnp.uint32),
                        pltpu.VMEM((D, EXPERT_FF), BF16),
                        pltpu.VMEM((D, EXPERT_FF), BF16),
                        pltpu.VMEM((EXPERT_FF, D), BF16),
                        pltpu.VMEM((2, MOE_STAGE_BYTES // (4 * EXPERT_FF), EXPERT_FF), F32),
                        pltpu.VMEM((2, MOE_STAGE_BYTES // (4 * D), D), F32),
                        pltpu.SemaphoreType.DMA((2,)),
                        pltpu.SemaphoreType.DMA((2,)),
                        pltpu.SemaphoreType.DMA((2,))],
    )
    return pl.pallas_call(
        functools.partial(_moe_kernel, layer=layer),
        grid_spec=grid_spec,
        out_shape=jax.ShapeDtypeStruct((P, W), jnp.uint32),
        compiler_params=_cp("arbitrary"),
        name="moe_experts",
    )(block_expert, jnp.stack([jnp.zeros((), jnp.int32), used]), idx(slot_tok), idx(slot_tok), idx(slot_dst),
      slot_w.reshape(P, 1), v_rows, wg, wu, wd)


def _moe_combine_kernel(h_ref, f0_ref, f1_ref, gate_ref, o_ref):
    a_lo, a_hi = _unpack_bf16_pair(f0_ref[...])
    b_lo, b_hi = _unpack_bf16_pair(f1_ref[...])
    half = a_lo.shape[1]
    o_ref[0, :, :half] = h_ref[0, :, :half] + gate_ref[0, :, :half] * (a_lo + b_lo)
    o_ref[0, :, half:] = h_ref[0, :, half:] + gate_ref[0, :, half:] * (a_hi + b_hi)


def moe_combine(h, out2, gate, row_offset, n_rows):
    B, T, D = h.shape
    tm = 256
    nt = T // tm
    blk0 = row_offset // tm
    tok = pl.BlockSpec((1, tm, D), lambda b, i: (b, i, 0))
    return pl.pallas_call(
        _moe_combine_kernel,
        grid=(B, nt),
        in_specs=[tok,
                  pl.BlockSpec((tm, D // 2), lambda b, i: (blk0 + b * nt + i, 0)),
                  pl.BlockSpec((tm, D // 2), lambda b, i: (n_rows // tm + blk0 + b * nt + i, 0)),
                  pl.BlockSpec((1, 1, D), lambda b, i: (b, 0, 0))],
        out_specs=tok,
        out_shape=jax.ShapeDtypeStruct((B, T, D), F32),
        compiler_params=_cp("parallel", "parallel"),
        name="moe_combine",
    )(h, out2, out2, gate)


def moe_dispatch(ids, wts):
    N = ids.shape[0]
    A = N * TOP_K
    e_flat = ids.reshape(A)
    order = jnp.argsort(e_flat).astype(jnp.int32)
    counts = jnp.sum((e_flat[:, None] == jnp.arange(N_EXPERTS)[None, :]).astype(jnp.int32), axis=0)
    padded = (counts + MOE_BLOCK - 1) // MOE_BLOCK * MOE_BLOCK
    ends = jnp.cumsum(padded)
    n_blocks = -(-(A + N_EXPERTS * (MOE_BLOCK - 1)) // MOE_BLOCK)
    block_start = jnp.arange(n_blocks, dtype=jnp.int32) * MOE_BLOCK
    block_expert = jnp.minimum(jnp.sum((ends[None, :] <= block_start[:, None]).astype(jnp.int32), axis=1),
                               N_EXPERTS - 1)
    into = block_start - (ends - padded)[block_expert]
    b_count = counts[block_expert]
    local = (into[:, None] + jnp.arange(MOE_BLOCK, dtype=jnp.int32)[None, :])
    real = local < b_count[:, None]
    src = jnp.clip((jnp.cumsum(counts) - counts)[block_expert][:, None] + local, 0, A - 1)
    a = order[src.reshape(-1)]
    real = real.reshape(-1)
    tok = a // TOP_K
    slot_tok = jnp.where(real, tok, 0).astype(jnp.int32)
    spare = A + jnp.cumsum(jnp.where(real, 0, 1)) - 1
    slot_dst = jnp.where(real, (a % TOP_K) * N + tok, spare).astype(jnp.int32)
    slot_w = jnp.where(real, wts.reshape(A)[a], 0.0)
    return slot_tok, slot_dst, slot_w, block_expert, (ends[-1] // MOE_BLOCK).astype(jnp.int32)


def hier_moe(v_rows, ids, wts, wg, wu, wd, layer):
    slot_tok, slot_dst, slot_w, block_expert, used = moe_dispatch(ids, wts)
    return moe_experts(v_rows, block_expert, used, slot_tok, slot_dst, slot_w, wg, wu, wd, layer)


def _layer_weights(l, P):
    w = {}
    w_in = P["w_in"][l]
    o = 0
    a = w_in[:, o:o + RWKV_COLS]; o += RWKV_COLS
    b = w_in[:, o:o + GROUP_W]; o += GROUP_W
    c = w_in[:, o:o + MLSTM_COLS]; o += MLSTM_COLS
    d = w_in[:, o:o + NA_COLS]
    D = w_in.shape[0]
    pad_a = RWKV_PAD_COLS - RWKV_COLS
    w["in_a"] = jnp.concatenate([a, jnp.zeros((D, pad_a), F32)], axis=1).astype(BF16)
    w["in_b"] = b.astype(BF16)
    w["in_c"] = jnp.concatenate([c, jnp.zeros((D, MLSTM_PAD_COLS - MLSTM_COLS), F32)], axis=1).astype(BF16)
    w["in_d"] = d.astype(BF16)
    w["out"] = P["w_out"][l].astype(BF16)

    z = jnp.zeros((RWKV_DECAY_RANK, GROUP_W), F32)
    wup, aup = P["rwkv_w_up"][l], P["rwkv_a_up"][l]
    rw = {
        "mu": jnp.concatenate([P["rwkv_mu"][l], jnp.zeros((2, pad_a), F32)], axis=1),
        "w_up": jnp.concatenate([jnp.concatenate([wup[0], z], axis=1),
                                 jnp.concatenate([z, wup[1]], axis=1)], axis=0).astype(BF16),
        "a_up": jnp.concatenate([jnp.concatenate([aup[0], z], axis=1),
                                 jnp.concatenate([z, aup[1]], axis=1)], axis=0).astype(BF16),
        "w0": P["rwkv_w0"][l].reshape(1, 2 * GROUP_W),
        "a0": P["rwkv_a0"][l].reshape(1, 2 * GROUP_W),
        "g_up": jnp.concatenate([P["rwkv_g_up"][l],
                                 jnp.zeros((256 - RWKV_GATE_RANK, GROUP_W), F32)], axis=0).astype(BF16),
        "k_k": P["rwkv_k_k"][l].reshape(1, GROUP_W),
        "k_a": P["rwkv_k_a"][l].reshape(1, GROUP_W),
        "r_k": P["rwkv_r_k"][l].reshape(1, GROUP_W),
    }
    w["rwkv"] = rw
    w["rwkv_ln"] = P["rwkv_ln"][l]
    w["pool_w"] = P["pool_w"][l].astype(BF16)
    w["pool_scale"] = P["pool_scale"][l]
    w["mlstm_conv"] = P["mlstm_conv"][l]
    gb = P["mlstm_gate_b"][l].reshape(1, 4 * MLSTM_HEADS)
    w["mlstm_gate_b"] = jnp.concatenate([gb, jnp.zeros((1, LANES - 4 * MLSTM_HEADS), F32)], axis=1)
    w["mlstm_norm"] = P["mlstm_norm"][l]
    w["na_qk_norm"] = P["na_qk_norm"][l]
    w["na_rpb"] = P["na_rpb"][l]
    rcat = jnp.concatenate([P["router_g_w"][l], P["router_e_w"][l],
                            jnp.zeros((D, LANES - N_GROUPS - N_EXPERTS), F32)], axis=1)
    rhi = rcat.astype(BF16)
    w["router_w"] = jnp.stack([rhi, (rcat - rhi.astype(F32)).astype(BF16)])
    w["router_b"] = jnp.concatenate([P["router_g_b"][l], P["router_e_b"][l],
                                     jnp.zeros((LANES - N_GROUPS - N_EXPERTS,), F32)]).reshape(1, LANES)
    return w


def rope_tables(T):
    t = jnp.arange(T)
    row = (t // GRID_W).astype(F32)
    col = (t % GRID_W).astype(F32)
    n_pairs = MLSTM_DK // 4
    inv = ROPE_BASE ** (-jnp.arange(n_pairs, dtype=F32) / n_pairs)
    ang = jnp.concatenate([row[:, None] * inv, col[:, None] * inv], axis=-1)
    cos = jnp.repeat(jnp.cos(ang), 2, axis=1)
    sin = jnp.repeat(jnp.sin(ang), 2, axis=1) * jnp.tile(jnp.array([-1.0, 1.0], F32), MLSTM_DK // 2)
    return cos, sin


def rwkv_mixer(p_ctx, p_lat, w, need_ctx):
    B = p_lat.shape[0]
    names = ("r", "kk", "v", "bonus", "g", "lw", "kd", "bb")
    fc = dict(zip(names, rwkv_prep(p_ctx, w["rwkv"])))
    fl = dict(zip(names, rwkv_prep(p_lat, w["rwkv"])))
    s0 = jnp.zeros((2, B, RWKV_PAIRS, LANES, LANES), F32)
    y_c, s_ctx = rwkv_scan(fc, s0)
    y_l, _ = rwkv_scan(fl, s_ctx)
    out_l = rwkv_readout(y_l, fl["bonus"], fl["g"], w["rwkv_ln"])
    out_c = rwkv_readout(y_c, fc["bonus"], fc["g"], w["rwkv_ln"]) if need_ctx else None
    return out_c, out_l


def mlstm_mixer(p_ctx, p_lat, w, rope, need_ctx):
    B = p_lat.shape[0]
    H = MLSTM_HEADS
    state = (jnp.zeros((2, B, H, MLSTM_DK, MLSTM_DV), F32),
             jnp.zeros((2, B, H, 1, MLSTM_DK), F32),
             jnp.zeros((2, B, H, 1, 1), F32))
    outs = []
    for p, tabs in ((p_ctx, None), (p_lat, rope)):
        q, k, gates = mlstm_prep(p, w["mlstm_conv"], w["mlstm_gate_b"], tabs)
        gates_t = jnp.swapaxes(gates[:, :, :4 * H], 1, 2)
        h, *state = mlstm_scan(q, k, p, gates, gates_t, tuple(state))
        outs.append(h)
    out_l = mlstm_readout(outs[1], p_lat, w["mlstm_norm"])
    out_c = mlstm_readout(outs[0], p_ctx, w["mlstm_norm"]) if need_ctx else None
    return out_c, out_l


def na_mixer(p_ctx, p_lat, w, need_ctx):
    qc, kc, vc = na_prep(p_ctx, w["na_qk_norm"])
    ql, kl, vl = na_prep(p_lat, w["na_qk_norm"])
    rows = p_lat.shape[1] // GRID_W
    bias = na_bias_table(w["na_rpb"], rows)
    out_l = na_lat_attention(ql, kl, vl, kc, vc, bias)
    out_c = na_ctx_attention(qc, kc, vc) if need_ctx else None
    return out_c, out_l


def token_mixers(u_ctx, u_lat, w, rope, need_ctx):
    outs_c, outs_l = [], []
    pa = (matmul(u_ctx, w["in_a"]), matmul(u_lat, w["in_a"]))
    oc, ol = rwkv_mixer(pa[0], pa[1], w, need_ctx)
    outs_c.append(oc); outs_l.append(ol)
    pb = (matmul(u_ctx, w["in_b"]), matmul(u_lat, w["in_b"]))
    outs_l.append(pool_mixer(pb[1], w["pool_w"], w["pool_scale"]))
    outs_c.append(pool_mixer(pb[0], w["pool_w"], w["pool_scale"]) if need_ctx else None)
    pc = (matmul(u_ctx, w["in_c"]), matmul(u_lat, w["in_c"]))
    oc, ol = mlstm_mixer(pc[0], pc[1], w, rope, need_ctx)
    outs_c.append(oc); outs_l.append(ol)
    pd = (matmul(u_ctx, w["in_d"]), matmul(u_lat, w["in_d"]))
    oc, ol = na_mixer(pd[0], pd[1], w, need_ctx)
    outs_c.append(oc); outs_l.append(ol)
    return outs_c, outs_l


def kernel(x, c, ctx, c_ctx, ada_w, ada_b, norm_mix, norm_ffn, w_in, w_out, rwkv_mu, rwkv_w0, rwkv_w_up, rwkv_a0, rwkv_a_up, rwkv_g_up, rwkv_k_k, rwkv_k_a, rwkv_r_k, rwkv_ln, pool_w, pool_scale, mlstm_conv, mlstm_gate_b, mlstm_norm, na_qk_norm, na_rpb, router_g_w, router_g_b, router_e_w, router_e_b, moe_w_gate, moe_w_up, moe_w_down):
    P = dict(w_in=w_in, w_out=w_out, rwkv_mu=rwkv_mu, rwkv_w0=rwkv_w0, rwkv_w_up=rwkv_w_up,
             rwkv_a0=rwkv_a0, rwkv_a_up=rwkv_a_up, rwkv_g_up=rwkv_g_up, rwkv_k_k=rwkv_k_k,
             rwkv_k_a=rwkv_k_a, rwkv_r_k=rwkv_r_k, rwkv_ln=rwkv_ln, pool_w=pool_w, pool_scale=pool_scale,
             mlstm_conv=mlstm_conv, mlstm_gate_b=mlstm_gate_b, mlstm_norm=mlstm_norm,
             na_qk_norm=na_qk_norm, na_rpb=na_rpb, router_g_w=router_g_w, router_g_b=router_g_b,
             router_e_w=router_e_w, router_e_b=router_e_b, moe_w_gate=moe_w_gate, moe_w_up=moe_w_up,
             moe_w_down=moe_w_down)
    B, T, D = x.shape
    Lc = ctx.shape[1]
    depth = ada_w.shape[0]
    rope = rope_tables(T)
    moe_w = (moe_w_gate, moe_w_up, moe_w_down)

    cvec = jnp.concatenate([c, c_ctx[None, :], jnp.zeros((SUBLANES - B - 1, D), F32)], axis=0)
    mods = ada_modulation(cvec, ada_w, ada_b)

    h_lat, h_ctx = x, ctx
    for l in range(depth):
        last = l == depth - 1
        w = _layer_weights(l, P)
        m = mods[l].reshape(SUBLANES, 6, D)
        m_lat = [m[:B, i][:, None, :] for i in range(6)]
        m_ctx = [jnp.broadcast_to(m[B, i][None, None, :], (B, 1, D)) for i in range(6)]

        u_lat = norm_modulate(h_lat, norm_mix[l], m_lat[0], m_lat[1])
        u_ctx = norm_modulate(h_ctx, norm_mix[l], m_ctx[0], m_ctx[1])
        ys_ctx, ys_lat = token_mixers(u_ctx, u_lat, w, rope, not last)
        h_lat = out_proj_residual(ys_lat, w["out"], h_lat, m_lat[2])
        n_rows = B * T if last else B * (T + Lc)
        streams = [(h_lat, m_lat[3], m_lat[4])]
        if not last:
            h_ctx = out_proj_residual(ys_ctx, w["out"], h_ctx, m_ctx[2])
            streams.append((h_ctx, m_ctx[3], m_ctx[4]))
        rows, ids, wts = ffn_norm_router(streams, norm_ffn[l], w["router_w"], w["router_b"])
        f2 = hier_moe(rows, ids[:, :TOP_K], wts[:, :TOP_K], *moe_w, l)
        if not last:
            h_ctx = moe_combine(h_ctx, f2, m_ctx[5], B * T, n_rows)
        h_lat = moe_combine(h_lat, f2, m_lat[5], 0, n_rows)
    return h_lat
```

```python
import functools
import math

import numpy as np
import jax
import jax.numpy as jnp
from jax import lax
from jax.experimental import pallas as pl
from jax.experimental.pallas import tpu as pltpu

F32 = jnp.float32
BF16 = jnp.bfloat16

D_MODEL = 4096
DEPTH = 2
GRID_W = 64
EPS = 1e-6
GROUP_W = 1024

RWKV_HEAD = 64
RWKV_DECAY_RANK = 64
RWKV_ICLR_RANK = 64
RWKV_GATE_RANK = 160
RWKV_GN_EPS = 64e-5
RWKV_COLS = 3 * GROUP_W + 2 * RWKV_DECAY_RANK + 2 * RWKV_ICLR_RANK + RWKV_GATE_RANK
RWKV_PAD_COLS = 3 * GROUP_W + 256 + 256

POOL_WINDOWS = (2, 4, 8, 16)
POOL_GROUP = 256
POOL_HALO = 8

MLSTM_HEADS = 4
MLSTM_DV = 256
MLSTM_DK = 128
MLSTM_COLS = 2 * MLSTM_HEADS * MLSTM_DK + 2 * GROUP_W + 4 * MLSTM_HEADS
MLSTM_GATE_COL = 2 * MLSTM_HEADS * MLSTM_DK + 2 * GROUP_W
MLSTM_PAD_COLS = MLSTM_GATE_COL + 512
ROPE_BASE = 10000.0

NA_HEADS = 8
NA_HEAD = 128
WIN_H = 8
WIN_W = 16
NA_COLS = 3 * GROUP_W
NA_QROWS = 4
NA_KROWS = 3 * NA_QROWS

N_GROUPS = 4
EXPERTS_PER_GROUP = 4
N_EXPERTS = 16
TOP_K = 2
EXPERT_FF = 1024
MOE_BLOCK = 256
MOE_FF_TILE = 256
MOE_OUT_TILE = 1024
MOE_STAGE_BYTES = 2 * 1024 * 1024

LANES = 128
SUBLANES = 8
VMEM_LIMIT = 56 * 1024 * 1024
NEG = -1e30

RWKV_CHUNK = 64
RWKV_PAIRS = 8
MLSTM_CHUNK = 256


def _cp(*sem):
    return pltpu.CompilerParams(dimension_semantics=sem, vmem_limit_bytes=VMEM_LIMIT)


def _dot(a, b):
    return jnp.dot(a.astype(BF16), b.astype(BF16), preferred_element_type=F32)


def _dot_nt(a, b):
    return lax.dot_general(a.astype(BF16), b.astype(BF16), (((1,), (1,)), ((), ())),
                           preferred_element_type=F32)


def _dot_tn(a, b):
    return lax.dot_general(a.astype(BF16), b.astype(BF16), (((0,), (0,)), ((), ())),
                           preferred_element_type=F32)


def _split2(x):
    hi = x.astype(BF16)
    lo = (x - hi.astype(F32)).astype(BF16)
    return hi, lo


def _dot3(a, b, dot=_dot):
    ah, al = _split2(a)
    bh, bl = _split2(b)
    return dot(ah, bh) + (dot(ah, bl) + dot(al, bh))


def _dot_exact_lhs(a_bf16, b):
    b0 = b.astype(BF16)
    r1 = b - b0.astype(F32)
    b1 = r1.astype(BF16)
    b2 = (r1 - b1.astype(F32)).astype(BF16)
    return _dot(a_bf16, b0) + (_dot(a_bf16, b1) + _dot(a_bf16, b2))


def _sigmoid(x):
    return 1.0 / (1.0 + jnp.exp(-x))


def _silu(x):
    return x * _sigmoid(x)


def _log_sigmoid(x):
    return jnp.minimum(x, 0.0) - jnp.log(1.0 + jnp.exp(-jnp.abs(x)))


def _pack_bf16_pair(lo, hi):
    lo_bits = lax.bitcast_convert_type(lo.astype(BF16).astype(F32), jnp.uint32)
    hi_bits = lax.bitcast_convert_type(hi.astype(BF16).astype(F32), jnp.uint32)
    return (lo_bits >> 16) | hi_bits


def _unpack_bf16_pair(w):
    lo = lax.bitcast_convert_type(w << 16, F32)
    hi = lax.bitcast_convert_type(w & jnp.uint32(0xFFFF0000), F32)
    return lo, hi


def _head_sum(x, width):
    r = lax.broadcasted_iota(jnp.int32, (LANES, LANES), 0) // width
    c = lax.broadcasted_iota(jnp.int32, (LANES, LANES), 1) // width
    e = jnp.where(r == c, 1.0, 0.0).astype(BF16)
    outs = []
    for j in range(x.shape[1] // LANES):
        hi, lo = _split2(x[:, j * LANES:(j + 1) * LANES])
        outs.append(_dot(hi, e) + _dot(lo, e))
    return jnp.concatenate(outs, axis=1)


def _ada_kernel(c_ref, w_ref, b_ref, o_ref):
    s = _silu(c_ref[...])
    o_ref[0] = _dot(s, w_ref[0]) + b_ref[0]


def ada_modulation(cvec, ada_w, ada_b):
    L, D, N = ada_w.shape
    tn = 1024
    return pl.pallas_call(
        _ada_kernel,
        grid=(L, N // tn),
        in_specs=[pl.BlockSpec((SUBLANES, D), lambda l, j: (0, 0)),
                  pl.BlockSpec((1, D, tn), lambda l, j: (l, 0, j)),
                  pl.BlockSpec((1, 1, tn), lambda l, j: (l, 0, j))],
        out_specs=pl.BlockSpec((1, SUBLANES, tn), lambda l, j: (l, 0, j)),
        out_shape=jax.ShapeDtypeStruct((L, SUBLANES, N), F32),
        compiler_params=_cp("parallel", "parallel"),
        name="ada_modulation",
    )(cvec, ada_w, ada_b.reshape(L, 1, N))


def _normmod_kernel(h_ref, g_ref, sh_ref, sc_ref, o_ref):
    x = h_ref[0]
    y = x * lax.rsqrt(jnp.mean(x * x, axis=-1, keepdims=True) + EPS) * g_ref[...]
    o_ref[0] = (y * (1.0 + sc_ref[0]) + sh_ref[0]).astype(o_ref.dtype)


def norm_modulate(h, g, shift, scale):
    B, T, D = h.shape
    tm = 256
    vec = pl.BlockSpec((1, 1, D), lambda b, i: (b, 0, 0))
    return pl.pallas_call(
        _normmod_kernel,
        grid=(B, T // tm),
        in_specs=[pl.BlockSpec((1, tm, D), lambda b, i: (b, i, 0)),
                  pl.BlockSpec((1, D), lambda b, i: (0, 0)), vec, vec],
        out_specs=pl.BlockSpec((1, tm, D), lambda b, i: (b, i, 0)),
        out_shape=jax.ShapeDtypeStruct((B, T, D), BF16),
        compiler_params=_cp("parallel", "parallel"),
        name="norm_modulate",
    )(h, g.reshape(1, D), shift, scale)


def _mm_kernel(x_ref, w_ref, o_ref):
    o_ref[0] = jnp.dot(x_ref[0], w_ref[...], preferred_element_type=F32).astype(o_ref.dtype)


def _pick_tile(n, cands):
    for t in cands:
        if n % t == 0:
            return t
    raise ValueError(n)


def matmul(x, w, out_dtype=F32):
    B, T, K = x.shape
    N = w.shape[1]
    tm = _pick_tile(T, (1024, 512, 256))
    tn = _pick_tile(N, (1024, 512))
    return pl.pallas_call(
        _mm_kernel,
        grid=(B, T // tm, N // tn),
        in_specs=[pl.BlockSpec((1, tm, K), lambda b, i, j: (b, i, 0)),
                  pl.BlockSpec((K, tn), lambda b, i, j: (0, j))],
        out_specs=pl.BlockSpec((1, tm, tn), lambda b, i, j: (b, i, j)),
        out_shape=jax.ShapeDtypeStruct((B, T, N), out_dtype),
        compiler_params=_cp("parallel", "parallel", "arbitrary"),
        name="in_proj",
    )(x, w)


def _outproj_kernel(ya_ref, yb_ref, yc_ref, yd_ref, w_ref, h_ref, gate_ref, o_ref):
    acc = jnp.dot(ya_ref[0], w_ref[0:GROUP_W, :], preferred_element_type=F32)
    acc += jnp.dot(yb_ref[0], w_ref[GROUP_W:2 * GROUP_W, :], preferred_element_type=F32)
    acc += jnp.dot(yc_ref[0], w_ref[2 * GROUP_W:3 * GROUP_W, :], preferred_element_type=F32)
    acc += jnp.dot(yd_ref[0], w_ref[3 * GROUP_W:4 * GROUP_W, :], preferred_element_type=F32)
    o_ref[0] = h_ref[0] + gate_ref[0] * acc


def out_proj_residual(ys, w, h, gate):
    B, T, D = h.shape
    tm = _pick_tile(T, (512, 256))
    tn = 1024
    ysp = pl.BlockSpec((1, tm, GROUP_W), lambda b, i, j: (b, i, 0))
    return pl.pallas_call(
        _outproj_kernel,
        grid=(B, T // tm, D // tn),
        in_specs=[ysp, ysp, ysp, ysp,
                  pl.BlockSpec((D, tn), lambda b, i, j: (0, j)),
                  pl.BlockSpec((1, tm, tn), lambda b, i, j: (b, i, j)),
                  pl.BlockSpec((1, 1, tn), lambda b, i, j: (b, 0, j))],
        out_specs=pl.BlockSpec((1, tm, tn), lambda b, i, j: (b, i, j)),
        out_shape=jax.ShapeDtypeStruct((B, T, D), F32),
        compiler_params=_cp("parallel", "parallel", "arbitrary"),
        name="out_proj",
    )(*ys, w, h, gate)


def _halo_specs(tm, width, halo=SUBLANES):
    per = tm // halo

    def prev_map(b, i):
        return (b, jnp.maximum(i * per - 1, 0), 0)

    def next_map(nb):
        return lambda b, i: (b, jnp.minimum((i + 1) * per, nb - 1), 0)

    return prev_map, next_map


def _shift_rows(x, prev_row, next_row):
    tm = x.shape[0]
    row = lax.broadcasted_iota(jnp.int32, x.shape, 0)
    prev = jnp.where(row == 0, prev_row, pltpu.roll(x, 1, axis=0))
    nxt = jnp.where(row == tm - 1, next_row, pltpu.roll(x, tm - 1, axis=0))
    return prev, nxt


def _rwkv_prep_kernel(p_ref, pp_ref, pn_ref, mu_ref, wup_ref, w0_ref, aup_ref, a0_ref, gup_ref,
                      kk_w_ref, ka_w_ref, rk_w_ref,
                      r_ref, kk_ref, v_ref, bonus_ref, g_ref, lw_ref, kd_ref, bb_ref):
    i = pl.program_id(1)
    last = pl.num_programs(1) - 1
    x = p_ref[0]
    prev_row = jnp.where(i == 0, 0.0, pp_ref[0, SUBLANES - 1:SUBLANES, :])
    next_row = jnp.where(i == last, 0.0, pn_ref[0, 0:1, :])
    prev, nxt = _shift_rows(x, prev_row, next_row)
    x = x + mu_ref[0:1, :] * (prev - x) + mu_ref[1:2, :] * (nxt - x)

    r = x[:, 0:GROUP_W]
    k = x[:, GROUP_W:2 * GROUP_W]
    v = x[:, 2 * GROUP_W:3 * GROUP_W]
    low = x[:, 3 * GROUP_W:3 * GROUP_W + 256]
    gd = x[:, 3 * GROUP_W + 256:3 * GROUP_W + 512]

    zw = _dot(jnp.tanh(low[:, 0:LANES]), wup_ref[...]) + w0_ref[...]
    za = _dot(low[:, LANES:2 * LANES], aup_ref[...]) + a0_ref[...]
    kkr = k * kk_w_ref[...]
    kk = kkr * lax.rsqrt(_head_sum(kkr * kkr, RWKV_HEAD) + EPS)
    r_ref[0] = r.astype(BF16)
    kk_ref[0] = kk.astype(BF16)
    v_ref[0] = v.astype(BF16)
    bonus_ref[0] = (_head_sum(r * k * rk_w_ref[...], RWKV_HEAD) * v).astype(BF16)
    g_ref[0] = _dot(_sigmoid(gd), gup_ref[...]).astype(BF16)
    for d in range(2):
        sl = slice(d * GROUP_W, (d + 1) * GROUP_W)
        lw_ref[d, 0] = (-math.exp(-0.5)) * _sigmoid(zw[:, sl])
        a = _sigmoid(za[:, sl])
        kd_ref[d, 0] = (k * (1.0 + (a - 1.0) * ka_w_ref[...])).astype(BF16)
        bb_ref[d, 0] = (kk * a).astype(BF16)


def rwkv_prep(p, w):
    B, T, W = p.shape
    tm = 128
    prev_map, next_map = _halo_specs(tm, W)
    nb8 = T // SUBLANES
    full = lambda shape: pl.BlockSpec(shape, lambda b, i: tuple(0 for _ in shape))
    tok = pl.BlockSpec((1, tm, GROUP_W), lambda b, i: (b, i, 0))
    tokd = pl.BlockSpec((2, 1, tm, GROUP_W), lambda b, i: (0, b, i, 0))
    sh = jax.ShapeDtypeStruct((B, T, GROUP_W), BF16)
    shd = jax.ShapeDtypeStruct((2, B, T, GROUP_W), BF16)
    shd_f32 = jax.ShapeDtypeStruct((2, B, T, GROUP_W), F32)
    return pl.pallas_call(
        _rwkv_prep_kernel,
        grid=(B, T // tm),
        in_specs=[pl.BlockSpec((1, tm, W), lambda b, i: (b, i, 0)),
                  pl.BlockSpec((1, SUBLANES, W), prev_map),
                  pl.BlockSpec((1, SUBLANES, W), next_map(nb8)),
                  full((2, W)), full((LANES, 2 * GROUP_W)), full((1, 2 * GROUP_W)),
                  full((LANES, 2 * GROUP_W)), full((1, 2 * GROUP_W)), full((256, GROUP_W)),
                  full((1, GROUP_W)), full((1, GROUP_W)), full((1, GROUP_W))],
        out_specs=[tok, tok, tok, tok, tok, tokd, tokd, tokd],
        out_shape=[sh, sh, sh, sh, sh, shd_f32, shd, shd],
        compiler_params=_cp("parallel", "parallel"),
        name="rwkv_prep",
    )(p, p, p, w["mu"], w["w_up"], w["w0"], w["a_up"], w["a0"], w["g_up"],
      w["k_k"], w["k_a"], w["r_k"])


def _dform(x, m0):
    return jnp.concatenate([jnp.where(m0, x, 0.0), jnp.where(m0, 0.0, x)], axis=0)


def _rwkv_chunk(tiles, sts, rev):
    n = len(tiles)
    C = tiles[0][0].shape[0]
    C2 = 2 * C
    sgn = jnp.where(rev, -1, 1)
    rr = lax.broadcasted_iota(jnp.int32, (C, C), 0)
    cc = lax.broadcasted_iota(jnp.int32, (C, C), 1)
    tri = jnp.where((rr - cc) * sgn >= 0, 1.0, 0.0).astype(BF16)
    lane = lax.broadcasted_iota(jnp.int32, (C, LANES), 1)
    m0 = lane < RWKV_HEAD
    rd = lax.broadcasted_iota(jnp.int32, (C2, C2), 0)
    cd = lax.broadcasted_iota(jnp.int32, (C2, C2), 1)
    same = (rd // C) == (cd // C)
    ahead = (rd % C - cd % C) * sgn
    strict = same & (ahead > 0)
    incl = same & (ahead >= 0)
    diag = rd == cd
    each = range(n)

    lins = [_dot_exact_lhs(tri, t[3]) for t in tiles]
    ltots = [jnp.sum(t[3], axis=0, keepdims=True) for t in tiles]
    a_d, r_d, k_d, b_d, v_d, kw_d, bw_d = [], [], [], [], [], [], []
    for (r, kk, v, lw, kd, bb), lin, ltot in zip(tiles, lins, ltots):
        einv = jnp.exp(-lin)
        ew = jnp.exp(ltot - lin)
        a_d.append(_dform(-kk * jnp.exp(lin - lw), m0).astype(BF16))
        r_d.append(_dform(r * jnp.exp(lin), m0).astype(BF16))
        k_d.append(_dform(kd * einv, m0).astype(BF16))
        b_d.append(_dform(bb * einv, m0).astype(BF16))
        v_d.append(_dform(v, m0).astype(BF16))
        kw_d.append(_dform(kd * ew, m0).astype(BF16))
        bw_d.append(_dform(bb * ew, m0).astype(BF16))

    grams = [_dot_nt(jnp.concatenate([a_d[i], r_d[i]], axis=0), jnp.concatenate([k_d[i], b_d[i]], axis=0))
             for i in each]
    a_ak = [jnp.where(strict, g[:C2, :C2], 0.0).astype(BF16) for g in grams]
    a_ab = [jnp.where(strict, g[:C2, C2:], 0.0) for g in grams]
    a_rk = [jnp.where(incl, g[C2:, :C2], 0.0).astype(BF16) for g in grams]
    a_rb = [jnp.where(incl, g[C2:, C2:], 0.0).astype(BF16) for g in grams]

    tinv = [jnp.where(diag, 1.0, 0.0) + a for a in a_ab]
    apow = a_ab
    for _ in range(int(math.log2(C)) - 1):
        apow = [_dot(a, a) for a in apow]
        tinv = [t + _dot(t, a) for t, a in zip(tinv, apow)]

    akv = [_dot(a_ak[i], v_d[i]) for i in each]
    p = [_dot(tinv[i], jnp.concatenate([a_d[i], akv[i].astype(BF16)], axis=1)) for i in each]
    ur = [_dot(jnp.concatenate([p[i][:, :LANES].astype(BF16), r_d[i]], axis=0), sts[i]) for i in each]
    u = [ur[i][:C2] + p[i][:, LANES:] for i in each]
    y2 = [ur[i][C2:] + _dot(a_rk[i], v_d[i]) + _dot(a_rb[i], u[i]) for i in each]
    ys = [y[:C] + y[C:] for y in y2]
    new = []
    for i in each:
        wc = jnp.broadcast_to(jnp.exp(ltots[i]), (LANES, LANES))
        wcol = jnp.sum(jnp.where(diag, wc, 0.0), axis=1, keepdims=True)
        new.append(sts[i] * wcol + _dot_tn(jnp.concatenate([bw_d[i], kw_d[i]], axis=0),
                                           jnp.concatenate([u[i].astype(BF16), v_d[i]], axis=0)))
    return ys, new


def _rwkv_scan_kernel(r_ref, kk_ref, v_ref, lw_ref, kd_ref, bb_ref, s0_ref, y_ref, sT_ref, st_scr,
                      *, npairs):
    d = pl.program_id(0)
    c = pl.program_id(3)
    nc = pl.num_programs(3)

    @pl.when(c == 0)
    def _():
        st_scr[...] = s0_ref[0, 0]

    lanes = [slice(j * LANES, (j + 1) * LANES) for j in range(npairs)]
    f32 = lambda a: a.astype(F32)
    tiles = [(f32(r_ref[0, :, sl]), f32(kk_ref[0, :, sl]), f32(v_ref[0, :, sl]),
              lw_ref[0, 0, :, sl], f32(kd_ref[0, 0, :, sl]), f32(bb_ref[0, 0, :, sl])) for sl in lanes]
    ys, new = _rwkv_chunk(tiles, [st_scr[j] for j in range(npairs)], d == 1)
    for j, sl in enumerate(lanes):
        y_ref[0, 0, :, sl] = ys[j]
        st_scr[j] = new[j]

    @pl.when(c == nc - 1)
    def _():
        sT_ref[0, 0] = st_scr[...]


def rwkv_scan(f, s0, npairs=RWKV_PAIRS):
    r, kk, v, lw, kd, bb = f["r"], f["kk"], f["v"], f["lw"], f["kd"], f["bb"]
    B, T, _ = r.shape
    C = RWKV_CHUNK
    nc = T // C
    ng = RWKV_PAIRS // npairs
    W = npairs * LANES
    chunk = lambda d, c: c + d * (nc - 1 - 2 * c)
    tok = pl.BlockSpec((1, C, W), lambda d, b, g, c: (b, chunk(d, c), g))
    tokd = pl.BlockSpec((1, 1, C, W), lambda d, b, g, c: (d, b, chunk(d, c), g))
    stsp = pl.BlockSpec((1, 1, npairs, LANES, LANES), lambda d, b, g, c: (d, b, g, 0, 0))
    return pl.pallas_call(
        functools.partial(_rwkv_scan_kernel, npairs=npairs),
        grid=(2, B, ng, nc),
        in_specs=[tok, tok, tok, tokd, tokd, tokd, stsp],
        out_specs=[tokd, stsp],
        out_shape=[jax.ShapeDtypeStruct((2, B, T, GROUP_W), F32),
                   jax.ShapeDtypeStruct(s0.shape, F32)],
        scratch_shapes=[pltpu.VMEM((npairs, LANES, LANES), F32)],
        compiler_params=_cp("parallel", "parallel", "parallel", "arbitrary"),
        name="rwkv_scan",
    )(r, kk, v, lw, kd, bb, s0)


def _rwkv_readout_kernel(yf_ref, yb_ref, bonus_ref, g_ref, ln_ref, o_ref):
    y = yf_ref[0, 0] + yb_ref[0, 0]
    mean = _head_sum(y, RWKV_HEAD) * (1.0 / RWKV_HEAD)
    cen = y - mean
    var = _head_sum(cen * cen, RWKV_HEAD) * (1.0 / RWKV_HEAD)
    yn = cen * lax.rsqrt(var + RWKV_GN_EPS) * ln_ref[0:1, :] + ln_ref[1:2, :]
    o_ref[0] = ((yn + bonus_ref[0].astype(F32)) * g_ref[0].astype(F32)).astype(o_ref.dtype)


def rwkv_readout(y, bonus, g, ln):
    _, B, T, W = y.shape
    tm = 256
    tok = pl.BlockSpec((1, tm, W), lambda b, i: (b, i, 0))
    return pl.pallas_call(
        _rwkv_readout_kernel,
        grid=(B, T // tm),
        in_specs=[pl.BlockSpec((1, 1, tm, W), lambda b, i: (0, b, i, 0)),
                  pl.BlockSpec((1, 1, tm, W), lambda b, i: (1, b, i, 0)),
                  tok, tok, pl.BlockSpec((2, W), lambda b, i: (0, 0))],
        out_specs=tok,
        out_shape=jax.ShapeDtypeStruct((B, T, W), BF16),
        compiler_params=_cp("parallel", "parallel"),
        name="rwkv_readout",
    )(y, y, bonus, g, ln)


def _pool_kernel(p_ref, pp_ref, pn_ref, w_ref, sc_ref, o_ref, *, total):
    i = pl.program_id(1)
    last = pl.num_programs(1) - 1
    x = p_ref[0]
    tm = x.shape[0]
    n = tm + 2 * POOL_HALO
    ext = jnp.concatenate([jnp.where(i == 0, 0.0, pp_ref[0]), x,
                           jnp.where(i == last, 0.0, pn_ref[0])], axis=0)
    t = i * tm + lax.broadcasted_iota(jnp.int32, (tm, 1), 0)
    outs = []
    for gi, win in enumerate(POOL_WINDOWS):
        sl = slice(gi * POOL_GROUP, (gi + 1) * POOL_GROUP)
        e = ext[:, sl]
        s = e + pltpu.roll(e, 1, axis=0)
        step = 1
        while 2 * step < win:
            s = pltpu.roll(s, step, axis=0) + pltpu.roll(s, n - step, axis=0)
            step *= 2
        h = win // 2
        cnt = (jnp.minimum(t + h, total) - jnp.maximum(t - h, 0)).astype(F32)
        z = s[POOL_HALO:POOL_HALO + tm] / cnt - x[:, sl]
        outs.append(_dot(z, w_ref[gi]))
    o_ref[0] = (jnp.concatenate(outs, axis=1) * sc_ref[...]).astype(o_ref.dtype)


def pool_mixer(p, pool_w, pool_scale):
    B, T, W = p.shape
    tm = 256
    prev_map, next_map = _halo_specs(tm, W)
    return pl.pallas_call(
        functools.partial(_pool_kernel, total=T),
        grid=(B, T // tm),
        in_specs=[pl.BlockSpec((1, tm, W), lambda b, i: (b, i, 0)),
                  pl.BlockSpec((1, POOL_HALO, W), prev_map),
                  pl.BlockSpec((1, POOL_HALO, W), next_map(T // POOL_HALO)),
                  pl.BlockSpec(pool_w.shape, lambda b, i: (0, 0, 0)),
                  pl.BlockSpec((1, W), lambda b, i: (0, 0))],
        out_specs=pl.BlockSpec((1, tm, W), lambda b, i: (b, i, 0)),
        out_shape=jax.ShapeDtypeStruct((B, T, W), BF16),
        compiler_params=_cp("parallel", "parallel"),
        name="pool_mixer",
    )(p, p, p, pool_w, pool_scale.reshape(1, W))


def _mlstm_prep_kernel(p_ref, pp_ref, pn_ref, g_ref, cw_ref, gb_ref, *rest, rope):
    if rope:
        cos_ref, sin_ref, q_ref, k_ref, go_ref = rest
    else:
        q_ref, k_ref, go_ref = rest
    i = pl.program_id(1)
    last = pl.num_programs(1) - 1
    x = p_ref[0]
    prev_row = jnp.where(i == 0, 0.0, pp_ref[0, SUBLANES - 1:SUBLANES, :])
    next_row = jnp.where(i == last, 0.0, pn_ref[0, 0:1, :])
    prev, nxt = _shift_rows(x, prev_row, next_row)
    qk = _silu(cw_ref[0:1, :] * prev + cw_ref[1:2, :] * x + cw_ref[2:3, :] * nxt)
    if rope:
        w = qk.shape[1]
        lane = lax.broadcasted_iota(jnp.int32, qk.shape, 1)
        partner = jnp.where(lane % 2 == 0, pltpu.roll(qk, w - 1, axis=1), pltpu.roll(qk, 1, axis=1))
        reps = w // LANES
        cos = jnp.concatenate([cos_ref[...]] * reps, axis=1)
        sin = jnp.concatenate([sin_ref[...]] * reps, axis=1)
        qk = qk * cos + partner * sin
    half = MLSTM_HEADS * MLSTM_DK
    q_ref[0] = qk[:, :half].astype(BF16)
    k_ref[0] = (qk[:, half:] * (MLSTM_DK ** -0.5)).astype(BF16)
    g = g_ref[0] + gb_ref[...]
    lane = lax.broadcasted_iota(jnp.int32, g.shape, 1)
    go_ref[0] = jnp.where((lane // MLSTM_HEADS) % 2 == 1, _log_sigmoid(g), g)


def mlstm_prep(p, conv_w, gate_b, rope_tabs):
    B, T, W = p.shape
    tm = 256
    QK = 2 * MLSTM_HEADS * MLSTM_DK
    prev_map, next_map = _halo_specs(tm, QK)
    rope = rope_tabs is not None
    in_specs = [pl.BlockSpec((1, tm, QK), lambda b, i: (b, i, 0)),
                pl.BlockSpec((1, SUBLANES, QK), prev_map),
                pl.BlockSpec((1, SUBLANES, QK), next_map(T // SUBLANES)),
                pl.BlockSpec((1, tm, LANES), lambda b, i: (b, i, MLSTM_GATE_COL // LANES)),
                pl.BlockSpec((3, QK), lambda b, i: (0, 0)),
                pl.BlockSpec((1, LANES), lambda b, i: (0, 0))]
    args = [p, p, p, p, conv_w, gate_b]
    if rope:
        in_specs += [pl.BlockSpec((tm, LANES), lambda b, i: (i, 0))] * 2
        args += list(rope_tabs)
    half = MLSTM_HEADS * MLSTM_DK
    return pl.pallas_call(
        functools.partial(_mlstm_prep_kernel, rope=rope),
        grid=(B, T // tm),
        in_specs=in_specs,
        out_specs=[pl.BlockSpec((1, tm, half), lambda b, i: (b, i, 0)),
                   pl.BlockSpec((1, tm, half), lambda b, i: (b, i, 0)),
                   pl.BlockSpec((1, tm, LANES), lambda b, i: (b, i, 0))],
        out_shape=[jax.ShapeDtypeStruct((B, T, half), BF16),
                   jax.ShapeDtypeStruct((B, T, half), BF16),
                   jax.ShapeDtypeStruct((B, T, LANES), F32)],
        compiler_params=_cp("parallel", "parallel"),
        name="mlstm_prep",
    )(*args)


def _mlstm_chunk_kernel(q_ref, k_ref, v_ref, gc_ref, gr_ref, c0_ref, n0_ref, m0_ref,
                        h_ref, cT_ref, nT_ref, mT_ref, c_scr, n_scr, m_scr):
    d = pl.program_id(0)
    c = pl.program_id(2)
    nc = pl.num_programs(2)
    H = MLSTM_HEADS

    @pl.when(c == 0)
    def _():
        c_scr[...] = c0_ref[0, 0]
        n_scr[...] = n0_ref[0, 0]
        m_scr[...] = m0_ref[0, 0]

    rev = d == 1
    L = q_ref.shape[1]
    rr = lax.broadcasted_iota(jnp.int32, (L, L), 0)
    cc = lax.broadcasted_iota(jnp.int32, (L, L), 1)
    sgn = jnp.where(rev, -1, 1)
    seen = (rr - cc) * sgn >= 0
    tri = jnp.where(seen, 1.0, 0.0).astype(BF16)
    tri_t = jnp.where((cc - rr) * sgn >= 0, 1.0, 0.0).astype(BF16)

    gc = gc_ref[0]
    gr = gr_ref[0]
    lane = lax.broadcasted_iota(jnp.int32, gc.shape, 1)
    subl = lax.broadcasted_iota(jnp.int32, gr.shape, 0)
    pick_c = lambda a, idx: jnp.sum(jnp.where(lane == idx, a, 0.0), axis=1, keepdims=True)
    pick_r = lambda a, idx: jnp.sum(jnp.where(subl == idx, a, 0.0), axis=0, keepdims=True)
    bc_all = _dot_exact_lhs(tri, gc)
    gr_hi = gr.astype(BF16)
    gr_r1 = gr - gr_hi.astype(F32)
    gr_mid = gr_r1.astype(BF16)
    gr_lo = (gr_r1 - gr_mid.astype(F32)).astype(BF16)
    br_all = _dot(gr_hi, tri_t) + (_dot(gr_mid, tri_t) + _dot(gr_lo, tri_t))

    heads = range(H)
    i_lane = [d * (2 * H) + j for j in heads]
    f_lane = [d * (2 * H) + H + j for j in heads]
    ig_c = [pick_c(gc, i_lane[j]) for j in heads]
    lf_c = [pick_c(gc, f_lane[j]) for j in heads]
    b_c = [pick_c(bc_all, f_lane[j]) for j in heads]
    ig_r = [pick_r(gr, i_lane[j]) for j in heads]
    b_r = [pick_r(br_all, f_lane[j]) for j in heads]
    m_prev = [m_scr[j] for j in heads]
    q = [q_ref[0, :, j * MLSTM_DK:(j + 1) * MLSTM_DK] for j in heads]
    k = [k_ref[0, :, j * MLSTM_DK:(j + 1) * MLSTM_DK] for j in heads]
    v = [v_ref[0, :, j * MLSTM_DV:(j + 1) * MLSTM_DV].astype(BF16) for j in heads]
    cst = [c_scr[j] for j in heads]
    nst = [n_scr[j] for j in heads]

    qk = [_dot_nt(q[j], k[j]) for j in heads]
    qc = [_dot(q[j], cst[j]) for j in heads]
    dlog = [jnp.where(seen, b_c[j] - b_r[j] + ig_r[j], NEG) for j in heads]
    inter = [b_c[j] + m_prev[j] for j in heads]
    m_t = [jnp.maximum(inter[j], jnp.max(dlog[j], axis=1, keepdims=True)) for j in heads]
    s = [qk[j] * jnp.exp(dlog[j] - m_t[j]) for j in heads]
    w_inter = [jnp.exp(inter[j] - m_t[j]) for j in heads]
    sv = [_dot(s[j], v[j]) for j in heads]
    for j in heads:
        num = sv[j] + w_inter[j] * qc[j]
        den = (jnp.sum(s[j], axis=1, keepdims=True)
               + w_inter[j] * jnp.sum(q[j].astype(F32) * nst[j], axis=1, keepdims=True))
        h_ref[0, 0, :, j * MLSTM_DV:(j + 1) * MLSTM_DV] = num / jnp.maximum(jnp.abs(den), jnp.exp(-m_t[j]))

    b_end = [jnp.sum(lf_c[j], axis=0, keepdims=True) for j in heads]
    g_s = [b_end[j] - b_c[j] + ig_c[j] for j in heads]
    m_new = [jnp.maximum(b_end[j] + m_prev[j], jnp.max(g_s[j], axis=0, keepdims=True)) for j in heads]
    kw = [k[j].astype(F32) * jnp.exp(g_s[j] - m_new[j]) for j in heads]
    kv = [_dot_tn(kw[j], v[j]) for j in heads]
    for j in heads:
        decay = jnp.exp(b_end[j] + m_prev[j] - m_new[j])
        c_scr[j] = decay * cst[j] + kv[j]
        n_scr[j] = decay * nst[j] + jnp.sum(kw[j], axis=0, keepdims=True)
        m_scr[j] = m_new[j]

    @pl.when(c == nc - 1)
    def _():
        cT_ref[0, 0] = c_scr[...]
        nT_ref[0, 0] = n_scr[...]
        mT_ref[0, 0] = m_scr[...]


def mlstm_scan(q, k, p, gates, gates_t, state):
    B, T, _ = q.shape
    L = min(MLSTM_CHUNK, T)
    nc = T // L
    H = MLSTM_HEADS
    QK = H * MLSTM_DK
    c0, n0, m0 = state
    chunk = lambda d, c: c + d * (nc - 1 - 2 * c)
    v_blk = (2 * QK) // GROUP_W
    csp = pl.BlockSpec((1, 1, H, MLSTM_DK, MLSTM_DV), lambda d, b, c: (d, b, 0, 0, 0))
    nsp = pl.BlockSpec((1, 1, H, 1, MLSTM_DK), lambda d, b, c: (d, b, 0, 0, 0))
    msp = pl.BlockSpec((1, 1, H, 1, 1), lambda d, b, c: (d, b, 0, 0, 0))
    return pl.pallas_call(
        _mlstm_chunk_kernel,
        grid=(2, B, nc),
        in_specs=[pl.BlockSpec((1, L, QK), lambda d, b, c: (b, chunk(d, c), 0)),
                  pl.BlockSpec((1, L, QK), lambda d, b, c: (b, chunk(d, c), 0)),
                  pl.BlockSpec((1, L, GROUP_W), lambda d, b, c: (b, chunk(d, c), v_blk)),
                  pl.BlockSpec((1, L, LANES), lambda d, b, c: (b, chunk(d, c), 0)),
                  pl.BlockSpec((1, 4 * H, L), lambda d, b, c: (b, 0, chunk(d, c))),
                  csp, nsp, msp],
        out_specs=[pl.BlockSpec((1, 1, L, GROUP_W), lambda d, b, c: (d, b, chunk(d, c), 0)),
                   csp, nsp, msp],
        out_shape=[jax.ShapeDtypeStruct((2, B, T, GROUP_W), F32),
                   jax.ShapeDtypeStruct(c0.shape, F32),
                   jax.ShapeDtypeStruct(n0.shape, F32),
                   jax.ShapeDtypeStruct(m0.shape, F32)],
        scratch_shapes=[pltpu.VMEM((H, MLSTM_DK, MLSTM_DV), F32),
                        pltpu.VMEM((H, 1, MLSTM_DK), F32),
                        pltpu.VMEM((H, 1, 1), F32)],
        compiler_params=_cp("parallel", "parallel", "arbitrary"),
        name="mlstm_scan",
    )(q, k, p, gates, gates_t, c0, n0, m0)


def _mlstm_readout_kernel(hf_ref, hb_ref, o_ref, nw_ref, out_ref):
    h = hf_ref[0, 0] + hb_ref[0, 0]
    outs = []
    for j in range(MLSTM_HEADS):
        hj = h[:, j * MLSTM_DV:(j + 1) * MLSTM_DV]
        outs.append(hj * lax.rsqrt(jnp.mean(hj * hj, axis=-1, keepdims=True) + EPS))
    hn = jnp.concatenate(outs, axis=1)
    out_ref[0] = (hn * nw_ref[...] * _sigmoid(o_ref[0])).astype(out_ref.dtype)


def mlstm_readout(h, p, norm_w):
    _, B, T, W = h.shape
    tm = 256
    o_blk = (2 * MLSTM_HEADS * MLSTM_DK + GROUP_W) // GROUP_W
    return pl.pallas_call(
        _mlstm_readout_kernel,
        grid=(B, T // tm),
        in_specs=[pl.BlockSpec((1, 1, tm, W), lambda b, i: (0, b, i, 0)),
                  pl.BlockSpec((1, 1, tm, W), lambda b, i: (1, b, i, 0)),
                  pl.BlockSpec((1, tm, W), lambda b, i: (b, i, o_blk)),
                  pl.BlockSpec((1, W), lambda b, i: (0, 0))],
        out_specs=pl.BlockSpec((1, tm, W), lambda b, i: (b, i, 0)),
        out_shape=jax.ShapeDtypeStruct((B, T, W), BF16),
        compiler_params=_cp("parallel", "parallel"),
        name="mlstm_readout",
    )(h, h, p, norm_w.reshape(1, W))


def _na_prep_kernel(p_ref, nw_ref, q_ref, k_ref, v_ref):
    x = p_ref[0]
    for j in range(NA_HEADS):
        sl = slice(j * NA_HEAD, (j + 1) * NA_HEAD)
        for src, dst, row, scale in ((0, q_ref, 0, NA_HEAD ** -0.5), (GROUP_W, k_ref, 1, 1.0)):
            z = x[:, src + j * NA_HEAD:src + (j + 1) * NA_HEAD]
            zn = z * lax.rsqrt(jnp.mean(z * z, axis=-1, keepdims=True) + EPS) * nw_ref[row:row + 1, :]
            dst[0, :, sl] = (zn * scale).astype(BF16)
    v_ref[0] = x[:, 2 * GROUP_W:].astype(BF16)


def na_prep(p, qk_norm):
    B, T, W = p.shape
    tm = 256
    tok = pl.BlockSpec((1, tm, GROUP_W), lambda b, i: (b, i, 0))
    sh = jax.ShapeDtypeStruct((B, T, GROUP_W), BF16)
    return pl.pallas_call(
        _na_prep_kernel,
        grid=(B, T // tm),
        in_specs=[pl.BlockSpec((1, tm, W), lambda b, i: (b, i, 0)),
                  pl.BlockSpec((2, NA_HEAD), lambda b, i: (0, 0))],
        out_specs=[tok, tok, tok],
        out_shape=[sh, sh, sh],
        compiler_params=_cp("parallel", "parallel"),
        name="na_prep",
    )(p, qk_norm)


def _softmax_av(s, v):
    m = jnp.max(s, axis=1, keepdims=True)
    e = jnp.exp(s - m)
    return _dot(e, v) * (1.0 / jnp.sum(e, axis=1, keepdims=True))


def _na_ctx_kernel(q_ref, k_ref, v_ref, o_ref):
    for j in range(NA_HEADS):
        sl = slice(j * NA_HEAD, (j + 1) * NA_HEAD)
        s = _dot_nt(q_ref[0, :, sl], k_ref[0, :, sl])
        o_ref[0, :, sl] = _softmax_av(s, v_ref[0, :, sl]).astype(o_ref.dtype)


def na_ctx_attention(q, k, v):
    B, T, W = q.shape
    tok = pl.BlockSpec((1, T, W), lambda b: (b, 0, 0))
    return pl.pallas_call(
        _na_ctx_kernel,
        grid=(B,),
        in_specs=[tok, tok, tok],
        out_specs=tok,
        out_shape=jax.ShapeDtypeStruct((B, T, W), BF16),
        compiler_params=_cp("parallel"),
        name="na_ctx_attention",
    )(q, k, v)


def _na_lat_kernel(q_ref, k0_ref, k1_ref, k2_ref, v0_ref, v1_ref, v2_ref, kc_ref, vc_ref, bias_ref, o_ref):
    for j in range(NA_HEADS):
        sl = slice(j * NA_HEAD, (j + 1) * NA_HEAD)
        q = q_ref[0, :, sl]
        kcat = jnp.concatenate([k0_ref[0, :, sl], k1_ref[0, :, sl], k2_ref[0, :, sl]], axis=0)
        vcat = jnp.concatenate([v0_ref[0, :, sl], v1_ref[0, :, sl], v2_ref[0, :, sl],
                                vc_ref[0, :, sl]], axis=0)
        s = jnp.concatenate([_dot_nt(q, kcat) + bias_ref[0, j], _dot_nt(q, kc_ref[0, :, sl])], axis=1)
        o_ref[0, :, sl] = _softmax_av(s, vcat).astype(o_ref.dtype)


def na_bias_table(rpb, rows):
    nq, nk = NA_QROWS, NA_KROWS
    starts = np.array([0, nq, rows - nq])
    r = starts[:, None] + np.arange(nq)[None, :]
    kr = starts[:, None] - nq + np.arange(nk)[None, :]
    r0 = np.clip(r - WIN_H // 2, 0, rows - WIN_H)
    rv = (kr[:, None, :] >= r0[:, :, None]) & (kr[:, None, :] < r0[:, :, None] + WIN_H)
    dr = kr[:, None, :] - r[:, :, None] + WIN_H - 1
    col = np.arange(GRID_W)
    c0 = np.clip(col - WIN_W // 2, 0, GRID_W - WIN_W)
    cv = (col[None, :] >= c0[:, None]) & (col[None, :] < c0[:, None] + WIN_W)
    pad = GRID_W - WIN_W
    rpb_pad = jnp.pad(rpb, ((0, 0), (0, 0), (pad, pad)))
    colbias = jnp.stack([rpb_pad[:, :, GRID_W - 1 - q:2 * GRID_W - 1 - q] for q in range(GRID_W)], axis=2)
    colbias = jnp.where(cv[None, None], colbias, NEG)
    blank = jnp.full((NA_HEADS, GRID_W, GRID_W), NEG, F32)
    pats = []
    for p in range(3):
        qrows = [jnp.concatenate([colbias[:, int(dr[p, i, j])] if rv[p, i, j] else blank
                                  for j in range(nk)], axis=2) for i in range(nq)]
        pats.append(jnp.concatenate(qrows, axis=1))
    return jnp.stack(pats)


def na_lat_attention(q, k, v, kc, vc, bias):
    B, T, W = q.shape
    tq = NA_QROWS * GRID_W
    nb = T // tq
    ctx_len = kc.shape[1]
    qsp = pl.BlockSpec((1, tq, W), lambda b, i: (b, i, 0))
    prv = pl.BlockSpec((1, tq, W), lambda b, i: (b, jnp.maximum(i - 1, 0), 0))
    nxt = pl.BlockSpec((1, tq, W), lambda b, i: (b, jnp.minimum(i + 1, nb - 1), 0))
    csp = pl.BlockSpec((1, ctx_len, W), lambda b, i: (b, 0, 0))
    pattern = lambda i: jnp.where(i == 0, 0, jnp.where(i == nb - 1, 2, 1))
    return pl.pallas_call(
        _na_lat_kernel,
        grid=(B, nb),
        in_specs=[qsp, prv, qsp, nxt, prv, qsp, nxt, csp, csp,
                  pl.BlockSpec((1, NA_HEADS, tq, NA_KROWS * GRID_W), lambda b, i: (pattern(i), 0, 0, 0))],
        out_specs=qsp,
        out_shape=jax.ShapeDtypeStruct((B, T, W), BF16),
        compiler_params=_cp("parallel", "arbitrary"),
        name="na_lat_attention",
    )(q, k, k, k, v, v, v, kc, vc, bias)


def _ffn_norm_router_body(h_ref, g_ref, sh_ref, sc_ref, rw_ref, rb_ref, v_ref, ids_ref, wts_ref):
    x = h_ref[0]
    y = x * lax.rsqrt(jnp.mean(x * x, axis=-1, keepdims=True) + EPS) * g_ref[...]
    v = y * (1.0 + sc_ref[0]) + sh_ref[0]
    half = v.shape[1] // 2
    v_ref[...] = _pack_bf16_pair(v[:, :half], v[:, half:])
    vh, vl = _split2(v)
    logits = (_dot(vh, rw_ref[0]) + (_dot(vh, rw_ref[1]) + _dot(vl, rw_ref[0]))) + rb_ref[...]
    lane = lax.broadcasted_iota(jnp.int32, logits.shape, 1).astype(F32)
    first = lambda mask: jnp.min(jnp.where(mask, lane, float(LANES)), axis=1, keepdims=True)

    g_mask = lane < N_GROUPS
    gl = jnp.where(g_mask, logits, NEG)
    gmax = jnp.max(gl, axis=1, keepdims=True)
    g_p = 1.0 / jnp.sum(jnp.exp(gl - gmax), axis=1, keepdims=True)
    g_idx = first(g_mask & (gl == gmax))
    e_lane = lane - N_GROUPS
    e_mask = (e_lane >= 0) & (e_lane < N_EXPERTS) & (jnp.floor(e_lane * (1.0 / EXPERTS_PER_GROUP)) == g_idx)
    el = jnp.where(e_mask, logits, NEG)
    e1 = jnp.max(el, axis=1, keepdims=True)
    i1 = first(e_mask & (el == e1))
    el2 = jnp.where(lane == i1, NEG, el)
    e2 = jnp.max(el2, axis=1, keepdims=True)
    i2 = first(e_mask & (lane != i1) & (el2 == e2))
    x2 = jnp.exp(e2 - e1)
    w1 = g_p / (1.0 + x2)
    w2 = g_p * x2 / (1.0 + x2)
    ids_ref[...] = jnp.where(lane == 0, i1 - N_GROUPS, jnp.where(lane == 1, i2 - N_GROUPS, 0.0)).astype(jnp.int32)
    wts_ref[...] = jnp.where(lane == 0, w1, jnp.where(lane == 1, w2, 0.0))


def _ffn_norm_router_kernel(*refs, bounds):
    ns = len(bounds) - 1
    g_ref, rw_ref, rb_ref, v_ref, ids_ref, wts_ref = refs[3 * ns:]
    r = pl.program_id(0)
    for s in range(ns):
        h_ref, sh_ref, sc_ref = refs[3 * s:3 * s + 3]

        @pl.when((r >= bounds[s]) & (r < bounds[s + 1]))
        def _():
            _ffn_norm_router_body(h_ref, g_ref, sh_ref, sc_ref, rw_ref, rb_ref, v_ref, ids_ref, wts_ref)


def ffn_norm_router(streams, g, rw, rb):
    D = g.shape[0]
    tm = 256
    bounds = [0]
    in_specs, args = [], []
    for h, shift, scale in streams:
        B, T, _ = h.shape
        nt = T // tm
        lo, nblk = bounds[-1], B * nt
        bounds.append(lo + nblk)
        local = lambda r, lo=lo, nblk=nblk: jnp.clip(r - lo, 0, nblk - 1)
        in_specs += [pl.BlockSpec((1, tm, D), lambda r, f=local, nt=nt: (f(r) // nt, f(r) % nt, 0)),
                     pl.BlockSpec((1, 1, D), lambda r, f=local, nt=nt: (f(r) // nt, 0, 0)),
                     pl.BlockSpec((1, 1, D), lambda r, f=local, nt=nt: (f(r) // nt, 0, 0))]
        args += [h, shift, scale]
    n_rows = bounds[-1] * tm
    in_specs += [pl.BlockSpec((1, D), lambda r: (0, 0)),
                 pl.BlockSpec((2, D, LANES), lambda r: (0, 0, 0)),
                 pl.BlockSpec((1, LANES), lambda r: (0, 0))]
    args += [g.reshape(1, D), rw, rb]
    lan = pl.BlockSpec((tm, LANES), lambda r: (r, 0))
    return pl.pallas_call(
        functools.partial(_ffn_norm_router_kernel, bounds=tuple(bounds)),
        grid=(bounds[-1],),
        in_specs=in_specs,
        out_specs=[pl.BlockSpec((tm, D // 2), lambda r: (r, 0)), lan, lan],
        out_shape=[jax.ShapeDtypeStruct((n_rows, D // 2), jnp.uint32),
                   jax.ShapeDtypeStruct((n_rows, LANES), jnp.int32),
                   jax.ShapeDtypeStruct((n_rows, LANES), F32)],
        compiler_params=_cp("parallel"),
        name="ffn_norm_router",
    )(*args)


def _moe_kernel(be_ref, zero_ref, tok_ref, tokn_ref, dstp_ref, sw_ref, v_hbm, wg_hbm, wu_hbm, wd_hbm, out_hbm,
                x0, x1, y0, y1, wg_buf, wu_buf, wd_buf, stage_up, stage_down, gsem, ssem, wsem, *, layer):
    i = pl.program_id(0)
    nb = pl.num_programs(0) - 1
    xs, ys = (x0, x1), (y0, y1)
    zero = zero_ref[0]
    used = zero_ref[1]

    def gather_row(idx_ref, s, j, after=0):
        return pltpu.make_async_copy(v_hbm.at[pl.ds(idx_ref[0, 0, j] + after, 1)], xs[s].at[pl.ds(j, 1)],
                                     gsem.at[s])

    def wait_gather(s):
        pltpu.make_async_copy(v_hbm.at[pl.ds(0, MOE_BLOCK)], xs[s], gsem.at[s]).wait()

    def scatter_row(s, j, after=0):
        return pltpu.make_async_copy(ys[s].at[pl.ds(j, 1)], out_hbm.at[pl.ds(dstp_ref[0, 0, j] + after, 1)],
                                     ssem.at[s])

    def wait_scatter(s):
        pltpu.make_async_copy(ys[s], out_hbm.at[pl.ds(0, MOE_BLOCK)], ssem.at[s]).wait()

    def after(result):
        return result[0, 0].astype(jnp.int32) * zero

    @pl.when(i == 0)
    def _():
        def body(j, carry):
            gather_row(tok_ref, 0, j).start()
            return carry
        lax.fori_loop(0, MOE_BLOCK, body, 0)

    e = be_ref[jnp.minimum(i, nb - 1)]

    @pl.when((i == 0) | ((i < used) & (e != be_ref[jnp.maximum(i - 1, 0)])))
    def _():
        chunks = []
        for src, dst, stage in ((wg_hbm, wg_buf, stage_up), (wu_hbm, wu_buf, stage_up),
                                (wd_hbm, wd_buf, stage_down)):
            rows = stage.shape[1]
            for r in range(dst.shape[0] // rows):
                k = len(chunks)
                cp = pltpu.make_async_copy(src.at[layer, e, pl.ds(r * rows, rows)], stage.at[k % 2],
                                           wsem.at[k % 2])
                chunks.append((cp, stage, dst, r * rows, rows))
        chunks[0][0].start()
        for k, (cp, stage, dst, r0, rows) in enumerate(chunks):
            if k + 1 < len(chunks):
                chunks[k + 1][0].start()
            cp.wait()
            dst[r0:r0 + rows, :] = stage[k % 2].astype(BF16)

    def step(s, scatter_prev):
        wait_gather(s)
        x_lo, x_hi = _unpack_bf16_pair(xs[s][...])
        x = jnp.concatenate([x_lo.astype(BF16), x_hi.astype(BF16)], axis=1)
        n_up = EXPERT_FF // MOE_FF_TILE
        per = MOE_BLOCK // (2 * n_up)
        acts = []
        for c in range(n_up):
            cols = slice(c * MOE_FF_TILE, (c + 1) * MOE_FF_TILE)
            hg = jnp.dot(x, wg_buf[:, cols], preferred_element_type=F32)
            hu = jnp.dot(x, wu_buf[:, cols], preferred_element_type=F32)
            for g, res in enumerate((hg, hu)):
                dep = after(res)
                for j in range((2 * c + g) * per, (2 * c + g + 1) * per):
                    gather_row(tokn_ref, 1 - s, j, dep).start()
            acts.append((_silu(hg) * hu).astype(BF16))
        act = jnp.concatenate(acts, axis=1)

        @pl.when(i >= 2)
        def _():
            wait_scatter(s)

        half = x.shape[1] // 2
        n_down = half // MOE_OUT_TILE
        per = MOE_BLOCK // n_down
        for c in range(n_down):
            cols = slice(c * MOE_OUT_TILE, (c + 1) * MOE_OUT_TILE)
            hcols = slice(half + c * MOE_OUT_TILE, half + (c + 1) * MOE_OUT_TILE)
            y_lo = jnp.dot(act, wd_buf[:, cols], preferred_element_type=F32) * sw_ref[...]
            y_hi = jnp.dot(act, wd_buf[:, hcols], preferred_element_type=F32) * sw_ref[...]
            ys[s][:, cols] = _pack_bf16_pair(y_lo, y_hi)
            if scatter_prev:
                dep = after(y_hi)
                for j in range(c * per, (c + 1) * per):
                    scatter_row(1 - s, j, dep).start()

    def idle(s):
        wait_gather(s)

        def fetch(j, carry):
            gather_row(tokn_ref, 1 - s, j).start()
            return carry
        lax.fori_loop(0, MOE_BLOCK, fetch, 0)
        wait_scatter(s)

        def send(j, carry):
            scatter_row(1 - s, j).start()
            return carry
        lax.fori_loop(0, MOE_BLOCK, send, 0)

    pl.when(i == 0)(functools.partial(step, 0, False))
    for s in range(2):
        pl.when((i > 0) & (i < used) & (i % 2 == s))(functools.partial(step, s, True))
        pl.when((i >= used) & (i < nb) & (i % 2 == s))(functools.partial(idle, s))

    @pl.when(i == nb)
    def _():
        def drain(s):
            wait_gather(s)
            wait_scatter(s)

            def body(j, carry):
                scatter_row(1 - s, j).start()
                return carry
            lax.fori_loop(0, MOE_BLOCK, body, 0)
            wait_scatter(1 - s)
        for s in range(2):
            pl.when(i % 2 == s)(functools.partial(drain, s))


def moe_experts(v_rows, block_expert, used, slot_tok, slot_dst, slot_w, wg, wu, wd, layer):
    P = slot_tok.shape[0]
    W = v_rows.shape[1]
    D = 2 * W
    nb = P // MOE_BLOCK
    assert nb >= 2
    idx = lambda a: a.reshape(nb, 1, MOE_BLOCK)
    smem = lambda imap: pl.BlockSpec((1, 1, MOE_BLOCK), imap, memory_space=pltpu.SMEM)
    hbm = pl.BlockSpec(memory_space=pl.ANY)
    last = nb - 1
    grid_spec = pltpu.PrefetchScalarGridSpec(
        num_scalar_prefetch=2,
        grid=(nb + 1,),
        in_specs=[smem(lambda i, be, z: (jnp.minimum(i, last), 0, 0)),
                  smem(lambda i, be, z: (jnp.minimum(i + 1, last), 0, 0)),
                  smem(lambda i, be, z: (jnp.maximum(i - 1, 0), 0, 0)),
                  pl.BlockSpec((MOE_BLOCK, 1), lambda i, be, z: (jnp.minimum(i, last), 0)),
                  hbm, hbm, hbm, hbm],
        out_specs=hbm,
        scratch_shapes=[pltpu.VMEM((MOE_BLOCK, W), jnp.uint32),
                        pltpu.VMEM((MOE_BLOCK, W), jnp.uint32),
                        pltpu.VMEM((MOE_BLOCK, W), jnp.uint32),
                        pltpu.VMEM((MOE_BLOCK, W), jnp.uint32),
                        pltpu.VMEM((D, EXPERT_FF), BF16),
                        pltpu.VMEM((D, EXPERT_FF), BF16),
                        pltpu.VMEM((EXPERT_FF, D), BF16),
                        pltpu.VMEM((2, MOE_STAGE_BYTES // (4 * EXPERT_FF), EXPERT_FF), F32),
                        pltpu.VMEM((2, MOE_STAGE_BYTES // (4 * D), D), F32),
                        pltpu.SemaphoreType.DMA((2,)),
                        pltpu.SemaphoreType.DMA((2,)),
                        pltpu.SemaphoreType.DMA((2,))],
    )
    return pl.pallas_call(
        functools.partial(_moe_kernel, layer=layer),
        grid_spec=grid_spec,
        out_shape=jax.ShapeDtypeStruct((P, W), jnp.uint32),
        compiler_params=_cp("arbitrary"),
        name="moe_experts",
    )(block_expert, jnp.stack([jnp.zeros((), jnp.int32), used]), idx(slot_tok), idx(slot_tok), idx(slot_dst),
      slot_w.reshape(P, 1), v_rows, wg, wu, wd)


def _moe_combine_kernel(h_ref, f0_ref, f1_ref, gate_ref, o_ref):
    a_lo, a_hi = _unpack_bf16_pair(f0_ref[...])
    b_lo, b_hi = _unpack_bf16_pair(f1_ref[...])
    half = a_lo.shape[1]
    o_ref[0, :, :half] = h_ref[0, :, :half] + gate_ref[0, :, :half] * (a_lo + b_lo)
    o_ref[0, :, half:] = h_ref[0, :, half:] + gate_ref[0, :, half:] * (a_hi + b_hi)


def moe_combine(h, out2, gate, row_offset, n_rows):
    B, T, D = h.shape
    tm = 256
    nt = T // tm
    blk0 = row_offset // tm
    tok = pl.BlockSpec((1, tm, D), lambda b, i: (b, i, 0))
    return pl.pallas_call(
        _moe_combine_kernel,
        grid=(B, nt),
        in_specs=[tok,
                  pl.BlockSpec((tm, D // 2), lambda b, i: (blk0 + b * nt + i, 0)),
                  pl.BlockSpec((tm, D // 2), lambda b, i: (n_rows // tm + blk0 + b * nt + i, 0)),
                  pl.BlockSpec((1, 1, D), lambda b, i: (b, 0, 0))],
        out_specs=tok,
        out_shape=jax.ShapeDtypeStruct((B, T, D), F32),
        compiler_params=_cp("parallel", "parallel"),
        name="moe_combine",
    )(h, out2, out2, gate)


def moe_dispatch(ids, wts):
    N = ids.shape[0]
    A = N * TOP_K
    e_flat = ids.reshape(A)
    order = jnp.argsort(e_flat).astype(jnp.int32)
    counts = jnp.sum((e_flat[:, None] == jnp.arange(N_EXPERTS)[None, :]).astype(jnp.int32), axis=0)
    padded = (counts + MOE_BLOCK - 1) // MOE_BLOCK * MOE_BLOCK
    ends = jnp.cumsum(padded)
    n_blocks = -(-(A + N_EXPERTS * (MOE_BLOCK - 1)) // MOE_BLOCK)
    block_start = jnp.arange(n_blocks, dtype=jnp.int32) * MOE_BLOCK
    block_expert = jnp.minimum(jnp.sum((ends[None, :] <= block_start[:, None]).astype(jnp.int32), axis=1),
                               N_EXPERTS - 1)
    into = block_start - (ends - padded)[block_expert]
    b_count = counts[block_expert]
    local = (into[:, None] + jnp.arange(MOE_BLOCK, dtype=jnp.int32)[None, :])
    real = local < b_count[:, None]
    src = jnp.clip((jnp.cumsum(counts) - counts)[block_expert][:, None] + local, 0, A - 1)
    a = order[src.reshape(-1)]
    real = real.reshape(-1)
    tok = a // TOP_K
    slot_tok = jnp.where(real, tok, 0).astype(jnp.int32)
    spare = A + jnp.cumsum(jnp.where(real, 0, 1)) - 1
    slot_dst = jnp.where(real, (a % TOP_K) * N + tok, spare).astype(jnp.int32)
    slot_w = jnp.where(real, wts.reshape(A)[a], 0.0)
    return slot_tok, slot_dst, slot_w, block_expert, (ends[-1] // MOE_BLOCK).astype(jnp.int32)


def hier_moe(v_rows, ids, wts, wg, wu, wd, layer):
    slot_tok, slot_dst, slot_w, block_expert, used = moe_dispatch(ids, wts)
    return moe_experts(v_rows, block_expert, used, slot_tok, slot_dst, slot_w, wg, wu, wd, layer)


def _layer_weights(l, P):
    w = {}
    w_in = P["w_in"][l]
    o = 0
    a = w_in[:, o:o + RWKV_COLS]; o += RWKV_COLS
    b = w_in[:, o:o + GROUP_W]; o += GROUP_W
    c = w_in[:, o:o + MLSTM_COLS]; o += MLSTM_COLS
    d = w_in[:, o:o + NA_COLS]
    D = w_in.shape[0]
    pad_a = RWKV_PAD_COLS - RWKV_COLS
    w["in_a"] = jnp.concatenate([a, jnp.zeros((D, pad_a), F32)], axis=1).astype(BF16)
    w["in_b"] = b.astype(BF16)
    w["in_c"] = jnp.concatenate([c, jnp.zeros((D, MLSTM_PAD_COLS - MLSTM_COLS), F32)], axis=1).astype(BF16)
    w["in_d"] = d.astype(BF16)
    w["out"] = P["w_out"][l].astype(BF16)

    z = jnp.zeros((RWKV_DECAY_RANK, GROUP_W), F32)
    wup, aup = P["rwkv_w_up"][l], P["rwkv_a_up"][l]
    rw = {
        "mu": jnp.concatenate([P["rwkv_mu"][l], jnp.zeros((2, pad_a), F32)], axis=1),
        "w_up": jnp.concatenate([jnp.concatenate([wup[0], z], axis=1),
                                 jnp.concatenate([z, wup[1]], axis=1)], axis=0).astype(BF16),
        "a_up": jnp.concatenate([jnp.concatenate([aup[0], z], axis=1),
                                 jnp.concatenate([z, aup[1]], axis=1)], axis=0).astype(BF16),
        "w0": P["rwkv_w0"][l].reshape(1, 2 * GROUP_W),
        "a0": P["rwkv_a0"][l].reshape(1, 2 * GROUP_W),
        "g_up": jnp.concatenate([P["rwkv_g_up"][l],
                                 jnp.zeros((256 - RWKV_GATE_RANK, GROUP_W), F32)], axis=0).astype(BF16),
        "k_k": P["rwkv_k_k"][l].reshape(1, GROUP_W),
        "k_a": P["rwkv_k_a"][l].reshape(1, GROUP_W),
        "r_k": P["rwkv_r_k"][l].reshape(1, GROUP_W),
    }
    w["rwkv"] = rw
    w["rwkv_ln"] = P["rwkv_ln"][l]
    w["pool_w"] = P["pool_w"][l].astype(BF16)
    w["pool_scale"] = P["pool_scale"][l]
    w["mlstm_conv"] = P["mlstm_conv"][l]
    gb = P["mlstm_gate_b"][l].reshape(1, 4 * MLSTM_HEADS)
    w["mlstm_gate_b"] = jnp.concatenate([gb, jnp.zeros((1, LANES - 4 * MLSTM_HEADS), F32)], axis=1)
    w["mlstm_norm"] = P["mlstm_norm"][l]
    w["na_qk_norm"] = P["na_qk_norm"][l]
    w["na_rpb"] = P["na_rpb"][l]
    rcat = jnp.concatenate([P["router_g_w"][l], P["router_e_w"][l],
                            jnp.zeros((D, LANES - N_GROUPS - N_EXPERTS), F32)], axis=1)
    rhi = rcat.astype(BF16)
    w["router_w"] = jnp.stack([rhi, (rcat - rhi.astype(F32)).astype(BF16)])
    w["router_b"] = jnp.concatenate([P["router_g_b"][l], P["router_e_b"][l],
                                     jnp.zeros((LANES - N_GROUPS - N_EXPERTS,), F32)]).reshape(1, LANES)
    return w


def rope_tables(T):
    t = jnp.arange(T)
    row = (t // GRID_W).astype(F32)
    col = (t % GRID_W).astype(F32)
    n_pairs = MLSTM_DK // 4
    inv = ROPE_BASE ** (-jnp.arange(n_pairs, dtype=F32) / n_pairs)
    ang = jnp.concatenate([row[:, None] * inv, col[:, None] * inv], axis=-1)
    cos = jnp.repeat(jnp.cos(ang), 2, axis=1)
    sin = jnp.repeat(jnp.sin(ang), 2, axis=1) * jnp.tile(jnp.array([-1.0, 1.0], F32), MLSTM_DK // 2)
    return cos, sin


def rwkv_mixer(p_ctx, p_lat, w, need_ctx):
    B = p_lat.shape[0]
    names = ("r", "kk", "v", "bonus", "g", "lw", "kd", "bb")
    fc = dict(zip(names, rwkv_prep(p_ctx, w["rwkv"])))
    fl = dict(zip(names, rwkv_prep(p_lat, w["rwkv"])))
    s0 = jnp.zeros((2, B, RWKV_PAIRS, LANES, LANES), F32)
    y_c, s_ctx = rwkv_scan(fc, s0)
    y_l, _ = rwkv_scan(fl, s_ctx)
    out_l = rwkv_readout(y_l, fl["bonus"], fl["g"], w["rwkv_ln"])
    out_c = rwkv_readout(y_c, fc["bonus"], fc["g"], w["rwkv_ln"]) if need_ctx else None
    return out_c, out_l


def mlstm_mixer(p_ctx, p_lat, w, rope, need_ctx):
    B = p_lat.shape[0]
    H = MLSTM_HEADS
    state = (jnp.zeros((2, B, H, MLSTM_DK, MLSTM_DV), F32),
             jnp.zeros((2, B, H, 1, MLSTM_DK), F32),
             jnp.zeros((2, B, H, 1, 1), F32))
    outs = []
    for p, tabs in ((p_ctx, None), (p_lat, rope)):
        q, k, gates = mlstm_prep(p, w["mlstm_conv"], w["mlstm_gate_b"], tabs)
        gates_t = jnp.swapaxes(gates[:, :, :4 * H], 1, 2)
        h, *state = mlstm_scan(q, k, p, gates, gates_t, tuple(state))
        outs.append(h)
    out_l = mlstm_readout(outs[1], p_lat, w["mlstm_norm"])
    out_c = mlstm_readout(outs[0], p_ctx, w["mlstm_norm"]) if need_ctx else None
    return out_c, out_l


def na_mixer(p_ctx, p_lat, w, need_ctx):
    qc, kc, vc = na_prep(p_ctx, w["na_qk_norm"])
    ql, kl, vl = na_prep(p_lat, w["na_qk_norm"])
    rows = p_lat.shape[1] // GRID_W
    bias = na_bias_table(w["na_rpb"], rows)
    out_l = na_lat_attention(ql, kl, vl, kc, vc, bias)
    out_c = na_ctx_attention(qc, kc, vc) if need_ctx else None
    return out_c, out_l


def token_mixers(u_ctx, u_lat, w, rope, need_ctx):
    outs_c, outs_l = [], []
    pa = (matmul(u_ctx, w["in_a"]), matmul(u_lat, w["in_a"]))
    oc, ol = rwkv_mixer(pa[0], pa[1], w, need_ctx)
    outs_c.append(oc); outs_l.append(ol)
    pb = (matmul(u_ctx, w["in_b"]), matmul(u_lat, w["in_b"]))
    outs_l.append(pool_mixer(pb[1], w["pool_w"], w["pool_scale"]))
    outs_c.append(pool_mixer(pb[0], w["pool_w"], w["pool_scale"]) if need_ctx else None)
    pc = (matmul(u_ctx, w["in_c"]), matmul(u_lat, w["in_c"]))
    oc, ol = mlstm_mixer(pc[0], pc[1], w, rope, need_ctx)
    outs_c.append(oc); outs_l.append(ol)
    pd = (matmul(u_ctx, w["in_d"]), matmul(u_lat, w["in_d"]))
    oc, ol = na_mixer(pd[0], pd[1], w, need_ctx)
    outs_c.append(oc); outs_l.append(ol)
    return outs_c, outs_l


def kernel(x, c, ctx, c_ctx, ada_w, ada_b, norm_mix, norm_ffn, w_in, w_out, rwkv_mu, rwkv_w0, rwkv_w_up, rwkv_a0, rwkv_a_up, rwkv_g_up, rwkv_k_k, rwkv_k_a, rwkv_r_k, rwkv_ln, pool_w, pool_scale, mlstm_conv, mlstm_gate_b, mlstm_norm, na_qk_norm, na_rpb, router_g_w, router_g_b, router_e_w, router_e_b, moe_w_gate, moe_w_up, moe_w_down):
    P = dict(w_in=w_in, w_out=w_out, rwkv_mu=rwkv_mu, rwkv_w0=rwkv_w0, rwkv_w_up=rwkv_w_up,
             rwkv_a0=rwkv_a0, rwkv_a_up=rwkv_a_up, rwkv_g_up=rwkv_g_up, rwkv_k_k=rwkv_k_k,
             rwkv_k_a=rwkv_k_a, rwkv_r_k=rwkv_r_k, rwkv_ln=rwkv_ln, pool_w=pool_w, pool_scale=pool_scale,
             mlstm_conv=mlstm_conv, mlstm_gate_b=mlstm_gate_b, mlstm_norm=mlstm_norm,
             na_qk_norm=na_qk_norm, na_rpb=na_rpb, router_g_w=router_g_w, router_g_b=router_g_b,
             router_e_w=router_e_w, router_e_b=router_e_b, moe_w_gate=moe_w_gate, moe_w_up=moe_w_up,
             moe_w_down=moe_w_down)
    B, T, D = x.shape
    Lc = ctx.shape[1]
    depth = ada_w.shape[0]
    rope = rope_tables(T)
    moe_w = (moe_w_gate, moe_w_up, moe_w_down)

    cvec = jnp.concatenate([c, c_ctx[None, :], jnp.zeros((SUBLANES - B - 1, D), F32)], axis=0)
    mods = ada_modulation(cvec, ada_w, ada_b)

    h_lat, h_ctx = x, ctx
    for l in range(depth):
        last = l == depth - 1
        w = _layer_weights(l, P)
        m = mods[l].reshape(SUBLANES, 6, D)
        m_lat = [m[:B, i][:, None, :] for i in range(6)]
        m_ctx = [jnp.broadcast_to(m[B, i][None, None, :], (B, 1, D)) for i in range(6)]

        u_lat = norm_modulate(h_lat, norm_mix[l], m_lat[0], m_lat[1])
        u_ctx = norm_modulate(h_ctx, norm_mix[l], m_ctx[0], m_ctx[1])
        ys_ctx, ys_lat = token_mixers(u_ctx, u_lat, w, rope, not last)
        h_lat = out_proj_residual(ys_lat, w["out"], h_lat, m_lat[2])
        n_rows = B * T if last else B * (T + Lc)
        streams = [(h_lat, m_lat[3], m_lat[4])]
        if not last:
            h_ctx = out_proj_residual(ys_ctx, w["out"], h_ctx, m_ctx[2])
            streams.append((h_ctx, m_ctx[3], m_ctx[4]))
        rows, ids, wts = ffn_norm_router(streams, norm_ffn[l], w["router_w"], w["router_b"])
        f2 = hier_moe(rows, ids[:, :TOP_K], wts[:, :TOP_K], *moe_w, l)
        if not last:
            h_ctx = moe_combine(h_ctx, f2, m_ctx[5], B * T, n_rows)
        h_lat = moe_combine(h_lat, f2, m_lat[5], 0, n_rows)
    return h_lat
```

```python
import functools
import math

import numpy as np
import jax
import jax.numpy as jnp
from jax import lax
from jax.experimental import pallas as pl
from jax.experimental.pallas import tpu as pltpu

F32 = jnp.float32
BF16 = jnp.bfloat16

D_MODEL = 4096
DEPTH = 2
GRID_W = 64
EPS = 1e-6
GROUP_W = 1024

RWKV_HEAD = 64
RWKV_DECAY_RANK = 64
RWKV_ICLR_RANK = 64
RWKV_GATE_RANK = 160
RWKV_GN_EPS = 64e-5
RWKV_COLS = 3 * GROUP_W + 2 * RWKV_DECAY_RANK + 2 * RWKV_ICLR_RANK + RWKV_GATE_RANK
RWKV_PAD_COLS = 3 * GROUP_W + 256 + 256

POOL_WINDOWS = (2, 4, 8, 16)
POOL_GROUP = 256
POOL_HALO = 8

MLSTM_HEADS = 4
MLSTM_DV = 256
MLSTM_DK = 128
MLSTM_COLS = 2 * MLSTM_HEADS * MLSTM_DK + 2 * GROUP_W + 4 * MLSTM_HEADS
MLSTM_GATE_COL = 2 * MLSTM_HEADS * MLSTM_DK + 2 * GROUP_W
MLSTM_PAD_COLS = MLSTM_GATE_COL + 512
ROPE_BASE = 10000.0

NA_HEADS = 8
NA_HEAD = 128
WIN_H = 8
WIN_W = 16
NA_COLS = 3 * GROUP_W
NA_QROWS = 4
NA_KROWS = 3 * NA_QROWS

N_GROUPS = 4
EXPERTS_PER_GROUP = 4
N_EXPERTS = 16
TOP_K = 2
EXPERT_FF = 1024
MOE_BLOCK = 256
MOE_FF_TILE = 256
MOE_OUT_TILE = 1024
MOE_STAGE_BYTES = 2 * 1024 * 1024

LANES = 128
SUBLANES = 8
VMEM_LIMIT = 56 * 1024 * 1024
NEG = -1e30

RWKV_CHUNK = 64
RWKV_PAIRS = 8
MLSTM_CHUNK = 256


def _cp(*sem):
    return pltpu.CompilerParams(dimension_semantics=sem, vmem_limit_bytes=VMEM_LIMIT)


def _dot(a, b):
    return jnp.dot(a.astype(BF16), b.astype(BF16), preferred_element_type=F32)


def _dot_nt(a, b):
    return lax.dot_general(a.astype(BF16), b.astype(BF16), (((1,), (1,)), ((), ())),
                           preferred_element_type=F32)


def _dot_tn(a, b):
    return lax.dot_general(a.astype(BF16), b.astype(BF16), (((0,), (0,)), ((), ())),
                           preferred_element_type=F32)


def _split2(x):
    hi = x.astype(BF16)
    lo = (x - hi.astype(F32)).astype(BF16)
    return hi, lo


def _dot3(a, b, dot=_dot):
    ah, al = _split2(a)
    bh, bl = _split2(b)
    return dot(ah, bh) + (dot(ah, bl) + dot(al, bh))


def _dot_exact_lhs(a_bf16, b):
    b0 = b.astype(BF16)
    r1 = b - b0.astype(F32)
    b1 = r1.astype(BF16)
    b2 = (r1 - b1.astype(F32)).astype(BF16)
    return _dot(a_bf16, b0) + (_dot(a_bf16, b1) + _dot(a_bf16, b2))


def _sigmoid(x):
    return 1.0 / (1.0 + jnp.exp(-x))


def _silu(x):
    return x * _sigmoid(x)


def _log_sigmoid(x):
    return jnp.minimum(x, 0.0) - jnp.log(1.0 + jnp.exp(-jnp.abs(x)))


def _pack_bf16_pair(lo, hi):
    lo_bits = lax.bitcast_convert_type(lo.astype(BF16).astype(F32), jnp.uint32)
    hi_bits = lax.bitcast_convert_type(hi.astype(BF16).astype(F32), jnp.uint32)
    return (lo_bits >> 16) | hi_bits


def _unpack_bf16_pair(w):
    lo = lax.bitcast_convert_type(w << 16, F32)
    hi = lax.bitcast_convert_type(w & jnp.uint32(0xFFFF0000), F32)
    return lo, hi


def _head_sum(x, width):
    r = lax.broadcasted_iota(jnp.int32, (LANES, LANES), 0) // width
    c = lax.broadcasted_iota(jnp.int32, (LANES, LANES), 1) // width
    e = jnp.where(r == c, 1.0, 0.0).astype(BF16)
    outs = []
    for j in range(x.shape[1] // LANES):
        hi, lo = _split2(x[:, j * LANES:(j + 1) * LANES])
        outs.append(_dot(hi, e) + _dot(lo, e))
    return jnp.concatenate(outs, axis=1)


def _ada_kernel(c_ref, w_ref, b_ref, o_ref):
    s = _silu(c_ref[...])
    o_ref[0] = _dot(s, w_ref[0]) + b_ref[0]


def ada_modulation(cvec, ada_w, ada_b):
    L, D, N = ada_w.shape
    tn = 1024
    return pl.pallas_call(
        _ada_kernel,
        grid=(L, N // tn),
        in_specs=[pl.BlockSpec((SUBLANES, D), lambda l, j: (0, 0)),
                  pl.BlockSpec((1, D, tn), lambda l, j: (l, 0, j)),
                  pl.BlockSpec((1, 1, tn), lambda l, j: (l, 0, j))],
        out_specs=pl.BlockSpec((1, SUBLANES, tn), lambda l, j: (l, 0, j)),
        out_shape=jax.ShapeDtypeStruct((L, SUBLANES, N), F32),
        compiler_params=_cp("parallel", "parallel"),
        name="ada_modulation",
    )(cvec, ada_w, ada_b.reshape(L, 1, N))


def _normmod_kernel(h_ref, g_ref, sh_ref, sc_ref, o_ref):
    x = h_ref[0]
    y = x * lax.rsqrt(jnp.mean(x * x, axis=-1, keepdims=True) + EPS) * g_ref[...]
    o_ref[0] = (y * (1.0 + sc_ref[0]) + sh_ref[0]).astype(o_ref.dtype)


def norm_modulate(h, g, shift, scale):
    B, T, D = h.shape
    tm = 256
    vec = pl.BlockSpec((1, 1, D), lambda b, i: (b, 0, 0))
    return pl.pallas_call(
        _normmod_kernel,
        grid=(B, T // tm),
        in_specs=[pl.BlockSpec((1, tm, D), lambda b, i: (b, i, 0)),
                  pl.BlockSpec((1, D), lambda b, i: (0, 0)), vec, vec],
        out_specs=pl.BlockSpec((1, tm, D), lambda b, i: (b, i, 0)),
        out_shape=jax.ShapeDtypeStruct((B, T, D), BF16),
        compiler_params=_cp("parallel", "parallel"),
        name="norm_modulate",
    )(h, g.reshape(1, D), shift, scale)


def _mm_kernel(x_ref, w_ref, o_ref):
    o_ref[0] = jnp.dot(x_ref[0], w_ref[...], preferred_element_type=F32).astype(o_ref.dtype)


def _pick_tile(n, cands):
    for t in cands:
        if n % t == 0:
            return t
    raise ValueError(n)


def matmul(x, w, out_dtype=F32):
    B, T, K = x.shape
    N = w.shape[1]
    tm = _pick_tile(T, (1024, 512, 256))
    tn = _pick_tile(N, (1024, 512))
    return pl.pallas_call(
        _mm_kernel,
        grid=(B, T // tm, N // tn),
        in_specs=[pl.BlockSpec((1, tm, K), lambda b, i, j: (b, i, 0)),
                  pl.BlockSpec((K, tn), lambda b, i, j: (0, j))],
        out_specs=pl.BlockSpec((1, tm, tn), lambda b, i, j: (b, i, j)),
        out_shape=jax.ShapeDtypeStruct((B, T, N), out_dtype),
        compiler_params=_cp("parallel", "parallel", "arbitrary"),
        name="in_proj",
    )(x, w)


def _outproj_kernel(ya_ref, yb_ref, yc_ref, yd_ref, w_ref, h_ref, gate_ref, o_ref):
    acc = jnp.dot(ya_ref[0], w_ref[0:GROUP_W, :], preferred_element_type=F32)
    acc += jnp.dot(yb_ref[0], w_ref[GROUP_W:2 * GROUP_W, :], preferred_element_type=F32)
    acc += jnp.dot(yc_ref[0], w_ref[2 * GROUP_W:3 * GROUP_W, :], preferred_element_type=F32)
    acc += jnp.dot(yd_ref[0], w_ref[3 * GROUP_W:4 * GROUP_W, :], preferred_element_type=F32)
    o_ref[0] = h_ref[0] + gate_ref[0] * acc


def out_proj_residual(ys, w, h, gate):
    B, T, D = h.shape
    tm = _pick_tile(T, (512, 256))
    tn = 1024
    ysp = pl.BlockSpec((1, tm, GROUP_W), lambda b, i, j: (b, i, 0))
    return pl.pallas_call(
        _outproj_kernel,
        grid=(B, T // tm, D // tn),
        in_specs=[ysp, ysp, ysp, ysp,
                  pl.BlockSpec((D, tn), lambda b, i, j: (0, j)),
                  pl.BlockSpec((1, tm, tn), lambda b, i, j: (b, i, j)),
                  pl.BlockSpec((1, 1, tn), lambda b, i, j: (b, 0, j))],
        out_specs=pl.BlockSpec((1, tm, tn), lambda b, i, j: (b, i, j)),
        out_shape=jax.ShapeDtypeStruct((B, T, D), F32),
        compiler_params=_cp("parallel", "parallel", "arbitrary"),
        name="out_proj",
    )(*ys, w, h, gate)


def _halo_specs(tm, width, halo=SUBLANES):
    per = tm // halo

    def prev_map(b, i):
        return (b, jnp.maximum(i * per - 1, 0), 0)

    def next_map(nb):
        return lambda b, i: (b, jnp.minimum((i + 1) * per, nb - 1), 0)

    return prev_map, next_map


def _shift_rows(x, prev_row, next_row):
    tm = x.shape[0]
    row = lax.broadcasted_iota(jnp.int32, x.shape, 0)
    prev = jnp.where(row == 0, prev_row, pltpu.roll(x, 1, axis=0))
    nxt = jnp.where(row == tm - 1, next_row, pltpu.roll(x, tm - 1, axis=0))
    return prev, nxt


def _rwkv_prep_kernel(p_ref, pp_ref, pn_ref, mu_ref, wup_ref, w0_ref, aup_ref, a0_ref, gup_ref,
                      kk_w_ref, ka_w_ref, rk_w_ref,
                      r_ref, kk_ref, v_ref, bonus_ref, g_ref, lw_ref, kd_ref, bb_ref):
    i = pl.program_id(1)
    last = pl.num_programs(1) - 1
    x = p_ref[0]
    prev_row = jnp.where(i == 0, 0.0, pp_ref[0, SUBLANES - 1:SUBLANES, :])
    next_row = jnp.where(i == last, 0.0, pn_ref[0, 0:1, :])
    prev, nxt = _shift_rows(x, prev_row, next_row)
    x = x + mu_ref[0:1, :] * (prev - x) + mu_ref[1:2, :] * (nxt - x)

    r = x[:, 0:GROUP_W]
    k = x[:, GROUP_W:2 * GROUP_W]
    v = x[:, 2 * GROUP_W:3 * GROUP_W]
    low = x[:, 3 * GROUP_W:3 * GROUP_W + 256]
    gd = x[:, 3 * GROUP_W + 256:3 * GROUP_W + 512]

    zw = _dot(jnp.tanh(low[:, 0:LANES]), wup_ref[...]) + w0_ref[...]
    za = _dot(low[:, LANES:2 * LANES], aup_ref[...]) + a0_ref[...]
    kkr = k * kk_w_ref[...]
    kk = kkr * lax.rsqrt(_head_sum(kkr * kkr, RWKV_HEAD) + EPS)
    r_ref[0] = r.astype(BF16)
    kk_ref[0] = kk.astype(BF16)
    v_ref[0] = v.astype(BF16)
    bonus_ref[0] = (_head_sum(r * k * rk_w_ref[...], RWKV_HEAD) * v).astype(BF16)
    g_ref[0] = _dot(_sigmoid(gd), gup_ref[...]).astype(BF16)
    for d in range(2):
        sl = slice(d * GROUP_W, (d + 1) * GROUP_W)
        lw_ref[d, 0] = (-math.exp(-0.5)) * _sigmoid(zw[:, sl])
        a = _sigmoid(za[:, sl])
        kd_ref[d, 0] = (k * (1.0 + (a - 1.0) * ka_w_ref[...])).astype(BF16)
        bb_ref[d, 0] = (kk * a).astype(BF16)


def rwkv_prep(p, w):
    B, T, W = p.shape
    tm = 128
    prev_map, next_map = _halo_specs(tm, W)
    nb8 = T // SUBLANES
    full = lambda shape: pl.BlockSpec(shape, lambda b, i: tuple(0 for _ in shape))
    tok = pl.BlockSpec((1, tm, GROUP_W), lambda b, i: (b, i, 0))
    tokd = pl.BlockSpec((2, 1, tm, GROUP_W), lambda b, i: (0, b, i, 0))
    sh = jax.ShapeDtypeStruct((B, T, GROUP_W), BF16)
    shd = jax.ShapeDtypeStruct((2, B, T, GROUP_W), BF16)
    shd_f32 = jax.ShapeDtypeStruct((2, B, T, GROUP_W), F32)
    return pl.pallas_call(
        _rwkv_prep_kernel,
        grid=(B, T // tm),
        in_specs=[pl.BlockSpec((1, tm, W), lambda b, i: (b, i, 0)),
                  pl.BlockSpec((1, SUBLANES, W), prev_map),
                  pl.BlockSpec((1, SUBLANES, W), next_map(nb8)),
                  full((2, W)), full((LANES, 2 * GROUP_W)), full((1, 2 * GROUP_W)),
                  full((LANES, 2 * GROUP_W)), full((1, 2 * GROUP_W)), full((256, GROUP_W)),
                  full((1, GROUP_W)), full((1, GROUP_W)), full((1, GROUP_W))],
        out_specs=[tok, tok, tok, tok, tok, tokd, tokd, tokd],
        out_shape=[sh, sh, sh, sh, sh, shd_f32, shd, shd],
        compiler_params=_cp("parallel", "parallel"),
        name="rwkv_prep",
    )(p, p, p, w["mu"], w["w_up"], w["w0"], w["a_up"], w["a0"], w["g_up"],
      w["k_k"], w["k_a"], w["r_k"])


def _dform(x, m0):
    return jnp.concatenate([jnp.where(m0, x, 0.0), jnp.where(m0, 0.0, x)], axis=0)


def _rwkv_chunk(tiles, sts, rev):
    n = len(tiles)
    C = tiles[0][0].shape[0]
    C2 = 2 * C
    sgn = jnp.where(rev, -1, 1)
    rr = lax.broadcasted_iota(jnp.int32, (C, C), 0)
    cc = lax.broadcasted_iota(jnp.int32, (C, C), 1)
    tri = jnp.where((rr - cc) * sgn >= 0, 1.0, 0.0).astype(BF16)
    lane = lax.broadcasted_iota(jnp.int32, (C, LANES), 1)
    m0 = lane < RWKV_HEAD
    rd = lax.broadcasted_iota(jnp.int32, (C2, C2), 0)
    cd = lax.broadcasted_iota(jnp.int32, (C2, C2), 1)
    same = (rd // C) == (cd // C)
    ahead = (rd % C - cd % C) * sgn
    strict = same & (ahead > 0)
    incl = same & (ahead >= 0)
    diag = rd == cd
    each = range(n)

    lins = [_dot_exact_lhs(tri, t[3]) for t in tiles]
    ltots = [jnp.sum(t[3], axis=0, keepdims=True) for t in tiles]
    a_d, r_d, k_d, b_d, v_d, kw_d, bw_d = [], [], [], [], [], [], []
    for (r, kk, v, lw, kd, bb), lin, ltot in zip(tiles, lins, ltots):
        einv = jnp.exp(-lin)
        ew = jnp.exp(ltot - lin)
        a_d.append(_dform(-kk * jnp.exp(lin - lw), m0).astype(BF16))
        r_d.append(_dform(r * jnp.exp(lin), m0).astype(BF16))
        k_d.append(_dform(kd * einv, m0).astype(BF16))
        b_d.append(_dform(bb * einv, m0).astype(BF16))
        v_d.append(_dform(v, m0).astype(BF16))
        kw_d.append(_dform(kd * ew, m0).astype(BF16))
        bw_d.append(_dform(bb * ew, m0).astype(BF16))

    grams = [_dot_nt(jnp.concatenate([a_d[i], r_d[i]], axis=0), jnp.concatenate([k_d[i], b_d[i]], axis=0))
             for i in each]
    a_ak = [jnp.where(strict, g[:C2, :C2], 0.0).astype(BF16) for g in grams]
    a_ab = [jnp.where(strict, g[:C2, C2:], 0.0) for g in grams]
    a_rk = [jnp.where(incl, g[C2:, :C2], 0.0).astype(BF16) for g in grams]
    a_rb = [jnp.where(incl, g[C2:, C2:], 0.0).astype(BF16) for g in grams]

    tinv = [jnp.where(diag, 1.0, 0.0) + a for a in a_ab]
    apow = a_ab
    for _ in range(int(math.log2(C)) - 1):
        apow = [_dot(a, a) for a in apow]
        tinv = [t + _dot(t, a) for t, a in zip(tinv, apow)]

    av = [_dot(jnp.concatenate([a_ak[i], a_rk[i]], axis=0), v_d[i]) for i in each]
    p = [_dot(tinv[i], jnp.concatenate([a_d[i], av[i][:C2].astype(BF16)], axis=1)) for i in each]
    ur = [_dot(jnp.concatenate([p[i][:, :LANES].astype(BF16), r_d[i]], axis=0), sts[i]) for i in each]
    u = [ur[i][:C2] + p[i][:, LANES:] for i in each]
    y2 = [ur[i][C2:] + av[i][C2:] + _dot(a_rb[i], u[i]) for i in each]
    ys = [y[:C] + y[C:] for y in y2]
    new = []
    for i in each:
        wc = jnp.broadcast_to(jnp.exp(ltots[i]), (LANES, LANES))
        wcol = jnp.sum(jnp.where(diag, wc, 0.0), axis=1, keepdims=True)
        new.append(sts[i] * wcol + _dot_tn(jnp.concatenate([bw_d[i], kw_d[i]], axis=0),
                                           jnp.concatenate([u[i].astype(BF16), v_d[i]], axis=0)))
    return ys, new


def _rwkv_scan_kernel(r_ref, kk_ref, v_ref, lw_ref, kd_ref, bb_ref, s0_ref, y_ref, sT_ref, st_scr,
                      *, npairs):
    d = pl.program_id(0)
    c = pl.program_id(3)
    nc = pl.num_programs(3)

    @pl.when(c == 0)
    def _():
        st_scr[...] = s0_ref[0, 0]

    lanes = [slice(j * LANES, (j + 1) * LANES) for j in range(npairs)]
    f32 = lambda a: a.astype(F32)
    tiles = [(f32(r_ref[0, :, sl]), f32(kk_ref[0, :, sl]), f32(v_ref[0, :, sl]),
              lw_ref[0, 0, :, sl], f32(kd_ref[0, 0, :, sl]), f32(bb_ref[0, 0, :, sl])) for sl in lanes]
    ys, new = _rwkv_chunk(tiles, [st_scr[j] for j in range(npairs)], d == 1)
    for j, sl in enumerate(lanes):
        y_ref[0, 0, :, sl] = ys[j]
        st_scr[j] = new[j]

    @pl.when(c == nc - 1)
    def _():
        sT_ref[0, 0] = st_scr[...]


def rwkv_scan(f, s0, npairs=RWKV_PAIRS):
    r, kk, v, lw, kd, bb = f["r"], f["kk"], f["v"], f["lw"], f["kd"], f["bb"]
    B, T, _ = r.shape
    C = RWKV_CHUNK
    nc = T // C
    ng = RWKV_PAIRS // npairs
    W = npairs * LANES
    chunk = lambda d, c: c + d * (nc - 1 - 2 * c)
    tok = pl.BlockSpec((1, C, W), lambda d, b, g, c: (b, chunk(d, c), g))
    tokd = pl.BlockSpec((1, 1, C, W), lambda d, b, g, c: (d, b, chunk(d, c), g))
    stsp = pl.BlockSpec((1, 1, npairs, LANES, LANES), lambda d, b, g, c: (d, b, g, 0, 0))
    return pl.pallas_call(
        functools.partial(_rwkv_scan_kernel, npairs=npairs),
        grid=(2, B, ng, nc),
        in_specs=[tok, tok, tok, tokd, tokd, tokd, stsp],
        out_specs=[tokd, stsp],
        out_shape=[jax.ShapeDtypeStruct((2, B, T, GROUP_W), F32),
                   jax.ShapeDtypeStruct(s0.shape, F32)],
        scratch_shapes=[pltpu.VMEM((npairs, LANES, LANES), F32)],
        compiler_params=_cp("parallel", "parallel", "parallel", "arbitrary"),
        name="rwkv_scan",
    )(r, kk, v, lw, kd, bb, s0)


def _rwkv_readout_kernel(yf_ref, yb_ref, bonus_ref, g_ref, ln_ref, o_ref):
    y = yf_ref[0, 0] + yb_ref[0, 0]
    mean = _head_sum(y, RWKV_HEAD) * (1.0 / RWKV_HEAD)
    cen = y - mean
    var = _head_sum(cen * cen, RWKV_HEAD) * (1.0 / RWKV_HEAD)
    yn = cen * lax.rsqrt(var + RWKV_GN_EPS) * ln_ref[0:1, :] + ln_ref[1:2, :]
    o_ref[0] = ((yn + bonus_ref[0].astype(F32)) * g_ref[0].astype(F32)).astype(o_ref.dtype)


def rwkv_readout(y, bonus, g, ln):
    _, B, T, W = y.shape
    tm = 256
    tok = pl.BlockSpec((1, tm, W), lambda b, i: (b, i, 0))
    return pl.pallas_call(
        _rwkv_readout_kernel,
        grid=(B, T // tm),
        in_specs=[pl.BlockSpec((1, 1, tm, W), lambda b, i: (0, b, i, 0)),
                  pl.BlockSpec((1, 1, tm, W), lambda b, i: (1, b, i, 0)),
                  tok, tok, pl.BlockSpec((2, W), lambda b, i: (0, 0))],
        out_specs=tok,
        out_shape=jax.ShapeDtypeStruct((B, T, W), BF16),
        compiler_params=_cp("parallel", "parallel"),
        name="rwkv_readout",
    )(y, y, bonus, g, ln)


def _pool_kernel(p_ref, pp_ref, pn_ref, w_ref, sc_ref, o_ref, *, total):
    i = pl.program_id(1)
    last = pl.num_programs(1) - 1
    x = p_ref[0]
    tm = x.shape[0]
    n = tm + 2 * POOL_HALO
    ext = jnp.concatenate([jnp.where(i == 0, 0.0, pp_ref[0]), x,
                           jnp.where(i == last, 0.0, pn_ref[0])], axis=0)
    t = i * tm + lax.broadcasted_iota(jnp.int32, (tm, 1), 0)
    outs = []
    for gi, win in enumerate(POOL_WINDOWS):
        sl = slice(gi * POOL_GROUP, (gi + 1) * POOL_GROUP)
        e = ext[:, sl]
        s = e + pltpu.roll(e, 1, axis=0)
        step = 1
        while 2 * step < win:
            s = pltpu.roll(s, step, axis=0) + pltpu.roll(s, n - step, axis=0)
            step *= 2
        h = win // 2
        cnt = (jnp.minimum(t + h, total) - jnp.maximum(t - h, 0)).astype(F32)
        z = s[POOL_HALO:POOL_HALO + tm] / cnt - x[:, sl]
        outs.append(_dot(z, w_ref[gi]))
    o_ref[0] = (jnp.concatenate(outs, axis=1) * sc_ref[...]).astype(o_ref.dtype)


def pool_mixer(p, pool_w, pool_scale):
    B, T, W = p.shape
    tm = 256
    prev_map, next_map = _halo_specs(tm, W)
    return pl.pallas_call(
        functools.partial(_pool_kernel, total=T),
        grid=(B, T // tm),
        in_specs=[pl.BlockSpec((1, tm, W), lambda b, i: (b, i, 0)),
                  pl.BlockSpec((1, POOL_HALO, W), prev_map),
                  pl.BlockSpec((1, POOL_HALO, W), next_map(T // POOL_HALO)),
                  pl.BlockSpec(pool_w.shape, lambda b, i: (0, 0, 0)),
                  pl.BlockSpec((1, W), lambda b, i: (0, 0))],
        out_specs=pl.BlockSpec((1, tm, W), lambda b, i: (b, i, 0)),
        out_shape=jax.ShapeDtypeStruct((B, T, W), BF16),
        compiler_params=_cp("parallel", "parallel"),
        name="pool_mixer",
    )(p, p, p, pool_w, pool_scale.reshape(1, W))


def _mlstm_prep_kernel(p_ref, pp_ref, pn_ref, g_ref, cw_ref, gb_ref, *rest, rope):
    if rope:
        cos_ref, sin_ref, q_ref, k_ref, go_ref = rest
    else:
        q_ref, k_ref, go_ref = rest
    i = pl.program_id(1)
    last = pl.num_programs(1) - 1
    x = p_ref[0]
    prev_row = jnp.where(i == 0, 0.0, pp_ref[0, SUBLANES - 1:SUBLANES, :])
    next_row = jnp.where(i == last, 0.0, pn_ref[0, 0:1, :])
    prev, nxt = _shift_rows(x, prev_row, next_row)
    qk = _silu(cw_ref[0:1, :] * prev + cw_ref[1:2, :] * x + cw_ref[2:3, :] * nxt)
    if rope:
        w = qk.shape[1]
        lane = lax.broadcasted_iota(jnp.int32, qk.shape, 1)
        partner = jnp.where(lane % 2 == 0, pltpu.roll(qk, w - 1, axis=1), pltpu.roll(qk, 1, axis=1))
        reps = w // LANES
        cos = jnp.concatenate([cos_ref[...]] * reps, axis=1)
        sin = jnp.concatenate([sin_ref[...]] * reps, axis=1)
        qk = qk * cos + partner * sin
    half = MLSTM_HEADS * MLSTM_DK
    q_ref[0] = qk[:, :half].astype(BF16)
    k_ref[0] = (qk[:, half:] * (MLSTM_DK ** -0.5)).astype(BF16)
    g = g_ref[0] + gb_ref[...]
    lane = lax.broadcasted_iota(jnp.int32, g.shape, 1)
    go_ref[0] = jnp.where((lane // MLSTM_HEADS) % 2 == 1, _log_sigmoid(g), g)


def mlstm_prep(p, conv_w, gate_b, rope_tabs):
    B, T, W = p.shape
    tm = 256
    QK = 2 * MLSTM_HEADS * MLSTM_DK
    prev_map, next_map = _halo_specs(tm, QK)
    rope = rope_tabs is not None
    in_specs = [pl.BlockSpec((1, tm, QK), lambda b, i: (b, i, 0)),
                pl.BlockSpec((1, SUBLANES, QK), prev_map),
                pl.BlockSpec((1, SUBLANES, QK), next_map(T // SUBLANES)),
                pl.BlockSpec((1, tm, LANES), lambda b, i: (b, i, MLSTM_GATE_COL // LANES)),
                pl.BlockSpec((3, QK), lambda b, i: (0, 0)),
                pl.BlockSpec((1, LANES), lambda b, i: (0, 0))]
    args = [p, p, p, p, conv_w, gate_b]
    if rope:
        in_specs += [pl.BlockSpec((tm, LANES), lambda b, i: (i, 0))] * 2
        args += list(rope_tabs)
    half = MLSTM_HEADS * MLSTM_DK
    return pl.pallas_call(
        functools.partial(_mlstm_prep_kernel, rope=rope),
        grid=(B, T // tm),
        in_specs=in_specs,
        out_specs=[pl.BlockSpec((1, tm, half), lambda b, i: (b, i, 0)),
                   pl.BlockSpec((1, tm, half), lambda b, i: (b, i, 0)),
                   pl.BlockSpec((1, tm, LANES), lambda b, i: (b, i, 0))],
        out_shape=[jax.ShapeDtypeStruct((B, T, half), BF16),
                   jax.ShapeDtypeStruct((B, T, half), BF16),
                   jax.ShapeDtypeStruct((B, T, LANES), F32)],
        compiler_params=_cp("parallel", "parallel"),
        name="mlstm_prep",
    )(*args)


def _mlstm_chunk_kernel(q_ref, k_ref, v_ref, gc_ref, gr_ref, c0_ref, n0_ref, m0_ref,
                        h_ref, cT_ref, nT_ref, mT_ref, c_scr, n_scr, m_scr):
    d = pl.program_id(0)
    c = pl.program_id(2)
    nc = pl.num_programs(2)
    H = MLSTM_HEADS

    @pl.when(c == 0)
    def _():
        c_scr[...] = c0_ref[0, 0]
        n_scr[...] = n0_ref[0, 0]
        m_scr[...] = m0_ref[0, 0]

    rev = d == 1
    L = q_ref.shape[1]
    rr = lax.broadcasted_iota(jnp.int32, (L, L), 0)
    cc = lax.broadcasted_iota(jnp.int32, (L, L), 1)
    sgn = jnp.where(rev, -1, 1)
    seen = (rr - cc) * sgn >= 0
    tri = jnp.where(seen, 1.0, 0.0).astype(BF16)
    tri_t = jnp.where((cc - rr) * sgn >= 0, 1.0, 0.0).astype(BF16)

    gc = gc_ref[0]
    gr = gr_ref[0]
    lane = lax.broadcasted_iota(jnp.int32, gc.shape, 1)
    subl = lax.broadcasted_iota(jnp.int32, gr.shape, 0)
    pick_c = lambda a, idx: jnp.sum(jnp.where(lane == idx, a, 0.0), axis=1, keepdims=True)
    pick_r = lambda a, idx: jnp.sum(jnp.where(subl == idx, a, 0.0), axis=0, keepdims=True)
    bc_all = _dot_exact_lhs(tri, gc)
    gr_hi = gr.astype(BF16)
    gr_r1 = gr - gr_hi.astype(F32)
    gr_mid = gr_r1.astype(BF16)
    gr_lo = (gr_r1 - gr_mid.astype(F32)).astype(BF16)
    br_all = _dot(gr_hi, tri_t) + (_dot(gr_mid, tri_t) + _dot(gr_lo, tri_t))

    heads = range(H)
    i_lane = [d * (2 * H) + j for j in heads]
    f_lane = [d * (2 * H) + H + j for j in heads]
    ig_c = [pick_c(gc, i_lane[j]) for j in heads]
    lf_c = [pick_c(gc, f_lane[j]) for j in heads]
    b_c = [pick_c(bc_all, f_lane[j]) for j in heads]
    ig_r = [pick_r(gr, i_lane[j]) for j in heads]
    b_r = [pick_r(br_all, f_lane[j]) for j in heads]
    m_prev = [m_scr[j] for j in heads]
    q = [q_ref[0, :, j * MLSTM_DK:(j + 1) * MLSTM_DK] for j in heads]
    k = [k_ref[0, :, j * MLSTM_DK:(j + 1) * MLSTM_DK] for j in heads]
    v = [v_ref[0, :, j * MLSTM_DV:(j + 1) * MLSTM_DV].astype(BF16) for j in heads]
    cst = [c_scr[j] for j in heads]
    nst = [n_scr[j] for j in heads]

    qk = [_dot_nt(q[j], k[j]) for j in heads]
    qc = [_dot(q[j], cst[j]) for j in heads]
    dlog = [jnp.where(seen, b_c[j] - b_r[j] + ig_r[j], NEG) for j in heads]
    inter = [b_c[j] + m_prev[j] for j in heads]
    m_t = [jnp.maximum(inter[j], jnp.max(dlog[j], axis=1, keepdims=True)) for j in heads]
    s = [qk[j] * jnp.exp(dlog[j] - m_t[j]) for j in heads]
    w_inter = [jnp.exp(inter[j] - m_t[j]) for j in heads]
    sv = [_dot(s[j], v[j]) for j in heads]
    for j in heads:
        num = sv[j] + w_inter[j] * qc[j]
        den = (jnp.sum(s[j], axis=1, keepdims=True)
               + w_inter[j] * jnp.sum(q[j].astype(F32) * nst[j], axis=1, keepdims=True))
        h_ref[0, 0, :, j * MLSTM_DV:(j + 1) * MLSTM_DV] = num / jnp.maximum(jnp.abs(den), jnp.exp(-m_t[j]))

    b_end = [jnp.sum(lf_c[j], axis=0, keepdims=True) for j in heads]
    g_s = [b_end[j] - b_c[j] + ig_c[j] for j in heads]
    m_new = [jnp.maximum(b_end[j] + m_prev[j], jnp.max(g_s[j], axis=0, keepdims=True)) for j in heads]
    kw = [k[j].astype(F32) * jnp.exp(g_s[j] - m_new[j]) for j in heads]
    kv = [_dot_tn(kw[j], v[j]) for j in heads]
    for j in heads:
        decay = jnp.exp(b_end[j] + m_prev[j] - m_new[j])
        c_scr[j] = decay * cst[j] + kv[j]
        n_scr[j] = decay * nst[j] + jnp.sum(kw[j], axis=0, keepdims=True)
        m_scr[j] = m_new[j]

    @pl.when(c == nc - 1)
    def _():
        cT_ref[0, 0] = c_scr[...]
        nT_ref[0, 0] = n_scr[...]
        mT_ref[0, 0] = m_scr[...]


def mlstm_scan(q, k, p, gates, gates_t, state):
    B, T, _ = q.shape
    L = min(MLSTM_CHUNK, T)
    nc = T // L
    H = MLSTM_HEADS
    QK = H * MLSTM_DK
    c0, n0, m0 = state
    chunk = lambda d, c: c + d * (nc - 1 - 2 * c)
    v_blk = (2 * QK) // GROUP_W
    csp = pl.BlockSpec((1, 1, H, MLSTM_DK, MLSTM_DV), lambda d, b, c: (d, b, 0, 0, 0))
    nsp = pl.BlockSpec((1, 1, H, 1, MLSTM_DK), lambda d, b, c: (d, b, 0, 0, 0))
    msp = pl.BlockSpec((1, 1, H, 1, 1), lambda d, b, c: (d, b, 0, 0, 0))
    return pl.pallas_call(
        _mlstm_chunk_kernel,
        grid=(2, B, nc),
        in_specs=[pl.BlockSpec((1, L, QK), lambda d, b, c: (b, chunk(d, c), 0)),
                  pl.BlockSpec((1, L, QK), lambda d, b, c: (b, chunk(d, c), 0)),
                  pl.BlockSpec((1, L, GROUP_W), lambda d, b, c: (b, chunk(d, c), v_blk)),
                  pl.BlockSpec((1, L, LANES), lambda d, b, c: (b, chunk(d, c), 0)),
                  pl.BlockSpec((1, 4 * H, L), lambda d, b, c: (b, 0, chunk(d, c))),
                  csp, nsp, msp],
        out_specs=[pl.BlockSpec((1, 1, L, GROUP_W), lambda d, b, c: (d, b, chunk(d, c), 0)),
                   csp, nsp, msp],
        out_shape=[jax.ShapeDtypeStruct((2, B, T, GROUP_W), F32),
                   jax.ShapeDtypeStruct(c0.shape, F32),
                   jax.ShapeDtypeStruct(n0.shape, F32),
                   jax.ShapeDtypeStruct(m0.shape, F32)],
        scratch_shapes=[pltpu.VMEM((H, MLSTM_DK, MLSTM_DV), F32),
                        pltpu.VMEM((H, 1, MLSTM_DK), F32),
                        pltpu.VMEM((H, 1, 1), F32)],
        compiler_params=_cp("parallel", "parallel", "arbitrary"),
        name="mlstm_scan",
    )(q, k, p, gates, gates_t, c0, n0, m0)


def _mlstm_readout_kernel(hf_ref, hb_ref, o_ref, nw_ref, out_ref):
    h = hf_ref[0, 0] + hb_ref[0, 0]
    outs = []
    for j in range(MLSTM_HEADS):
        hj = h[:, j * MLSTM_DV:(j + 1) * MLSTM_DV]
        outs.append(hj * lax.rsqrt(jnp.mean(hj * hj, axis=-1, keepdims=True) + EPS))
    hn = jnp.concatenate(outs, axis=1)
    out_ref[0] = (hn * nw_ref[...] * _sigmoid(o_ref[0])).astype(out_ref.dtype)


def mlstm_readout(h, p, norm_w):
    _, B, T, W = h.shape
    tm = 256
    o_blk = (2 * MLSTM_HEADS * MLSTM_DK + GROUP_W) // GROUP_W
    return pl.pallas_call(
        _mlstm_readout_kernel,
        grid=(B, T // tm),
        in_specs=[pl.BlockSpec((1, 1, tm, W), lambda b, i: (0, b, i, 0)),
                  pl.BlockSpec((1, 1, tm, W), lambda b, i: (1, b, i, 0)),
                  pl.BlockSpec((1, tm, W), lambda b, i: (b, i, o_blk)),
                  pl.BlockSpec((1, W), lambda b, i: (0, 0))],
        out_specs=pl.BlockSpec((1, tm, W), lambda b, i: (b, i, 0)),
        out_shape=jax.ShapeDtypeStruct((B, T, W), BF16),
        compiler_params=_cp("parallel", "parallel"),
        name="mlstm_readout",
    )(h, h, p, norm_w.reshape(1, W))


def _na_prep_kernel(p_ref, nw_ref, q_ref, k_ref, v_ref):
    x = p_ref[0]
    for j in range(NA_HEADS):
        sl = slice(j * NA_HEAD, (j + 1) * NA_HEAD)
        for src, dst, row, scale in ((0, q_ref, 0, NA_HEAD ** -0.5), (GROUP_W, k_ref, 1, 1.0)):
            z = x[:, src + j * NA_HEAD:src + (j + 1) * NA_HEAD]
            zn = z * lax.rsqrt(jnp.mean(z * z, axis=-1, keepdims=True) + EPS) * nw_ref[row:row + 1, :]
            dst[0, :, sl] = (zn * scale).astype(BF16)
    v_ref[0] = x[:, 2 * GROUP_W:].astype(BF16)


def na_prep(p, qk_norm):
    B, T, W = p.shape
    tm = 256
    tok = pl.BlockSpec((1, tm, GROUP_W), lambda b, i: (b, i, 0))
    sh = jax.ShapeDtypeStruct((B, T, GROUP_W), BF16)
    return pl.pallas_call(
        _na_prep_kernel,
        grid=(B, T // tm),
        in_specs=[pl.BlockSpec((1, tm, W), lambda b, i: (b, i, 0)),
                  pl.BlockSpec((2, NA_HEAD), lambda b, i: (0, 0))],
        out_specs=[tok, tok, tok],
        out_shape=[sh, sh, sh],
        compiler_params=_cp("parallel", "parallel"),
        name="na_prep",
    )(p, qk_norm)


def _softmax_av(s, v):
    m = jnp.max(s, axis=1, keepdims=True)
    e = jnp.exp(s - m)
    return _dot(e, v) * (1.0 / jnp.sum(e, axis=1, keepdims=True))


def _na_ctx_kernel(q_ref, k_ref, v_ref, o_ref):
    for j in range(NA_HEADS):
        sl = slice(j * NA_HEAD, (j + 1) * NA_HEAD)
        s = _dot_nt(q_ref[0, :, sl], k_ref[0, :, sl])
        o_ref[0, :, sl] = _softmax_av(s, v_ref[0, :, sl]).astype(o_ref.dtype)


def na_ctx_attention(q, k, v):
    B, T, W = q.shape
    tok = pl.BlockSpec((1, T, W), lambda b: (b, 0, 0))
    return pl.pallas_call(
        _na_ctx_kernel,
        grid=(B,),
        in_specs=[tok, tok, tok],
        out_specs=tok,
        out_shape=jax.ShapeDtypeStruct((B, T, W), BF16),
        compiler_params=_cp("parallel"),
        name="na_ctx_attention",
    )(q, k, v)


def _na_lat_kernel(q_ref, k0_ref, k1_ref, k2_ref, v0_ref, v1_ref, v2_ref, kc_ref, vc_ref, bias_ref, o_ref):
    for j in range(NA_HEADS):
        sl = slice(j * NA_HEAD, (j + 1) * NA_HEAD)
        q = q_ref[0, :, sl]
        kcat = jnp.concatenate([k0_ref[0, :, sl], k1_ref[0, :, sl], k2_ref[0, :, sl]], axis=0)
        vcat = jnp.concatenate([v0_ref[0, :, sl], v1_ref[0, :, sl], v2_ref[0, :, sl],
                                vc_ref[0, :, sl]], axis=0)
        s = jnp.concatenate([_dot_nt(q, kcat) + bias_ref[0, j], _dot_nt(q, kc_ref[0, :, sl])], axis=1)
        o_ref[0, :, sl] = _softmax_av(s, vcat).astype(o_ref.dtype)


def na_bias_table(rpb, rows):
    nq, nk = NA_QROWS, NA_KROWS
    starts = np.array([0, nq, rows - nq])
    r = starts[:, None] + np.arange(nq)[None, :]
    kr = starts[:, None] - nq + np.arange(nk)[None, :]
    r0 = np.clip(r - WIN_H // 2, 0, rows - WIN_H)
    rv = (kr[:, None, :] >= r0[:, :, None]) & (kr[:, None, :] < r0[:, :, None] + WIN_H)
    dr = kr[:, None, :] - r[:, :, None] + WIN_H - 1
    col = np.arange(GRID_W)
    c0 = np.clip(col - WIN_W // 2, 0, GRID_W - WIN_W)
    cv = (col[None, :] >= c0[:, None]) & (col[None, :] < c0[:, None] + WIN_W)
    pad = GRID_W - WIN_W
    rpb_pad = jnp.pad(rpb, ((0, 0), (0, 0), (pad, pad)))
    colbias = jnp.stack([rpb_pad[:, :, GRID_W - 1 - q:2 * GRID_W - 1 - q] for q in range(GRID_W)], axis=2)
    colbias = jnp.where(cv[None, None], colbias, NEG)
    blank = jnp.full((NA_HEADS, GRID_W, GRID_W), NEG, F32)
    pats = []
    for p in range(3):
        qrows = [jnp.concatenate([colbias[:, int(dr[p, i, j])] if rv[p, i, j] else blank
                                  for j in range(nk)], axis=2) for i in range(nq)]
        pats.append(jnp.concatenate(qrows, axis=1))
    return jnp.stack(pats)


def na_lat_attention(q, k, v, kc, vc, bias):
    B, T, W = q.shape
    tq = NA_QROWS * GRID_W
    nb = T // tq
    ctx_len = kc.shape[1]
    qsp = pl.BlockSpec((1, tq, W), lambda b, i: (b, i, 0))
    prv = pl.BlockSpec((1, tq, W), lambda b, i: (b, jnp.maximum(i - 1, 0), 0))
    nxt = pl.BlockSpec((1, tq, W), lambda b, i: (b, jnp.minimum(i + 1, nb - 1), 0))
    csp = pl.BlockSpec((1, ctx_len, W), lambda b, i: (b, 0, 0))
    pattern = lambda i: jnp.where(i == 0, 0, jnp.where(i == nb - 1, 2, 1))
    return pl.pallas_call(
        _na_lat_kernel,
        grid=(B, nb),
        in_specs=[qsp, prv, qsp, nxt, prv, qsp, nxt, csp, csp,
                  pl.BlockSpec((1, NA_HEADS, tq, NA_KROWS * GRID_W), lambda b, i: (pattern(i), 0, 0, 0))],
        out_specs=qsp,
        out_shape=jax.ShapeDtypeStruct((B, T, W), BF16),
        compiler_params=_cp("parallel", "arbitrary"),
        name="na_lat_attention",
    )(q, k, k, k, v, v, v, kc, vc, bias)


def _ffn_norm_router_body(h_ref, g_ref, sh_ref, sc_ref, rw_ref, rb_ref, v_ref, ids_ref, wts_ref):
    x = h_ref[0]
    y = x * lax.rsqrt(jnp.mean(x * x, axis=-1, keepdims=True) + EPS) * g_ref[...]
    v = y * (1.0 + sc_ref[0]) + sh_ref[0]
    half = v.shape[1] // 2
    v_ref[...] = _pack_bf16_pair(v[:, :half], v[:, half:])
    vh, vl = _split2(v)
    logits = (_dot(vh, rw_ref[0]) + (_dot(vh, rw_ref[1]) + _dot(vl, rw_ref[0]))) + rb_ref[...]
    lane = lax.broadcasted_iota(jnp.int32, logits.shape, 1).astype(F32)
    first = lambda mask: jnp.min(jnp.where(mask, lane, float(LANES)), axis=1, keepdims=True)

    g_mask = lane < N_GROUPS
    gl = jnp.where(g_mask, logits, NEG)
    gmax = jnp.max(gl, axis=1, keepdims=True)
    g_p = 1.0 / jnp.sum(jnp.exp(gl - gmax), axis=1, keepdims=True)
    g_idx = first(g_mask & (gl == gmax))
    e_lane = lane - N_GROUPS
    e_mask = (e_lane >= 0) & (e_lane < N_EXPERTS) & (jnp.floor(e_lane * (1.0 / EXPERTS_PER_GROUP)) == g_idx)
    el = jnp.where(e_mask, logits, NEG)
    e1 = jnp.max(el, axis=1, keepdims=True)
    i1 = first(e_mask & (el == e1))
    el2 = jnp.where(lane == i1, NEG, el)
    e2 = jnp.max(el2, axis=1, keepdims=True)
    i2 = first(e_mask & (lane != i1) & (el2 == e2))
    x2 = jnp.exp(e2 - e1)
    w1 = g_p / (1.0 + x2)
    w2 = g_p * x2 / (1.0 + x2)
    ids_ref[...] = jnp.where(lane == 0, i1 - N_GROUPS, jnp.where(lane == 1, i2 - N_GROUPS, 0.0)).astype(jnp.int32)
    wts_ref[...] = jnp.where(lane == 0, w1, jnp.where(lane == 1, w2, 0.0))


def _ffn_norm_router_kernel(*refs, bounds):
    ns = len(bounds) - 1
    g_ref, rw_ref, rb_ref, v_ref, ids_ref, wts_ref = refs[3 * ns:]
    r = pl.program_id(0)
    for s in range(ns):
        h_ref, sh_ref, sc_ref = refs[3 * s:3 * s + 3]

        @pl.when((r >= bounds[s]) & (r < bounds[s + 1]))
        def _():
            _ffn_norm_router_body(h_ref, g_ref, sh_ref, sc_ref, rw_ref, rb_ref, v_ref, ids_ref, wts_ref)


def ffn_norm_router(streams, g, rw, rb):
    D = g.shape[0]
    tm = 256
    bounds = [0]
    in_specs, args = [], []
    for h, shift, scale in streams:
        B, T, _ = h.shape
        nt = T // tm
        lo, nblk = bounds[-1], B * nt
        bounds.append(lo + nblk)
        local = lambda r, lo=lo, nblk=nblk: jnp.clip(r - lo, 0, nblk - 1)
        in_specs += [pl.BlockSpec((1, tm, D), lambda r, f=local, nt=nt: (f(r) // nt, f(r) % nt, 0)),
                     pl.BlockSpec((1, 1, D), lambda r, f=local, nt=nt: (f(r) // nt, 0, 0)),
                     pl.BlockSpec((1, 1, D), lambda r, f=local, nt=nt: (f(r) // nt, 0, 0))]
        args += [h, shift, scale]
    n_rows = bounds[-1] * tm
    in_specs += [pl.BlockSpec((1, D), lambda r: (0, 0)),
                 pl.BlockSpec((2, D, LANES), lambda r: (0, 0, 0)),
                 pl.BlockSpec((1, LANES), lambda r: (0, 0))]
    args += [g.reshape(1, D), rw, rb]
    lan = pl.BlockSpec((tm, LANES), lambda r: (r, 0))
    return pl.pallas_call(
        functools.partial(_ffn_norm_router_kernel, bounds=tuple(bounds)),
        grid=(bounds[-1],),
        in_specs=in_specs,
        out_specs=[pl.BlockSpec((tm, D // 2), lambda r: (r, 0)), lan, lan],
        out_shape=[jax.ShapeDtypeStruct((n_rows, D // 2), jnp.uint32),
                   jax.ShapeDtypeStruct((n_rows, LANES), jnp.int32),
                   jax.ShapeDtypeStruct((n_rows, LANES), F32)],
        compiler_params=_cp("parallel"),
        name="ffn_norm_router",
    )(*args)


def _moe_kernel(be_ref, zero_ref, tok_ref, tokn_ref, dstp_ref, sw_ref, v_hbm, wg_hbm, wu_hbm, wd_hbm, out_hbm,
                x0, x1, y0, y1, wg_buf, wu_buf, wd_buf, stage_up, stage_down, gsem, ssem, wsem, *, layer):
    i = pl.program_id(0)
    nb = pl.num_programs(0) - 1
    xs, ys = (x0, x1), (y0, y1)
    zero = zero_ref[0]
    used = zero_ref[1]

    def gather_row(idx_ref, s, j, after=0):
        return pltpu.make_async_copy(v_hbm.at[pl.ds(idx_ref[0, 0, j] + after, 1)], xs[s].at[pl.ds(j, 1)],
                                     gsem.at[s])

    def wait_gather(s):
        pltpu.make_async_copy(v_hbm.at[pl.ds(0, MOE_BLOCK)], xs[s], gsem.at[s]).wait()

    def scatter_row(s, j, after=0):
        return pltpu.make_async_copy(ys[s].at[pl.ds(j, 1)], out_hbm.at[pl.ds(dstp_ref[0, 0, j] + after, 1)],
                                     ssem.at[s])

    def wait_scatter(s):
        pltpu.make_async_copy(ys[s], out_hbm.at[pl.ds(0, MOE_BLOCK)], ssem.at[s]).wait()

    def after(result):
        return result[0, 0].astype(jnp.int32) * zero

    @pl.when(i == 0)
    def _():
        def body(j, carry):
            gather_row(tok_ref, 0, j).start()
            return carry
        lax.fori_loop(0, MOE_BLOCK, body, 0)

    e = be_ref[jnp.minimum(i, nb - 1)]

    @pl.when((i == 0) | ((i < used) & (e != be_ref[jnp.maximum(i - 1, 0)])))
    def _():
        chunks = []
        for src, dst, stage in ((wg_hbm, wg_buf, stage_up), (wu_hbm, wu_buf, stage_up),
                                (wd_hbm, wd_buf, stage_down)):
            rows = stage.shape[1]
            for r in range(dst.shape[0] // rows):
                k = len(chunks)
                cp = pltpu.make_async_copy(src.at[layer, e, pl.ds(r * rows, rows)], stage.at[k % 2],
                                           wsem.at[k % 2])
                chunks.append((cp, stage, dst, r * rows, rows))
        chunks[0][0].start()
        for k, (cp, stage, dst, r0, rows) in enumerate(chunks):
            if k + 1 < len(chunks):
                chunks[k + 1][0].start()
            cp.wait()
            dst[r0:r0 + rows, :] = stage[k % 2].astype(BF16)

    def step(s, scatter_prev):
        wait_gather(s)
        x_lo, x_hi = _unpack_bf16_pair(xs[s][...])
        x = jnp.concatenate([x_lo.astype(BF16), x_hi.astype(BF16)], axis=1)
        n_up = EXPERT_FF // MOE_FF_TILE
        per = MOE_BLOCK // (2 * n_up)
        acts = []
        for c in range(n_up):
            cols = slice(c * MOE_FF_TILE, (c + 1) * MOE_FF_TILE)
            hg = jnp.dot(x, wg_buf[:, cols], preferred_element_type=F32)
            hu = jnp.dot(x, wu_buf[:, cols], preferred_element_type=F32)
            for g, res in enumerate((hg, hu)):
                dep = after(res)
                for j in range((2 * c + g) * per, (2 * c + g + 1) * per):
                    gather_row(tokn_ref, 1 - s, j, dep).start()
            acts.append((_silu(hg) * hu).astype(BF16))
        act = jnp.concatenate(acts, axis=1)

        @pl.when(i >= 2)
        def _():
            wait_scatter(s)

        half = x.shape[1] // 2
        n_down = half // MOE_OUT_TILE
        per = MOE_BLOCK // n_down
        for c in range(n_down):
            cols = slice(c * MOE_OUT_TILE, (c + 1) * MOE_OUT_TILE)
            hcols = slice(half + c * MOE_OUT_TILE, half + (c + 1) * MOE_OUT_TILE)
            y_lo = jnp.dot(act, wd_buf[:, cols], preferred_element_type=F32) * sw_ref[...]
            y_hi = jnp.dot(act, wd_buf[:, hcols], preferred_element_type=F32) * sw_ref[...]
            ys[s][:, cols] = _pack_bf16_pair(y_lo, y_hi)
            if scatter_prev:
                dep = after(y_hi)
                for j in range(c * per, (c + 1) * per):
                    scatter_row(1 - s, j, dep).start()

    def idle(s):
        wait_gather(s)

        def fetch(j, carry):
            gather_row(tokn_ref, 1 - s, j).start()
            return carry
        lax.fori_loop(0, MOE_BLOCK, fetch, 0)
        wait_scatter(s)

        def send(j, carry):
            scatter_row(1 - s, j).start()
            return carry
        lax.fori_loop(0, MOE_BLOCK, send, 0)

    pl.when(i == 0)(functools.partial(step, 0, False))
    for s in range(2):
        pl.when((i > 0) & (i < used) & (i % 2 == s))(functools.partial(step, s, True))
        pl.when((i >= used) & (i < nb) & (i % 2 == s))(functools.partial(idle, s))

    @pl.when(i == nb)
    def _():
        def drain(s):
            wait_gather(s)
            wait_scatter(s)

            def body(j, carry):
                scatter_row(1 - s, j).start()
                return carry
            lax.fori_loop(0, MOE_BLOCK, body, 0)
            wait_scatter(1 - s)
        for s in range(2):
            pl.when(i % 2 == s)(functools.partial(drain, s))


def moe_experts(v_rows, block_expert, used, slot_tok, slot_dst, slot_w, wg, wu, wd, layer):
    P = slot_tok.shape[0]
    W = v_rows.shape[1]
    D = 2 * W
    nb = P // MOE_BLOCK
    assert nb >= 2
    idx = lambda a: a.reshape(nb, 1, MOE_BLOCK)
    smem = lambda imap: pl.BlockSpec((1, 1, MOE_BLOCK), imap, memory_space=pltpu.SMEM)
    hbm = pl.BlockSpec(memory_space=pl.ANY)
    last = nb - 1
    grid_spec = pltpu.PrefetchScalarGridSpec(
        num_scalar_prefetch=2,
        grid=(nb + 1,),
        in_specs=[smem(lambda i, be, z: (jnp.minimum(i, last), 0, 0)),
                  smem(lambda i, be, z: (jnp.minimum(i + 1, last), 0, 0)),
                  smem(lambda i, be, z: (jnp.maximum(i - 1, 0), 0, 0)),
                  pl.BlockSpec((MOE_BLOCK, 1), lambda i, be, z: (jnp.minimum(i, last), 0)),
                  hbm, hbm, hbm, hbm],
        out_specs=hbm,
        scratch_shapes=[pltpu.VMEM((MOE_BLOCK, W), jnp.uint32),
                        pltpu.VMEM((MOE_BLOCK, W), jnp.uint32),
                        pltpu.VMEM((MOE_BLOCK, W), jnp.uint32),
                        pltpu.VMEM((MOE_BLOCK, W), jnp.uint32),
                        pltpu.VMEM((D, EXPERT_FF), BF16),
                        pltpu.VMEM((D, EXPERT_FF), BF16),
                        pltpu.VMEM((EXPERT_FF, D), BF16),
                        pltpu.VMEM((2, MOE_STAGE_BYTES // (4 * EXPERT_FF), EXPERT_FF), F32),
                        pltpu.VMEM((2, MOE_STAGE_BYTES // (4 * D), D), F32),
                        pltpu.SemaphoreType.DMA((2,)),
                        pltpu.SemaphoreType.DMA((2,)),
                        pltpu.SemaphoreType.DMA((2,))],
    )
    return pl.pallas_call(
        functools.partial(_moe_kernel, layer=layer),
        grid_spec=grid_spec,
        out_shape=jax.ShapeDtypeStruct((P, W), jnp.uint32),
        compiler_params=_cp("arbitrary"),
        name="moe_experts",
    )(block_expert, jnp.stack([jnp.zeros((), jnp.int32), used]), idx(slot_tok), idx(slot_tok), idx(slot_dst),
      slot_w.reshape(P, 1), v_rows, wg, wu, wd)


def _moe_combine_kernel(h_ref, f0_ref, f1_ref, gate_ref, *rest):
    a_lo, a_hi = _unpack_bf16_pair(f0_ref[...])
    b_lo, b_hi = _unpack_bf16_pair(f1_ref[...])
    f = jnp.concatenate([a_lo + b_lo, a_hi + b_hi], axis=1)
    x = h_ref[0] + gate_ref[0] * f
    if len(rest) == 1:
        (o_ref,) = rest
    else:
        g_ref, sh_ref, sc_ref, o_ref, u_ref = rest
        y = x * lax.rsqrt(jnp.mean(x * x, axis=-1, keepdims=True) + EPS) * g_ref[...]
        u_ref[0] = (y * (1.0 + sc_ref[0]) + sh_ref[0]).astype(u_ref.dtype)
    o_ref[0] = x


def moe_combine(h, out2, gate, row_offset, n_rows, next_norm=None):
    B, T, D = h.shape
    tm = 256
    nt = T // tm
    blk0 = row_offset // tm
    tok = pl.BlockSpec((1, tm, D), lambda b, i: (b, i, 0))
    vec = pl.BlockSpec((1, 1, D), lambda b, i: (b, 0, 0))
    in_specs = [tok,
                pl.BlockSpec((tm, D // 2), lambda b, i: (blk0 + b * nt + i, 0)),
                pl.BlockSpec((tm, D // 2), lambda b, i: (n_rows // tm + blk0 + b * nt + i, 0)),
                vec]
    args = [h, out2, out2, gate]
    out_specs, out_shape = tok, jax.ShapeDtypeStruct((B, T, D), F32)
    if next_norm is not None:
        g, shift, scale = next_norm
        in_specs += [pl.BlockSpec((1, D), lambda b, i: (0, 0)), vec, vec]
        args += [g.reshape(1, D), shift, scale]
        out_specs, out_shape = [tok, tok], [out_shape, jax.ShapeDtypeStruct((B, T, D), BF16)]
    return pl.pallas_call(
        _moe_combine_kernel,
        grid=(B, nt),
        in_specs=in_specs,
        out_specs=out_specs,
        out_shape=out_shape,
        compiler_params=_cp("parallel", "parallel"),
        name="moe_combine",
    )(*args)


def moe_dispatch(ids, wts):
    N = ids.shape[0]
    A = N * TOP_K
    e_flat = ids.reshape(A)
    order = jnp.argsort(e_flat).astype(jnp.int32)
    counts = jnp.sum((e_flat[:, None] == jnp.arange(N_EXPERTS)[None, :]).astype(jnp.int32), axis=0)
    padded = (counts + MOE_BLOCK - 1) // MOE_BLOCK * MOE_BLOCK
    ends = jnp.cumsum(padded)
    n_blocks = -(-(A + N_EXPERTS * (MOE_BLOCK - 1)) // MOE_BLOCK)
    block_start = jnp.arange(n_blocks, dtype=jnp.int32) * MOE_BLOCK
    block_expert = jnp.minimum(jnp.sum((ends[None, :] <= block_start[:, None]).astype(jnp.int32), axis=1),
                               N_EXPERTS - 1)
    into = block_start - (ends - padded)[block_expert]
    b_count = counts[block_expert]
    local = (into[:, None] + jnp.arange(MOE_BLOCK, dtype=jnp.int32)[None, :])
    real = local < b_count[:, None]
    src = jnp.clip((jnp.cumsum(counts) - counts)[block_expert][:, None] + local, 0, A - 1)
    a = order[src.reshape(-1)]
    real = real.reshape(-1)
    tok = a // TOP_K
    slot_tok = jnp.where(real, tok, 0).astype(jnp.int32)
    spare = A + jnp.cumsum(jnp.where(real, 0, 1)) - 1
    slot_dst = jnp.where(real, (a % TOP_K) * N + tok, spare).astype(jnp.int32)
    slot_w = jnp.where(real, wts.reshape(A)[a], 0.0)
    return slot_tok, slot_dst, slot_w, block_expert, (ends[-1] // MOE_BLOCK).astype(jnp.int32)


def hier_moe(v_rows, ids, wts, wg, wu, wd, layer):
    slot_tok, slot_dst, slot_w, block_expert, used = moe_dispatch(ids, wts)
    return moe_experts(v_rows, block_expert, used, slot_tok, slot_dst, slot_w, wg, wu, wd, layer)


def _layer_weights(l, P):
    w = {}
    w_in = P["w_in"][l]
    o = 0
    a = w_in[:, o:o + RWKV_COLS]; o += RWKV_COLS
    b = w_in[:, o:o + GROUP_W]; o += GROUP_W
    c = w_in[:, o:o + MLSTM_COLS]; o += MLSTM_COLS
    d = w_in[:, o:o + NA_COLS]
    D = w_in.shape[0]
    pad_a = RWKV_PAD_COLS - RWKV_COLS
    w["in_a"] = jnp.concatenate([a, jnp.zeros((D, pad_a), F32)], axis=1).astype(BF16)
    w["in_b"] = b.astype(BF16)
    w["in_c"] = jnp.concatenate([c, jnp.zeros((D, MLSTM_PAD_COLS - MLSTM_COLS), F32)], axis=1).astype(BF16)
    w["in_d"] = d.astype(BF16)
    w["out"] = P["w_out"][l].astype(BF16)

    z = jnp.zeros((RWKV_DECAY_RANK, GROUP_W), F32)
    wup, aup = P["rwkv_w_up"][l], P["rwkv_a_up"][l]
    rw = {
        "mu": jnp.concatenate([P["rwkv_mu"][l], jnp.zeros((2, pad_a), F32)], axis=1),
        "w_up": jnp.concatenate([jnp.concatenate([wup[0], z], axis=1),
                                 jnp.concatenate([z, wup[1]], axis=1)], axis=0).astype(BF16),
        "a_up": jnp.concatenate([jnp.concatenate([aup[0], z], axis=1),
                                 jnp.concatenate([z, aup[1]], axis=1)], axis=0).astype(BF16),
        "w0": P["rwkv_w0"][l].reshape(1, 2 * GROUP_W),
        "a0": P["rwkv_a0"][l].reshape(1, 2 * GROUP_W),
        "g_up": jnp.concatenate([P["rwkv_g_up"][l],
                                 jnp.zeros((256 - RWKV_GATE_RANK, GROUP_W), F32)], axis=0).astype(BF16),
        "k_k": P["rwkv_k_k"][l].reshape(1, GROUP_W),
        "k_a": P["rwkv_k_a"][l].reshape(1, GROUP_W),
        "r_k": P["rwkv_r_k"][l].reshape(1, GROUP_W),
    }
    w["rwkv"] = rw
    w["rwkv_ln"] = P["rwkv_ln"][l]
    w["pool_w"] = P["pool_w"][l].astype(BF16)
    w["pool_scale"] = P["pool_scale"][l]
    w["mlstm_conv"] = P["mlstm_conv"][l]
    gb = P["mlstm_gate_b"][l].reshape(1, 4 * MLSTM_HEADS)
    w["mlstm_gate_b"] = jnp.concatenate([gb, jnp.zeros((1, LANES - 4 * MLSTM_HEADS), F32)], axis=1)
    w["mlstm_norm"] = P["mlstm_norm"][l]
    w["na_qk_norm"] = P["na_qk_norm"][l]
    w["na_rpb"] = P["na_rpb"][l]
    rcat = jnp.concatenate([P["router_g_w"][l], P["router_e_w"][l],
                            jnp.zeros((D, LANES - N_GROUPS - N_EXPERTS), F32)], axis=1)
    rhi = rcat.astype(BF16)
    w["router_w"] = jnp.stack([rhi, (rcat - rhi.astype(F32)).astype(BF16)])
    w["router_b"] = jnp.concatenate([P["router_g_b"][l], P["router_e_b"][l],
                                     jnp.zeros((LANES - N_GROUPS - N_EXPERTS,), F32)]).reshape(1, LANES)
    return w


def rope_tables(T):
    t = jnp.arange(T)
    row = (t // GRID_W).astype(F32)
    col = (t % GRID_W).astype(F32)
    n_pairs = MLSTM_DK // 4
    inv = ROPE_BASE ** (-jnp.arange(n_pairs, dtype=F32) / n_pairs)
    ang = jnp.concatenate([row[:, None] * inv, col[:, None] * inv], axis=-1)
    cos = jnp.repeat(jnp.cos(ang), 2, axis=1)
    sin = jnp.repeat(jnp.sin(ang), 2, axis=1) * jnp.tile(jnp.array([-1.0, 1.0], F32), MLSTM_DK // 2)
    return cos, sin


def rwkv_mixer(p_ctx, p_lat, w, need_ctx):
    B = p_lat.shape[0]
    names = ("r", "kk", "v", "bonus", "g", "lw", "kd", "bb")
    fc = dict(zip(names, rwkv_prep(p_ctx, w["rwkv"])))
    fl = dict(zip(names, rwkv_prep(p_lat, w["rwkv"])))
    s0 = jnp.zeros((2, B, RWKV_PAIRS, LANES, LANES), F32)
    y_c, s_ctx = rwkv_scan(fc, s0)
    y_l, _ = rwkv_scan(fl, s_ctx)
    out_l = rwkv_readout(y_l, fl["bonus"], fl["g"], w["rwkv_ln"])
    out_c = rwkv_readout(y_c, fc["bonus"], fc["g"], w["rwkv_ln"]) if need_ctx else None
    return out_c, out_l


def mlstm_mixer(p_ctx, p_lat, w, rope, need_ctx):
    B = p_lat.shape[0]
    H = MLSTM_HEADS
    state = (jnp.zeros((2, B, H, MLSTM_DK, MLSTM_DV), F32),
             jnp.zeros((2, B, H, 1, MLSTM_DK), F32),
             jnp.zeros((2, B, H, 1, 1), F32))
    outs = []
    for p, tabs in ((p_ctx, None), (p_lat, rope)):
        q, k, gates = mlstm_prep(p, w["mlstm_conv"], w["mlstm_gate_b"], tabs)
        gates_t = jnp.swapaxes(gates[:, :, :4 * H], 1, 2)
        h, *state = mlstm_scan(q, k, p, gates, gates_t, tuple(state))
        outs.append(h)
    out_l = mlstm_readout(outs[1], p_lat, w["mlstm_norm"])
    out_c = mlstm_readout(outs[0], p_ctx, w["mlstm_norm"]) if need_ctx else None
    return out_c, out_l


def na_mixer(p_ctx, p_lat, w, need_ctx):
    qc, kc, vc = na_prep(p_ctx, w["na_qk_norm"])
    ql, kl, vl = na_prep(p_lat, w["na_qk_norm"])
    rows = p_lat.shape[1] // GRID_W
    bias = na_bias_table(w["na_rpb"], rows)
    out_l = na_lat_attention(ql, kl, vl, kc, vc, bias)
    out_c = na_ctx_attention(qc, kc, vc) if need_ctx else None
    return out_c, out_l


def token_mixers(u_ctx, u_lat, w, rope, need_ctx):
    outs_c, outs_l = [], []
    pa = (matmul(u_ctx, w["in_a"]), matmul(u_lat, w["in_a"]))
    oc, ol = rwkv_mixer(pa[0], pa[1], w, need_ctx)
    outs_c.append(oc); outs_l.append(ol)
    pb = (matmul(u_ctx, w["in_b"]), matmul(u_lat, w["in_b"]))
    outs_l.append(pool_mixer(pb[1], w["pool_w"], w["pool_scale"]))
    outs_c.append(pool_mixer(pb[0], w["pool_w"], w["pool_scale"]) if need_ctx else None)
    pc = (matmul(u_ctx, w["in_c"]), matmul(u_lat, w["in_c"]))
    oc, ol = mlstm_mixer(pc[0], pc[1], w, rope, need_ctx)
    outs_c.append(oc); outs_l.append(ol)
    pd = (matmul(u_ctx, w["in_d"]), matmul(u_lat, w["in_d"]))
    oc, ol = na_mixer(pd[0], pd[1], w, need_ctx)
    outs_c.append(oc); outs_l.append(ol)
    return outs_c, outs_l


def kernel(x, c, ctx, c_ctx, ada_w, ada_b, norm_mix, norm_ffn, w_in, w_out, rwkv_mu, rwkv_w0, rwkv_w_up, rwkv_a0, rwkv_a_up, rwkv_g_up, rwkv_k_k, rwkv_k_a, rwkv_r_k, rwkv_ln, pool_w, pool_scale, mlstm_conv, mlstm_gate_b, mlstm_norm, na_qk_norm, na_rpb, router_g_w, router_g_b, router_e_w, router_e_b, moe_w_gate, moe_w_up, moe_w_down):
    P = dict(w_in=w_in, w_out=w_out, rwkv_mu=rwkv_mu, rwkv_w0=rwkv_w0, rwkv_w_up=rwkv_w_up,
             rwkv_a0=rwkv_a0, rwkv_a_up=rwkv_a_up, rwkv_g_up=rwkv_g_up, rwkv_k_k=rwkv_k_k,
             rwkv_k_a=rwkv_k_a, rwkv_r_k=rwkv_r_k, rwkv_ln=rwkv_ln, pool_w=pool_w, pool_scale=pool_scale,
             mlstm_conv=mlstm_conv, mlstm_gate_b=mlstm_gate_b, mlstm_norm=mlstm_norm,
             na_qk_norm=na_qk_norm, na_rpb=na_rpb, router_g_w=router_g_w, router_g_b=router_g_b,
             router_e_w=router_e_w, router_e_b=router_e_b, moe_w_gate=moe_w_gate, moe_w_up=moe_w_up,
             moe_w_down=moe_w_down)
    B, T, D = x.shape
    Lc = ctx.shape[1]
    depth = ada_w.shape[0]
    rope = rope_tables(T)
    moe_w = (moe_w_gate, moe_w_up, moe_w_down)

    cvec = jnp.concatenate([c, c_ctx[None, :], jnp.zeros((SUBLANES - B - 1, D), F32)], axis=0)
    mods = ada_modulation(cvec, ada_w, ada_b)

    def layer_mods(l):
        m = mods[l].reshape(SUBLANES, 6, D)
        return ([m[:B, i][:, None, :] for i in range(6)],
                [jnp.broadcast_to(m[B, i][None, None, :], (B, 1, D)) for i in range(6)])

    h_lat, h_ctx = x, ctx
    u_lat = None
    for l in range(depth):
        last = l == depth - 1
        w = _layer_weights(l, P)
        m_lat, m_ctx = layer_mods(l)

        if u_lat is None:
            u_lat = norm_modulate(h_lat, norm_mix[l], m_lat[0], m_lat[1])
        u_ctx = norm_modulate(h_ctx, norm_mix[l], m_ctx[0], m_ctx[1])
        ys_ctx, ys_lat = token_mixers(u_ctx, u_lat, w, rope, not last)
        h_lat = out_proj_residual(ys_lat, w["out"], h_lat, m_lat[2])
        n_rows = B * T if last else B * (T + Lc)
        streams = [(h_lat, m_lat[3], m_lat[4])]
        if not last:
            h_ctx = out_proj_residual(ys_ctx, w["out"], h_ctx, m_ctx[2])
            streams.append((h_ctx, m_ctx[3], m_ctx[4]))
        rows, ids, wts = ffn_norm_router(streams, norm_ffn[l], w["router_w"], w["router_b"])
        f2 = hier_moe(rows, ids[:, :TOP_K], wts[:, :TOP_K], *moe_w, l)
        if last:
            h_lat = moe_combine(h_lat, f2, m_lat[5], 0, n_rows)
        else:
            h_ctx = moe_combine(h_ctx, f2, m_ctx[5], B * T, n_rows)
            nxt = layer_mods(l + 1)[0]
            h_lat, u_lat = moe_combine(h_lat, f2, m_lat[5], 0, n_rows,
                                       next_norm=(norm_mix[l + 1], nxt[0], nxt[1]))
    return h_lat
```

```python
import functools
import math

import numpy as np
import jax
import jax.numpy as jnp
from jax import lax
from jax.experimental import pallas as pl
from jax.experimental.pallas import tpu as pltpu

F32 = jnp.float32
BF16 = jnp.bfloat16

D_MODEL = 4096
DEPTH = 2
GRID_W = 64
EPS = 1e-6
GROUP_W = 1024

RWKV_HEAD = 64
RWKV_DECAY_RANK = 64
RWKV_ICLR_RANK = 64
RWKV_GATE_RANK = 160
RWKV_GN_EPS = 64e-5
RWKV_COLS = 3 * GROUP_W + 2 * RWKV_DECAY_RANK + 2 * RWKV_ICLR_RANK + RWKV_GATE_RANK
RWKV_PAD_COLS = 3 * GROUP_W + 256 + 256

POOL_WINDOWS = (2, 4, 8, 16)
POOL_GROUP = 256
POOL_HALO = 8

MLSTM_HEADS = 4
MLSTM_DV = 256
MLSTM_DK = 128
MLSTM_COLS = 2 * MLSTM_HEADS * MLSTM_DK + 2 * GROUP_W + 4 * MLSTM_HEADS
MLSTM_GATE_COL = 2 * MLSTM_HEADS * MLSTM_DK + 2 * GROUP_W
MLSTM_PAD_COLS = MLSTM_GATE_COL + 512
ROPE_BASE = 10000.0

NA_HEADS = 8
NA_HEAD = 128
WIN_H = 8
WIN_W = 16
NA_COLS = 3 * GROUP_W
NA_QROWS = 4
NA_KROWS = 3 * NA_QROWS

N_GROUPS = 4
EXPERTS_PER_GROUP = 4
N_EXPERTS = 16
TOP_K = 2
EXPERT_FF = 1024
MOE_BLOCK = 256
MOE_FF_TILE = 256
MOE_OUT_TILE = 1024
MOE_STAGE_BYTES = 2 * 1024 * 1024

LANES = 128
SUBLANES = 8
VMEM_LIMIT = 56 * 1024 * 1024
NEG = -1e30

RWKV_CHUNK = 64
RWKV_PAIRS = 8
MLSTM_CHUNK = 256


def _cp(*sem):
    return pltpu.CompilerParams(dimension_semantics=sem, vmem_limit_bytes=VMEM_LIMIT)


def _dot(a, b):
    return jnp.dot(a.astype(BF16), b.astype(BF16), preferred_element_type=F32)


def _dot_nt(a, b):
    return lax.dot_general(a.astype(BF16), b.astype(BF16), (((1,), (1,)), ((), ())),
                           preferred_element_type=F32)


def _dot_tn(a, b):
    return lax.dot_general(a.astype(BF16), b.astype(BF16), (((0,), (0,)), ((), ())),
                           preferred_element_type=F32)


def _split2(x):
    hi = x.astype(BF16)
    lo = (x - hi.astype(F32)).astype(BF16)
    return hi, lo


def _dot3(a, b, dot=_dot):
    ah, al = _split2(a)
    bh, bl = _split2(b)
    return dot(ah, bh) + (dot(ah, bl) + dot(al, bh))


def _dot_exact_lhs(a_bf16, b):
    b0 = b.astype(BF16)
    r1 = b - b0.astype(F32)
    b1 = r1.astype(BF16)
    b2 = (r1 - b1.astype(F32)).astype(BF16)
    return _dot(a_bf16, b0) + (_dot(a_bf16, b1) + _dot(a_bf16, b2))


def _sigmoid(x):
    return 1.0 / (1.0 + jnp.exp(-x))


def _silu(x):
    return x * _sigmoid(x)


def _log_sigmoid(x):
    return jnp.minimum(x, 0.0) - jnp.log(1.0 + jnp.exp(-jnp.abs(x)))


def _pack_bf16_pair(lo, hi):
    lo_bits = lax.bitcast_convert_type(lo.astype(BF16).astype(F32), jnp.uint32)
    hi_bits = lax.bitcast_convert_type(hi.astype(BF16).astype(F32), jnp.uint32)
    return (lo_bits >> 16) | hi_bits


def _unpack_bf16_pair(w):
    lo = lax.bitcast_convert_type(w << 16, F32)
    hi = lax.bitcast_convert_type(w & jnp.uint32(0xFFFF0000), F32)
    return lo, hi


def _head_sum(x, width):
    r = lax.broadcasted_iota(jnp.int32, (LANES, LANES), 0) // width
    c = lax.broadcasted_iota(jnp.int32, (LANES, LANES), 1) // width
    e = jnp.where(r == c, 1.0, 0.0).astype(BF16)
    outs = []
    for j in range(x.shape[1] // LANES):
        hi, lo = _split2(x[:, j * LANES:(j + 1) * LANES])
        outs.append(_dot(hi, e) + _dot(lo, e))
    return jnp.concatenate(outs, axis=1)


def _ada_kernel(c_ref, w_ref, b_ref, o_ref):
    s = _silu(c_ref[...])
    o_ref[0] = _dot(s, w_ref[0]) + b_ref[0]


def ada_modulation(cvec, ada_w, ada_b):
    L, D, N = ada_w.shape
    tn = 1024
    return pl.pallas_call(
        _ada_kernel,
        grid=(L, N // tn),
        in_specs=[pl.BlockSpec((SUBLANES, D), lambda l, j: (0, 0)),
                  pl.BlockSpec((1, D, tn), lambda l, j: (l, 0, j)),
                  pl.BlockSpec((1, 1, tn), lambda l, j: (l, 0, j))],
        out_specs=pl.BlockSpec((1, SUBLANES, tn), lambda l, j: (l, 0, j)),
        out_shape=jax.ShapeDtypeStruct((L, SUBLANES, N), F32),
        compiler_params=_cp("parallel", "parallel"),
        name="ada_modulation",
    )(cvec, ada_w, ada_b.reshape(L, 1, N))


def _normmod_kernel(h_ref, g_ref, sh_ref, sc_ref, o_ref):
    x = h_ref[0]
    y = x * lax.rsqrt(jnp.mean(x * x, axis=-1, keepdims=True) + EPS) * g_ref[...]
    o_ref[0] = (y * (1.0 + sc_ref[0]) + sh_ref[0]).astype(o_ref.dtype)


def norm_modulate(h, g, shift, scale):
    B, T, D = h.shape
    tm = 256
    vec = pl.BlockSpec((1, 1, D), lambda b, i: (b, 0, 0))
    return pl.pallas_call(
        _normmod_kernel,
        grid=(B, T // tm),
        in_specs=[pl.BlockSpec((1, tm, D), lambda b, i: (b, i, 0)),
                  pl.BlockSpec((1, D), lambda b, i: (0, 0)), vec, vec],
        out_specs=pl.BlockSpec((1, tm, D), lambda b, i: (b, i, 0)),
        out_shape=jax.ShapeDtypeStruct((B, T, D), BF16),
        compiler_params=_cp("parallel", "parallel"),
        name="norm_modulate",
    )(h, g.reshape(1, D), shift, scale)


def _mm_kernel(x_ref, w_ref, o_ref):
    o_ref[0] = jnp.dot(x_ref[0], w_ref[...], preferred_element_type=F32).astype(o_ref.dtype)


def _pick_tile(n, cands):
    for t in cands:
        if n % t == 0:
            return t
    raise ValueError(n)


def matmul(x, w, out_dtype=F32):
    B, T, K = x.shape
    N = w.shape[1]
    tm = _pick_tile(T, (1024, 512, 256))
    tn = _pick_tile(N, (1024, 512))
    return pl.pallas_call(
        _mm_kernel,
        grid=(B, T // tm, N // tn),
        in_specs=[pl.BlockSpec((1, tm, K), lambda b, i, j: (b, i, 0)),
                  pl.BlockSpec((K, tn), lambda b, i, j: (0, j))],
        out_specs=pl.BlockSpec((1, tm, tn), lambda b, i, j: (b, i, j)),
        out_shape=jax.ShapeDtypeStruct((B, T, N), out_dtype),
        compiler_params=_cp("parallel", "parallel", "arbitrary"),
        name="in_proj",
    )(x, w)


def _outproj_kernel(ya_ref, yb_ref, yc_ref, yd_ref, w_ref, h_ref, gate_ref, o_ref):
    acc = jnp.dot(ya_ref[0], w_ref[0:GROUP_W, :], preferred_element_type=F32)
    acc += jnp.dot(yb_ref[0], w_ref[GROUP_W:2 * GROUP_W, :], preferred_element_type=F32)
    acc += jnp.dot(yc_ref[0], w_ref[2 * GROUP_W:3 * GROUP_W, :], preferred_element_type=F32)
    acc += jnp.dot(yd_ref[0], w_ref[3 * GROUP_W:4 * GROUP_W, :], preferred_element_type=F32)
    o_ref[0] = h_ref[0] + gate_ref[0] * acc


def out_proj_residual(ys, w, h, gate):
    B, T, D = h.shape
    tm = _pick_tile(T, (512, 256))
    tn = 1024
    ysp = pl.BlockSpec((1, tm, GROUP_W), lambda b, i, j: (b, i, 0))
    return pl.pallas_call(
        _outproj_kernel,
        grid=(B, T // tm, D // tn),
        in_specs=[ysp, ysp, ysp, ysp,
                  pl.BlockSpec((D, tn), lambda b, i, j: (0, j)),
                  pl.BlockSpec((1, tm, tn), lambda b, i, j: (b, i, j)),
                  pl.BlockSpec((1, 1, tn), lambda b, i, j: (b, 0, j))],
        out_specs=pl.BlockSpec((1, tm, tn), lambda b, i, j: (b, i, j)),
        out_shape=jax.ShapeDtypeStruct((B, T, D), F32),
        compiler_params=_cp("parallel", "parallel", "arbitrary"),
        name="out_proj",
    )(*ys, w, h, gate)


def _halo_specs(tm, width, halo=SUBLANES):
    per = tm // halo

    def prev_map(b, i):
        return (b, jnp.maximum(i * per - 1, 0), 0)

    def next_map(nb):
        return lambda b, i: (b, jnp.minimum((i + 1) * per, nb - 1), 0)

    return prev_map, next_map


def _shift_rows(x, prev_row, next_row):
    tm = x.shape[0]
    row = lax.broadcasted_iota(jnp.int32, x.shape, 0)
    prev = jnp.where(row == 0, prev_row, pltpu.roll(x, 1, axis=0))
    nxt = jnp.where(row == tm - 1, next_row, pltpu.roll(x, tm - 1, axis=0))
    return prev, nxt


def _rwkv_prep_kernel(p_ref, pp_ref, pn_ref, mu_ref, wup_ref, w0_ref, aup_ref, a0_ref, gup_ref,
                      kk_w_ref, ka_w_ref, rk_w_ref,
                      r_ref, kk_ref, v_ref, bonus_ref, g_ref, lw_ref, kd_ref, bb_ref):
    i = pl.program_id(1)
    last = pl.num_programs(1) - 1
    x = p_ref[0]
    prev_row = jnp.where(i == 0, 0.0, pp_ref[0, SUBLANES - 1:SUBLANES, :])
    next_row = jnp.where(i == last, 0.0, pn_ref[0, 0:1, :])
    prev, nxt = _shift_rows(x, prev_row, next_row)
    x = x + mu_ref[0:1, :] * (prev - x) + mu_ref[1:2, :] * (nxt - x)

    r = x[:, 0:GROUP_W]
    k = x[:, GROUP_W:2 * GROUP_W]
    v = x[:, 2 * GROUP_W:3 * GROUP_W]
    low = x[:, 3 * GROUP_W:3 * GROUP_W + 256]
    gd = x[:, 3 * GROUP_W + 256:3 * GROUP_W + 512]

    zw = _dot(jnp.tanh(low[:, 0:LANES]), wup_ref[...]) + w0_ref[...]
    za = _dot(low[:, LANES:2 * LANES], aup_ref[...]) + a0_ref[...]
    kkr = k * kk_w_ref[...]
    kk = kkr * lax.rsqrt(_head_sum(kkr * kkr, RWKV_HEAD) + EPS)
    r_ref[0] = r.astype(BF16)
    kk_ref[0] = kk.astype(BF16)
    v_ref[0] = v.astype(BF16)
    bonus_ref[0] = (_head_sum(r * k * rk_w_ref[...], RWKV_HEAD) * v).astype(BF16)
    g_ref[0] = _dot(_sigmoid(gd), gup_ref[...]).astype(BF16)
    for d in range(2):
        sl = slice(d * GROUP_W, (d + 1) * GROUP_W)
        lw_ref[d, 0] = (-math.exp(-0.5)) * _sigmoid(zw[:, sl])
        a = _sigmoid(za[:, sl])
        kd_ref[d, 0] = (k * (1.0 + (a - 1.0) * ka_w_ref[...])).astype(BF16)
        bb_ref[d, 0] = (kk * a).astype(BF16)


def rwkv_prep(p, w):
    B, T, W = p.shape
    tm = 128
    prev_map, next_map = _halo_specs(tm, W)
    nb8 = T // SUBLANES
    full = lambda shape: pl.BlockSpec(shape, lambda b, i: tuple(0 for _ in shape))
    tok = pl.BlockSpec((1, tm, GROUP_W), lambda b, i: (b, i, 0))
    tokd = pl.BlockSpec((2, 1, tm, GROUP_W), lambda b, i: (0, b, i, 0))
    sh = jax.ShapeDtypeStruct((B, T, GROUP_W), BF16)
    shd = jax.ShapeDtypeStruct((2, B, T, GROUP_W), BF16)
    shd_f32 = jax.ShapeDtypeStruct((2, B, T, GROUP_W), F32)
    return pl.pallas_call(
        _rwkv_prep_kernel,
        grid=(B, T // tm),
        in_specs=[pl.BlockSpec((1, tm, W), lambda b, i: (b, i, 0)),
                  pl.BlockSpec((1, SUBLANES, W), prev_map),
                  pl.BlockSpec((1, SUBLANES, W), next_map(nb8)),
                  full((2, W)), full((LANES, 2 * GROUP_W)), full((1, 2 * GROUP_W)),
                  full((LANES, 2 * GROUP_W)), full((1, 2 * GROUP_W)), full((256, GROUP_W)),
                  full((1, GROUP_W)), full((1, GROUP_W)), full((1, GROUP_W))],
        out_specs=[tok, tok, tok, tok, tok, tokd, tokd, tokd],
        out_shape=[sh, sh, sh, sh, sh, shd_f32, shd, shd],
        compiler_params=_cp("parallel", "parallel"),
        name="rwkv_prep",
    )(p, p, p, w["mu"], w["w_up"], w["w0"], w["a_up"], w["a0"], w["g_up"],
      w["k_k"], w["k_a"], w["r_k"])


def _dform(x, m0):
    return jnp.concatenate([jnp.where(m0, x, 0.0), jnp.where(m0, 0.0, x)], axis=0)


def _rwkv_chunk(tiles, sts, rev):
    n = len(tiles)
    C = tiles[0][0].shape[0]
    C2 = 2 * C
    sgn = jnp.where(rev, -1, 1)
    rr = lax.broadcasted_iota(jnp.int32, (C, C), 0)
    cc = lax.broadcasted_iota(jnp.int32, (C, C), 1)
    tri = jnp.where((rr - cc) * sgn >= 0, 1.0, 0.0).astype(BF16)
    lane = lax.broadcasted_iota(jnp.int32, (C, LANES), 1)
    m0 = lane < RWKV_HEAD
    rd = lax.broadcasted_iota(jnp.int32, (C2, C2), 0)
    cd = lax.broadcasted_iota(jnp.int32, (C2, C2), 1)
    same = (rd // C) == (cd // C)
    ahead = (rd % C - cd % C) * sgn
    strict = same & (ahead > 0)
    incl = same & (ahead >= 0)
    diag = rd == cd
    each = range(n)

    lins = [_dot_exact_lhs(tri, t[3]) for t in tiles]
    ltots = [jnp.sum(t[3], axis=0, keepdims=True) for t in tiles]
    a_d, r_d, k_d, b_d, v_d, kw_d, bw_d = [], [], [], [], [], [], []
    for (r, kk, v, lw, kd, bb), lin, ltot in zip(tiles, lins, ltots):
        einv = jnp.exp(-lin)
        ew = jnp.exp(ltot - lin)
        a_d.append(_dform(-kk * jnp.exp(lin - lw), m0).astype(BF16))
        r_d.append(_dform(r * jnp.exp(lin), m0).astype(BF16))
        k_d.append(_dform(kd * einv, m0).astype(BF16))
        b_d.append(_dform(bb * einv, m0).astype(BF16))
        v_d.append(_dform(v, m0).astype(BF16))
        kw_d.append(_dform(kd * ew, m0).astype(BF16))
        bw_d.append(_dform(bb * ew, m0).astype(BF16))

    grams = [_dot_nt(jnp.concatenate([a_d[i], r_d[i]], axis=0), jnp.concatenate([k_d[i], b_d[i]], axis=0))
             for i in each]
    a_ak = [jnp.where(strict, g[:C2, :C2], 0.0).astype(BF16) for g in grams]
    a_ab = [jnp.where(strict, g[:C2, C2:], 0.0) for g in grams]
    a_rk = [jnp.where(incl, g[C2:, :C2], 0.0).astype(BF16) for g in grams]
    a_rb = [jnp.where(incl, g[C2:, C2:], 0.0).astype(BF16) for g in grams]

    tinv = [jnp.where(diag, 1.0, 0.0) + a for a in a_ab]
    apow = a_ab
    for _ in range(int(math.log2(C)) - 1):
        apow = [_dot(a, a) for a in apow]
        tinv = [t + _dot(t, a) for t, a in zip(tinv, apow)]

    av = [_dot(jnp.concatenate([a_ak[i], a_rk[i]], axis=0), v_d[i]) for i in each]
    p = [_dot(tinv[i], jnp.concatenate([a_d[i], av[i][:C2].astype(BF16)], axis=1)) for i in each]
    ur = [_dot(jnp.concatenate([p[i][:, :LANES].astype(BF16), r_d[i]], axis=0), sts[i]) for i in each]
    u = [ur[i][:C2] + p[i][:, LANES:] for i in each]
    y2 = [ur[i][C2:] + av[i][C2:] + _dot(a_rb[i], u[i]) for i in each]
    ys = [y[:C] + y[C:] for y in y2]
    new = []
    for i in each:
        wc = jnp.broadcast_to(jnp.exp(ltots[i]), (LANES, LANES))
        wcol = jnp.sum(jnp.where(diag, wc, 0.0), axis=1, keepdims=True)
        new.append(sts[i] * wcol + _dot_tn(jnp.concatenate([bw_d[i], kw_d[i]], axis=0),
                                           jnp.concatenate([u[i].astype(BF16), v_d[i]], axis=0)))
    return ys, new


def _rwkv_scan_kernel(r_ref, kk_ref, v_ref, lw_ref, kd_ref, bb_ref, s0_ref, y_ref, sT_ref, st_scr,
                      *, npairs):
    d = pl.program_id(0)
    c = pl.program_id(3)
    nc = pl.num_programs(3)

    @pl.when(c == 0)
    def _():
        st_scr[...] = s0_ref[0, 0]

    lanes = [slice(j * LANES, (j + 1) * LANES) for j in range(npairs)]
    f32 = lambda a: a.astype(F32)
    tiles = [(f32(r_ref[0, :, sl]), f32(kk_ref[0, :, sl]), f32(v_ref[0, :, sl]),
              lw_ref[0, 0, :, sl], f32(kd_ref[0, 0, :, sl]), f32(bb_ref[0, 0, :, sl])) for sl in lanes]
    ys, new = _rwkv_chunk(tiles, [st_scr[j] for j in range(npairs)], d == 1)
    for j, sl in enumerate(lanes):
        y_ref[0, 0, :, sl] = ys[j]
        st_scr[j] = new[j]

    @pl.when(c == nc - 1)
    def _():
        sT_ref[0, 0] = st_scr[...]


def rwkv_scan(f, s0, npairs=RWKV_PAIRS):
    r, kk, v, lw, kd, bb = f["r"], f["kk"], f["v"], f["lw"], f["kd"], f["bb"]
    B, T, _ = r.shape
    C = RWKV_CHUNK
    nc = T // C
    ng = RWKV_PAIRS // npairs
    W = npairs * LANES
    chunk = lambda d, c: c + d * (nc - 1 - 2 * c)
    tok = pl.BlockSpec((1, C, W), lambda d, b, g, c: (b, chunk(d, c), g))
    tokd = pl.BlockSpec((1, 1, C, W), lambda d, b, g, c: (d, b, chunk(d, c), g))
    stsp = pl.BlockSpec((1, 1, npairs, LANES, LANES), lambda d, b, g, c: (d, b, g, 0, 0))
    return pl.pallas_call(
        functools.partial(_rwkv_scan_kernel, npairs=npairs),
        grid=(2, B, ng, nc),
        in_specs=[tok, tok, tok, tokd, tokd, tokd, stsp],
        out_specs=[tokd, stsp],
        out_shape=[jax.ShapeDtypeStruct((2, B, T, GROUP_W), F32),
                   jax.ShapeDtypeStruct(s0.shape, F32)],
        scratch_shapes=[pltpu.VMEM((npairs, LANES, LANES), F32)],
        compiler_params=_cp("parallel", "parallel", "parallel", "arbitrary"),
        name="rwkv_scan",
    )(r, kk, v, lw, kd, bb, s0)


def _rwkv_readout_kernel(yf_ref, yb_ref, bonus_ref, g_ref, ln_ref, o_ref):
    y = yf_ref[0, 0] + yb_ref[0, 0]
    mean = _head_sum(y, RWKV_HEAD) * (1.0 / RWKV_HEAD)
    cen = y - mean
    var = _head_sum(cen * cen, RWKV_HEAD) * (1.0 / RWKV_HEAD)
    yn = cen * lax.rsqrt(var + RWKV_GN_EPS) * ln_ref[0:1, :] + ln_ref[1:2, :]
    o_ref[0] = ((yn + bonus_ref[0].astype(F32)) * g_ref[0].astype(F32)).astype(o_ref.dtype)


def rwkv_readout(y, bonus, g, ln):
    _, B, T, W = y.shape
    tm = 256
    tok = pl.BlockSpec((1, tm, W), lambda b, i: (b, i, 0))
    return pl.pallas_call(
        _rwkv_readout_kernel,
        grid=(B, T // tm),
        in_specs=[pl.BlockSpec((1, 1, tm, W), lambda b, i: (0, b, i, 0)),
                  pl.BlockSpec((1, 1, tm, W), lambda b, i: (1, b, i, 0)),
                  tok, tok, pl.BlockSpec((2, W), lambda b, i: (0, 0))],
        out_specs=tok,
        out_shape=jax.ShapeDtypeStruct((B, T, W), BF16),
        compiler_params=_cp("parallel", "parallel"),
        name="rwkv_readout",
    )(y, y, bonus, g, ln)


def _pool_kernel(p_ref, pp_ref, pn_ref, w_ref, sc_ref, o_ref, *, total):
    i = pl.program_id(1)
    last = pl.num_programs(1) - 1
    x = p_ref[0]
    tm = x.shape[0]
    n = tm + 2 * POOL_HALO
    ext = jnp.concatenate([jnp.where(i == 0, 0.0, pp_ref[0]), x,
                           jnp.where(i == last, 0.0, pn_ref[0])], axis=0)
    t = i * tm + lax.broadcasted_iota(jnp.int32, (tm, 1), 0)
    outs = []
    for gi, win in enumerate(POOL_WINDOWS):
        sl = slice(gi * POOL_GROUP, (gi + 1) * POOL_GROUP)
        e = ext[:, sl]
        s = e + pltpu.roll(e, 1, axis=0)
        step = 1
        while 2 * step < win:
            s = pltpu.roll(s, step, axis=0) + pltpu.roll(s, n - step, axis=0)
            step *= 2
        h = win // 2
        cnt = (jnp.minimum(t + h, total) - jnp.maximum(t - h, 0)).astype(F32)
        z = s[POOL_HALO:POOL_HALO + tm] / cnt - x[:, sl]
        outs.append(_dot(z, w_ref[gi]))
    o_ref[0] = (jnp.concatenate(outs, axis=1) * sc_ref[...]).astype(o_ref.dtype)


def pool_mixer(p, pool_w, pool_scale):
    B, T, W = p.shape
    tm = 256
    prev_map, next_map = _halo_specs(tm, W)
    return pl.pallas_call(
        functools.partial(_pool_kernel, total=T),
        grid=(B, T // tm),
        in_specs=[pl.BlockSpec((1, tm, W), lambda b, i: (b, i, 0)),
                  pl.BlockSpec((1, POOL_HALO, W), prev_map),
                  pl.BlockSpec((1, POOL_HALO, W), next_map(T // POOL_HALO)),
                  pl.BlockSpec(pool_w.shape, lambda b, i: (0, 0, 0)),
                  pl.BlockSpec((1, W), lambda b, i: (0, 0))],
        out_specs=pl.BlockSpec((1, tm, W), lambda b, i: (b, i, 0)),
        out_shape=jax.ShapeDtypeStruct((B, T, W), BF16),
        compiler_params=_cp("parallel", "parallel"),
        name="pool_mixer",
    )(p, p, p, pool_w, pool_scale.reshape(1, W))


def _mlstm_prep_kernel(p_ref, pp_ref, pn_ref, g_ref, cw_ref, gb_ref, *rest, rope):
    if rope:
        cos_ref, sin_ref, q_ref, k_ref, go_ref = rest
    else:
        q_ref, k_ref, go_ref = rest
    i = pl.program_id(1)
    last = pl.num_programs(1) - 1
    x = p_ref[0]
    prev_row = jnp.where(i == 0, 0.0, pp_ref[0, SUBLANES - 1:SUBLANES, :])
    next_row = jnp.where(i == last, 0.0, pn_ref[0, 0:1, :])
    prev, nxt = _shift_rows(x, prev_row, next_row)
    qk = _silu(cw_ref[0:1, :] * prev + cw_ref[1:2, :] * x + cw_ref[2:3, :] * nxt)
    if rope:
        w = qk.shape[1]
        lane = lax.broadcasted_iota(jnp.int32, qk.shape, 1)
        partner = jnp.where(lane % 2 == 0, pltpu.roll(qk, w - 1, axis=1), pltpu.roll(qk, 1, axis=1))
        reps = w // LANES
        cos = jnp.concatenate([cos_ref[...]] * reps, axis=1)
        sin = jnp.concatenate([sin_ref[...]] * reps, axis=1)
        qk = qk * cos + partner * sin
    half = MLSTM_HEADS * MLSTM_DK
    q_ref[0] = qk[:, :half].astype(BF16)
    k_ref[0] = (qk[:, half:] * (MLSTM_DK ** -0.5)).astype(BF16)
    g = g_ref[0] + gb_ref[...]
    lane = lax.broadcasted_iota(jnp.int32, g.shape, 1)
    go_ref[0] = jnp.where((lane // MLSTM_HEADS) % 2 == 1, _log_sigmoid(g), g)


def mlstm_prep(p, conv_w, gate_b, rope_tabs):
    B, T, W = p.shape
    tm = 256
    QK = 2 * MLSTM_HEADS * MLSTM_DK
    prev_map, next_map = _halo_specs(tm, QK)
    rope = rope_tabs is not None
    in_specs = [pl.BlockSpec((1, tm, QK), lambda b, i: (b, i, 0)),
                pl.BlockSpec((1, SUBLANES, QK), prev_map),
                pl.BlockSpec((1, SUBLANES, QK), next_map(T // SUBLANES)),
                pl.BlockSpec((1, tm, LANES), lambda b, i: (b, i, MLSTM_GATE_COL // LANES)),
                pl.BlockSpec((3, QK), lambda b, i: (0, 0)),
                pl.BlockSpec((1, LANES), lambda b, i: (0, 0))]
    args = [p, p, p, p, conv_w, gate_b]
    if rope:
        in_specs += [pl.BlockSpec((tm, LANES), lambda b, i: (i, 0))] * 2
        args += list(rope_tabs)
    half = MLSTM_HEADS * MLSTM_DK
    return pl.pallas_call(
        functools.partial(_mlstm_prep_kernel, rope=rope),
        grid=(B, T // tm),
        in_specs=in_specs,
        out_specs=[pl.BlockSpec((1, tm, half), lambda b, i: (b, i, 0)),
                   pl.BlockSpec((1, tm, half), lambda b, i: (b, i, 0)),
                   pl.BlockSpec((1, tm, LANES), lambda b, i: (b, i, 0))],
        out_shape=[jax.ShapeDtypeStruct((B, T, half), BF16),
                   jax.ShapeDtypeStruct((B, T, half), BF16),
                   jax.ShapeDtypeStruct((B, T, LANES), F32)],
        compiler_params=_cp("parallel", "parallel"),
        name="mlstm_prep",
    )(*args)


def _mlstm_chunk_kernel(q_ref, k_ref, v_ref, gc_ref, gr_ref, c0_ref, n0_ref, m0_ref,
                        h_ref, cT_ref, nT_ref, mT_ref, c_scr, n_scr, m_scr):
    d = pl.program_id(0)
    c = pl.program_id(2)
    nc = pl.num_programs(2)
    H = MLSTM_HEADS

    @pl.when(c == 0)
    def _():
        c_scr[...] = c0_ref[0, 0]
        n_scr[...] = n0_ref[0, 0]
        m_scr[...] = m0_ref[0, 0]

    rev = d == 1
    L = q_ref.shape[1]
    rr = lax.broadcasted_iota(jnp.int32, (L, L), 0)
    cc = lax.broadcasted_iota(jnp.int32, (L, L), 1)
    sgn = jnp.where(rev, -1, 1)
    seen = (rr - cc) * sgn >= 0
    tri = jnp.where(seen, 1.0, 0.0).astype(BF16)
    tri_t = jnp.where((cc - rr) * sgn >= 0, 1.0, 0.0).astype(BF16)

    gc = gc_ref[0]
    gr = gr_ref[0]
    lane = lax.broadcasted_iota(jnp.int32, gc.shape, 1)
    subl = lax.broadcasted_iota(jnp.int32, gr.shape, 0)
    pick_c = lambda a, idx: jnp.sum(jnp.where(lane == idx, a, 0.0), axis=1, keepdims=True)
    pick_r = lambda a, idx: jnp.sum(jnp.where(subl == idx, a, 0.0), axis=0, keepdims=True)
    bc_all = _dot_exact_lhs(tri, gc)
    gr_hi = gr.astype(BF16)
    gr_r1 = gr - gr_hi.astype(F32)
    gr_mid = gr_r1.astype(BF16)
    gr_lo = (gr_r1 - gr_mid.astype(F32)).astype(BF16)
    br_all = _dot(gr_hi, tri_t) + (_dot(gr_mid, tri_t) + _dot(gr_lo, tri_t))

    heads = range(H)
    i_lane = [d * (2 * H) + j for j in heads]
    f_lane = [d * (2 * H) + H + j for j in heads]
    ig_c = [pick_c(gc, i_lane[j]) for j in heads]
    lf_c = [pick_c(gc, f_lane[j]) for j in heads]
    b_c = [pick_c(bc_all, f_lane[j]) for j in heads]
    ig_r = [pick_r(gr, i_lane[j]) for j in heads]
    b_r = [pick_r(br_all, f_lane[j]) for j in heads]
    m_prev = [m_scr[j] for j in heads]
    q = [q_ref[0, :, j * MLSTM_DK:(j + 1) * MLSTM_DK] for j in heads]
    k = [k_ref[0, :, j * MLSTM_DK:(j + 1) * MLSTM_DK] for j in heads]
    v = [v_ref[0, :, j * MLSTM_DV:(j + 1) * MLSTM_DV].astype(BF16) for j in heads]
    cst = [c_scr[j] for j in heads]
    nst = [n_scr[j] for j in heads]

    qk = [_dot_nt(q[j], k[j]) for j in heads]
    qc = [_dot(q[j], cst[j]) for j in heads]
    dlog = [jnp.where(seen, b_c[j] - b_r[j] + ig_r[j], NEG) for j in heads]
    inter = [b_c[j] + m_prev[j] for j in heads]
    m_t = [jnp.maximum(inter[j], jnp.max(dlog[j], axis=1, keepdims=True)) for j in heads]
    s = [qk[j] * jnp.exp(dlog[j] - m_t[j]) for j in heads]
    w_inter = [jnp.exp(inter[j] - m_t[j]) for j in heads]
    sv = [_dot(s[j], v[j]) for j in heads]
    for j in heads:
        num = sv[j] + w_inter[j] * qc[j]
        den = (jnp.sum(s[j], axis=1, keepdims=True)
               + w_inter[j] * jnp.sum(q[j].astype(F32) * nst[j], axis=1, keepdims=True))
        h_ref[0, 0, :, j * MLSTM_DV:(j + 1) * MLSTM_DV] = num / jnp.maximum(jnp.abs(den), jnp.exp(-m_t[j]))

    b_end = [jnp.sum(lf_c[j], axis=0, keepdims=True) for j in heads]
    g_s = [b_end[j] - b_c[j] + ig_c[j] for j in heads]
    m_new = [jnp.maximum(b_end[j] + m_prev[j], jnp.max(g_s[j], axis=0, keepdims=True)) for j in heads]
    kw = [k[j].astype(F32) * jnp.exp(g_s[j] - m_new[j]) for j in heads]
    kv = [_dot_tn(kw[j], v[j]) for j in heads]
    for j in heads:
        decay = jnp.exp(b_end[j] + m_prev[j] - m_new[j])
        c_scr[j] = decay * cst[j] + kv[j]
        n_scr[j] = decay * nst[j] + jnp.sum(kw[j], axis=0, keepdims=True)
        m_scr[j] = m_new[j]

    @pl.when(c == nc - 1)
    def _():
        cT_ref[0, 0] = c_scr[...]
        nT_ref[0, 0] = n_scr[...]
        mT_ref[0, 0] = m_scr[...]


def mlstm_scan(q, k, p, gates, gates_t, state):
    B, T, _ = q.shape
    L = min(MLSTM_CHUNK, T)
    nc = T // L
    H = MLSTM_HEADS
    QK = H * MLSTM_DK
    c0, n0, m0 = state
    chunk = lambda d, c: c + d * (nc - 1 - 2 * c)
    v_blk = (2 * QK) // GROUP_W
    csp = pl.BlockSpec((1, 1, H, MLSTM_DK, MLSTM_DV), lambda d, b, c: (d, b, 0, 0, 0))
    nsp = pl.BlockSpec((1, 1, H, 1, MLSTM_DK), lambda d, b, c: (d, b, 0, 0, 0))
    msp = pl.BlockSpec((1, 1, H, 1, 1), lambda d, b, c: (d, b, 0, 0, 0))
    return pl.pallas_call(
        _mlstm_chunk_kernel,
        grid=(2, B, nc),
        in_specs=[pl.BlockSpec((1, L, QK), lambda d, b, c: (b, chunk(d, c), 0)),
                  pl.BlockSpec((1, L, QK), lambda d, b, c: (b, chunk(d, c), 0)),
                  pl.BlockSpec((1, L, GROUP_W), lambda d, b, c: (b, chunk(d, c), v_blk)),
                  pl.BlockSpec((1, L, LANES), lambda d, b, c: (b, chunk(d, c), 0)),
                  pl.BlockSpec((1, 4 * H, L), lambda d, b, c: (b, 0, chunk(d, c))),
                  csp, nsp, msp],
        out_specs=[pl.BlockSpec((1, 1, L, GROUP_W), lambda d, b, c: (d, b, chunk(d, c), 0)),
                   csp, nsp, msp],
        out_shape=[jax.ShapeDtypeStruct((2, B, T, GROUP_W), F32),
                   jax.ShapeDtypeStruct(c0.shape, F32),
                   jax.ShapeDtypeStruct(n0.shape, F32),
                   jax.ShapeDtypeStruct(m0.shape, F32)],
        scratch_shapes=[pltpu.VMEM((H, MLSTM_DK, MLSTM_DV), F32),
                        pltpu.VMEM((H, 1, MLSTM_DK), F32),
                        pltpu.VMEM((H, 1, 1), F32)],
        compiler_params=_cp("parallel", "parallel", "arbitrary"),
        name="mlstm_scan",
    )(q, k, p, gates, gates_t, c0, n0, m0)


def _mlstm_readout_kernel(hf_ref, hb_ref, o_ref, nw_ref, out_ref):
    h = hf_ref[0, 0] + hb_ref[0, 0]
    outs = []
    for j in range(MLSTM_HEADS):
        hj = h[:, j * MLSTM_DV:(j + 1) * MLSTM_DV]
        outs.append(hj * lax.rsqrt(jnp.mean(hj * hj, axis=-1, keepdims=True) + EPS))
    hn = jnp.concatenate(outs, axis=1)
    out_ref[0] = (hn * nw_ref[...] * _sigmoid(o_ref[0])).astype(out_ref.dtype)


def mlstm_readout(h, p, norm_w):
    _, B, T, W = h.shape
    tm = 256
    o_blk = (2 * MLSTM_HEADS * MLSTM_DK + GROUP_W) // GROUP_W
    return pl.pallas_call(
        _mlstm_readout_kernel,
        grid=(B, T // tm),
        in_specs=[pl.BlockSpec((1, 1, tm, W), lambda b, i: (0, b, i, 0)),
                  pl.BlockSpec((1, 1, tm, W), lambda b, i: (1, b, i, 0)),
                  pl.BlockSpec((1, tm, W), lambda b, i: (b, i, o_blk)),
                  pl.BlockSpec((1, W), lambda b, i: (0, 0))],
        out_specs=pl.BlockSpec((1, tm, W), lambda b, i: (b, i, 0)),
        out_shape=jax.ShapeDtypeStruct((B, T, W), BF16),
        compiler_params=_cp("parallel", "parallel"),
        name="mlstm_readout",
    )(h, h, p, norm_w.reshape(1, W))


def _na_prep_kernel(p_ref, nw_ref, q_ref, k_ref, v_ref):
    x = p_ref[0]
    for j in range(NA_HEADS):
        sl = slice(j * NA_HEAD, (j + 1) * NA_HEAD)
        for src, dst, row, scale in ((0, q_ref, 0, NA_HEAD ** -0.5), (GROUP_W, k_ref, 1, 1.0)):
            z = x[:, src + j * NA_HEAD:src + (j + 1) * NA_HEAD]
            zn = z * lax.rsqrt(jnp.mean(z * z, axis=-1, keepdims=True) + EPS) * nw_ref[row:row + 1, :]
            dst[0, :, sl] = (zn * scale).astype(BF16)
    v_ref[0] = x[:, 2 * GROUP_W:].astype(BF16)


def na_prep(p, qk_norm):
    B, T, W = p.shape
    tm = 256
    tok = pl.BlockSpec((1, tm, GROUP_W), lambda b, i: (b, i, 0))
    sh = jax.ShapeDtypeStruct((B, T, GROUP_W), BF16)
    return pl.pallas_call(
        _na_prep_kernel,
        grid=(B, T // tm),
        in_specs=[pl.BlockSpec((1, tm, W), lambda b, i: (b, i, 0)),
                  pl.BlockSpec((2, NA_HEAD), lambda b, i: (0, 0))],
        out_specs=[tok, tok, tok],
        out_shape=[sh, sh, sh],
        compiler_params=_cp("parallel", "parallel"),
        name="na_prep",
    )(p, qk_norm)


def _softmax_av(s, v):
    m = jnp.max(s, axis=1, keepdims=True)
    e = jnp.exp(s - m)
    return _dot(e, v) * (1.0 / jnp.sum(e, axis=1, keepdims=True))


def _na_ctx_kernel(q_ref, k_ref, v_ref, o_ref):
    for j in range(NA_HEADS):
        sl = slice(j * NA_HEAD, (j + 1) * NA_HEAD)
        s = _dot_nt(q_ref[0, :, sl], k_ref[0, :, sl])
        o_ref[0, :, sl] = _softmax_av(s, v_ref[0, :, sl]).astype(o_ref.dtype)


def na_ctx_attention(q, k, v):
    B, T, W = q.shape
    tok = pl.BlockSpec((1, T, W), lambda b: (b, 0, 0))
    return pl.pallas_call(
        _na_ctx_kernel,
        grid=(B,),
        in_specs=[tok, tok, tok],
        out_specs=tok,
        out_shape=jax.ShapeDtypeStruct((B, T, W), BF16),
        compiler_params=_cp("parallel"),
        name="na_ctx_attention",
    )(q, k, v)


def _na_lat_kernel(q_ref, k0_ref, k1_ref, k2_ref, v0_ref, v1_ref, v2_ref, kc_ref, vc_ref, bias_ref, o_ref):
    for j in range(NA_HEADS):
        sl = slice(j * NA_HEAD, (j + 1) * NA_HEAD)
        q = q_ref[0, :, sl]
        kcat = jnp.concatenate([k0_ref[0, :, sl], k1_ref[0, :, sl], k2_ref[0, :, sl]], axis=0)
        vcat = jnp.concatenate([v0_ref[0, :, sl], v1_ref[0, :, sl], v2_ref[0, :, sl],
                                vc_ref[0, :, sl]], axis=0)
        s = jnp.concatenate([_dot_nt(q, kcat) + bias_ref[0, j], _dot_nt(q, kc_ref[0, :, sl])], axis=1)
        o_ref[0, :, sl] = _softmax_av(s, vcat).astype(o_ref.dtype)


def na_bias_table(rpb, rows):
    nq, nk = NA_QROWS, NA_KROWS
    starts = np.array([0, nq, rows - nq])
    r = starts[:, None] + np.arange(nq)[None, :]
    kr = starts[:, None] - nq + np.arange(nk)[None, :]
    r0 = np.clip(r - WIN_H // 2, 0, rows - WIN_H)
    rv = (kr[:, None, :] >= r0[:, :, None]) & (kr[:, None, :] < r0[:, :, None] + WIN_H)
    dr = kr[:, None, :] - r[:, :, None] + WIN_H - 1
    col = np.arange(GRID_W)
    c0 = np.clip(col - WIN_W // 2, 0, GRID_W - WIN_W)
    cv = (col[None, :] >= c0[:, None]) & (col[None, :] < c0[:, None] + WIN_W)
    pad = GRID_W - WIN_W
    rpb_pad = jnp.pad(rpb, ((0, 0), (0, 0), (pad, pad)))
    colbias = jnp.stack([rpb_pad[:, :, GRID_W - 1 - q:2 * GRID_W - 1 - q] for q in range(GRID_W)], axis=2)
    colbias = jnp.where(cv[None, None], colbias, NEG)
    blank = jnp.full((NA_HEADS, GRID_W, GRID_W), NEG, F32)
    pats = []
    for p in range(3):
        qrows = [jnp.concatenate([colbias[:, int(dr[p, i, j])] if rv[p, i, j] else blank
                                  for j in range(nk)], axis=2) for i in range(nq)]
        pats.append(jnp.concatenate(qrows, axis=1))
    return jnp.stack(pats)


def na_lat_attention(q, k, v, kc, vc, bias):
    B, T, W = q.shape
    tq = NA_QROWS * GRID_W
    nb = T // tq
    ctx_len = kc.shape[1]
    qsp = pl.BlockSpec((1, tq, W), lambda b, i: (b, i, 0))
    prv = pl.BlockSpec((1, tq, W), lambda b, i: (b, jnp.maximum(i - 1, 0), 0))
    nxt = pl.BlockSpec((1, tq, W), lambda b, i: (b, jnp.minimum(i + 1, nb - 1), 0))
    csp = pl.BlockSpec((1, ctx_len, W), lambda b, i: (b, 0, 0))
    pattern = lambda i: jnp.where(i == 0, 0, jnp.where(i == nb - 1, 2, 1))
    return pl.pallas_call(
        _na_lat_kernel,
        grid=(B, nb),
        in_specs=[qsp, prv, qsp, nxt, prv, qsp, nxt, csp, csp,
                  pl.BlockSpec((1, NA_HEADS, tq, NA_KROWS * GRID_W), lambda b, i: (pattern(i), 0, 0, 0))],
        out_specs=qsp,
        out_shape=jax.ShapeDtypeStruct((B, T, W), BF16),
        compiler_params=_cp("parallel", "arbitrary"),
        name="na_lat_attention",
    )(q, k, k, k, v, v, v, kc, vc, bias)


def _ffn_norm_router_body(h_ref, g_ref, sh_ref, sc_ref, rw_ref, rb_ref, v_ref, ids_ref, wts_ref):
    x = h_ref[0]
    y = x * lax.rsqrt(jnp.mean(x * x, axis=-1, keepdims=True) + EPS) * g_ref[...]
    v = y * (1.0 + sc_ref[0]) + sh_ref[0]
    half = v.shape[1] // 2
    v_ref[...] = _pack_bf16_pair(v[:, :half], v[:, half:])
    vh, vl = _split2(v)
    logits = (_dot(vh, rw_ref[0]) + (_dot(vh, rw_ref[1]) + _dot(vl, rw_ref[0]))) + rb_ref[...]
    lane = lax.broadcasted_iota(jnp.int32, logits.shape, 1).astype(F32)
    first = lambda mask: jnp.min(jnp.where(mask, lane, float(LANES)), axis=1, keepdims=True)

    g_mask = lane < N_GROUPS
    gl = jnp.where(g_mask, logits, NEG)
    gmax = jnp.max(gl, axis=1, keepdims=True)
    g_p = 1.0 / jnp.sum(jnp.exp(gl - gmax), axis=1, keepdims=True)
    g_idx = first(g_mask & (gl == gmax))
    e_lane = lane - N_GROUPS
    e_mask = (e_lane >= 0) & (e_lane < N_EXPERTS) & (jnp.floor(e_lane * (1.0 / EXPERTS_PER_GROUP)) == g_idx)
    el = jnp.where(e_mask, logits, NEG)
    e1 = jnp.max(el, axis=1, keepdims=True)
    i1 = first(e_mask & (el == e1))
    el2 = jnp.where(lane == i1, NEG, el)
    e2 = jnp.max(el2, axis=1, keepdims=True)
    i2 = first(e_mask & (lane != i1) & (el2 == e2))
    x2 = jnp.exp(e2 - e1)
    w1 = g_p / (1.0 + x2)
    w2 = g_p * x2 / (1.0 + x2)
    ids_ref[...] = jnp.where(lane == 0, i1 - N_GROUPS, jnp.where(lane == 1, i2 - N_GROUPS, 0.0)).astype(jnp.int32)
    wts_ref[...] = jnp.where(lane == 0, w1, jnp.where(lane == 1, w2, 0.0))


def _ffn_norm_router_kernel(*refs, bounds):
    ns = len(bounds) - 1
    g_ref, rw_ref, rb_ref, v_ref, ids_ref, wts_ref = refs[3 * ns:]
    r = pl.program_id(0)
    for s in range(ns):
        h_ref, sh_ref, sc_ref = refs[3 * s:3 * s + 3]

        @pl.when((r >= bounds[s]) & (r < bounds[s + 1]))
        def _():
            _ffn_norm_router_body(h_ref, g_ref, sh_ref, sc_ref, rw_ref, rb_ref, v_ref, ids_ref, wts_ref)


def ffn_norm_router(streams, g, rw, rb):
    D = g.shape[0]
    tm = 256
    bounds = [0]
    in_specs, args = [], []
    for h, shift, scale in streams:
        B, T, _ = h.shape
        nt = T // tm
        lo, nblk = bounds[-1], B * nt
        bounds.append(lo + nblk)
        local = lambda r, lo=lo, nblk=nblk: jnp.clip(r - lo, 0, nblk - 1)
        in_specs += [pl.BlockSpec((1, tm, D), lambda r, f=local, nt=nt: (f(r) // nt, f(r) % nt, 0)),
                     pl.BlockSpec((1, 1, D), lambda r, f=local, nt=nt: (f(r) // nt, 0, 0)),
                     pl.BlockSpec((1, 1, D), lambda r, f=local, nt=nt: (f(r) // nt, 0, 0))]
        args += [h, shift, scale]
    n_rows = bounds[-1] * tm
    in_specs += [pl.BlockSpec((1, D), lambda r: (0, 0)),
                 pl.BlockSpec((2, D, LANES), lambda r: (0, 0, 0)),
                 pl.BlockSpec((1, LANES), lambda r: (0, 0))]
    args += [g.reshape(1, D), rw, rb]
    lan = pl.BlockSpec((tm, LANES), lambda r: (r, 0))
    return pl.pallas_call(
        functools.partial(_ffn_norm_router_kernel, bounds=tuple(bounds)),
        grid=(bounds[-1],),
        in_specs=in_specs,
        out_specs=[pl.BlockSpec((tm, D // 2), lambda r: (r, 0)), lan, lan],
        out_shape=[jax.ShapeDtypeStruct((n_rows, D // 2), jnp.uint32),
                   jax.ShapeDtypeStruct((n_rows, LANES), jnp.int32),
                   jax.ShapeDtypeStruct((n_rows, LANES), F32)],
        compiler_params=_cp("parallel"),
        name="ffn_norm_router",
    )(*args)


def _moe_kernel(be_ref, zero_ref, tok_ref, tokn_ref, dstp_ref, sw_ref, v_hbm, wg_hbm, wu_hbm, wd_hbm, out_hbm,
                x0, x1, y0, y1, wg_buf, wu_buf, wd_buf, stage_up, stage_down, gsem, ssem, wsem, *, layer):
    i = pl.program_id(0)
    nb = pl.num_programs(0) - 1
    xs, ys = (x0, x1), (y0, y1)
    zero = zero_ref[0]
    used = zero_ref[1]

    def gather_row(idx_ref, s, j, after=0):
        return pltpu.make_async_copy(v_hbm.at[pl.ds(idx_ref[0, 0, j] + after, 1)], xs[s].at[pl.ds(j, 1)],
                                     gsem.at[s])

    def wait_gather(s):
        pltpu.make_async_copy(v_hbm.at[pl.ds(0, MOE_BLOCK)], xs[s], gsem.at[s]).wait()

    def scatter_row(s, j, after=0):
        return pltpu.make_async_copy(ys[s].at[pl.ds(j, 1)], out_hbm.at[pl.ds(dstp_ref[0, 0, j] + after, 1)],
                                     ssem.at[s])

    def wait_scatter(s):
        pltpu.make_async_copy(ys[s], out_hbm.at[pl.ds(0, MOE_BLOCK)], ssem.at[s]).wait()

    def after(result):
        return result[0, 0].astype(jnp.int32) * zero

    @pl.when(i == 0)
    def _():
        def body(j, carry):
            gather_row(tok_ref, 0, j).start()
            return carry
        lax.fori_loop(0, MOE_BLOCK, body, 0)

    e = be_ref[jnp.minimum(i, nb - 1)]

    @pl.when((i == 0) | ((i < used) & (e != be_ref[jnp.maximum(i - 1, 0)])))
    def _():
        chunks = []
        for src, dst, stage in ((wg_hbm, wg_buf, stage_up), (wu_hbm, wu_buf, stage_up),
                                (wd_hbm, wd_buf, stage_down)):
            rows = stage.shape[1]
            for r in range(dst.shape[0] // rows):
                k = len(chunks)
                cp = pltpu.make_async_copy(src.at[layer, e, pl.ds(r * rows, rows)], stage.at[k % 2],
                                           wsem.at[k % 2])
                chunks.append((cp, stage, dst, r * rows, rows))
        chunks[0][0].start()
        for k, (cp, stage, dst, r0, rows) in enumerate(chunks):
            if k + 1 < len(chunks):
                chunks[k + 1][0].start()
            cp.wait()
            dst[r0:r0 + rows, :] = stage[k % 2].astype(BF16)

    def step(s, scatter_prev):
        wait_gather(s)
        x_lo, x_hi = _unpack_bf16_pair(xs[s][...])
        x = jnp.concatenate([x_lo.astype(BF16), x_hi.astype(BF16)], axis=1)
        n_up = EXPERT_FF // MOE_FF_TILE
        per = MOE_BLOCK // (2 * n_up)
        acts = []
        for c in range(n_up):
            cols = slice(c * MOE_FF_TILE, (c + 1) * MOE_FF_TILE)
            hg = jnp.dot(x, wg_buf[:, cols], preferred_element_type=F32)
            hu = jnp.dot(x, wu_buf[:, cols], preferred_element_type=F32)
            for g, res in enumerate((hg, hu)):
                dep = after(res)
                for j in range((2 * c + g) * per, (2 * c + g + 1) * per):
                    gather_row(tokn_ref, 1 - s, j, dep).start(priority=j % 2)
            acts.append((_silu(hg) * hu).astype(BF16))
        act = jnp.concatenate(acts, axis=1)

        @pl.when(i >= 2)
        def _():
            wait_scatter(s)

        half = x.shape[1] // 2
        n_down = half // MOE_OUT_TILE
        per = MOE_BLOCK // n_down
        for c in range(n_down):
            cols = slice(c * MOE_OUT_TILE, (c + 1) * MOE_OUT_TILE)
            hcols = slice(half + c * MOE_OUT_TILE, half + (c + 1) * MOE_OUT_TILE)
            y_lo = jnp.dot(act, wd_buf[:, cols], preferred_element_type=F32) * sw_ref[...]
            y_hi = jnp.dot(act, wd_buf[:, hcols], preferred_element_type=F32) * sw_ref[...]
            ys[s][:, cols] = _pack_bf16_pair(y_lo, y_hi)
            if scatter_prev:
                dep = after(y_hi)
                for j in range(c * per, (c + 1) * per):
                    scatter_row(1 - s, j, dep).start(priority=j % 2)

    def idle(s):
        wait_gather(s)

        def fetch(j, carry):
            gather_row(tokn_ref, 1 - s, j).start()
            return carry
        lax.fori_loop(0, MOE_BLOCK, fetch, 0)
        wait_scatter(s)

        def send(j, carry):
            scatter_row(1 - s, j).start()
            return carry
        lax.fori_loop(0, MOE_BLOCK, send, 0)

    pl.when(i == 0)(functools.partial(step, 0, False))
    for s in range(2):
        pl.when((i > 0) & (i < used) & (i % 2 == s))(functools.partial(step, s, True))
        pl.when((i >= used) & (i < nb) & (i % 2 == s))(functools.partial(idle, s))

    @pl.when(i == nb)
    def _():
        def drain(s):
            wait_gather(s)
            wait_scatter(s)

            def body(j, carry):
                scatter_row(1 - s, j).start()
                return carry
            lax.fori_loop(0, MOE_BLOCK, body, 0)
            wait_scatter(1 - s)
        for s in range(2):
            pl.when(i % 2 == s)(functools.partial(drain, s))


def moe_experts(v_rows, block_expert, used, slot_tok, slot_dst, slot_w, wg, wu, wd, layer):
    P = slot_tok.shape[0]
    W = v_rows.shape[1]
    D = 2 * W
    nb = P // MOE_BLOCK
    assert nb >= 2
    idx = lambda a: a.reshape(nb, 1, MOE_BLOCK)
    smem = lambda imap: pl.BlockSpec((1, 1, MOE_BLOCK), imap, memory_space=pltpu.SMEM)
    hbm = pl.BlockSpec(memory_space=pl.ANY)
    last = nb - 1
    grid_spec = pltpu.PrefetchScalarGridSpec(
        num_scalar_prefetch=2,
        grid=(nb + 1,),
        in_specs=[smem(lambda i, be, z: (jnp.minimum(i, last), 0, 0)),
                  smem(lambda i, be, z: (jnp.minimum(i + 1, last), 0, 0)),
                  smem(lambda i, be, z: (jnp.maximum(i - 1, 0), 0, 0)),
                  pl.BlockSpec((MOE_BLOCK, 1), lambda i, be, z: (jnp.minimum(i, last), 0)),
                  hbm, hbm, hbm, hbm],
        out_specs=hbm,
        scratch_shapes=[pltpu.VMEM((MOE_BLOCK, W), jnp.uint32),
                        pltpu.VMEM((MOE_BLOCK, W), jnp.uint32),
                        pltpu.VMEM((MOE_BLOCK, W), jnp.uint32),
                        pltpu.VMEM((MOE_BLOCK, W), jnp.uint32),
                        pltpu.VMEM((D, EXPERT_FF), BF16),
                        pltpu.VMEM((D, EXPERT_FF), BF16),
                        pltpu.VMEM((EXPERT_FF, D), BF16),
                        pltpu.VMEM((2, MOE_STAGE_BYTES // (4 * EXPERT_FF), EXPERT_FF), F32),
                        pltpu.VMEM((2, MOE_STAGE_BYTES // (4 * D), D), F32),
                        pltpu.SemaphoreType.DMA((2,)),
                        pltpu.SemaphoreType.DMA((2,)),
                        pltpu.SemaphoreType.DMA((2,))],
    )
    return pl.pallas_call(
        functools.partial(_moe_kernel, layer=layer),
        grid_spec=grid_spec,
        out_shape=jax.ShapeDtypeStruct((P, W), jnp.uint32),
        compiler_params=_cp("arbitrary"),
        name="moe_experts",
    )(block_expert, jnp.stack([jnp.zeros((), jnp.int32), used]), idx(slot_tok), idx(slot_tok), idx(slot_dst),
      slot_w.reshape(P, 1), v_rows, wg, wu, wd)


def _moe_combine_kernel(h_ref, f0_ref, f1_ref, gate_ref, *rest):
    a_lo, a_hi = _unpack_bf16_pair(f0_ref[...])
    b_lo, b_hi = _unpack_bf16_pair(f1_ref[...])
    f = jnp.concatenate([a_lo + b_lo, a_hi + b_hi], axis=1)
    x = h_ref[0] + gate_ref[0] * f
    if len(rest) == 1:
        (o_ref,) = rest
    else:
        g_ref, sh_ref, sc_ref, o_ref, u_ref = rest
        y = x * lax.rsqrt(jnp.mean(x * x, axis=-1, keepdims=True) + EPS) * g_ref[...]
        u_ref[0] = (y * (1.0 + sc_ref[0]) + sh_ref[0]).astype(u_ref.dtype)
    o_ref[0] = x


def moe_combine(h, out2, gate, row_offset, n_rows, next_norm=None):
    B, T, D = h.shape
    tm = 256
    nt = T // tm
    blk0 = row_offset // tm
    tok = pl.BlockSpec((1, tm, D), lambda b, i: (b, i, 0))
    vec = pl.BlockSpec((1, 1, D), lambda b, i: (b, 0, 0))
    in_specs = [tok,
                pl.BlockSpec((tm, D // 2), lambda b, i: (blk0 + b * nt + i, 0)),
                pl.BlockSpec((tm, D // 2), lambda b, i: (n_rows // tm + blk0 + b * nt + i, 0)),
                vec]
    args = [h, out2, out2, gate]
    out_specs, out_shape = tok, jax.ShapeDtypeStruct((B, T, D), F32)
    if next_norm is not None:
        g, shift, scale = next_norm
        in_specs += [pl.BlockSpec((1, D), lambda b, i: (0, 0)), vec, vec]
        args += [g.reshape(1, D), shift, scale]
        out_specs, out_shape = [tok, tok], [out_shape, jax.ShapeDtypeStruct((B, T, D), BF16)]
    return pl.pallas_call(
        _moe_combine_kernel,
        grid=(B, nt),
        in_specs=in_specs,
        out_specs=out_specs,
        out_shape=out_shape,
        compiler_params=_cp("parallel", "parallel"),
        name="moe_combine",
    )(*args)


def moe_dispatch(ids, wts):
    N = ids.shape[0]
    A = N * TOP_K
    e_flat = ids.reshape(A)
    order = jnp.argsort(e_flat).astype(jnp.int32)
    counts = jnp.sum((e_flat[:, None] == jnp.arange(N_EXPERTS)[None, :]).astype(jnp.int32), axis=0)
    padded = (counts + MOE_BLOCK - 1) // MOE_BLOCK * MOE_BLOCK
    ends = jnp.cumsum(padded)
    n_blocks = -(-(A + N_EXPERTS * (MOE_BLOCK - 1)) // MOE_BLOCK)
    block_start = jnp.arange(n_blocks, dtype=jnp.int32) * MOE_BLOCK
    block_expert = jnp.minimum(jnp.sum((ends[None, :] <= block_start[:, None]).astype(jnp.int32), axis=1),
                               N_EXPERTS - 1)
    into = block_start - (ends - padded)[block_expert]
    b_count = counts[block_expert]
    local = (into[:, None] + jnp.arange(MOE_BLOCK, dtype=jnp.int32)[None, :])
    real = local < b_count[:, None]
    src = jnp.clip((jnp.cumsum(counts) - counts)[block_expert][:, None] + local, 0, A - 1)
    a = order[src.reshape(-1)]
    real = real.reshape(-1)
    tok = a // TOP_K
    slot_tok = jnp.where(real, tok, 0).astype(jnp.int32)
    spare = A + jnp.cumsum(jnp.where(real, 0, 1)) - 1
    slot_dst = jnp.where(real, (a % TOP_K) * N + tok, spare).astype(jnp.int32)
    slot_w = jnp.where(real, wts.reshape(A)[a], 0.0)
    return slot_tok, slot_dst, slot_w, block_expert, (ends[-1] // MOE_BLOCK).astype(jnp.int32)


def hier_moe(v_rows, ids, wts, wg, wu, wd, layer):
    slot_tok, slot_dst, slot_w, block_expert, used = moe_dispatch(ids, wts)
    return moe_experts(v_rows, block_expert, used, slot_tok, slot_dst, slot_w, wg, wu, wd, layer)


def _layer_weights(l, P):
    w = {}
    w_in = P["w_in"][l]
    o = 0
    a = w_in[:, o:o + RWKV_COLS]; o += RWKV_COLS
    b = w_in[:, o:o + GROUP_W]; o += GROUP_W
    c = w_in[:, o:o + MLSTM_COLS]; o += MLSTM_COLS
    d = w_in[:, o:o + NA_COLS]
    D = w_in.shape[0]
    pad_a = RWKV_PAD_COLS - RWKV_COLS
    w["in_a"] = jnp.concatenate([a, jnp.zeros((D, pad_a), F32)], axis=1).astype(BF16)
    w["in_b"] = b.astype(BF16)
    w["in_c"] = jnp.concatenate([c, jnp.zeros((D, MLSTM_PAD_COLS - MLSTM_COLS), F32)], axis=1).astype(BF16)
    w["in_d"] = d.astype(BF16)
    w["out"] = P["w_out"][l].astype(BF16)

    z = jnp.zeros((RWKV_DECAY_RANK, GROUP_W), F32)
    wup, aup = P["rwkv_w_up"][l], P["rwkv_a_up"][l]
    rw = {
        "mu": jnp.concatenate([P["rwkv_mu"][l], jnp.zeros((2, pad_a), F32)], axis=1),
        "w_up": jnp.concatenate([jnp.concatenate([wup[0], z], axis=1),
                                 jnp.concatenate([z, wup[1]], axis=1)], axis=0).astype(BF16),
        "a_up": jnp.concatenate([jnp.concatenate([aup[0], z], axis=1),
                                 jnp.concatenate([z, aup[1]], axis=1)], axis=0).astype(BF16),
        "w0": P["rwkv_w0"][l].reshape(1, 2 * GROUP_W),
        "a0": P["rwkv_a0"][l].reshape(1, 2 * GROUP_W),
        "g_up": jnp.concatenate([P["rwkv_g_up"][l],
                                 jnp.zeros((256 - RWKV_GATE_RANK, GROUP_W), F32)], axis=0).astype(BF16),
        "k_k": P["rwkv_k_k"][l].reshape(1, GROUP_W),
        "k_a": P["rwkv_k_a"][l].reshape(1, GROUP_W),
        "r_k": P["rwkv_r_k"][l].reshape(1, GROUP_W),
    }
    w["rwkv"] = rw
    w["rwkv_ln"] = P["rwkv_ln"][l]
    w["pool_w"] = P["pool_w"][l].astype(BF16)
    w["pool_scale"] = P["pool_scale"][l]
    w["mlstm_conv"] = P["mlstm_conv"][l]
    gb = P["mlstm_gate_b"][l].reshape(1, 4 * MLSTM_HEADS)
    w["mlstm_gate_b"] = jnp.concatenate([gb, jnp.zeros((1, LANES - 4 * MLSTM_HEADS), F32)], axis=1)
    w["mlstm_norm"] = P["mlstm_norm"][l]
    w["na_qk_norm"] = P["na_qk_norm"][l]
    w["na_rpb"] = P["na_rpb"][l]
    rcat = jnp.concatenate([P["router_g_w"][l], P["router_e_w"][l],
                            jnp.zeros((D, LANES - N_GROUPS - N_EXPERTS), F32)], axis=1)
    rhi = rcat.astype(BF16)
    w["router_w"] = jnp.stack([rhi, (rcat - rhi.astype(F32)).astype(BF16)])
    w["router_b"] = jnp.concatenate([P["router_g_b"][l], P["router_e_b"][l],
                                     jnp.zeros((LANES - N_GROUPS - N_EXPERTS,), F32)]).reshape(1, LANES)
    return w


def rope_tables(T):
    t = jnp.arange(T)
    row = (t // GRID_W).astype(F32)
    col = (t % GRID_W).astype(F32)
    n_pairs = MLSTM_DK // 4
    inv = ROPE_BASE ** (-jnp.arange(n_pairs, dtype=F32) / n_pairs)
    ang = jnp.concatenate([row[:, None] * inv, col[:, None] * inv], axis=-1)
    cos = jnp.repeat(jnp.cos(ang), 2, axis=1)
    sin = jnp.repeat(jnp.sin(ang), 2, axis=1) * jnp.tile(jnp.array([-1.0, 1.0], F32), MLSTM_DK // 2)
    return cos, sin


def rwkv_mixer(p_ctx, p_lat, w, need_ctx):
    B = p_lat.shape[0]
    names = ("r", "kk", "v", "bonus", "g", "lw", "kd", "bb")
    fc = dict(zip(names, rwkv_prep(p_ctx, w["rwkv"])))
    fl = dict(zip(names, rwkv_prep(p_lat, w["rwkv"])))
    s0 = jnp.zeros((2, B, RWKV_PAIRS, LANES, LANES), F32)
    y_c, s_ctx = rwkv_scan(fc, s0)
    y_l, _ = rwkv_scan(fl, s_ctx)
    out_l = rwkv_readout(y_l, fl["bonus"], fl["g"], w["rwkv_ln"])
    out_c = rwkv_readout(y_c, fc["bonus"], fc["g"], w["rwkv_ln"]) if need_ctx else None
    return out_c, out_l


def mlstm_mixer(p_ctx, p_lat, w, rope, need_ctx):
    B = p_lat.shape[0]
    H = MLSTM_HEADS
    state = (jnp.zeros((2, B, H, MLSTM_DK, MLSTM_DV), F32),
             jnp.zeros((2, B, H, 1, MLSTM_DK), F32),
             jnp.zeros((2, B, H, 1, 1), F32))
    outs = []
    for p, tabs in ((p_ctx, None), (p_lat, rope)):
        q, k, gates = mlstm_prep(p, w["mlstm_conv"], w["mlstm_gate_b"], tabs)
        gates_t = jnp.swapaxes(gates[:, :, :4 * H], 1, 2)
        h, *state = mlstm_scan(q, k, p, gates, gates_t, tuple(state))
        outs.append(h)
    out_l = mlstm_readout(outs[1], p_lat, w["mlstm_norm"])
    out_c = mlstm_readout(outs[0], p_ctx, w["mlstm_norm"]) if need_ctx else None
    return out_c, out_l


def na_mixer(p_ctx, p_lat, w, need_ctx):
    qc, kc, vc = na_prep(p_ctx, w["na_qk_norm"])
    ql, kl, vl = na_prep(p_lat, w["na_qk_norm"])
    rows = p_lat.shape[1] // GRID_W
    bias = na_bias_table(w["na_rpb"], rows)
    out_l = na_lat_attention(ql, kl, vl, kc, vc, bias)
    out_c = na_ctx_attention(qc, kc, vc) if need_ctx else None
    return out_c, out_l


def token_mixers(u_ctx, u_lat, w, rope, need_ctx):
    outs_c, outs_l = [], []
    pa = (matmul(u_ctx, w["in_a"]), matmul(u_lat, w["in_a"]))
    oc, ol = rwkv_mixer(pa[0], pa[1], w, need_ctx)
    outs_c.append(oc); outs_l.append(ol)
    pb = (matmul(u_ctx, w["in_b"]), matmul(u_lat, w["in_b"]))
    outs_l.append(pool_mixer(pb[1], w["pool_w"], w["pool_scale"]))
    outs_c.append(pool_mixer(pb[0], w["pool_w"], w["pool_scale"]) if need_ctx else None)
    pc = (matmul(u_ctx, w["in_c"]), matmul(u_lat, w["in_c"]))
    oc, ol = mlstm_mixer(pc[0], pc[1], w, rope, need_ctx)
    outs_c.append(oc); outs_l.append(ol)
    pd = (matmul(u_ctx, w["in_d"]), matmul(u_lat, w["in_d"]))
    oc, ol = na_mixer(pd[0], pd[1], w, need_ctx)
    outs_c.append(oc); outs_l.append(ol)
    return outs_c, outs_l


def kernel(x, c, ctx, c_ctx, ada_w, ada_b, norm_mix, norm_ffn, w_in, w_out, rwkv_mu, rwkv_w0, rwkv_w_up, rwkv_a0, rwkv_a_up, rwkv_g_up, rwkv_k_k, rwkv_k_a, rwkv_r_k, rwkv_ln, pool_w, pool_scale, mlstm_conv, mlstm_gate_b, mlstm_norm, na_qk_norm, na_rpb, router_g_w, router_g_b, router_e_w, router_e_b, moe_w_gate, moe_w_up, moe_w_down):
    P = dict(w_in=w_in, w_out=w_out, rwkv_mu=rwkv_mu, rwkv_w0=rwkv_w0, rwkv_w_up=rwkv_w_up,
             rwkv_a0=rwkv_a0, rwkv_a_up=rwkv_a_up, rwkv_g_up=rwkv_g_up, rwkv_k_k=rwkv_k_k,
             rwkv_k_a=rwkv_k_a, rwkv_r_k=rwkv_r_k, rwkv_ln=rwkv_ln, pool_w=pool_w, pool_scale=pool_scale,
             mlstm_conv=mlstm_conv, mlstm_gate_b=mlstm_gate_b, mlstm_norm=mlstm_norm,
             na_qk_norm=na_qk_norm, na_rpb=na_rpb, router_g_w=router_g_w, router_g_b=router_g_b,
             router_e_w=router_e_w, router_e_b=router_e_b, moe_w_gate=moe_w_gate, moe_w_up=moe_w_up,
             moe_w_down=moe_w_down)
    B, T, D = x.shape
    Lc = ctx.shape[1]
    depth = ada_w.shape[0]
    rope = rope_tables(T)
    moe_w = (moe_w_gate, moe_w_up, moe_w_down)

    cvec = jnp.concatenate([c, c_ctx[None, :], jnp.zeros((SUBLANES - B - 1, D), F32)], axis=0)
    mods = ada_modulation(cvec, ada_w, ada_b)

    def layer_mods(l):
        m = mods[l].reshape(SUBLANES, 6, D)
        return ([m[:B, i][:, None, :] for i in range(6)],
                [jnp.broadcast_to(m[B, i][None, None, :], (B, 1, D)) for i in range(6)])

    h_lat, h_ctx = x, ctx
    u_lat = None
    for l in range(depth):
        last = l == depth - 1
        w = _layer_weights(l, P)
        m_lat, m_ctx = layer_mods(l)

        if u_lat is None:
            u_lat = norm_modulate(h_lat, norm_mix[l], m_lat[0], m_lat[1])
        u_ctx = norm_modulate(h_ctx, norm_mix[l], m_ctx[0], m_ctx[1])
        ys_ctx, ys_lat = token_mixers(u_ctx, u_lat, w, rope, not last)
        h_lat = out_proj_residual(ys_lat, w["out"], h_lat, m_lat[2])
        n_rows = B * T if last else B * (T + Lc)
        streams = [(h_lat, m_lat[3], m_lat[4])]
        if not last:
            h_ctx = out_proj_residual(ys_ctx, w["out"], h_ctx, m_ctx[2])
            streams.append((h_ctx, m_ctx[3], m_ctx[4]))
        rows, ids, wts = ffn_norm_router(streams, norm_ffn[l], w["router_w"], w["router_b"])
        f2 = hier_moe(rows, ids[:, :TOP_K], wts[:, :TOP_K], *moe_w, l)
        if last:
            h_lat = moe_combine(h_lat, f2, m_lat[5], 0, n_rows)
        else:
            h_ctx = moe_combine(h_ctx, f2, m_ctx[5], B * T, n_rows)
            nxt = layer_mods(l + 1)[0]
            h_lat, u_lat = moe_combine(h_lat, f2, m_lat[5], 0, n_rows,
                                       next_norm=(norm_mix[l + 1], nxt[0], nxt[1]))
    return h_lat
```
